```python
import math
import jax
import jax.numpy as jnp
from jax import lax
import numpy as np


D_MODEL = 1024
BATCH = 4
SEQ = 4096
DEPTH = 2

GRID_W = 64
CTX_LEN = 256
Q_BLOCK = 128
ROPE_THETA = 10000.0
NORM_EPS = 1e-6

MLA_HEADS = 8
MLA_Q_RANK = 384
MLA_KV_RANK = 256
MLA_NOPE = 64
MLA_ROPE = 32
MLA_V = 64
MLA_QK = MLA_NOPE + MLA_ROPE

SSD_HEADS = 16
SSD_HEAD_DIM = 64
SSD_INNER = SSD_HEADS * SSD_HEAD_DIM
SSD_GROUPS = 2
SSD_STATE = 128
SSD_CONV = 3
SSD_CHUNK = 128
SSD_CONV_CH = SSD_INNER + 2 * SSD_GROUPS * SSD_STATE

GQA_HEADS = 8
GQA_KV_HEADS = 2
GQA_HEAD_DIM = 64

N_BRANCH = 3
MLA_IN = MLA_Q_RANK + MLA_KV_RANK + MLA_ROPE
SSD_IN = SSD_INNER + SSD_CONV_CH + 2 * SSD_HEADS
GQA_IN = (GQA_HEADS + 2 * GQA_KV_HEADS) * GQA_HEAD_DIM
GATE_IN = N_BRANCH * D_MODEL
IN_WIDTH = MLA_IN + SSD_IN + GQA_IN + GATE_IN
MLA_OUT = MLA_HEADS * MLA_V
GQA_OUT = GQA_HEADS * GQA_HEAD_DIM
OUT_WIDTH = MLA_OUT + SSD_INNER + GQA_OUT

FFN_HIDDEN = -(-8 * D_MODEL // (3 * 256)) * 256
DEEPNORM_ALPHA = (2 * DEPTH) ** 0.25
DEEPNORM_BETA = (8 * DEPTH) ** -0.25

kernel_name = 'hybrid_mla_ssd_gqa_diffusion_block'


def layer_norm(x):
    xf = x.astype(jnp.float32)
    mu = jnp.mean(xf, axis=-1, keepdims=True)
    var = jnp.mean(jnp.square(xf - mu), axis=-1, keepdims=True)
    return ((xf - mu) * lax.rsqrt(var + NORM_EPS)).astype(x.dtype)


def layer_norm_affine(x, g, b):
    return layer_norm(x) * g + b


def rms_norm(x, g):
    xf = x.astype(jnp.float32)
    y = xf * lax.rsqrt(jnp.mean(jnp.square(xf), axis=-1, keepdims=True) + NORM_EPS)
    return (y * g).astype(x.dtype)


def axial_rope_tables(n_tokens, rot_dim):
    rows = n_tokens // GRID_W
    row = jnp.repeat(jnp.arange(rows), GRID_W)
    col = jnp.tile(jnp.arange(GRID_W), rows)
    n_freq = rot_dim // 4
    inv_freq = ROPE_THETA ** (-jnp.arange(n_freq, dtype=jnp.float32) / n_freq)
    ang = jnp.concatenate([row[:, None] * inv_freq, col[:, None] * inv_freq], axis=-1)
    return jnp.cos(ang), jnp.sin(ang)


def apply_rope(x, cos, sin):
    x1, x2 = jnp.split(x, 2, axis=-1)
    cos = cos[None, :, None, :].astype(x.dtype)
    sin = sin[None, :, None, :].astype(x.dtype)
    return jnp.concatenate([x1 * cos - x2 * sin, x1 * sin + x2 * cos], axis=-1)


def block_attention(q, k, v, scale):
    b, lq, hk, r, d = q.shape
    dv = v.shape[-1]
    n_blk = lq // Q_BLOCK
    qb = jnp.moveaxis(q.reshape(b, n_blk, Q_BLOCK, hk, r, d), 1, 0)

    def one_block(qi):
        s = jnp.einsum('bqhrd,bkhd->bhrqk', qi, k).astype(jnp.float32) * scale
        p = jax.nn.softmax(s, axis=-1).astype(v.dtype)
        return jnp.einsum('bhrqk,bkhd->bqhrd', p, v)

    o = lax.map(one_block, qb)
    return jnp.moveaxis(o, 0, 1).reshape(b, lq, hk * r * dv)


def mla_queries(p, w_uq, g_q, rope):
    b, l, _ = p.shape
    q = (rms_norm(p[..., :MLA_Q_RANK], g_q) @ w_uq).reshape(b, l, MLA_HEADS, MLA_QK)
    q_nope, q_rope = jnp.split(q, [MLA_NOPE], axis=-1)
    if rope is not None:
        q_rope = apply_rope(q_rope, *rope)
    return jnp.concatenate([q_nope, q_rope], axis=-1)[:, :, :, None, :]


def mla_keys_values(p, w_ukv, g_kv, rope):
    b, l, _ = p.shape
    ckv = p[..., MLA_Q_RANK:MLA_Q_RANK + MLA_KV_RANK]
    k_rope = p[..., MLA_Q_RANK + MLA_KV_RANK:][:, :, None, :]
    kv = (rms_norm(ckv, g_kv) @ w_ukv).reshape(b, l, MLA_HEADS, MLA_NOPE + MLA_V)
    k_nope, v = jnp.split(kv, [MLA_NOPE], axis=-1)
    if rope is not None:
        k_rope = apply_rope(k_rope, *rope)
    k = jnp.concatenate([k_nope, jnp.broadcast_to(k_rope, (b, l, MLA_HEADS, MLA_ROPE))], axis=-1)
    return k, v


def centred_depthwise_conv(u, w, bias):
    y = lax.conv_general_dilated(
        u, w[:, None, :].astype(u.dtype), window_strides=(1,),
        padding=[((SSD_CONV - 1) // 2, SSD_CONV // 2)],
        dimension_numbers=('NWC', 'WIO', 'NWC'), feature_group_count=u.shape[-1])
    return y + bias


def ssd_prepare(p, conv_w, conv_b, dt_bias):
    b, l, _ = p.shape
    z, xbc, dt = jnp.split(p, [SSD_INNER, SSD_INNER + SSD_CONV_CH], axis=-1)
    xbc = jax.nn.silu(centred_depthwise_conv(xbc, conv_w, conv_b))
    xs, bm, cm = jnp.split(xbc, [SSD_INNER, SSD_INNER + SSD_GROUPS * SSD_STATE], axis=-1)
    xs = xs.reshape(b, l, SSD_HEADS, SSD_HEAD_DIM)
    bm = bm.reshape(b, l, SSD_GROUPS, SSD_STATE)
    cm = cm.reshape(b, l, SSD_GROUPS, SSD_STATE)
    dt = jax.nn.softplus(dt.astype(jnp.float32) + dt_bias.reshape(-1).astype(jnp.float32))
    return z, xs, bm, cm, dt[..., :SSD_HEADS], dt[..., SSD_HEADS:]


def ssd_scan(x, dt, a, bm, cm, h0, return_y):
    b, l, h, p = x.shape
    g, n = bm.shape[-2:]
    r = h // g
    nc = l // SSD_CHUNK
    f32 = jnp.float32
    xdt = (x.astype(f32) * dt[..., None]).reshape(b, nc, SSD_CHUNK, g, r, p)
    bc = bm.astype(f32).reshape(b, nc, SSD_CHUNK, g, n)
    a_cum = jnp.cumsum((dt * a).reshape(b, nc, SSD_CHUNK, g, r), axis=2)
    decay_to_end = jnp.exp(a_cum[:, :, -1:] - a_cum)
    chunk_states = jnp.einsum('bckgn,bckgr,bckgrp->bcgrpn', bc, decay_to_end, xdt)
    chunk_decay = jnp.exp(a_cum[:, :, -1])

    def step(h_prev, inp):
        s, d = inp
        h_new = d[..., None, None] * h_prev + s
        return h_new, (h_prev if return_y else None)

    h_last, h_in = lax.scan(step, h0, (jnp.moveaxis(chunk_states, 1, 0), jnp.moveaxis(chunk_decay, 1, 0)))
    if not return_y:
        return None, h_last
    cc = cm.astype(f32).reshape(b, nc, SSD_CHUNK, g, n)
    seg = a_cum[:, :, :, None] - a_cum[:, :, None]
    in_order = jnp.tril(jnp.ones((SSD_CHUNK, SSD_CHUNK), bool))[None, None, :, :, None, None]
    decay = jnp.exp(jnp.where(in_order, seg, -jnp.inf))
    scores = jnp.einsum('bcqgn,bckgn->bcqkg', cc, bc)
    y_diag = jnp.einsum('bcqkg,bcqkgr,bckgrp->bcqgrp', scores, decay, xdt)
    y_off = jnp.einsum('bcqgn,bcgrpn,bcqgr->bcqgrp', cc, jnp.moveaxis(h_in, 0, 1), jnp.exp(a_cum))
    return (y_diag + y_off).reshape(b, l, h, p).astype(x.dtype), h_last


def ssd_branch(px, pc, conv_w, conv_b, a_log, dt_bias, d_skip, g_ssd, with_ctx):
    zx, xx, bx, cx, dtx_f, dtx_b = ssd_prepare(px, conv_w, conv_b, dt_bias)
    zc, xc, bc, cc, dtc_f, dtc_b = ssd_prepare(pc, conv_w, conv_b, dt_bias)
    a = -jnp.exp(a_log.astype(jnp.float32))
    b = px.shape[0]
    h0 = jnp.zeros((b, SSD_GROUPS, SSD_HEADS // SSD_GROUPS, SSD_HEAD_DIM, SSD_STATE), jnp.float32)
    fl = lambda t: jnp.flip(t, axis=1)
    yc_f, hc_f = ssd_scan(xc, dtc_f, a[0], bc, cc, h0, with_ctx)
    yc_b, hc_b = ssd_scan(fl(xc), fl(dtc_b), a[1], fl(bc), fl(cc), h0, with_ctx)
    yx_f, _ = ssd_scan(xx, dtx_f, a[0], bx, cx, hc_f, True)
    yx_b, _ = ssd_scan(fl(xx), fl(dtx_b), a[1], fl(bx), fl(cx), hc_b, True)

    def finish(y_f, y_b_rev, xs, z):
        y = y_f + fl(y_b_rev) + d_skip[:, None].astype(xs.dtype) * xs
        return rms_norm(y.reshape(z.shape) * jax.nn.silu(z), g_ssd)

    out_x = finish(yx_f, yx_b, xx, zx)
    out_c = finish(yc_f, yc_b, xc, zc) if with_ctx else None
    return out_x, out_c


def gqa_queries(p, g_q, rope):
    b, l, _ = p.shape
    q = rms_norm(p[..., :GQA_OUT].reshape(b, l, GQA_HEADS, GQA_HEAD_DIM), g_q)
    if rope is not None:
        q = apply_rope(q, *rope)
    return q.reshape(b, l, GQA_KV_HEADS, GQA_HEADS // GQA_KV_HEADS, GQA_HEAD_DIM)


def gqa_keys_values(p, g_k, rope):
    b, l, _ = p.shape
    kv_w = GQA_KV_HEADS * GQA_HEAD_DIM
    k = rms_norm(p[..., GQA_OUT:GQA_OUT + kv_w].reshape(b, l, GQA_KV_HEADS, GQA_HEAD_DIM), g_k)
    v = p[..., GQA_OUT + kv_w:].reshape(b, l, GQA_KV_HEADS, GQA_HEAD_DIM)
    if rope is not None:
        k = apply_rope(k, *rope)
    return k, v


def merge_branches(y_mla, y_ssd, y_gqa, gate_logits, b_gate, w_out):
    g = jax.nn.sigmoid((gate_logits + b_gate).astype(jnp.float32)).astype(y_mla.dtype)
    g_mla, g_ssd, g_gqa = jnp.split(g, N_BRANCH, axis=-1)
    w_mla, w_ssd, w_gqa = jnp.split(w_out, [MLA_OUT, MLA_OUT + SSD_INNER], axis=0)
    return g_mla * (y_mla @ w_mla) + g_ssd * (y_ssd @ w_ssd) + g_gqa * (y_gqa @ w_gqa)


def hybrid_mixer(hx, hc, rope_mla, rope_gqa, w_in, b_gate, w_uq, g_q_mla, w_ukv, g_kv_mla,
                 conv_w, conv_b, a_log, dt_bias, d_skip, g_ssd, g_q_gqa, g_k_gqa, w_out, with_ctx):
    cuts = [MLA_IN, MLA_IN + SSD_IN, MLA_IN + SSD_IN + GQA_IN]
    px_mla, px_ssd, px_gqa, px_gate = jnp.split(hx @ w_in, cuts, axis=-1)
    pc_mla, pc_ssd, pc_gqa, pc_gate = jnp.split(hc @ w_in, cuts, axis=-1)
    mla_scale = MLA_QK ** -0.5
    gqa_scale = GQA_HEAD_DIM ** -0.5

    kx, vx = mla_keys_values(px_mla, w_ukv, g_kv_mla, rope_mla)
    kc, vc = mla_keys_values(pc_mla, w_ukv, g_kv_mla, None)
    yx_mla = block_attention(mla_queries(px_mla, w_uq, g_q_mla, rope_mla),
                             jnp.concatenate([kc, kx], axis=1), jnp.concatenate([vc, vx], axis=1), mla_scale)
    yx_ssd, yc_ssd = ssd_branch(px_ssd, pc_ssd, conv_w, conv_b, a_log, dt_bias, d_skip, g_ssd, with_ctx)
    kx_g, vx_g = gqa_keys_values(px_gqa, g_k_gqa, rope_gqa)
    kc_g, vc_g = gqa_keys_values(pc_gqa, g_k_gqa, None)
    yx_gqa = block_attention(gqa_queries(px_gqa, g_q_gqa, rope_gqa),
                             jnp.concatenate([kc_g, kx_g], axis=1), jnp.concatenate([vc_g, vx_g], axis=1), gqa_scale)

    mx = merge_branches(yx_mla, yx_ssd, yx_gqa, px_gate, b_gate, w_out)
    mc = None
    if with_ctx:
        yc_mla = block_attention(mla_queries(pc_mla, w_uq, g_q_mla, None), kc, vc, mla_scale)
        yc_gqa = block_attention(gqa_queries(pc_gqa, g_q_gqa, None), kc_g, vc_g, gqa_scale)
        mc = merge_branches(yc_mla, yc_ssd, yc_gqa, pc_gate, b_gate, w_out)
    return mx, mc


def swiglu(h, w_ffn_in, w_ffn_out):
    gate, up = jnp.split(h @ w_ffn_in, 2, axis=-1)
    return (jax.nn.silu(gate) * up) @ w_ffn_out


def setup_inputs(seed: int = 0) -> dict:
    key = jax.random.key(seed)
    k = jax.random.split(key, 27)
    f32 = jnp.float32
    L = DEPTH

    def nrm(i, shape, scale):
        return jax.random.normal(k[i], shape, f32) * scale

    def gain(i, shape):
        return 1.0 + nrm(i, shape, 0.02)

    dt0 = jnp.exp(jax.random.uniform(k[15], (L, 2, SSD_HEADS), f32, math.log(1e-3), math.log(1e-1)))
    return {
        'x': nrm(0, (BATCH, SEQ, D_MODEL), 1.0),
        'c': nrm(1, (BATCH, D_MODEL), 1.0),
        'ctx': nrm(2, (BATCH, CTX_LEN, D_MODEL), 1.0),
        'c_ctx': nrm(3, (D_MODEL,), 1.0),
        'w_mod': nrm(4, (L, D_MODEL, 6 * D_MODEL), 0.5 * D_MODEL ** -0.5),
        'b_mod': nrm(5, (L, 6 * D_MODEL), 0.01),
        'w_in': nrm(6, (L, D_MODEL, IN_WIDTH), D_MODEL ** -0.5),
        'b_gate': nrm(7, (L, GATE_IN), 0.01),
        'w_uq': nrm(8, (L, MLA_Q_RANK, MLA_HEADS * MLA_QK), MLA_Q_RANK ** -0.5),
        'g_q_mla': gain(9, (L, MLA_Q_RANK)),
        'w_ukv': nrm(10, (L, MLA_KV_RANK, MLA_HEADS * (MLA_NOPE + MLA_V)), MLA_KV_RANK ** -0.5),
        'g_kv_mla': gain(11, (L, MLA_KV_RANK)),
        'conv_w': nrm(12, (L, SSD_CONV, SSD_CONV_CH), SSD_CONV ** -0.5),
        'conv_b': nrm(13, (L, SSD_CONV_CH), 0.01),
        'a_log': jnp.log(jax.random.uniform(k[14], (L, 2, SSD_HEADS), f32, 1.0, 16.0)),
        'dt_bias': dt0 + jnp.log(-jnp.expm1(-dt0)),
        'd_skip': gain(16, (L, SSD_HEADS)),
        'g_ssd': gain(17, (L, SSD_INNER)),
        'g_q_gqa': gain(18, (L, GQA_HEAD_DIM)),
        'g_k_gqa': gain(19, (L, GQA_HEAD_DIM)),
        'w_out': nrm(20, (L, OUT_WIDTH, D_MODEL), DEEPNORM_BETA * OUT_WIDTH ** -0.5),
        'ln1_g': gain(21, (L, D_MODEL)),
        'ln1_b': nrm(22, (L, D_MODEL), 0.01),
        'w_ffn_in': nrm(23, (L, D_MODEL, 2 * FFN_HIDDEN), D_MODEL ** -0.5),
        'w_ffn_out': nrm(24, (L, FFN_HIDDEN, D_MODEL), DEEPNORM_BETA * FFN_HIDDEN ** -0.5),
        'ln2_g': gain(25, (L, D_MODEL)),
        'ln2_b': nrm(26, (L, D_MODEL), 0.01),
    }


def reference(x, c, ctx, c_ctx, w_mod, b_mod, w_in, b_gate, w_uq, g_q_mla, w_ukv, g_kv_mla,
              conv_w, conv_b, a_log, dt_bias, d_skip, g_ssd, g_q_gqa, g_k_gqa, w_out,
              ln1_g, ln1_b, w_ffn_in, w_ffn_out, ln2_g, ln2_b):
    n_lat = x.shape[1]
    rope_mla = axial_rope_tables(n_lat, MLA_ROPE)
    rope_gqa = axial_rope_tables(n_lat, GQA_HEAD_DIM)
    silu_c = jax.nn.silu(c)[:, None, :]
    silu_cc = jax.nn.silu(c_ctx)
    xc = ctx
    for i in range(DEPTH):
        with_ctx = i < DEPTH - 1
        sh1, sc1, g1, sh2, sc2, g2 = jnp.split(silu_c @ w_mod[i] + b_mod[i], 6, axis=-1)
        csh1, csc1, cg1, csh2, csc2, cg2 = jnp.split(silu_cc @ w_mod[i] + b_mod[i], 6, axis=-1)
        hx = layer_norm(x) * (1 + sc1) + sh1
        hc = layer_norm(xc) * (1 + csc1) + csh1
        mx, mc = hybrid_mixer(hx, hc, rope_mla, rope_gqa, w_in[i], b_gate[i], w_uq[i], g_q_mla[i],
                              w_ukv[i], g_kv_mla[i], conv_w[i], conv_b[i], a_log[i], dt_bias[i],
                              d_skip[i], g_ssd[i], g_q_gqa[i], g_k_gqa[i], w_out[i], with_ctx)
        x = layer_norm_affine(DEEPNORM_ALPHA * x + g1 * mx, ln1_g[i], ln1_b[i])
        fx = swiglu(layer_norm(x) * (1 + sc2) + sh2, w_ffn_in[i], w_ffn_out[i])
        x = layer_norm_affine(DEEPNORM_ALPHA * x + g2 * fx, ln2_g[i], ln2_b[i])
        if with_ctx:
            xc = layer_norm_affine(DEEPNORM_ALPHA * xc + cg1 * mc, ln1_g[i], ln1_b[i])
            fc = swiglu(layer_norm(xc) * (1 + csc2) + csh2, w_ffn_in[i], w_ffn_out[i])
            xc = layer_norm_affine(DEEPNORM_ALPHA * xc + cg2 * fc, ln2_g[i], ln2_b[i])
    return x
```

```python
import functools
import math

import jax
import jax.numpy as jnp
import numpy as np
from jax import lax
from jax.experimental import pallas as pl
from jax.experimental.pallas import tpu as pltpu

F32 = jnp.float32
BF16 = jnp.bfloat16

D_MODEL = 1024
DEPTH = 2
GRID_W = 64
ROPE_THETA = 10000.0
NORM_EPS = 1e-6

MLA_HEADS = 8
MLA_Q_RANK = 384
MLA_KV_RANK = 256
MLA_NOPE = 64
MLA_ROPE = 32
MLA_V = 64
MLA_QK = MLA_NOPE + MLA_ROPE

SSD_HEADS = 16
SSD_HEAD_DIM = 64
SSD_INNER = SSD_HEADS * SSD_HEAD_DIM
SSD_GROUPS = 2
SSD_STATE = 128
SSD_CHUNK = 128
SSD_CONV_CH = SSD_INNER + 2 * SSD_GROUPS * SSD_STATE

GQA_HEADS = 8
GQA_KV_HEADS = 2
GQA_HEAD_DIM = 64

MLA_IN = MLA_Q_RANK + MLA_KV_RANK + MLA_ROPE
SSD_IN = SSD_INNER + SSD_CONV_CH + 2 * SSD_HEADS
GQA_IN = (GQA_HEADS + 2 * GQA_KV_HEADS) * GQA_HEAD_DIM
GATE_IN = 3 * D_MODEL
MLA_OUT = MLA_HEADS * MLA_V
GQA_OUT = GQA_HEADS * GQA_HEAD_DIM
FFN_HIDDEN = -(-8 * D_MODEL // (3 * 256)) * 256
DEEPNORM_ALPHA = (2 * DEPTH) ** 0.25

LANE = 128
MOD_ROWS = 8
LOG2E = 1.4426950408889634

C_PM = 0
W_PM = MLA_Q_RANK + MLA_KV_RANK + LANE
C_Z = C_PM + W_PM
C_XBC = C_Z + SSD_INNER
C_DT = C_XBC + SSD_CONV_CH
C_PG = C_DT + LANE
C_GATE = C_PG + GQA_IN
W_INP = C_GATE + GATE_IN

VMEM_LIMIT = 56 * 1024 * 1024


def _cparams(sem):
    return pltpu.CompilerParams(dimension_semantics=sem, vmem_limit_bytes=VMEM_LIMIT)


def _const_spec(shape):
    nd = len(shape)
    return pl.BlockSpec(shape, lambda *_: (0,) * nd, pipeline_mode=pl.Buffered(1))


def _layer_norm(x):
    mu = jnp.mean(x, axis=-1, keepdims=True)
    xc = x - mu
    var = jnp.mean(xc * xc, axis=-1, keepdims=True)
    return xc * lax.rsqrt(var + NORM_EPS)


def _split_dot(x, m01, parts):
    acc = None
    rem = x
    for _ in range(parts):
        piece = rem.astype(BF16)
        term = jnp.dot(piece, m01, preferred_element_type=F32)
        acc = term if acc is None else acc + term
        rem = rem - piece.astype(F32)
    return acc


def _mod_rows(mod_ref, col, width, is_ctx):
    b = pl.program_id(0)
    lat = mod_ref[pl.ds(b, 1), col:col + width]
    ctx = mod_ref[MOD_ROWS // 2:MOD_ROWS // 2 + 1, col:col + width]
    return jnp.where(is_ctx, ctx, lat)


def _is_ctx_rows(tm, ctx_len, tile0=0):
    t = pl.program_id(1) + tile0
    rows = t * tm + lax.broadcasted_iota(jnp.int32, (tm, 1), 0)
    return rows < ctx_len


def _mod_kernel(c_ref, w_ref, b_ref, o_ref):
    c = c_ref[...]
    s = (c * jax.nn.sigmoid(c)).astype(BF16)
    o_ref[0] = jnp.dot(s, w_ref[0].astype(BF16), preferred_element_type=F32) + b_ref[0]


def _modulation(cvec, w_mod, b_mod):
    depth, d, n = w_mod.shape
    tn = 1024
    return pl.pallas_call(
        _mod_kernel,
        grid=(depth, n // tn),
        in_specs=[
            pl.BlockSpec((MOD_ROWS, d), lambda l, j: (0, 0)),
            pl.BlockSpec((1, d, tn), lambda l, j: (l, 0, j)),
            pl.BlockSpec((1, 1, tn), lambda l, j: (l, 0, j)),
        ],
        out_specs=pl.BlockSpec((1, MOD_ROWS, tn), lambda l, j: (l, 0, j)),
        out_shape=jax.ShapeDtypeStruct((depth, MOD_ROWS, n), F32),
        compiler_params=_cparams(("arbitrary", "arbitrary")),
        name="modulation",
    )(cvec, w_mod, b_mod.reshape(depth, 1, n))


def _inproj_kernel(x_ref, mod_ref, w_ref, bg_ref, pm_ref, z_ref, xbc_ref, dt_ref, pg_ref, gate_ref,
                   *, tm, ctx_len):
    is_ctx = _is_ctx_rows(tm, ctx_len)
    d = D_MODEL
    sh = _mod_rows(mod_ref, 0, d, is_ctx)
    sc = _mod_rows(mod_ref, d, d, is_ctx)
    h = (_layer_norm(x_ref[0]) * (1.0 + sc) + sh).astype(BF16)

    def proj(c0, width):
        return jnp.dot(h, w_ref[:, c0:c0 + width], preferred_element_type=F32)

    pm_ref[0] = proj(C_PM, W_PM)
    z_ref[0] = proj(C_Z, SSD_INNER)
    xbc_ref[0] = proj(C_XBC, SSD_CONV_CH)
    dt_ref[0] = proj(C_DT, LANE)
    pg_ref[0] = proj(C_PG, GQA_IN)
    for j in range(3):
        c0 = j * D_MODEL
        logits = proj(C_GATE + c0, D_MODEL) + bg_ref[:, c0:c0 + D_MODEL]
        gate_ref[0, :, c0:c0 + D_MODEL] = jax.nn.sigmoid(logits).astype(gate_ref.dtype)


def _in_projection(xa, mod, w_inp, b_gate, *, ctx_len, tm):
    bsz, t_all, d = xa.shape
    row = lambda w: pl.BlockSpec((1, tm, w), lambda b, t: (b, t, 0))
    out_w = (W_PM, SSD_INNER, SSD_CONV_CH, LANE, GQA_IN, GATE_IN)
    out_dt = (F32, F32, F32, F32, F32, BF16)
    return pl.pallas_call(
        functools.partial(_inproj_kernel, tm=tm, ctx_len=ctx_len),
        grid=(bsz, t_all // tm),
        in_specs=[row(d), _const_spec(mod.shape), _const_spec(w_inp.shape), _const_spec(b_gate.shape)],
        out_specs=[row(w) for w in out_w],
        out_shape=[jax.ShapeDtypeStruct((bsz, t_all, w), dt) for w, dt in zip(out_w, out_dt)],
        compiler_params=_cparams(("parallel", "parallel")),
        name="in_projection",
    )(xa, mod, w_inp, b_gate)


def _rope_lanes(x, c, s_fwd, s_bwd, half):
    n = x.shape[-1]
    return x * c + pltpu.roll(x, n - half, axis=1) * s_fwd + pltpu.roll(x, half, axis=1) * s_bwd


def _mla_prep_kernel(pm_ref, gq_ref, gkv_ref, wuq_ref, wukv_ref, tq_ref, tk_ref, q_ref, k_ref, v_ref):
    pm = pm_ref[0]
    pq = pm[:, :MLA_Q_RANK]
    ckv = pm[:, MLA_Q_RANK:MLA_Q_RANK + MLA_KV_RANK]
    kr = pm[:, MLA_Q_RANK + MLA_KV_RANK:]

    def rms(x, g):
        return x * lax.rsqrt(jnp.mean(x * x, axis=-1, keepdims=True) + NORM_EPS) * g

    q = jnp.dot(rms(pq, gq_ref[...]).astype(BF16), wuq_ref[...], preferred_element_type=F32)
    kv = jnp.dot(rms(ckv, gkv_ref[...]).astype(BF16), wukv_ref[...], preferred_element_type=F32)
    half = MLA_ROPE // 2
    kr = _rope_lanes(kr, tk_ref[0], tk_ref[1], tk_ref[2], half)
    cq, sq1, sq2 = tq_ref[0], tq_ref[1], tq_ref[2]
    for hd in range(MLA_HEADS):
        sl = slice(hd * LANE, (hd + 1) * LANE)
        q_ref[0, :, sl] = _rope_lanes(q[:, sl], cq, sq1, sq2, half).astype(q_ref.dtype)
        k_ref[0, :, sl] = (kv[:, sl] + kr).astype(k_ref.dtype)
    v_ref[0] = kv[:, MLA_HEADS * LANE:].astype(v_ref.dtype)


def _mla_prep(pm, g_q, g_kv, wuq_p, wukv_p, tab_q, tab_k, *, tm):
    bsz, t_all, _ = pm.shape
    row = lambda w: pl.BlockSpec((1, tm, w), lambda b, t: (b, t, 0))
    tab = pl.BlockSpec((3, tm, LANE), lambda b, t: (0, t, 0))
    hw = MLA_HEADS * LANE
    return pl.pallas_call(
        _mla_prep_kernel,
        grid=(bsz, t_all // tm),
        in_specs=[row(W_PM), _const_spec(g_q.shape), _const_spec(g_kv.shape),
                  _const_spec(wuq_p.shape), _const_spec(wukv_p.shape), tab, tab],
        out_specs=[row(hw), row(hw), row(MLA_OUT)],
        out_shape=[jax.ShapeDtypeStruct((bsz, t_all, hw), BF16),
                   jax.ShapeDtypeStruct((bsz, t_all, hw), BF16),
                   jax.ShapeDtypeStruct((bsz, t_all, MLA_OUT), BF16)],
        compiler_params=_cparams(("parallel", "parallel")),
        name="mla_prep",
    )(pm, g_q, g_kv, wuq_p, wukv_p, tab_q, tab_k)


def _gqa_prep_kernel(pg_ref, gq_ref, gk_ref, bd_ref, tq_ref, tk_ref, q_ref, k_ref, v_ref):
    pg = pg_ref[0]
    hd = GQA_HEAD_DIM
    q = pg[:, :GQA_OUT]
    k = pg[:, GQA_OUT:GQA_OUT + LANE]
    v = pg[:, GQA_OUT + LANE:]
    half = hd // 2

    def head_rms(x, g):
        w = x.shape[-1]
        ms = _split_dot(x * x, bd_ref[:w, :w], 2) * (1.0 / hd)
        return x * lax.rsqrt(ms + NORM_EPS) * g

    qn = head_rms(q, gq_ref[...])
    for j in range(GQA_OUT // LANE):
        sl = slice(j * LANE, (j + 1) * LANE)
        q_ref[0, :, sl] = _rope_lanes(qn[:, sl], tq_ref[0], tq_ref[1], tq_ref[2], half).astype(q_ref.dtype)
    kn = head_rms(k, gk_ref[...])
    kn = _rope_lanes(kn, tk_ref[0], tk_ref[1], tk_ref[2], half)
    lane = lax.broadcasted_iota(jnp.int32, kn.shape, 1)
    low = lane < hd
    kn_sw = pltpu.roll(kn, hd, axis=1)
    zero = jnp.zeros_like(kn)
    k_ref[0, :, 0 * LANE:1 * LANE] = jnp.where(low, kn, zero).astype(k_ref.dtype)
    k_ref[0, :, 1 * LANE:2 * LANE] = jnp.where(low, zero, kn_sw).astype(k_ref.dtype)
    k_ref[0, :, 2 * LANE:3 * LANE] = jnp.where(low, kn_sw, zero).astype(k_ref.dtype)
    k_ref[0, :, 3 * LANE:4 * LANE] = jnp.where(low, zero, kn).astype(k_ref.dtype)
    v_sw = pltpu.roll(v, hd, axis=1)
    v_ref[0, :, 0 * LANE:1 * LANE] = jnp.where(low, v, v_sw).astype(v_ref.dtype)
    v_ref[0, :, 1 * LANE:2 * LANE] = jnp.where(low, v_sw, v).astype(v_ref.dtype)


def _gqa_prep(pg, g_q, g_k, bd, tab_q, tab_k, *, tm):
    bsz, t_all, _ = pg.shape
    row = lambda w: pl.BlockSpec((1, tm, w), lambda b, t: (b, t, 0))
    tab = pl.BlockSpec((3, tm, LANE), lambda b, t: (0, t, 0))
    kw = GQA_KV_HEADS * 2 * LANE
    vw = GQA_KV_HEADS * LANE
    return pl.pallas_call(
        _gqa_prep_kernel,
        grid=(bsz, t_all // tm),
        in_specs=[row(GQA_IN), _const_spec(g_q.shape), _const_spec(g_k.shape), _const_spec(bd.shape), tab, tab],
        out_specs=[row(GQA_OUT), row(kw), row(vw)],
        out_shape=[jax.ShapeDtypeStruct((bsz, t_all, GQA_OUT), BF16),
                   jax.ShapeDtypeStruct((bsz, t_all, kw), BF16),
                   jax.ShapeDtypeStruct((bsz, t_all, vw), BF16)],
        compiler_params=_cparams(("parallel", "parallel")),
        name="gqa_prep",
    )(pg, g_q, g_k, bd, tab_q, tab_k)


def _attn_kernel(q_ref, k_ref, v_ref, o_ref, *, units, tq, tk, ctx_len, q_tile0):
    qt = pl.program_id(2) + q_tile0
    t_all = k_ref.shape[1]
    n_chunks = jnp.where(qt * tq < ctx_len, ctx_len // tk, t_all // tk)
    half = LANE // 2
    outs = []
    for (qb, kb) in units:
        q = q_ref[0, :, qb * LANE:(qb + 1) * LANE]

        def body(j, carry, q=q, kb=kb):
            m, l, acc = carry
            r0 = pl.multiple_of(j * tk, tk)
            kc = k_ref[0, pl.ds(r0, tk), kb * LANE:(kb + 1) * LANE]
            vc = v_ref[0, pl.ds(r0, tk), :]
            s = lax.dot_general(q, kc, (((1,), (1,)), ((), ())), preferred_element_type=F32)
            m_new = jnp.maximum(m, jnp.max(s, axis=-1, keepdims=True))
            alpha = jnp.exp2(m - m_new)
            p = jnp.exp2(s - m_new)
            l_new = alpha * l + jnp.sum(p, axis=-1, keepdims=True)
            acc_new = alpha * acc + jnp.dot(p.astype(BF16), vc, preferred_element_type=F32)
            return m_new, l_new, acc_new

        init = (jnp.full((tq, 1), -jnp.inf, F32), jnp.zeros((tq, 1), F32), jnp.zeros((tq, LANE), F32))
        _, l, acc = lax.fori_loop(0, n_chunks, body, init)
        outs.append(acc / l)
    lane = lax.broadcasted_iota(jnp.int32, (tq, LANE), 1)
    for j in range(len(units) // 2):
        o_ref[0, :, j * LANE:(j + 1) * LANE] = jnp.where(lane < half, outs[2 * j], outs[2 * j + 1]).astype(o_ref.dtype)


def _attention(q, k, v, *, groups, q_w, k_w, units, ctx_len, with_ctx, tq, tk):
    bsz, t_all, _ = q.shape
    out_w = (len(units) // 2) * LANE
    q_tile0 = 0 if with_ctx else ctx_len // tq
    nq = t_all // tq - q_tile0
    return pl.pallas_call(
        functools.partial(_attn_kernel, units=units, tq=tq, tk=tk, ctx_len=ctx_len, q_tile0=q_tile0),
        grid=(bsz, groups, nq),
        in_specs=[
            pl.BlockSpec((1, tq, q_w), lambda b, g, t: (b, t + q_tile0, g)),
            pl.BlockSpec((1, t_all, k_w), lambda b, g, t: (b, 0, g)),
            pl.BlockSpec((1, t_all, LANE), lambda b, g, t: (b, 0, g)),
        ],
        out_specs=pl.BlockSpec((1, tq, out_w), lambda b, g, t: (b, t + q_tile0, g)),
        out_shape=jax.ShapeDtypeStruct((bsz, t_all, groups * out_w), BF16),
        compiler_params=_cparams(("parallel", "parallel", "arbitrary")),
        name="attention",
    )(q, k, v)


def _ssd_chunk_index(i, n_chunks, ctx_chunks, reverse):
    if not reverse:
        return i
    return jnp.where(i < ctx_chunks, ctx_chunks - 1 - i, n_chunks + ctx_chunks - 1 - i)


def _ssd_kernel(*refs, reverse, n_chunks, ctx_chunks):
    if reverse:
        (xbc_ref, prev_ref, next_ref, dt_ref, cw_ref, cb_ref, dtb_ref, a_ref, tri_ref, ehd_ref,
         yf_ref, z_ref, dsk_ref, g_ref, y_ref, h_ref) = refs
    else:
        (xbc_ref, prev_ref, next_ref, dt_ref, cw_ref, cb_ref, dtb_ref, a_ref, tri_ref, ehd_ref,
         y_ref, h_ref) = refs
    q = SSD_CHUNK
    i = pl.program_id(1)
    ci = _ssd_chunk_index(i, n_chunks, ctx_chunks, reverse)

    @pl.when(i == 0)
    def _():
        h_ref[...] = jnp.zeros_like(h_ref)

    x = xbc_ref[0]
    at_start = jnp.logical_or(ci == 0, ci == ctx_chunks)
    at_end = jnp.logical_or(ci == ctx_chunks - 1, ci == n_chunks - 1)
    prev_row = jnp.where(at_start, 0.0, prev_ref[0, 7:8, :])
    next_row = jnp.where(at_end, 0.0, next_ref[0, 0:1, :])
    row = lax.broadcasted_iota(jnp.int32, (q, 1), 0)
    x_prev = jnp.where(row == 0, prev_row, pltpu.roll(x, 1, axis=0))
    x_next = jnp.where(row == q - 1, next_row, pltpu.roll(x, q - 1, axis=0))
    u = cw_ref[0:1, :] * x_prev + cw_ref[1:2, :] * x + cw_ref[2:3, :] * x_next + cb_ref[...]
    u = u * jax.nn.sigmoid(u)
    xs = u[:, :SSD_INNER]
    gs = SSD_STATE
    bm = [u[:, SSD_INNER + g * gs:SSD_INNER + (g + 1) * gs] for g in range(SSD_GROUPS)]
    cm = [u[:, SSD_INNER + (SSD_GROUPS + g) * gs:SSD_INNER + (SSD_GROUPS + g + 1) * gs]
          for g in range(SSD_GROUPS)]

    dt = jax.nn.softplus(dt_ref[0] + dtb_ref[...])
    da = dt * a_ref[...]
    a_cum = _tri_cumsum(tri_ref, da)
    end = 0 if reverse else q - 1
    a_tot = a_cum[end:end + 1, :]
    e_cum = jnp.exp(a_cum)
    dte = jnp.exp(a_tot - a_cum)
    dt_x = _split_dot(dt, ehd_ref[...], 2)
    ec_x = _split_dot(e_cum, ehd_ref[...], 2)
    dte_x = _split_dot(dte, ehd_ref[...], 2)
    xdt = xs * dt_x
    xdt_b = xdt.astype(BF16)
    xst_b = (xdt * dte_x).astype(BF16)
    a_cum_t = a_cum.T

    kk = lax.broadcasted_iota(jnp.int32, (q, q), 1)
    qq = lax.broadcasted_iota(jnp.int32, (q, q), 0)
    in_order = (kk >= qq) if reverse else (kk <= qq)
    lane = lax.broadcasted_iota(jnp.int32, (q, LANE), 1)
    off = SSD_HEADS if reverse else 0
    hpg = SSD_HEADS // SSD_GROUPS
    gw = hpg * SSD_HEAD_DIM
    y_parts = []
    for g in range(SSD_GROUPS):
        bm_b = bm[g].astype(BF16)
        cm_b = cm[g].astype(BF16)
        scores = lax.dot_general(cm_b, bm_b, (((1,), (1,)), ((), ())), preferred_element_type=F32)
        h_prev = h_ref[g]
        y_off = jnp.dot(cm_b, h_prev.astype(BF16), preferred_element_type=F32) * ec_x[:, g * gw:(g + 1) * gw]
        pair_out = []
        for j in range(hpg // 2):
            xp = xdt_b[:, g * gw + j * LANE:g * gw + (j + 1) * LANE]
            res = []
            for e in range(2):
                hd = g * hpg + 2 * j + e
                a_q = a_cum[:, off + hd:off + hd + 1]
                a_k = a_cum_t[off + hd:off + hd + 1, :]
                decay = jnp.exp(jnp.where(in_order, a_q - a_k, -jnp.inf))
                pmat = (scores * decay).astype(BF16)
                res.append(jnp.dot(pmat, xp, preferred_element_type=F32))
            pair_out.append(jnp.where(lane < SSD_HEAD_DIM, res[0], res[1]))
        y_parts.append(jnp.concatenate(pair_out, axis=1) + y_off)
        st = jnp.dot(bm[g].T.astype(BF16), xst_b[:, g * gw:(g + 1) * gw], preferred_element_type=F32)
        h_ref[g] = h_prev * ec_x[end:end + 1, g * gw:(g + 1) * gw] + st
    y = jnp.concatenate(y_parts, axis=1)

    if reverse:
        yt = yf_ref[0] + y + dsk_ref[...] * xs
        zz = z_ref[0]
        yt = yt * (zz * jax.nn.sigmoid(zz))
        yt = yt * lax.rsqrt(jnp.mean(yt * yt, axis=-1, keepdims=True) + NORM_EPS) * g_ref[...]
        y_ref[0] = yt.astype(y_ref.dtype)
    else:
        y_ref[0] = y


def _tri_cumsum(tri_ref, da):
    acc = None
    rem = da
    for _ in range(3):
        piece = rem.astype(BF16)
        term = jnp.dot(tri_ref[...], piece, preferred_element_type=F32)
        acc = term if acc is None else acc + term
        rem = rem - piece.astype(F32)
    return acc


def _ssd_direction(xbc, dt, conv_w, conv_b, dt_bias, a_row, tri, ehd, extra, *, ctx_len, reverse):
    bsz, t_all, cch = xbc.shape
    q = SSD_CHUNK
    n_chunks = t_all // q
    ctx_chunks = ctx_len // q
    halo = 8
    hb = q // halo
    n_halo = t_all // halo
    cidx = lambda i: _ssd_chunk_index(i, n_chunks, ctx_chunks, reverse)
    chunk = lambda w: pl.BlockSpec((1, q, w), lambda b, i: (b, cidx(i), 0))
    in_specs = [
        chunk(cch),
        pl.BlockSpec((1, halo, cch), lambda b, i: (b, jnp.maximum(cidx(i) * hb - 1, 0), 0)),
        pl.BlockSpec((1, halo, cch), lambda b, i: (b, jnp.minimum((cidx(i) + 1) * hb, n_halo - 1), 0)),
        chunk(LANE),
        _const_spec(conv_w.shape), _const_spec(conv_b.shape), _const_spec(dt_bias.shape),
        _const_spec(a_row.shape), _const_spec(tri.shape), _const_spec(ehd.shape),
    ]
    args = [xbc, xbc, xbc, dt, conv_w, conv_b, dt_bias, a_row, tri, ehd]
    if reverse:
        y_f, z, d_x, g_ssd = extra
        in_specs += [chunk(SSD_INNER), chunk(SSD_INNER), _const_spec(d_x.shape), _const_spec(g_ssd.shape)]
        args += [y_f, z, d_x, g_ssd]
    out_dtype = BF16 if reverse else F32
    hpg = SSD_HEADS // SSD_GROUPS
    return pl.pallas_call(
        functools.partial(_ssd_kernel, reverse=reverse, n_chunks=n_chunks, ctx_chunks=ctx_chunks),
        grid=(bsz, n_chunks),
        in_specs=in_specs,
        out_specs=chunk(SSD_INNER),
        out_shape=jax.ShapeDtypeStruct((bsz, t_all, SSD_INNER), out_dtype),
        scratch_shapes=[pltpu.VMEM((SSD_GROUPS, SSD_STATE, hpg * SSD_HEAD_DIM), F32)],
        compiler_params=_cparams(("parallel", "arbitrary")),
        name="ssd_bwd" if reverse else "ssd_fwd",
    )(*args)


def _post_kernel(x_ref, ym_ref, ys_ref, yg_ref, gate_ref, mod_ref, wo_ref, wi_ref, wf_ref,
                 l1g_ref, l1b_ref, l2g_ref, l2b_ref, o_ref, act_ref, *, tm, ctx_len, th, tile0):
    is_ctx = _is_ctx_rows(tm, ctx_len, tile0)
    d = D_MODEL
    mix = None
    row0 = 0
    for j, y_ref in enumerate((ym_ref, ys_ref, yg_ref)):
        kw = y_ref.shape[-1]
        br = jnp.dot(y_ref[0], wo_ref[row0:row0 + kw, :], preferred_element_type=F32)
        br = gate_ref[0, :, j * d:(j + 1) * d].astype(F32) * br
        mix = br if mix is None else mix + br
        row0 += kw
    g1 = _mod_rows(mod_ref, 2 * d, d, is_ctx)
    x1 = _layer_norm(DEEPNORM_ALPHA * x_ref[0] + g1 * mix) * l1g_ref[...] + l1b_ref[...]

    sh2 = _mod_rows(mod_ref, 3 * d, d, is_ctx)
    sc2 = _mod_rows(mod_ref, 4 * d, d, is_ctx)
    g2 = _mod_rows(mod_ref, 5 * d, d, is_ctx)
    h2 = (_layer_norm(x1) * (1.0 + sc2) + sh2).astype(BF16)
    hid = FFN_HIDDEN
    for c0 in range(0, hid, th):
        gt = jnp.dot(h2, wi_ref[:, c0:c0 + th], preferred_element_type=F32)
        up = jnp.dot(h2, wi_ref[:, hid + c0:hid + c0 + th], preferred_element_type=F32)
        act_ref[:, c0:c0 + th] = (gt * jax.nn.sigmoid(gt) * up).astype(BF16)
    fx = jnp.dot(act_ref[...], wf_ref[...], preferred_element_type=F32)
    o_ref[0] = _layer_norm(DEEPNORM_ALPHA * x1 + g2 * fx) * l2g_ref[...] + l2b_ref[...]


def _post_mixer(xa, y_mla, y_ssd, y_gqa, gate, mod, w_out, w_ffn_in, w_ffn_out, l1g, l1b, l2g, l2b,
                *, ctx_len, tm, with_ctx):
    bsz, t_all, d = xa.shape
    tile0 = 0 if with_ctx else ctx_len // tm
    row = lambda w: pl.BlockSpec((1, tm, w), lambda b, t: (b, t + tile0, 0))
    vec = _const_spec((1, d))
    t_out = t_all - tile0 * tm
    return pl.pallas_call(
        functools.partial(_post_kernel, tm=tm, ctx_len=ctx_len, th=256, tile0=tile0),
        grid=(bsz, t_out // tm),
        in_specs=[row(d), row(MLA_OUT), row(SSD_INNER), row(GQA_OUT), row(GATE_IN), _const_spec(mod.shape),
                  _const_spec(w_out.shape), _const_spec(w_ffn_in.shape), _const_spec(w_ffn_out.shape),
                  vec, vec, vec, vec],
        out_specs=pl.BlockSpec((1, tm, d), lambda b, t: (b, t, 0)),
        out_shape=jax.ShapeDtypeStruct((bsz, t_out, d), F32),
        scratch_shapes=[pltpu.VMEM((tm, FFN_HIDDEN), BF16)],
        compiler_params=_cparams(("parallel", "parallel")),
        name="post_mixer",
    )(xa, y_mla, y_ssd, y_gqa, gate, mod, w_out, w_ffn_in, w_ffn_out, l1g, l1b, l2g, l2b)


def _rope_tables(n_lat, ctx_len, rot_dim, lane_dim, lane_off, scale):
    rows = n_lat // GRID_W
    row = jnp.repeat(jnp.arange(rows), GRID_W)
    col = jnp.tile(jnp.arange(GRID_W), rows)
    n_freq = rot_dim // 4
    inv_freq = ROPE_THETA ** (-jnp.arange(n_freq, dtype=F32) / n_freq)
    ang = jnp.concatenate([row[:, None] * inv_freq, col[:, None] * inv_freq], axis=-1)
    cos, sin = jnp.cos(ang), jnp.sin(ang)
    half = rot_dim // 2
    lane = np.arange(LANE)
    rel = (lane % lane_dim) - lane_off
    is_rot = (rel >= 0) & (rel < rot_dim)
    idx = np.where(is_rot, rel % half, 0)
    first = is_rot & (rel < half)
    second = is_rot & (rel >= half)
    c = jnp.where(is_rot[None, :], cos[:, idx], 1.0)
    s1 = jnp.where(first[None, :], -sin[:, idx], 0.0)
    s2 = jnp.where(second[None, :], sin[:, idx], 0.0)
    lat = jnp.stack([c, s1, s2])
    ctx = jnp.stack([jnp.ones((ctx_len, LANE), F32), jnp.zeros((ctx_len, LANE), F32),
                     jnp.zeros((ctx_len, LANE), F32)])
    return (jnp.concatenate([ctx, lat], axis=1) * scale).astype(F32)


def _pad_cols(w, width, offset=0):
    out = jnp.zeros((w.shape[0], width), w.dtype)
    return out.at[:, offset:offset + w.shape[1]].set(w)


def _layout_w_in(w):
    c = 0
    q_lat = w[:, c:c + MLA_Q_RANK]; c += MLA_Q_RANK
    kv_lat = w[:, c:c + MLA_KV_RANK]; c += MLA_KV_RANK
    k_rope = _pad_cols(w[:, c:c + MLA_ROPE], LANE, MLA_NOPE); c += MLA_ROPE
    z = w[:, c:c + SSD_INNER]; c += SSD_INNER
    xbc = w[:, c:c + SSD_CONV_CH]; c += SSD_CONV_CH
    dt = _pad_cols(w[:, c:c + 2 * SSD_HEADS], LANE); c += 2 * SSD_HEADS
    rest = w[:, c:]
    return jnp.concatenate([q_lat, kv_lat, k_rope, z, xbc, dt, rest], axis=1).astype(BF16)


def _layout_w_uq(w):
    w = w.reshape(w.shape[0], MLA_HEADS, MLA_QK)
    w = jnp.pad(w, ((0, 0), (0, 0), (0, LANE - MLA_QK)))
    return w.reshape(w.shape[0], MLA_HEADS * LANE).astype(BF16)


def _layout_w_ukv(w):
    w = w.reshape(w.shape[0], MLA_HEADS, MLA_NOPE + MLA_V)
    k = jnp.pad(w[:, :, :MLA_NOPE], ((0, 0), (0, 0), (0, LANE - MLA_NOPE)))
    v = w[:, :, MLA_NOPE:]
    return jnp.concatenate([k.reshape(w.shape[0], -1), v.reshape(w.shape[0], -1)], axis=1).astype(BF16)


def _lane_row(vec, width=LANE):
    return _pad_cols(vec.reshape(1, -1).astype(F32), width)


def kernel(x, c, ctx, c_ctx, w_mod, b_mod, w_in, b_gate, w_uq, g_q_mla, w_ukv, g_kv_mla, conv_w, conv_b,
           a_log, dt_bias, d_skip, g_ssd, g_q_gqa, g_k_gqa, w_out, ln1_g, ln1_b, w_ffn_in, w_ffn_out,
           ln2_g, ln2_b):
    bsz, n_lat, d = x.shape
    ctx_len = ctx.shape[1]
    depth = w_mod.shape[0]
    assert d == D_MODEL and bsz <= MOD_ROWS // 2
    assert ctx_len % 256 == 0 and n_lat % 256 == 0
    tm = 256

    cvec = jnp.zeros((MOD_ROWS, d), F32).at[:bsz].set(c).at[MOD_ROWS // 2].set(c_ctx)
    mod_all = _modulation(cvec, w_mod, b_mod)

    mla_scale = MLA_QK ** -0.5 * LOG2E
    gqa_scale = GQA_HEAD_DIM ** -0.5 * LOG2E
    tab_mq = _rope_tables(n_lat, ctx_len, MLA_ROPE, LANE, MLA_NOPE, mla_scale)
    tab_mk = _rope_tables(n_lat, ctx_len, MLA_ROPE, LANE, MLA_NOPE, 1.0)
    tab_gq = _rope_tables(n_lat, ctx_len, GQA_HEAD_DIM, GQA_HEAD_DIM, 0, gqa_scale)
    tab_gk = _rope_tables(n_lat, ctx_len, GQA_HEAD_DIM, GQA_HEAD_DIM, 0, 1.0)

    qn = SSD_CHUNK
    r = np.arange(qn)
    tri_f = jnp.asarray((r[None, :] <= r[:, None]).astype(np.float32), BF16)
    tri_b = jnp.asarray((r[None, :] >= r[:, None]).astype(np.float32), BF16)
    lane1k = np.arange(SSD_INNER) // SSD_HEAD_DIM
    ehd_f = jnp.asarray((np.arange(LANE)[:, None] == lane1k[None, :]).astype(np.float32), BF16)
    ehd_b = jnp.asarray((np.arange(LANE)[:, None] == (lane1k[None, :] + SSD_HEADS)).astype(np.float32), BF16)
    blk = np.arange(GQA_OUT) // GQA_HEAD_DIM
    bd = jnp.asarray((blk[:, None] == blk[None, :]).astype(np.float32), BF16)

    xa = jnp.concatenate([ctx, x], axis=1)
    for i in range(depth):
        with_ctx = i < depth - 1
        mod = mod_all[i]
        w_inp = _layout_w_in(w_in[i])
        pm, z, xbc, dtr, pg, gate = _in_projection(xa, mod, w_inp, b_gate[i].reshape(1, -1),
                                                   ctx_len=ctx_len, tm=tm)
        q_m, k_m, v_m = _mla_prep(pm, g_q_mla[i].reshape(1, -1), g_kv_mla[i].reshape(1, -1),
                                  _layout_w_uq(w_uq[i]), _layout_w_ukv(w_ukv[i]), tab_mq, tab_mk, tm=tm)
        y_mla = _attention(q_m, k_m, v_m, groups=MLA_HEADS // 2, q_w=2 * LANE, k_w=2 * LANE,
                           units=((0, 0), (1, 1)), ctx_len=ctx_len, with_ctx=with_ctx, tq=256, tk=256)
        q_g, k_g, v_g = _gqa_prep(pg, jnp.tile(g_q_gqa[i], GQA_HEADS).reshape(1, -1),
                                  jnp.tile(g_k_gqa[i], GQA_KV_HEADS).reshape(1, -1), bd, tab_gq, tab_gk, tm=tm)
        y_gqa = _attention(q_g, k_g, v_g, groups=GQA_KV_HEADS, q_w=2 * LANE, k_w=2 * LANE,
                           units=((0, 0), (0, 1), (1, 0), (1, 1)), ctx_len=ctx_len, with_ctx=with_ctx,
                           tq=256, tk=256)
        a = -jnp.exp(a_log[i].astype(F32))
        a_row = _lane_row(a.reshape(-1))
        dtb_row = _lane_row(dt_bias[i].reshape(-1))
        cb = conv_b[i].reshape(1, -1)
        y_f = _ssd_direction(xbc, dtr, conv_w[i], cb, dtb_row, a_row, tri_f, ehd_f, None,
                             ctx_len=ctx_len, reverse=False)
        d_x = jnp.repeat(d_skip[i], SSD_HEAD_DIM).reshape(1, -1)
        y_ssd = _ssd_direction(xbc, dtr, conv_w[i], cb, dtb_row, a_row, tri_b, ehd_b,
                               (y_f, z, d_x, g_ssd[i].reshape(1, -1)), ctx_len=ctx_len, reverse=True)
        xa = _post_mixer(xa, y_mla, y_ssd, y_gqa, gate, mod, w_out[i].astype(BF16),
                         w_ffn_in[i].astype(BF16), w_ffn_out[i].astype(BF16),
                         ln1_g[i].reshape(1, -1), ln1_b[i].reshape(1, -1),
                         ln2_g[i].reshape(1, -1), ln2_b[i].reshape(1, -1), ctx_len=ctx_len, tm=tm,
                         with_ctx=with_ctx)
    return xa
```

```python
import functools
import math

import jax
import jax.numpy as jnp
import numpy as np
from jax import lax
from jax.experimental import pallas as pl
from jax.experimental.pallas import tpu as pltpu

F32 = jnp.float32
BF16 = jnp.bfloat16

D_MODEL = 1024
DEPTH = 2
GRID_W = 64
ROPE_THETA = 10000.0
NORM_EPS = 1e-6

MLA_HEADS = 8
MLA_Q_RANK = 384
MLA_KV_RANK = 256
MLA_NOPE = 64
MLA_ROPE = 32
MLA_V = 64
MLA_QK = MLA_NOPE + MLA_ROPE

SSD_HEADS = 16
SSD_HEAD_DIM = 64
SSD_INNER = SSD_HEADS * SSD_HEAD_DIM
SSD_GROUPS = 2
SSD_STATE = 128
SSD_CHUNK = 128
SSD_CONV_CH = SSD_INNER + 2 * SSD_GROUPS * SSD_STATE

GQA_HEADS = 8
GQA_KV_HEADS = 2
GQA_HEAD_DIM = 64

MLA_IN = MLA_Q_RANK + MLA_KV_RANK + MLA_ROPE
SSD_IN = SSD_INNER + SSD_CONV_CH + 2 * SSD_HEADS
GQA_IN = (GQA_HEADS + 2 * GQA_KV_HEADS) * GQA_HEAD_DIM
GATE_IN = 3 * D_MODEL
MLA_OUT = MLA_HEADS * MLA_V
GQA_OUT = GQA_HEADS * GQA_HEAD_DIM
FFN_HIDDEN = -(-8 * D_MODEL // (3 * 256)) * 256
DEEPNORM_ALPHA = (2 * DEPTH) ** 0.25

LANE = 128
MOD_ROWS = 8
LOG2E = 1.4426950408889634

C_PM = 0
W_PM = MLA_Q_RANK + MLA_KV_RANK + LANE
C_Z = C_PM + W_PM
C_XBC = C_Z + SSD_INNER
C_DT = C_XBC + SSD_CONV_CH
C_PG = C_DT + LANE
C_GATE = C_PG + GQA_IN
W_INP = C_GATE + GATE_IN

VMEM_LIMIT = 56 * 1024 * 1024


def _cparams(sem):
    return pltpu.CompilerParams(dimension_semantics=sem, vmem_limit_bytes=VMEM_LIMIT)


def _const_spec(shape):
    nd = len(shape)
    return pl.BlockSpec(shape, lambda *_: (0,) * nd, pipeline_mode=pl.Buffered(1))


def _layer_norm(x):
    mu = jnp.mean(x, axis=-1, keepdims=True)
    xc = x - mu
    var = jnp.mean(xc * xc, axis=-1, keepdims=True)
    return xc * lax.rsqrt(var + NORM_EPS)


def _split_dot(x, m01, parts):
    acc = None
    rem = x
    for _ in range(parts):
        piece = rem.astype(BF16)
        term = jnp.dot(piece, m01, preferred_element_type=F32)
        acc = term if acc is None else acc + term
        rem = rem - piece.astype(F32)
    return acc


def _mod_rows(mod_ref, col, width, is_ctx):
    b = pl.program_id(0)
    lat = mod_ref[pl.ds(b, 1), col:col + width]
    ctx = mod_ref[MOD_ROWS // 2:MOD_ROWS // 2 + 1, col:col + width]
    return jnp.where(is_ctx, ctx, lat)


def _is_ctx_rows(tm, ctx_len, tile0=0):
    t = pl.program_id(1) + tile0
    rows = t * tm + lax.broadcasted_iota(jnp.int32, (tm, 1), 0)
    return rows < ctx_len


def _mod_kernel(c_ref, w_ref, b_ref, o_ref):
    c = c_ref[...]
    s = (c * jax.nn.sigmoid(c)).astype(BF16)
    o_ref[0] = jnp.dot(s, w_ref[0].astype(BF16), preferred_element_type=F32) + b_ref[0]


def _modulation(cvec, w_mod, b_mod):
    depth, d, n = w_mod.shape
    tn = 1024
    return pl.pallas_call(
        _mod_kernel,
        grid=(depth, n // tn),
        in_specs=[
            pl.BlockSpec((MOD_ROWS, d), lambda l, j: (0, 0)),
            pl.BlockSpec((1, d, tn), lambda l, j: (l, 0, j)),
            pl.BlockSpec((1, 1, tn), lambda l, j: (l, 0, j)),
        ],
        out_specs=pl.BlockSpec((1, MOD_ROWS, tn), lambda l, j: (l, 0, j)),
        out_shape=jax.ShapeDtypeStruct((depth, MOD_ROWS, n), F32),
        compiler_params=_cparams(("arbitrary", "arbitrary")),
        name="modulation",
    )(cvec, w_mod, b_mod.reshape(depth, 1, n))


def _inproj_kernel(x_ref, mod_ref, w_ref, bg_ref, pm_ref, z_ref, xbc_ref, dt_ref, pg_ref, gate_ref,
                   *, tm, ctx_len):
    is_ctx = _is_ctx_rows(tm, ctx_len)
    d = D_MODEL
    sh = _mod_rows(mod_ref, 0, d, is_ctx)
    sc = _mod_rows(mod_ref, d, d, is_ctx)
    h = (_layer_norm(x_ref[0]) * (1.0 + sc) + sh).astype(BF16)

    def proj(c0, width):
        return jnp.dot(h, w_ref[:, c0:c0 + width], preferred_element_type=F32)

    pm_ref[0] = proj(C_PM, W_PM)
    z_ref[0] = proj(C_Z, SSD_INNER)
    xbc_ref[0] = proj(C_XBC, SSD_CONV_CH)
    dt_ref[0] = proj(C_DT, LANE)
    pg_ref[0] = proj(C_PG, GQA_IN)
    for j in range(3):
        c0 = j * D_MODEL
        logits = proj(C_GATE + c0, D_MODEL) + bg_ref[:, c0:c0 + D_MODEL]
        gate_ref[0, :, c0:c0 + D_MODEL] = jax.nn.sigmoid(logits).astype(gate_ref.dtype)


def _in_projection(xa, mod, w_inp, b_gate, *, ctx_len, tm):
    bsz, t_all, d = xa.shape
    row = lambda w: pl.BlockSpec((1, tm, w), lambda b, t: (b, t, 0))
    out_w = (W_PM, SSD_INNER, SSD_CONV_CH, LANE, GQA_IN, GATE_IN)
    out_dt = (F32, F32, F32, F32, F32, BF16)
    return pl.pallas_call(
        functools.partial(_inproj_kernel, tm=tm, ctx_len=ctx_len),
        grid=(bsz, t_all // tm),
        in_specs=[row(d), _const_spec(mod.shape), _const_spec(w_inp.shape), _const_spec(b_gate.shape)],
        out_specs=[row(w) for w in out_w],
        out_shape=[jax.ShapeDtypeStruct((bsz, t_all, w), dt) for w, dt in zip(out_w, out_dt)],
        compiler_params=_cparams(("parallel", "parallel")),
        name="in_projection",
    )(xa, mod, w_inp, b_gate)


def _rope_lanes(x, c, s_fwd, s_bwd, half):
    n = x.shape[-1]
    return x * c + pltpu.roll(x, n - half, axis=1) * s_fwd + pltpu.roll(x, half, axis=1) * s_bwd


def _mla_prep_kernel(pm_ref, gq_ref, gkv_ref, wuq_ref, wukv_ref, tq_ref, tk_ref, q_ref, k_ref, v_ref):
    pm = pm_ref[0]
    pq = pm[:, :MLA_Q_RANK]
    ckv = pm[:, MLA_Q_RANK:MLA_Q_RANK + MLA_KV_RANK]
    kr = pm[:, MLA_Q_RANK + MLA_KV_RANK:]

    def rms(x, g):
        return x * lax.rsqrt(jnp.mean(x * x, axis=-1, keepdims=True) + NORM_EPS) * g

    q = jnp.dot(rms(pq, gq_ref[...]).astype(BF16), wuq_ref[...], preferred_element_type=F32)
    kv = jnp.dot(rms(ckv, gkv_ref[...]).astype(BF16), wukv_ref[...], preferred_element_type=F32)
    half = MLA_ROPE // 2
    kr = _rope_lanes(kr, tk_ref[0], tk_ref[1], tk_ref[2], half)
    cq, sq1, sq2 = tq_ref[0], tq_ref[1], tq_ref[2]
    for hd in range(MLA_HEADS):
        sl = slice(hd * LANE, (hd + 1) * LANE)
        q_ref[0, :, sl] = _rope_lanes(q[:, sl], cq, sq1, sq2, half).astype(q_ref.dtype)
        k_ref[0, :, sl] = (kv[:, sl] + kr).astype(k_ref.dtype)
    v_ref[0] = kv[:, MLA_HEADS * LANE:].astype(v_ref.dtype)


def _mla_prep(pm, g_q, g_kv, wuq_p, wukv_p, tab_q, tab_k, *, tm):
    bsz, t_all, _ = pm.shape
    row = lambda w: pl.BlockSpec((1, tm, w), lambda b, t: (b, t, 0))
    tab = pl.BlockSpec((3, tm, LANE), lambda b, t: (0, t, 0))
    hw = MLA_HEADS * LANE
    return pl.pallas_call(
        _mla_prep_kernel,
        grid=(bsz, t_all // tm),
        in_specs=[row(W_PM), _const_spec(g_q.shape), _const_spec(g_kv.shape),
                  _const_spec(wuq_p.shape), _const_spec(wukv_p.shape), tab, tab],
        out_specs=[row(hw), row(hw), row(MLA_OUT)],
        out_shape=[jax.ShapeDtypeStruct((bsz, t_all, hw), BF16),
                   jax.ShapeDtypeStruct((bsz, t_all, hw), BF16),
                   jax.ShapeDtypeStruct((bsz, t_all, MLA_OUT), BF16)],
        compiler_params=_cparams(("parallel", "parallel")),
        name="mla_prep",
    )(pm, g_q, g_kv, wuq_p, wukv_p, tab_q, tab_k)


def _gqa_prep_kernel(pg_ref, gq_ref, gk_ref, bd_ref, tq_ref, tk_ref, q_ref, k_ref, v_ref):
    pg = pg_ref[0]
    hd = GQA_HEAD_DIM
    q = pg[:, :GQA_OUT]
    k = pg[:, GQA_OUT:GQA_OUT + LANE]
    v = pg[:, GQA_OUT + LANE:]
    half = hd // 2

    def head_rms(x, g):
        w = x.shape[-1]
        ms = _split_dot(x * x, bd_ref[:w, :w], 2) * (1.0 / hd)
        return x * lax.rsqrt(ms + NORM_EPS) * g

    qn = head_rms(q, gq_ref[...])
    for j in range(GQA_OUT // LANE):
        sl = slice(j * LANE, (j + 1) * LANE)
        q_ref[0, :, sl] = _rope_lanes(qn[:, sl], tq_ref[0], tq_ref[1], tq_ref[2], half).astype(q_ref.dtype)
    kn = head_rms(k, gk_ref[...])
    kn = _rope_lanes(kn, tk_ref[0], tk_ref[1], tk_ref[2], half)
    lane = lax.broadcasted_iota(jnp.int32, kn.shape, 1)
    low = lane < hd
    kn_sw = pltpu.roll(kn, hd, axis=1)
    zero = jnp.zeros_like(kn)
    k_ref[0, :, 0 * LANE:1 * LANE] = jnp.where(low, kn, zero).astype(k_ref.dtype)
    k_ref[0, :, 1 * LANE:2 * LANE] = jnp.where(low, zero, kn_sw).astype(k_ref.dtype)
    k_ref[0, :, 2 * LANE:3 * LANE] = jnp.where(low, kn_sw, zero).astype(k_ref.dtype)
    k_ref[0, :, 3 * LANE:4 * LANE] = jnp.where(low, zero, kn).astype(k_ref.dtype)
    v_sw = pltpu.roll(v, hd, axis=1)
    v_ref[0, :, 0 * LANE:1 * LANE] = jnp.where(low, v, v_sw).astype(v_ref.dtype)
    v_ref[0, :, 1 * LANE:2 * LANE] = jnp.where(low, v_sw, v).astype(v_ref.dtype)


def _gqa_prep(pg, g_q, g_k, bd, tab_q, tab_k, *, tm):
    bsz, t_all, _ = pg.shape
    row = lambda w: pl.BlockSpec((1, tm, w), lambda b, t: (b, t, 0))
    tab = pl.BlockSpec((3, tm, LANE), lambda b, t: (0, t, 0))
    kw = GQA_KV_HEADS * 2 * LANE
    vw = GQA_KV_HEADS * LANE
    return pl.pallas_call(
        _gqa_prep_kernel,
        grid=(bsz, t_all // tm),
        in_specs=[row(GQA_IN), _const_spec(g_q.shape), _const_spec(g_k.shape), _const_spec(bd.shape), tab, tab],
        out_specs=[row(GQA_OUT), row(kw), row(vw)],
        out_shape=[jax.ShapeDtypeStruct((bsz, t_all, GQA_OUT), BF16),
                   jax.ShapeDtypeStruct((bsz, t_all, kw), BF16),
                   jax.ShapeDtypeStruct((bsz, t_all, vw), BF16)],
        compiler_params=_cparams(("parallel", "parallel")),
        name="gqa_prep",
    )(pg, g_q, g_k, bd, tab_q, tab_k)


def _attn_kernel(q_ref, k_ref, v_ref, o_ref, m_ref, l_ref, acc_ref, *, pairs, tq, tk, ctx_len, q_tile0):
    qt = pl.program_id(2) + q_tile0
    t_all = k_ref.shape[1]
    n_loop = jnp.where(qt * tq < ctx_len, 0, (t_all - ctx_len) // tk)

    def chunk(r0, size, first):
        for pi, (vb, units) in enumerate(pairs):
            s = jnp.concatenate(
                [lax.dot_general(q_ref[0, :, qb * LANE:(qb + 1) * LANE],
                                 k_ref[0, pl.ds(r0, size), kb * LANE:(kb + 1) * LANE],
                                 (((1,), (1,)), ((), ())), preferred_element_type=F32)
                 for qb, kb in units], axis=0)
            vc = v_ref[0, pl.ds(r0, size), vb * LANE:(vb + 1) * LANE]
            m_cur = jnp.max(s, axis=1, keepdims=True)
            if first:
                m_next = jnp.broadcast_to(m_cur, (2 * tq, LANE))
            else:
                m_prev = m_ref[pi]
                m_next = jnp.maximum(m_prev, m_cur)
                alpha = jnp.exp2(m_prev - m_next)
            p = jnp.exp2(s - jnp.concatenate([m_next] * (size // LANE), axis=1))
            l_cur = jnp.sum(p, axis=1, keepdims=True)
            pv = jnp.dot(p.astype(BF16), vc, preferred_element_type=F32)
            if first:
                l_ref[pi] = jnp.broadcast_to(l_cur, (2 * tq, LANE))
                acc_ref[pi] = pv
            else:
                l_ref[pi] = alpha * l_ref[pi] + l_cur
                acc_ref[pi] = alpha * acc_ref[pi] + pv
            m_ref[pi] = m_next

    chunk(0, ctx_len, True)

    def body(j, carry):
        chunk(pl.multiple_of(ctx_len + j * tk, LANE), tk, False)
        return carry

    lax.fori_loop(0, n_loop, body, 0)
    lane = lax.broadcasted_iota(jnp.int32, (tq, LANE), 1)
    for pi in range(len(pairs)):
        o = acc_ref[pi] / l_ref[pi]
        o_ref[0, :, pi * LANE:(pi + 1) * LANE] = jnp.where(lane < LANE // 2, o[:tq], o[tq:]).astype(o_ref.dtype)


def _attention(q, k, v, *, groups, q_w, k_w, v_w, pairs, ctx_len, with_ctx, tq, tk):
    bsz, t_all, _ = q.shape
    out_w = len(pairs) * LANE
    q_tile0 = 0 if with_ctx else ctx_len // tq
    nq = t_all // tq - q_tile0
    state = pltpu.VMEM((len(pairs), 2 * tq, LANE), F32)
    return pl.pallas_call(
        functools.partial(_attn_kernel, pairs=pairs, tq=tq, tk=tk, ctx_len=ctx_len, q_tile0=q_tile0),
        grid=(bsz, groups, nq),
        in_specs=[
            pl.BlockSpec((1, tq, q_w), lambda b, g, t: (b, t + q_tile0, g)),
            pl.BlockSpec((1, t_all, k_w), lambda b, g, t: (b, 0, g)),
            pl.BlockSpec((1, t_all, v_w), lambda b, g, t: (b, 0, g)),
        ],
        out_specs=pl.BlockSpec((1, tq, out_w), lambda b, g, t: (b, t + q_tile0, g)),
        out_shape=jax.ShapeDtypeStruct((bsz, t_all, groups * out_w), BF16),
        scratch_shapes=[state, state, state],
        compiler_params=_cparams(("parallel", "parallel", "arbitrary")),
        name="attention",
    )(q, k, v)


def _ssd_chunk_index(i, n_chunks, ctx_chunks, reverse):
    if not reverse:
        return i
    return jnp.where(i < ctx_chunks, ctx_chunks - 1 - i, n_chunks + ctx_chunks - 1 - i)


def _ssd_kernel(*refs, reverse, n_chunks, ctx_chunks):
    if reverse:
        (xbc_ref, prev_ref, next_ref, dt_ref, cw_ref, cb_ref, dtb_ref, a_ref, tri_ref, ehd_ref,
         yf_ref, z_ref, dsk_ref, g_ref, y_ref, h_ref) = refs
    else:
        (xbc_ref, prev_ref, next_ref, dt_ref, cw_ref, cb_ref, dtb_ref, a_ref, tri_ref, ehd_ref,
         y_ref, h_ref) = refs
    q = SSD_CHUNK
    i = pl.program_id(1)
    ci = _ssd_chunk_index(i, n_chunks, ctx_chunks, reverse)

    @pl.when(i == 0)
    def _():
        h_ref[...] = jnp.zeros_like(h_ref)

    x = xbc_ref[0]
    at_start = jnp.logical_or(ci == 0, ci == ctx_chunks)
    at_end = jnp.logical_or(ci == ctx_chunks - 1, ci == n_chunks - 1)
    prev_row = jnp.where(at_start, 0.0, prev_ref[0, 7:8, :])
    next_row = jnp.where(at_end, 0.0, next_ref[0, 0:1, :])
    row = lax.broadcasted_iota(jnp.int32, (q, 1), 0)
    x_prev = jnp.where(row == 0, prev_row, pltpu.roll(x, 1, axis=0))
    x_next = jnp.where(row == q - 1, next_row, pltpu.roll(x, q - 1, axis=0))
    u = cw_ref[0:1, :] * x_prev + cw_ref[1:2, :] * x + cw_ref[2:3, :] * x_next + cb_ref[...]
    u = u * jax.nn.sigmoid(u)
    xs = u[:, :SSD_INNER]
    gs = SSD_STATE
    bm = [u[:, SSD_INNER + g * gs:SSD_INNER + (g + 1) * gs] for g in range(SSD_GROUPS)]
    cm = [u[:, SSD_INNER + (SSD_GROUPS + g) * gs:SSD_INNER + (SSD_GROUPS + g + 1) * gs]
          for g in range(SSD_GROUPS)]

    dt = jax.nn.softplus(dt_ref[0] + dtb_ref[...])
    da = dt * a_ref[...]
    a_cum = _tri_cumsum(tri_ref, da)
    end = 0 if reverse else q - 1
    a_tot = a_cum[end:end + 1, :]
    e_cum = jnp.exp(a_cum)
    dte = jnp.exp(a_tot - a_cum)
    dt_x = _split_dot(dt, ehd_ref[...], 2)
    ec_x = _split_dot(e_cum, ehd_ref[...], 2)
    dte_x = _split_dot(dte, ehd_ref[...], 2)
    xdt = xs * dt_x
    xdt_b = xdt.astype(BF16)
    xst_b = (xdt * dte_x).astype(BF16)
    a_cum_t = a_cum.T

    kk = lax.broadcasted_iota(jnp.int32, (q, q), 1)
    qq = lax.broadcasted_iota(jnp.int32, (q, q), 0)
    in_order = (kk >= qq) if reverse else (kk <= qq)
    lane = lax.broadcasted_iota(jnp.int32, (q, LANE), 1)
    off = SSD_HEADS if reverse else 0
    hpg = SSD_HEADS // SSD_GROUPS
    gw = hpg * SSD_HEAD_DIM
    y_parts = []
    for g in range(SSD_GROUPS):
        bm_b = bm[g].astype(BF16)
        cm_b = cm[g].astype(BF16)
        scores = lax.dot_general(cm_b, bm_b, (((1,), (1,)), ((), ())), preferred_element_type=F32)
        h_prev = h_ref[g]
        y_off = jnp.dot(cm_b, h_prev.astype(BF16), preferred_element_type=F32) * ec_x[:, g * gw:(g + 1) * gw]
        pair_out = []
        for j in range(hpg // 2):
            xp = xdt_b[:, g * gw + j * LANE:g * gw + (j + 1) * LANE]
            res = []
            for e in range(2):
                hd = g * hpg + 2 * j + e
                a_q = a_cum[:, off + hd:off + hd + 1]
                a_k = a_cum_t[off + hd:off + hd + 1, :]
                decay = jnp.exp(jnp.where(in_order, a_q - a_k, -jnp.inf))
                pmat = (scores * decay).astype(BF16)
                res.append(jnp.dot(pmat, xp, preferred_element_type=F32))
            pair_out.append(jnp.where(lane < SSD_HEAD_DIM, res[0], res[1]))
        y_parts.append(jnp.concatenate(pair_out, axis=1) + y_off)
        st = jnp.dot(bm[g].T.astype(BF16), xst_b[:, g * gw:(g + 1) * gw], preferred_element_type=F32)
        h_ref[g] = h_prev * ec_x[end:end + 1, g * gw:(g + 1) * gw] + st
    y = jnp.concatenate(y_parts, axis=1)

    if reverse:
        yt = yf_ref[0] + y + dsk_ref[...] * xs
        zz = z_ref[0]
        yt = yt * (zz * jax.nn.sigmoid(zz))
        yt = yt * lax.rsqrt(jnp.mean(yt * yt, axis=-1, keepdims=True) + NORM_EPS) * g_ref[...]
        y_ref[0] = yt.astype(y_ref.dtype)
    else:
        y_ref[0] = y


def _tri_cumsum(tri_ref, da):
    acc = None
    rem = da
    for _ in range(3):
        piece = rem.astype(BF16)
        term = jnp.dot(tri_ref[...], piece, preferred_element_type=F32)
        acc = term if acc is None else acc + term
        rem = rem - piece.astype(F32)
    return acc


def _ssd_direction(xbc, dt, conv_w, conv_b, dt_bias, a_row, tri, ehd, extra, *, ctx_len, reverse):
    bsz, t_all, cch = xbc.shape
    q = SSD_CHUNK
    n_chunks = t_all // q
    ctx_chunks = ctx_len // q
    halo = 8
    hb = q // halo
    n_halo = t_all // halo
    cidx = lambda i: _ssd_chunk_index(i, n_chunks, ctx_chunks, reverse)
    chunk = lambda w: pl.BlockSpec((1, q, w), lambda b, i: (b, cidx(i), 0))
    in_specs = [
        chunk(cch),
        pl.BlockSpec((1, halo, cch), lambda b, i: (b, jnp.maximum(cidx(i) * hb - 1, 0), 0)),
        pl.BlockSpec((1, halo, cch), lambda b, i: (b, jnp.minimum((cidx(i) + 1) * hb, n_halo - 1), 0)),
        chunk(LANE),
        _const_spec(conv_w.shape), _const_spec(conv_b.shape), _const_spec(dt_bias.shape),
        _const_spec(a_row.shape), _const_spec(tri.shape), _const_spec(ehd.shape),
    ]
    args = [xbc, xbc, xbc, dt, conv_w, conv_b, dt_bias, a_row, tri, ehd]
    if reverse:
        y_f, z, d_x, g_ssd = extra
        in_specs += [chunk(SSD_INNER), chunk(SSD_INNER), _const_spec(d_x.shape), _const_spec(g_ssd.shape)]
        args += [y_f, z, d_x, g_ssd]
    out_dtype = BF16 if reverse else F32
    hpg = SSD_HEADS // SSD_GROUPS
    return pl.pallas_call(
        functools.partial(_ssd_kernel, reverse=reverse, n_chunks=n_chunks, ctx_chunks=ctx_chunks),
        grid=(bsz, n_chunks),
        in_specs=in_specs,
        out_specs=chunk(SSD_INNER),
        out_shape=jax.ShapeDtypeStruct((bsz, t_all, SSD_INNER), out_dtype),
        scratch_shapes=[pltpu.VMEM((SSD_GROUPS, SSD_STATE, hpg * SSD_HEAD_DIM), F32)],
        compiler_params=_cparams(("parallel", "arbitrary")),
        name="ssd_bwd" if reverse else "ssd_fwd",
    )(*args)


def _post_kernel(x_ref, ym_ref, ys_ref, yg_ref, gate_ref, mod_ref, wo_ref, wi_ref, wf_ref,
                 l1g_ref, l1b_ref, l2g_ref, l2b_ref, o_ref, act_ref, *, tm, ctx_len, th, tile0):
    is_ctx = _is_ctx_rows(tm, ctx_len, tile0)
    d = D_MODEL
    mix = None
    row0 = 0
    for j, y_ref in enumerate((ym_ref, ys_ref, yg_ref)):
        kw = y_ref.shape[-1]
        br = jnp.dot(y_ref[0], wo_ref[row0:row0 + kw, :], preferred_element_type=F32)
        br = gate_ref[0, :, j * d:(j + 1) * d].astype(F32) * br
        mix = br if mix is None else mix + br
        row0 += kw
    g1 = _mod_rows(mod_ref, 2 * d, d, is_ctx)
    x1 = _layer_norm(DEEPNORM_ALPHA * x_ref[0] + g1 * mix) * l1g_ref[...] + l1b_ref[...]

    sh2 = _mod_rows(mod_ref, 3 * d, d, is_ctx)
    sc2 = _mod_rows(mod_ref, 4 * d, d, is_ctx)
    g2 = _mod_rows(mod_ref, 5 * d, d, is_ctx)
    h2 = (_layer_norm(x1) * (1.0 + sc2) + sh2).astype(BF16)
    hid = FFN_HIDDEN
    for c0 in range(0, hid, th):
        gt = jnp.dot(h2, wi_ref[:, c0:c0 + th], preferred_element_type=F32)
        up = jnp.dot(h2, wi_ref[:, hid + c0:hid + c0 + th], preferred_element_type=F32)
        act_ref[:, c0:c0 + th] = (gt * jax.nn.sigmoid(gt) * up).astype(BF16)
    fx = jnp.dot(act_ref[...], wf_ref[...], preferred_element_type=F32)
    o_ref[0] = _layer_norm(DEEPNORM_ALPHA * x1 + g2 * fx) * l2g_ref[...] + l2b_ref[...]


def _post_mixer(xa, y_mla, y_ssd, y_gqa, gate, mod, w_out, w_ffn_in, w_ffn_out, l1g, l1b, l2g, l2b,
                *, ctx_len, tm, with_ctx):
    bsz, t_all, d = xa.shape
    tile0 = 0 if with_ctx else ctx_len // tm
    row = lambda w: pl.BlockSpec((1, tm, w), lambda b, t: (b, t + tile0, 0))
    vec = _const_spec((1, d))
    t_out = t_all - tile0 * tm
    return pl.pallas_call(
        functools.partial(_post_kernel, tm=tm, ctx_len=ctx_len, th=256, tile0=tile0),
        grid=(bsz, t_out // tm),
        in_specs=[row(d), row(MLA_OUT), row(SSD_INNER), row(GQA_OUT), row(GATE_IN), _const_spec(mod.shape),
                  _const_spec(w_out.shape), _const_spec(w_ffn_in.shape), _const_spec(w_ffn_out.shape),
                  vec, vec, vec, vec],
        out_specs=pl.BlockSpec((1, tm, d), lambda b, t: (b, t, 0)),
        out_shape=jax.ShapeDtypeStruct((bsz, t_out, d), F32),
        scratch_shapes=[pltpu.VMEM((tm, FFN_HIDDEN), BF16)],
        compiler_params=_cparams(("parallel", "parallel")),
        name="post_mixer",
    )(xa, y_mla, y_ssd, y_gqa, gate, mod, w_out, w_ffn_in, w_ffn_out, l1g, l1b, l2g, l2b)


def _rope_tables(n_lat, ctx_len, rot_dim, lane_dim, lane_off, scale):
    rows = n_lat // GRID_W
    row = jnp.repeat(jnp.arange(rows), GRID_W)
    col = jnp.tile(jnp.arange(GRID_W), rows)
    n_freq = rot_dim // 4
    inv_freq = ROPE_THETA ** (-jnp.arange(n_freq, dtype=F32) / n_freq)
    ang = jnp.concatenate([row[:, None] * inv_freq, col[:, None] * inv_freq], axis=-1)
    cos, sin = jnp.cos(ang), jnp.sin(ang)
    half = rot_dim // 2
    lane = np.arange(LANE)
    rel = (lane % lane_dim) - lane_off
    is_rot = (rel >= 0) & (rel < rot_dim)
    idx = np.where(is_rot, rel % half, 0)
    first = is_rot & (rel < half)
    second = is_rot & (rel >= half)
    c = jnp.where(is_rot[None, :], cos[:, idx], 1.0)
    s1 = jnp.where(first[None, :], -sin[:, idx], 0.0)
    s2 = jnp.where(second[None, :], sin[:, idx], 0.0)
    lat = jnp.stack([c, s1, s2])
    ctx = jnp.stack([jnp.ones((ctx_len, LANE), F32), jnp.zeros((ctx_len, LANE), F32),
                     jnp.zeros((ctx_len, LANE), F32)])
    return (jnp.concatenate([ctx, lat], axis=1) * scale).astype(F32)


def _pad_cols(w, width, offset=0):
    out = jnp.zeros((w.shape[0], width), w.dtype)
    return out.at[:, offset:offset + w.shape[1]].set(w)


def _layout_w_in(w):
    c = 0
    q_lat = w[:, c:c + MLA_Q_RANK]; c += MLA_Q_RANK
    kv_lat = w[:, c:c + MLA_KV_RANK]; c += MLA_KV_RANK
    k_rope = _pad_cols(w[:, c:c + MLA_ROPE], LANE, MLA_NOPE); c += MLA_ROPE
    z = w[:, c:c + SSD_INNER]; c += SSD_INNER
    xbc = w[:, c:c + SSD_CONV_CH]; c += SSD_CONV_CH
    dt = _pad_cols(w[:, c:c + 2 * SSD_HEADS], LANE); c += 2 * SSD_HEADS
    rest = w[:, c:]
    return jnp.concatenate([q_lat, kv_lat, k_rope, z, xbc, dt, rest], axis=1).astype(BF16)


def _layout_w_uq(w):
    w = w.reshape(w.shape[0], MLA_HEADS, MLA_QK)
    w = jnp.pad(w, ((0, 0), (0, 0), (0, LANE - MLA_QK)))
    return w.reshape(w.shape[0], MLA_HEADS * LANE).astype(BF16)


def _layout_w_ukv(w):
    w = w.reshape(w.shape[0], MLA_HEADS, MLA_NOPE + MLA_V)
    k = jnp.pad(w[:, :, :MLA_NOPE], ((0, 0), (0, 0), (0, LANE - MLA_NOPE)))
    v = w[:, :, MLA_NOPE:]
    return jnp.concatenate([k.reshape(w.shape[0], -1), v.reshape(w.shape[0], -1)], axis=1).astype(BF16)


def _lane_row(vec, width=LANE):
    return _pad_cols(vec.reshape(1, -1).astype(F32), width)


def kernel(x, c, ctx, c_ctx, w_mod, b_mod, w_in, b_gate, w_uq, g_q_mla, w_ukv, g_kv_mla, conv_w, conv_b,
           a_log, dt_bias, d_skip, g_ssd, g_q_gqa, g_k_gqa, w_out, ln1_g, ln1_b, w_ffn_in, w_ffn_out,
           ln2_g, ln2_b):
    bsz, n_lat, d = x.shape
    ctx_len = ctx.shape[1]
    depth = w_mod.shape[0]
    assert d == D_MODEL and bsz <= MOD_ROWS // 2
    assert ctx_len % 256 == 0 and n_lat % 256 == 0
    tm = 256

    cvec = jnp.zeros((MOD_ROWS, d), F32).at[:bsz].set(c).at[MOD_ROWS // 2].set(c_ctx)
    mod_all = _modulation(cvec, w_mod, b_mod)

    mla_scale = MLA_QK ** -0.5 * LOG2E
    gqa_scale = GQA_HEAD_DIM ** -0.5 * LOG2E
    tab_mq = _rope_tables(n_lat, ctx_len, MLA_ROPE, LANE, MLA_NOPE, mla_scale)
    tab_mk = _rope_tables(n_lat, ctx_len, MLA_ROPE, LANE, MLA_NOPE, 1.0)
    tab_gq = _rope_tables(n_lat, ctx_len, GQA_HEAD_DIM, GQA_HEAD_DIM, 0, gqa_scale)
    tab_gk = _rope_tables(n_lat, ctx_len, GQA_HEAD_DIM, GQA_HEAD_DIM, 0, 1.0)

    qn = SSD_CHUNK
    r = np.arange(qn)
    tri_f = jnp.asarray((r[None, :] <= r[:, None]).astype(np.float32), BF16)
    tri_b = jnp.asarray((r[None, :] >= r[:, None]).astype(np.float32), BF16)
    lane1k = np.arange(SSD_INNER) // SSD_HEAD_DIM
    ehd_f = jnp.asarray((np.arange(LANE)[:, None] == lane1k[None, :]).astype(np.float32), BF16)
    ehd_b = jnp.asarray((np.arange(LANE)[:, None] == (lane1k[None, :] + SSD_HEADS)).astype(np.float32), BF16)
    blk = np.arange(GQA_OUT) // GQA_HEAD_DIM
    bd = jnp.asarray((blk[:, None] == blk[None, :]).astype(np.float32), BF16)

    xa = jnp.concatenate([ctx, x], axis=1)
    for i in range(depth):
        with_ctx = i < depth - 1
        mod = mod_all[i]
        w_inp = _layout_w_in(w_in[i])
        pm, z, xbc, dtr, pg, gate = _in_projection(xa, mod, w_inp, b_gate[i].reshape(1, -1),
                                                   ctx_len=ctx_len, tm=tm)
        q_m, k_m, v_m = _mla_prep(pm, g_q_mla[i].reshape(1, -1), g_kv_mla[i].reshape(1, -1),
                                  _layout_w_uq(w_uq[i]), _layout_w_ukv(w_ukv[i]), tab_mq, tab_mk, tm=tm)
        y_mla = _attention(q_m, k_m, v_m, groups=MLA_HEADS // 4, q_w=4 * LANE, k_w=4 * LANE, v_w=2 * LANE,
                           pairs=((0, ((0, 0), (1, 1))), (1, ((2, 2), (3, 3)))),
                           ctx_len=ctx_len, with_ctx=with_ctx, tq=256, tk=512)
        q_g, k_g, v_g = _gqa_prep(pg, jnp.tile(g_q_gqa[i], GQA_HEADS).reshape(1, -1),
                                  jnp.tile(g_k_gqa[i], GQA_KV_HEADS).reshape(1, -1), bd, tab_gq, tab_gk, tm=tm)
        y_gqa = _attention(q_g, k_g, v_g, groups=GQA_KV_HEADS, q_w=2 * LANE, k_w=2 * LANE, v_w=LANE,
                           pairs=((0, ((0, 0), (0, 1))), (0, ((1, 0), (1, 1)))),
                           ctx_len=ctx_len, with_ctx=with_ctx, tq=256, tk=512)
        a = -jnp.exp(a_log[i].astype(F32))
        a_row = _lane_row(a.reshape(-1))
        dtb_row = _lane_row(dt_bias[i].reshape(-1))
        cb = conv_b[i].reshape(1, -1)
        y_f = _ssd_direction(xbc, dtr, conv_w[i], cb, dtb_row, a_row, tri_f, ehd_f, None,
                             ctx_len=ctx_len, reverse=False)
        d_x = jnp.repeat(d_skip[i], SSD_HEAD_DIM).reshape(1, -1)
        y_ssd = _ssd_direction(xbc, dtr, conv_w[i], cb, dtb_row, a_row, tri_b, ehd_b,
                               (y_f, z, d_x, g_ssd[i].reshape(1, -1)), ctx_len=ctx_len, reverse=True)
        xa = _post_mixer(xa, y_mla, y_ssd, y_gqa, gate, mod, w_out[i].astype(BF16),
                         w_ffn_in[i].astype(BF16), w_ffn_out[i].astype(BF16),
                         ln1_g[i].reshape(1, -1), ln1_b[i].reshape(1, -1),
                         ln2_g[i].reshape(1, -1), ln2_b[i].reshape(1, -1), ctx_len=ctx_len, tm=tm,
                         with_ctx=with_ctx)
    return xa
```

```python
import functools
import math

import jax
import jax.numpy as jnp
import numpy as np
from jax import lax
from jax.experimental import pallas as pl
from jax.experimental.pallas import tpu as pltpu

F32 = jnp.float32
BF16 = jnp.bfloat16

D_MODEL = 1024
DEPTH = 2
GRID_W = 64
ROPE_THETA = 10000.0
NORM_EPS = 1e-6

MLA_HEADS = 8
MLA_Q_RANK = 384
MLA_KV_RANK = 256
MLA_NOPE = 64
MLA_ROPE = 32
MLA_V = 64
MLA_QK = MLA_NOPE + MLA_ROPE

SSD_HEADS = 16
SSD_HEAD_DIM = 64
SSD_INNER = SSD_HEADS * SSD_HEAD_DIM
SSD_GROUPS = 2
SSD_STATE = 128
SSD_CHUNK = 128
SSD_CONV_CH = SSD_INNER + 2 * SSD_GROUPS * SSD_STATE

GQA_HEADS = 8
GQA_KV_HEADS = 2
GQA_HEAD_DIM = 64

MLA_IN = MLA_Q_RANK + MLA_KV_RANK + MLA_ROPE
SSD_IN = SSD_INNER + SSD_CONV_CH + 2 * SSD_HEADS
GQA_IN = (GQA_HEADS + 2 * GQA_KV_HEADS) * GQA_HEAD_DIM
GATE_IN = 3 * D_MODEL
MLA_OUT = MLA_HEADS * MLA_V
GQA_OUT = GQA_HEADS * GQA_HEAD_DIM
FFN_HIDDEN = -(-8 * D_MODEL // (3 * 256)) * 256
DEEPNORM_ALPHA = (2 * DEPTH) ** 0.25

LANE = 128
MOD_ROWS = 8
LOG2E = 1.4426950408889634

C_PM = 0
W_PM = MLA_Q_RANK + MLA_KV_RANK + LANE
C_Z = C_PM + W_PM
C_XBC = C_Z + SSD_INNER
C_DT = C_XBC + SSD_CONV_CH
C_PG = C_DT + LANE
C_GATE = C_PG + GQA_IN
W_INP = C_GATE + GATE_IN

VMEM_LIMIT = 56 * 1024 * 1024
ROW_TILE = 256
ATTN_TQ = 512
ATTN_TK = 512
ATTN_ROW_BLOCK = 64


def _cparams(sem):
    return pltpu.CompilerParams(dimension_semantics=sem, vmem_limit_bytes=VMEM_LIMIT)


def _const_spec(shape):
    nd = len(shape)
    return pl.BlockSpec(shape, lambda *_: (0,) * nd, pipeline_mode=pl.Buffered(1))


def _layer_norm(x):
    mu = jnp.mean(x, axis=-1, keepdims=True)
    xc = x - mu
    var = jnp.mean(xc * xc, axis=-1, keepdims=True)
    return xc * lax.rsqrt(var + NORM_EPS)


def _split_dot(x, m01, parts):
    acc = None
    rem = x
    for _ in range(parts):
        piece = rem.astype(BF16)
        term = jnp.dot(piece, m01, preferred_element_type=F32)
        acc = term if acc is None else acc + term
        rem = rem - piece.astype(F32)
    return acc


def _mod_rows(mod_ref, col, width, is_ctx):
    b = pl.program_id(0)
    lat = mod_ref[pl.ds(b, 1), col:col + width]
    ctx = mod_ref[MOD_ROWS // 2:MOD_ROWS // 2 + 1, col:col + width]
    return jnp.where(is_ctx, ctx, lat)


def _is_ctx_rows(tm, n_lat):
    t = pl.program_id(1)
    rows = t * tm + lax.broadcasted_iota(jnp.int32, (tm, 1), 0)
    return rows >= n_lat


def _mod_kernel(c_ref, w_ref, b_ref, o_ref):
    c = c_ref[...]
    s = (c * jax.nn.sigmoid(c)).astype(BF16)
    o_ref[0] = jnp.dot(s, w_ref[0].astype(BF16), preferred_element_type=F32) + b_ref[0]


def _modulation(cvec, w_mod, b_mod):
    depth, d, n = w_mod.shape
    tn = 1024
    return pl.pallas_call(
        _mod_kernel,
        grid=(depth, n // tn),
        in_specs=[
            pl.BlockSpec((MOD_ROWS, d), lambda l, j: (0, 0)),
            pl.BlockSpec((1, d, tn), lambda l, j: (l, 0, j)),
            pl.BlockSpec((1, 1, tn), lambda l, j: (l, 0, j)),
        ],
        out_specs=pl.BlockSpec((1, MOD_ROWS, tn), lambda l, j: (l, 0, j)),
        out_shape=jax.ShapeDtypeStruct((depth, MOD_ROWS, n), F32),
        compiler_params=_cparams(("arbitrary", "arbitrary")),
        name="modulation",
    )(cvec, w_mod, b_mod.reshape(depth, 1, n))


def _inproj_kernel(x_ref, mod_ref, w_ref, bg_ref, pm_ref, z_ref, xbc_ref, dt_ref, pg_ref, gate_ref,
                   *, tm, n_lat):
    is_ctx = _is_ctx_rows(tm, n_lat)
    d = D_MODEL
    sh = _mod_rows(mod_ref, 0, d, is_ctx)
    sc = _mod_rows(mod_ref, d, d, is_ctx)
    h = (_layer_norm(x_ref[0]) * (1.0 + sc) + sh).astype(BF16)

    def proj(c0, width):
        return jnp.dot(h, w_ref[:, c0:c0 + width], preferred_element_type=F32)

    pm_ref[0] = proj(C_PM, W_PM)
    z_ref[0] = proj(C_Z, SSD_INNER)
    xbc_ref[0] = proj(C_XBC, SSD_CONV_CH)
    dt_ref[0] = proj(C_DT, LANE)
    pg_ref[0] = proj(C_PG, GQA_IN)
    for j in range(3):
        c0 = j * D_MODEL
        logits = proj(C_GATE + c0, D_MODEL) + bg_ref[:, c0:c0 + D_MODEL]
        gate_ref[0, :, c0:c0 + D_MODEL] = jax.nn.sigmoid(logits).astype(gate_ref.dtype)


def _in_projection(xa, mod, w_inp, b_gate, *, n_lat, tm):
    bsz, t_all, d = xa.shape
    row = lambda w: pl.BlockSpec((1, tm, w), lambda b, t: (b, t, 0))
    out_w = (W_PM, SSD_INNER, SSD_CONV_CH, LANE, GQA_IN, GATE_IN)
    out_dt = (F32, F32, F32, F32, F32, BF16)
    return pl.pallas_call(
        functools.partial(_inproj_kernel, tm=tm, n_lat=n_lat),
        grid=(bsz, t_all // tm),
        in_specs=[row(d), _const_spec(mod.shape), _const_spec(w_inp.shape), _const_spec(b_gate.shape)],
        out_specs=[row(w) for w in out_w],
        out_shape=[jax.ShapeDtypeStruct((bsz, t_all, w), dt) for w, dt in zip(out_w, out_dt)],
        compiler_params=_cparams(("parallel", "parallel")),
        name="in_projection",
    )(xa, mod, w_inp, b_gate)


def _rope_lanes(x, c, s_fwd, s_bwd, half):
    n = x.shape[-1]
    return x * c + pltpu.roll(x, n - half, axis=1) * s_fwd + pltpu.roll(x, half, axis=1) * s_bwd


def _mla_prep_kernel(pm_ref, gq_ref, gkv_ref, wuq_ref, wukv_ref, tq_ref, tk_ref, q_ref, k_ref, v_ref):
    pm = pm_ref[0]
    pq = pm[:, :MLA_Q_RANK]
    ckv = pm[:, MLA_Q_RANK:MLA_Q_RANK + MLA_KV_RANK]
    kr = pm[:, MLA_Q_RANK + MLA_KV_RANK:]

    def rms(x, g):
        return x * lax.rsqrt(jnp.mean(x * x, axis=-1, keepdims=True) + NORM_EPS) * g

    q = jnp.dot(rms(pq, gq_ref[...]).astype(BF16), wuq_ref[...], preferred_element_type=F32)
    kv = jnp.dot(rms(ckv, gkv_ref[...]).astype(BF16), wukv_ref[...], preferred_element_type=F32)
    half = MLA_ROPE // 2
    kr = _rope_lanes(kr, tk_ref[0], tk_ref[1], tk_ref[2], half)
    cq, sq1, sq2 = tq_ref[0], tq_ref[1], tq_ref[2]
    for hd in range(MLA_HEADS):
        sl = slice(hd * LANE, (hd + 1) * LANE)
        q_ref[0, :, sl] = _rope_lanes(q[:, sl], cq, sq1, sq2, half).astype(q_ref.dtype)
        k_ref[0, :, sl] = (kv[:, sl] + kr).astype(k_ref.dtype)
    ones = jnp.ones((pm.shape[0], LANE), v_ref.dtype)
    for j in range(MLA_HEADS // 2):
        c0 = MLA_HEADS * LANE + j * LANE
        v_ref[0, :, 2 * j * LANE:(2 * j + 1) * LANE] = kv[:, c0:c0 + LANE].astype(v_ref.dtype)
        v_ref[0, :, (2 * j + 1) * LANE:(2 * j + 2) * LANE] = ones


def _mla_prep(pm, g_q, g_kv, wuq_p, wukv_p, tab_q, tab_k, *, tm):
    bsz, t_all, _ = pm.shape
    row = lambda w: pl.BlockSpec((1, tm, w), lambda b, t: (b, t, 0))
    tab = pl.BlockSpec((3, tm, LANE), lambda b, t: (0, t, 0))
    hw = MLA_HEADS * LANE
    return pl.pallas_call(
        _mla_prep_kernel,
        grid=(bsz, t_all // tm),
        in_specs=[row(W_PM), _const_spec(g_q.shape), _const_spec(g_kv.shape),
                  _const_spec(wuq_p.shape), _const_spec(wukv_p.shape), tab, tab],
        out_specs=[row(hw), row(hw), row(hw)],
        out_shape=[jax.ShapeDtypeStruct((bsz, t_all, hw), BF16)] * 3,
        compiler_params=_cparams(("parallel", "parallel")),
        name="mla_prep",
    )(pm, g_q, g_kv, wuq_p, wukv_p, tab_q, tab_k)


def _gqa_prep_kernel(pg_ref, gq_ref, gk_ref, bd_ref, tq_ref, tk_ref, q_ref, k_ref, v_ref):
    pg = pg_ref[0]
    hd = GQA_HEAD_DIM
    q = pg[:, :GQA_OUT]
    k = pg[:, GQA_OUT:GQA_OUT + LANE]
    v = pg[:, GQA_OUT + LANE:]
    half = hd // 2

    def head_rms(x, g):
        w = x.shape[-1]
        ms = _split_dot(x * x, bd_ref[:w, :w], 2) * (1.0 / hd)
        return x * lax.rsqrt(ms + NORM_EPS) * g

    qn = head_rms(q, gq_ref[...])
    for j in range(GQA_OUT // LANE):
        sl = slice(j * LANE, (j + 1) * LANE)
        q_ref[0, :, sl] = _rope_lanes(qn[:, sl], tq_ref[0], tq_ref[1], tq_ref[2], half).astype(q_ref.dtype)
    kn = head_rms(k, gk_ref[...])
    kn = _rope_lanes(kn, tk_ref[0], tk_ref[1], tk_ref[2], half)
    lane = lax.broadcasted_iota(jnp.int32, kn.shape, 1)
    low = lane < hd
    kn_sw = pltpu.roll(kn, hd, axis=1)
    zero = jnp.zeros_like(kn)
    k_ref[0, :, 0 * LANE:1 * LANE] = jnp.where(low, kn, zero).astype(k_ref.dtype)
    k_ref[0, :, 1 * LANE:2 * LANE] = jnp.where(low, zero, kn_sw).astype(k_ref.dtype)
    k_ref[0, :, 2 * LANE:3 * LANE] = jnp.where(low, kn_sw, zero).astype(k_ref.dtype)
    k_ref[0, :, 3 * LANE:4 * LANE] = jnp.where(low, zero, kn).astype(k_ref.dtype)
    v_sw = pltpu.roll(v, hd, axis=1)
    ones = jnp.ones(v.shape, v_ref.dtype)
    v_ref[0, :, 0 * LANE:1 * LANE] = jnp.where(low, v, v_sw).astype(v_ref.dtype)
    v_ref[0, :, 1 * LANE:2 * LANE] = ones
    v_ref[0, :, 2 * LANE:3 * LANE] = jnp.where(low, v_sw, v).astype(v_ref.dtype)
    v_ref[0, :, 3 * LANE:4 * LANE] = ones


def _gqa_prep(pg, g_q, g_k, bd, tab_q, tab_k, *, tm):
    bsz, t_all, _ = pg.shape
    row = lambda w: pl.BlockSpec((1, tm, w), lambda b, t: (b, t, 0))
    tab = pl.BlockSpec((3, tm, LANE), lambda b, t: (0, t, 0))
    kw = GQA_KV_HEADS * 2 * LANE
    vw = GQA_KV_HEADS * 2 * LANE
    return pl.pallas_call(
        _gqa_prep_kernel,
        grid=(bsz, t_all // tm),
        in_specs=[row(GQA_IN), _const_spec(g_q.shape), _const_spec(g_k.shape), _const_spec(bd.shape), tab, tab],
        out_specs=[row(GQA_OUT), row(kw), row(vw)],
        out_shape=[jax.ShapeDtypeStruct((bsz, t_all, GQA_OUT), BF16),
                   jax.ShapeDtypeStruct((bsz, t_all, kw), BF16),
                   jax.ShapeDtypeStruct((bsz, t_all, vw), BF16)],
        compiler_params=_cparams(("parallel", "parallel")),
        name="gqa_prep",
    )(pg, g_q, g_k, bd, tab_q, tab_k)


def _attn_kernel(q_ref, k_ref, v_ref, o_ref, m_ref, alpha_ref, acc_ref, p_ref, sa_ref, sb_ref,
                 *, pairs, tq, tk, n_lat, ctx_len, ctx_queries):
    rb = ATTN_ROW_BLOCK

    def scores(r0, size, s_ref):
        for pi, (_, units) in enumerate(pairs):
            for ui, (qb, kb) in enumerate(units):
                s_ref[pi, ui * tq:(ui + 1) * tq, :size] = lax.dot_general(
                    q_ref[0, :, qb * LANE:(qb + 1) * LANE],
                    k_ref[0, pl.ds(r0, size), kb * LANE:(kb + 1) * LANE],
                    (((1,), (1,)), ((), ())), preferred_element_type=F32)

    def update(r0, size, s_ref):
        for pi, (vb, _) in enumerate(pairs):
            for r in range(0, 2 * tq, rb):
                s = s_ref[pi, r:r + rb, :size]
                m_prev = m_ref[pi, r:r + rb, :]
                m_next = jnp.maximum(m_prev, jnp.max(s, axis=1, keepdims=True))
                x = s - jnp.concatenate([m_next] * (size // LANE), axis=1)
                p_ref[pi, r:r + rb, :size] = jnp.exp2(x.astype(BF16))
                alpha_ref[pi, r:r + rb, :] = jnp.exp2(m_prev - m_next)
                m_ref[pi, r:r + rb, :] = m_next
            vc = v_ref[0, pl.ds(r0, size), vb * 2 * LANE:(vb + 1) * 2 * LANE]
            pv = jnp.dot(p_ref[pi, :, :size], vc, preferred_element_type=F32)
            alpha = alpha_ref[pi]
            acc_ref[pi] = jnp.concatenate([alpha, alpha], axis=1) * acc_ref[pi] + pv

    m_ref[...] = jnp.full(m_ref.shape, -jnp.inf, F32)
    acc_ref[...] = jnp.zeros(acc_ref.shape, F32)
    scores(n_lat, ctx_len, sa_ref)
    if ctx_queries:
        update(n_lat, ctx_len, sa_ref)
    else:
        n_pairs_of_chunks = n_lat // (2 * tk)
        scores(0, tk, sb_ref)
        update(n_lat, ctx_len, sa_ref)

        def body(jj, carry):
            r = pl.multiple_of(jj * (2 * tk), 2 * tk)
            scores(r + tk, tk, sa_ref)
            update(r, tk, sb_ref)
            scores(r + 2 * tk, tk, sb_ref)
            update(r + tk, tk, sa_ref)
            return carry

        lax.fori_loop(0, n_pairs_of_chunks - 1, body, 0)
        r = n_lat - 2 * tk
        scores(r + tk, tk, sa_ref)
        update(r, tk, sb_ref)
        update(r + tk, tk, sa_ref)
    lane = lax.broadcasted_iota(jnp.int32, (tq, LANE), 1)
    for pi in range(len(pairs)):
        o = acc_ref[pi, :, :LANE] / acc_ref[pi, :, LANE:]
        o_ref[0, :, pi * LANE:(pi + 1) * LANE] = jnp.where(lane < LANE // 2, o[:tq], o[tq:]).astype(o_ref.dtype)


def _attention(q, k, v, *, groups, q_w, k_w, v_w, pairs, n_lat, ctx_queries, tq, tk):
    bsz, t_all, _ = q.shape
    ctx_len = t_all - n_lat
    out_w = len(pairs) * LANE
    rows = ctx_len if ctx_queries else n_lat
    q_tile0 = n_lat // tq if ctx_queries else 0
    assert rows % tq == 0 and n_lat % tq == 0 and n_lat % (2 * tk) == 0 and ctx_len <= tk
    assert (2 * tq) % ATTN_ROW_BLOCK == 0
    state = pltpu.VMEM((len(pairs), 2 * tq, LANE), F32)
    acc = pltpu.VMEM((len(pairs), 2 * tq, 2 * LANE), F32)
    pbuf = pltpu.VMEM((len(pairs), 2 * tq, tk), BF16)
    sbuf = pltpu.VMEM((len(pairs), 2 * tq, tk), F32)
    return pl.pallas_call(
        functools.partial(_attn_kernel, pairs=pairs, tq=tq, tk=tk, n_lat=n_lat, ctx_len=ctx_len,
                          ctx_queries=ctx_queries),
        grid=(bsz, groups, rows // tq),
        in_specs=[
            pl.BlockSpec((1, tq, q_w), lambda b, g, t: (b, t + q_tile0, g)),
            pl.BlockSpec((1, t_all, k_w), lambda b, g, t: (b, 0, g)),
            pl.BlockSpec((1, t_all, v_w), lambda b, g, t: (b, 0, g)),
        ],
        out_specs=pl.BlockSpec((1, tq, out_w), lambda b, g, t: (b, t, g)),
        out_shape=jax.ShapeDtypeStruct((bsz, rows, groups * out_w), BF16),
        scratch_shapes=[state, state, acc, pbuf, sbuf, sbuf],
        compiler_params=_cparams(("parallel", "parallel", "arbitrary")),
        name="attention_ctx" if ctx_queries else "attention",
    )(q, k, v)


def _ssd_chunk_index(i, n_chunks, ctx_chunks, reverse):
    if reverse:
        return n_chunks - 1 - i
    lat_chunks = n_chunks - ctx_chunks
    return jnp.where(i < ctx_chunks, lat_chunks + i, i - ctx_chunks)


def _ssd_kernel(*refs, reverse, n_chunks, ctx_chunks):
    if reverse:
        (xbc_ref, prev_ref, next_ref, dt_ref, cw_ref, cb_ref, dtb_ref, a_ref, tri_ref, ehd_ref,
         yf_ref, z_ref, dsk_ref, g_ref, y_ref, h_ref) = refs
    else:
        (xbc_ref, prev_ref, next_ref, dt_ref, cw_ref, cb_ref, dtb_ref, a_ref, tri_ref, ehd_ref,
         y_ref, h_ref) = refs
    q = SSD_CHUNK
    i = pl.program_id(1)
    ci = _ssd_chunk_index(i, n_chunks, ctx_chunks, reverse)

    @pl.when(i == 0)
    def _():
        h_ref[...] = jnp.zeros_like(h_ref)

    x = xbc_ref[0]
    lat_chunks = n_chunks - ctx_chunks
    at_start = jnp.logical_or(ci == 0, ci == lat_chunks)
    at_end = jnp.logical_or(ci == lat_chunks - 1, ci == n_chunks - 1)
    prev_row = jnp.where(at_start, 0.0, prev_ref[0, 7:8, :])
    next_row = jnp.where(at_end, 0.0, next_ref[0, 0:1, :])
    row = lax.broadcasted_iota(jnp.int32, (q, 1), 0)
    x_prev = jnp.where(row == 0, prev_row, pltpu.roll(x, 1, axis=0))
    x_next = jnp.where(row == q - 1, next_row, pltpu.roll(x, q - 1, axis=0))
    u = cw_ref[0:1, :] * x_prev + cw_ref[1:2, :] * x + cw_ref[2:3, :] * x_next + cb_ref[...]
    u = u * jax.nn.sigmoid(u)
    xs = u[:, :SSD_INNER]
    gs = SSD_STATE
    bm = [u[:, SSD_INNER + g * gs:SSD_INNER + (g + 1) * gs] for g in range(SSD_GROUPS)]
    cm = [u[:, SSD_INNER + (SSD_GROUPS + g) * gs:SSD_INNER + (SSD_GROUPS + g + 1) * gs]
          for g in range(SSD_GROUPS)]

    dt = jax.nn.softplus(dt_ref[0] + dtb_ref[...])
    da = dt * a_ref[...]
    a_cum = _tri_cumsum(tri_ref, da)
    end = 0 if reverse else q - 1
    a_tot = a_cum[end:end + 1, :]
    e_cum = jnp.exp(a_cum)
    dte = jnp.exp(a_tot - a_cum)
    dt_x = _split_dot(dt, ehd_ref[...], 2)
    ec_x = _split_dot(e_cum, ehd_ref[...], 2)
    dte_x = _split_dot(dte, ehd_ref[...], 2)
    xdt = xs * dt_x
    xdt_b = xdt.astype(BF16)
    xst_b = (xdt * dte_x).astype(BF16)
    a_cum_t = a_cum.T

    kk = lax.broadcasted_iota(jnp.int32, (q, q), 1)
    qq = lax.broadcasted_iota(jnp.int32, (q, q), 0)
    in_order = (kk >= qq) if reverse else (kk <= qq)
    lane = lax.broadcasted_iota(jnp.int32, (q, LANE), 1)
    off = SSD_HEADS if reverse else 0
    hpg = SSD_HEADS // SSD_GROUPS
    gw = hpg * SSD_HEAD_DIM
    y_parts = []
    for g in range(SSD_GROUPS):
        bm_b = bm[g].astype(BF16)
        cm_b = cm[g].astype(BF16)
        scores = lax.dot_general(cm_b, bm_b, (((1,), (1,)), ((), ())), preferred_element_type=F32)
        h_prev = h_ref[g]
        y_off = jnp.dot(cm_b, h_prev.astype(BF16), preferred_element_type=F32) * ec_x[:, g * gw:(g + 1) * gw]
        pair_out = []
        for j in range(hpg // 2):
            xp = xdt_b[:, g * gw + j * LANE:g * gw + (j + 1) * LANE]
            res = []
            for e in range(2):
                hd = g * hpg + 2 * j + e
                a_q = a_cum[:, off + hd:off + hd + 1]
                a_k = a_cum_t[off + hd:off + hd + 1, :]
                decay = jnp.exp(jnp.where(in_order, a_q - a_k, -jnp.inf))
                pmat = (scores * decay).astype(BF16)
                res.append(jnp.dot(pmat, xp, preferred_element_type=F32))
            pair_out.append(jnp.where(lane < SSD_HEAD_DIM, res[0], res[1]))
        y_parts.append(jnp.concatenate(pair_out, axis=1) + y_off)
        st = jnp.dot(bm[g].T.astype(BF16), xst_b[:, g * gw:(g + 1) * gw], preferred_element_type=F32)
        h_ref[g] = h_prev * ec_x[end:end + 1, g * gw:(g + 1) * gw] + st
    y = jnp.concatenate(y_parts, axis=1)

    if reverse:
        yt = yf_ref[0] + y + dsk_ref[...] * xs
        zz = z_ref[0]
        yt = yt * (zz * jax.nn.sigmoid(zz))
        yt = yt * lax.rsqrt(jnp.mean(yt * yt, axis=-1, keepdims=True) + NORM_EPS) * g_ref[...]
        y_ref[0] = yt.astype(y_ref.dtype)
    else:
        y_ref[0] = y


def _tri_cumsum(tri_ref, da):
    acc = None
    rem = da
    for _ in range(3):
        piece = rem.astype(BF16)
        term = jnp.dot(tri_ref[...], piece, preferred_element_type=F32)
        acc = term if acc is None else acc + term
        rem = rem - piece.astype(F32)
    return acc


def _ssd_direction(xbc, dt, conv_w, conv_b, dt_bias, a_row, tri, ehd, extra, *, ctx_len, reverse):
    bsz, t_all, cch = xbc.shape
    q = SSD_CHUNK
    n_chunks = t_all // q
    ctx_chunks = ctx_len // q
    halo = 8
    hb = q // halo
    n_halo = t_all // halo
    cidx = lambda i: _ssd_chunk_index(i, n_chunks, ctx_chunks, reverse)
    chunk = lambda w: pl.BlockSpec((1, q, w), lambda b, i: (b, cidx(i), 0))
    in_specs = [
        chunk(cch),
        pl.BlockSpec((1, halo, cch), lambda b, i: (b, jnp.maximum(cidx(i) * hb - 1, 0), 0)),
        pl.BlockSpec((1, halo, cch), lambda b, i: (b, jnp.minimum((cidx(i) + 1) * hb, n_halo - 1), 0)),
        chunk(LANE),
        _const_spec(conv_w.shape), _const_spec(conv_b.shape), _const_spec(dt_bias.shape),
        _const_spec(a_row.shape), _const_spec(tri.shape), _const_spec(ehd.shape),
    ]
    args = [xbc, xbc, xbc, dt, conv_w, conv_b, dt_bias, a_row, tri, ehd]
    if reverse:
        y_f, z, d_x, g_ssd = extra
        in_specs += [chunk(SSD_INNER), chunk(SSD_INNER), _const_spec(d_x.shape), _const_spec(g_ssd.shape)]
        args += [y_f, z, d_x, g_ssd]
    out_dtype = BF16 if reverse else F32
    hpg = SSD_HEADS // SSD_GROUPS
    return pl.pallas_call(
        functools.partial(_ssd_kernel, reverse=reverse, n_chunks=n_chunks, ctx_chunks=ctx_chunks),
        grid=(bsz, n_chunks),
        in_specs=in_specs,
        out_specs=chunk(SSD_INNER),
        out_shape=jax.ShapeDtypeStruct((bsz, t_all, SSD_INNER), out_dtype),
        scratch_shapes=[pltpu.VMEM((SSD_GROUPS, SSD_STATE, hpg * SSD_HEAD_DIM), F32)],
        compiler_params=_cparams(("parallel", "arbitrary")),
        name="ssd_bwd" if reverse else "ssd_fwd",
    )(*args)


def _post_kernel(x_ref, ym_ref, ys_ref, yg_ref, gate_ref, mod_ref, wo_ref, wi_ref, wf_ref,
                 l1g_ref, l1b_ref, l2g_ref, l2b_ref, o_ref, act_ref, *, tm, n_lat, th):
    is_ctx = _is_ctx_rows(tm, n_lat)
    d = D_MODEL
    mix = None
    row0 = 0
    for j, y_ref in enumerate((ym_ref, ys_ref, yg_ref)):
        kw = y_ref.shape[-1]
        br = jnp.dot(y_ref[0], wo_ref[row0:row0 + kw, :], preferred_element_type=F32)
        br = gate_ref[0, :, j * d:(j + 1) * d].astype(F32) * br
        mix = br if mix is None else mix + br
        row0 += kw
    g1 = _mod_rows(mod_ref, 2 * d, d, is_ctx)
    x1 = _layer_norm(DEEPNORM_ALPHA * x_ref[0] + g1 * mix) * l1g_ref[...] + l1b_ref[...]

    sh2 = _mod_rows(mod_ref, 3 * d, d, is_ctx)
    sc2 = _mod_rows(mod_ref, 4 * d, d, is_ctx)
    g2 = _mod_rows(mod_ref, 5 * d, d, is_ctx)
    h2 = (_layer_norm(x1) * (1.0 + sc2) + sh2).astype(BF16)
    hid = FFN_HIDDEN
    for c0 in range(0, hid, th):
        gt = jnp.dot(h2, wi_ref[:, c0:c0 + th], preferred_element_type=F32)
        up = jnp.dot(h2, wi_ref[:, hid + c0:hid + c0 + th], preferred_element_type=F32)
        act_ref[:, c0:c0 + th] = (gt * jax.nn.sigmoid(gt) * up).astype(BF16)
    fx = jnp.dot(act_ref[...], wf_ref[...], preferred_element_type=F32)
    o_ref[0] = _layer_norm(DEEPNORM_ALPHA * x1 + g2 * fx) * l2g_ref[...] + l2b_ref[...]


def _post_mixer(xa, y_mla, y_ssd, y_gqa, gate, mod, w_out, w_ffn_in, w_ffn_out, l1g, l1b, l2g, l2b,
                *, n_lat, tm, with_ctx):
    bsz, t_all, d = xa.shape
    row = lambda w: pl.BlockSpec((1, tm, w), lambda b, t: (b, t, 0))
    vec = _const_spec((1, d))
    t_out = t_all if with_ctx else n_lat
    return pl.pallas_call(
        functools.partial(_post_kernel, tm=tm, n_lat=n_lat, th=256),
        grid=(bsz, t_out // tm),
        in_specs=[row(d), row(MLA_OUT), row(SSD_INNER), row(GQA_OUT), row(GATE_IN), _const_spec(mod.shape),
                  _const_spec(w_out.shape), _const_spec(w_ffn_in.shape), _const_spec(w_ffn_out.shape),
                  vec, vec, vec, vec],
        out_specs=pl.BlockSpec((1, tm, d), lambda b, t: (b, t, 0)),
        out_shape=jax.ShapeDtypeStruct((bsz, t_out, d), F32),
        scratch_shapes=[pltpu.VMEM((tm, FFN_HIDDEN), BF16)],
        compiler_params=_cparams(("parallel", "parallel")),
        name="post_mixer",
    )(xa, y_mla, y_ssd, y_gqa, gate, mod, w_out, w_ffn_in, w_ffn_out, l1g, l1b, l2g, l2b)


def _rope_tables(n_lat, ctx_len, rot_dim, lane_dim, lane_off, scale):
    rows = n_lat // GRID_W
    row = jnp.repeat(jnp.arange(rows), GRID_W)
    col = jnp.tile(jnp.arange(GRID_W), rows)
    n_freq = rot_dim // 4
    inv_freq = ROPE_THETA ** (-jnp.arange(n_freq, dtype=F32) / n_freq)
    ang = jnp.concatenate([row[:, None] * inv_freq, col[:, None] * inv_freq], axis=-1)
    cos, sin = jnp.cos(ang), jnp.sin(ang)
    half = rot_dim // 2
    lane = np.arange(LANE)
    rel = (lane % lane_dim) - lane_off
    is_rot = (rel >= 0) & (rel < rot_dim)
    idx = np.where(is_rot, rel % half, 0)
    first = is_rot & (rel < half)
    second = is_rot & (rel >= half)
    c = jnp.where(is_rot[None, :], cos[:, idx], 1.0)
    s1 = jnp.where(first[None, :], -sin[:, idx], 0.0)
    s2 = jnp.where(second[None, :], sin[:, idx], 0.0)
    lat = jnp.stack([c, s1, s2])
    ctx = jnp.stack([jnp.ones((ctx_len, LANE), F32), jnp.zeros((ctx_len, LANE), F32),
                     jnp.zeros((ctx_len, LANE), F32)])
    return (jnp.concatenate([lat, ctx], axis=1) * scale).astype(F32)


def _pad_cols(w, width, offset=0):
    out = jnp.zeros((w.shape[0], width), w.dtype)
    return out.at[:, offset:offset + w.shape[1]].set(w)


def _layout_w_in(w):
    c = 0
    q_lat = w[:, c:c + MLA_Q_RANK]; c += MLA_Q_RANK
    kv_lat = w[:, c:c + MLA_KV_RANK]; c += MLA_KV_RANK
    k_rope = _pad_cols(w[:, c:c + MLA_ROPE], LANE, MLA_NOPE); c += MLA_ROPE
    z = w[:, c:c + SSD_INNER]; c += SSD_INNER
    xbc = w[:, c:c + SSD_CONV_CH]; c += SSD_CONV_CH
    dt = _pad_cols(w[:, c:c + 2 * SSD_HEADS], LANE); c += 2 * SSD_HEADS
    rest = w[:, c:]
    return jnp.concatenate([q_lat, kv_lat, k_rope, z, xbc, dt, rest], axis=1).astype(BF16)


def _layout_w_uq(w):
    w = w.reshape(w.shape[0], MLA_HEADS, MLA_QK)
    w = jnp.pad(w, ((0, 0), (0, 0), (0, LANE - MLA_QK)))
    return w.reshape(w.shape[0], MLA_HEADS * LANE).astype(BF16)


def _layout_w_ukv(w):
    w = w.reshape(w.shape[0], MLA_HEADS, MLA_NOPE + MLA_V)
    k = jnp.pad(w[:, :, :MLA_NOPE], ((0, 0), (0, 0), (0, LANE - MLA_NOPE)))
    v = w[:, :, MLA_NOPE:]
    return jnp.concatenate([k.reshape(w.shape[0], -1), v.reshape(w.shape[0], -1)], axis=1).astype(BF16)


def _lane_row(vec, width=LANE):
    return _pad_cols(vec.reshape(1, -1).astype(F32), width)


def kernel(x, c, ctx, c_ctx, w_mod, b_mod, w_in, b_gate, w_uq, g_q_mla, w_ukv, g_kv_mla, conv_w, conv_b,
           a_log, dt_bias, d_skip, g_ssd, g_q_gqa, g_k_gqa, w_out, ln1_g, ln1_b, w_ffn_in, w_ffn_out,
           ln2_g, ln2_b):
    bsz, n_lat, d = x.shape
    ctx_len = ctx.shape[1]
    depth = w_mod.shape[0]
    assert d == D_MODEL and bsz <= MOD_ROWS // 2
    tm = ROW_TILE
    assert ctx_len % tm == 0 and n_lat % tm == 0

    cvec = jnp.zeros((MOD_ROWS, d), F32).at[:bsz].set(c).at[MOD_ROWS // 2].set(c_ctx)
    mod_all = _modulation(cvec, w_mod, b_mod)

    mla_scale = MLA_QK ** -0.5 * LOG2E
    gqa_scale = GQA_HEAD_DIM ** -0.5 * LOG2E
    tab_mq = _rope_tables(n_lat, ctx_len, MLA_ROPE, LANE, MLA_NOPE, mla_scale)
    tab_mk = _rope_tables(n_lat, ctx_len, MLA_ROPE, LANE, MLA_NOPE, 1.0)
    tab_gq = _rope_tables(n_lat, ctx_len, GQA_HEAD_DIM, GQA_HEAD_DIM, 0, gqa_scale)
    tab_gk = _rope_tables(n_lat, ctx_len, GQA_HEAD_DIM, GQA_HEAD_DIM, 0, 1.0)

    qn = SSD_CHUNK
    r = np.arange(qn)
    tri_f = jnp.asarray((r[None, :] <= r[:, None]).astype(np.float32), BF16)
    tri_b = jnp.asarray((r[None, :] >= r[:, None]).astype(np.float32), BF16)
    lane1k = np.arange(SSD_INNER) // SSD_HEAD_DIM
    ehd_f = jnp.asarray((np.arange(LANE)[:, None] == lane1k[None, :]).astype(np.float32), BF16)
    ehd_b = jnp.asarray((np.arange(LANE)[:, None] == (lane1k[None, :] + SSD_HEADS)).astype(np.float32), BF16)
    blk = np.arange(GQA_OUT) // GQA_HEAD_DIM
    bd = jnp.asarray((blk[:, None] == blk[None, :]).astype(np.float32), BF16)

    mla_attn = functools.partial(_attention, groups=MLA_HEADS // 4, q_w=4 * LANE, k_w=4 * LANE, v_w=4 * LANE,
                                 pairs=((0, ((0, 0), (1, 1))), (1, ((2, 2), (3, 3)))), n_lat=n_lat)
    gqa_attn = functools.partial(_attention, groups=GQA_KV_HEADS, q_w=2 * LANE, k_w=2 * LANE, v_w=2 * LANE,
                                 pairs=((0, ((0, 0), (0, 1))), (0, ((1, 0), (1, 1)))), n_lat=n_lat)

    def attend(fn, q, k, v, with_ctx):
        y = fn(q, k, v, ctx_queries=False, tq=ATTN_TQ, tk=ATTN_TK)
        if with_ctx:
            y_c = fn(q, k, v, ctx_queries=True, tq=ctx_len, tk=ATTN_TK)
            y = jnp.concatenate([y, y_c], axis=1)
        return y

    xa = jnp.concatenate([x, ctx], axis=1)
    for i in range(depth):
        with_ctx = i < depth - 1
        mod = mod_all[i]
        w_inp = _layout_w_in(w_in[i])
        pm, z, xbc, dtr, pg, gate = _in_projection(xa, mod, w_inp, b_gate[i].reshape(1, -1),
                                                   n_lat=n_lat, tm=tm)
        q_m, k_m, v_m = _mla_prep(pm, g_q_mla[i].reshape(1, -1), g_kv_mla[i].reshape(1, -1),
                                  _layout_w_uq(w_uq[i]), _layout_w_ukv(w_ukv[i]), tab_mq, tab_mk, tm=tm)
        y_mla = attend(mla_attn, q_m, k_m, v_m, with_ctx)
        q_g, k_g, v_g = _gqa_prep(pg, jnp.tile(g_q_gqa[i], GQA_HEADS).reshape(1, -1),
                                  jnp.tile(g_k_gqa[i], GQA_KV_HEADS).reshape(1, -1), bd, tab_gq, tab_gk, tm=tm)
        y_gqa = attend(gqa_attn, q_g, k_g, v_g, with_ctx)
        a = -jnp.exp(a_log[i].astype(F32))
        a_row = _lane_row(a.reshape(-1))
        dtb_row = _lane_row(dt_bias[i].reshape(-1))
        cb = conv_b[i].reshape(1, -1)
        y_f = _ssd_direction(xbc, dtr, conv_w[i], cb, dtb_row, a_row, tri_f, ehd_f, None,
                             ctx_len=ctx_len, reverse=False)
        d_x = jnp.repeat(d_skip[i], SSD_HEAD_DIM).reshape(1, -1)
        y_ssd = _ssd_direction(xbc, dtr, conv_w[i], cb, dtb_row, a_row, tri_b, ehd_b,
                               (y_f, z, d_x, g_ssd[i].reshape(1, -1)), ctx_len=ctx_len, reverse=True)
        xa = _post_mixer(xa, y_mla, y_ssd, y_gqa, gate, mod, w_out[i].astype(BF16),
                         w_ffn_in[i].astype(BF16), w_ffn_out[i].astype(BF16),
                         ln1_g[i].reshape(1, -1), ln1_b[i].reshape(1, -1),
                         ln2_g[i].reshape(1, -1), ln2_b[i].reshape(1, -1), n_lat=n_lat, tm=tm,
                         with_ctx=with_ctx)
    return xa
```

```python
import functools
import math

import jax
import jax.numpy as jnp
import numpy as np
from jax import lax
from jax.experimental import pallas as pl
from jax.experimental.pallas import tpu as pltpu

F32 = jnp.float32
BF16 = jnp.bfloat16

D_MODEL = 1024
DEPTH = 2
GRID_W = 64
ROPE_THETA = 10000.0
NORM_EPS = 1e-6

MLA_HEADS = 8
MLA_Q_RANK = 384
MLA_KV_RANK = 256
MLA_NOPE = 64
MLA_ROPE = 32
MLA_V = 64
MLA_QK = MLA_NOPE + MLA_ROPE

SSD_HEADS = 16
SSD_HEAD_DIM = 64
SSD_INNER = SSD_HEADS * SSD_HEAD_DIM
SSD_GROUPS = 2
SSD_STATE = 128
SSD_CHUNK = 128
SSD_CONV_CH = SSD_INNER + 2 * SSD_GROUPS * SSD_STATE

GQA_HEADS = 8
GQA_KV_HEADS = 2
GQA_HEAD_DIM = 64

MLA_IN = MLA_Q_RANK + MLA_KV_RANK + MLA_ROPE
SSD_IN = SSD_INNER + SSD_CONV_CH + 2 * SSD_HEADS
GQA_IN = (GQA_HEADS + 2 * GQA_KV_HEADS) * GQA_HEAD_DIM
GATE_IN = 3 * D_MODEL
MLA_OUT = MLA_HEADS * MLA_V
GQA_OUT = GQA_HEADS * GQA_HEAD_DIM
FFN_HIDDEN = -(-8 * D_MODEL // (3 * 256)) * 256
DEEPNORM_ALPHA = (2 * DEPTH) ** 0.25

LANE = 128
MOD_ROWS = 8
LOG2E = 1.4426950408889634

C_PM = 0
W_PM = MLA_Q_RANK + MLA_KV_RANK + LANE
C_Z = C_PM + W_PM
C_XBC = C_Z + SSD_INNER
C_DT = C_XBC + SSD_CONV_CH
C_PG = C_DT + LANE
C_GATE = C_PG + GQA_IN
W_INP = C_GATE + GATE_IN

VMEM_LIMIT = 56 * 1024 * 1024
ROW_TILE = 256
ATTN_TQ = 512
ATTN_TK = 512
ATTN_COL_BLOCK = 128
VT_TILE = ROW_TILE
VT_ROWS = 144


def _cparams(sem):
    return pltpu.CompilerParams(dimension_semantics=sem, vmem_limit_bytes=VMEM_LIMIT)


def _const_spec(shape):
    nd = len(shape)
    return pl.BlockSpec(shape, lambda *_: (0,) * nd, pipeline_mode=pl.Buffered(1))


def _layer_norm(x):
    mu = jnp.mean(x, axis=-1, keepdims=True)
    xc = x - mu
    var = jnp.mean(xc * xc, axis=-1, keepdims=True)
    return xc * lax.rsqrt(var + NORM_EPS)


def _split_dot(x, m01, parts):
    acc = None
    rem = x
    for _ in range(parts):
        piece = rem.astype(BF16)
        term = jnp.dot(piece, m01, preferred_element_type=F32)
        acc = term if acc is None else acc + term
        rem = rem - piece.astype(F32)
    return acc


def _mod_rows(mod_ref, col, width, is_ctx):
    b = pl.program_id(0)
    lat = mod_ref[pl.ds(b, 1), col:col + width]
    ctx = mod_ref[MOD_ROWS // 2:MOD_ROWS // 2 + 1, col:col + width]
    return jnp.where(is_ctx, ctx, lat)


def _is_ctx_rows(tm, n_lat):
    t = pl.program_id(1)
    rows = t * tm + lax.broadcasted_iota(jnp.int32, (tm, 1), 0)
    return rows >= n_lat


def _mod_kernel(c_ref, w_ref, b_ref, o_ref):
    c = c_ref[...]
    s = (c * jax.nn.sigmoid(c)).astype(BF16)
    o_ref[0] = jnp.dot(s, w_ref[0].astype(BF16), preferred_element_type=F32) + b_ref[0]


def _modulation(cvec, w_mod, b_mod):
    depth, d, n = w_mod.shape
    tn = 1024
    return pl.pallas_call(
        _mod_kernel,
        grid=(depth, n // tn),
        in_specs=[
            pl.BlockSpec((MOD_ROWS, d), lambda l, j: (0, 0)),
            pl.BlockSpec((1, d, tn), lambda l, j: (l, 0, j)),
            pl.BlockSpec((1, 1, tn), lambda l, j: (l, 0, j)),
        ],
        out_specs=pl.BlockSpec((1, MOD_ROWS, tn), lambda l, j: (l, 0, j)),
        out_shape=jax.ShapeDtypeStruct((depth, MOD_ROWS, n), F32),
        compiler_params=_cparams(("arbitrary", "arbitrary")),
        name="modulation",
    )(cvec, w_mod, b_mod.reshape(depth, 1, n))


def _inproj_kernel(x_ref, mod_ref, w_ref, bg_ref, pm_ref, z_ref, xbc_ref, dt_ref, pg_ref, gate_ref,
                   *, tm, n_lat):
    is_ctx = _is_ctx_rows(tm, n_lat)
    d = D_MODEL
    sh = _mod_rows(mod_ref, 0, d, is_ctx)
    sc = _mod_rows(mod_ref, d, d, is_ctx)
    h = (_layer_norm(x_ref[0]) * (1.0 + sc) + sh).astype(BF16)

    def proj(c0, width):
        return jnp.dot(h, w_ref[:, c0:c0 + width], preferred_element_type=F32)

    pm_ref[0] = proj(C_PM, W_PM)
    z_ref[0] = proj(C_Z, SSD_INNER)
    xbc_ref[0] = proj(C_XBC, SSD_CONV_CH)
    dt_ref[0] = proj(C_DT, LANE)
    pg_ref[0] = proj(C_PG, GQA_IN)
    for j in range(3):
        c0 = j * D_MODEL
        logits = proj(C_GATE + c0, D_MODEL) + bg_ref[:, c0:c0 + D_MODEL]
        gate_ref[0, :, c0:c0 + D_MODEL] = jax.nn.sigmoid(logits).astype(gate_ref.dtype)


def _in_projection(xa, mod, w_inp, b_gate, *, n_lat, tm):
    bsz, t_all, d = xa.shape
    row = lambda w: pl.BlockSpec((1, tm, w), lambda b, t: (b, t, 0))
    out_w = (W_PM, SSD_INNER, SSD_CONV_CH, LANE, GQA_IN, GATE_IN)
    out_dt = (F32, F32, F32, F32, F32, BF16)
    return pl.pallas_call(
        functools.partial(_inproj_kernel, tm=tm, n_lat=n_lat),
        grid=(bsz, t_all // tm),
        in_specs=[row(d), _const_spec(mod.shape), _const_spec(w_inp.shape), _const_spec(b_gate.shape)],
        out_specs=[row(w) for w in out_w],
        out_shape=[jax.ShapeDtypeStruct((bsz, t_all, w), dt) for w, dt in zip(out_w, out_dt)],
        compiler_params=_cparams(("parallel", "parallel")),
        name="in_projection",
    )(xa, mod, w_inp, b_gate)


def _rope_lanes(x, c, s_fwd, s_bwd, half):
    n = x.shape[-1]
    return x * c + pltpu.roll(x, n - half, axis=1) * s_fwd + pltpu.roll(x, half, axis=1) * s_bwd


def _mla_prep_kernel(pm_ref, gq_ref, gkv_ref, wuq_ref, wukv_ref, tq_ref, tk_ref, q_ref, k_ref, v_ref):
    pm = pm_ref[0]
    pq = pm[:, :MLA_Q_RANK]
    ckv = pm[:, MLA_Q_RANK:MLA_Q_RANK + MLA_KV_RANK]
    kr = pm[:, MLA_Q_RANK + MLA_KV_RANK:]

    def rms(x, g):
        return x * lax.rsqrt(jnp.mean(x * x, axis=-1, keepdims=True) + NORM_EPS) * g

    q = jnp.dot(rms(pq, gq_ref[...]).astype(BF16), wuq_ref[...], preferred_element_type=F32)
    kv = jnp.dot(rms(ckv, gkv_ref[...]).astype(BF16), wukv_ref[...], preferred_element_type=F32)
    half = MLA_ROPE // 2
    kr = _rope_lanes(kr, tk_ref[0], tk_ref[1], tk_ref[2], half)
    cq, sq1, sq2 = tq_ref[0], tq_ref[1], tq_ref[2]
    for hd in range(MLA_HEADS):
        sl = slice(hd * LANE, (hd + 1) * LANE)
        q_ref[0, :, sl] = _rope_lanes(q[:, sl], cq, sq1, sq2, half).astype(q_ref.dtype)
        k_ref[0, :, sl] = (kv[:, sl] + kr).astype(k_ref.dtype)
    ones = jnp.ones((VT_ROWS - LANE, pm.shape[0]), v_ref.dtype)
    for j in range(MLA_HEADS // 2):
        c0 = MLA_HEADS * LANE + j * LANE
        v_ref[0, 0, j * VT_ROWS:j * VT_ROWS + LANE, :] = kv[:, c0:c0 + LANE].T.astype(v_ref.dtype)
        v_ref[0, 0, j * VT_ROWS + LANE:(j + 1) * VT_ROWS, :] = ones


def _mla_prep(pm, g_q, g_kv, wuq_p, wukv_p, tab_q, tab_k, *, tm):
    bsz, t_all, _ = pm.shape
    row = lambda w: pl.BlockSpec((1, tm, w), lambda b, t: (b, t, 0))
    tab = pl.BlockSpec((3, tm, LANE), lambda b, t: (0, t, 0))
    hw = MLA_HEADS * LANE
    vt_rows = (MLA_HEADS // 2) * VT_ROWS
    assert tm == VT_TILE
    return pl.pallas_call(
        _mla_prep_kernel,
        grid=(bsz, t_all // tm),
        in_specs=[row(W_PM), _const_spec(g_q.shape), _const_spec(g_kv.shape),
                  _const_spec(wuq_p.shape), _const_spec(wukv_p.shape), tab, tab],
        out_specs=[row(hw), row(hw), pl.BlockSpec((1, 1, vt_rows, tm), lambda b, t: (b, t, 0, 0))],
        out_shape=[jax.ShapeDtypeStruct((bsz, t_all, hw), BF16),
                   jax.ShapeDtypeStruct((bsz, t_all, hw), BF16),
                   jax.ShapeDtypeStruct((bsz, t_all // tm, vt_rows, tm), BF16)],
        compiler_params=_cparams(("parallel", "parallel")),
        name="mla_prep",
    )(pm, g_q, g_kv, wuq_p, wukv_p, tab_q, tab_k)


def _gqa_prep_kernel(pg_ref, gq_ref, gk_ref, bd_ref, tq_ref, tk_ref, q_ref, k_ref, v_ref):
    pg = pg_ref[0]
    hd = GQA_HEAD_DIM
    q = pg[:, :GQA_OUT]
    k = pg[:, GQA_OUT:GQA_OUT + LANE]
    v = pg[:, GQA_OUT + LANE:]
    half = hd // 2

    def head_rms(x, g):
        w = x.shape[-1]
        ms = _split_dot(x * x, bd_ref[:w, :w], 2) * (1.0 / hd)
        return x * lax.rsqrt(ms + NORM_EPS) * g

    qn = head_rms(q, gq_ref[...])
    for j in range(GQA_OUT // LANE):
        sl = slice(j * LANE, (j + 1) * LANE)
        q_ref[0, :, sl] = _rope_lanes(qn[:, sl], tq_ref[0], tq_ref[1], tq_ref[2], half).astype(q_ref.dtype)
    kn = head_rms(k, gk_ref[...])
    kn = _rope_lanes(kn, tk_ref[0], tk_ref[1], tk_ref[2], half)
    lane = lax.broadcasted_iota(jnp.int32, kn.shape, 1)
    low = lane < hd
    kn_sw = pltpu.roll(kn, hd, axis=1)
    zero = jnp.zeros_like(kn)
    k_ref[0, :, 0 * LANE:1 * LANE] = jnp.where(low, kn, zero).astype(k_ref.dtype)
    k_ref[0, :, 1 * LANE:2 * LANE] = jnp.where(low, zero, kn_sw).astype(k_ref.dtype)
    k_ref[0, :, 2 * LANE:3 * LANE] = jnp.where(low, kn_sw, zero).astype(k_ref.dtype)
    k_ref[0, :, 3 * LANE:4 * LANE] = jnp.where(low, zero, kn).astype(k_ref.dtype)
    v_t = v.T.astype(v_ref.dtype)
    ones = jnp.ones((VT_ROWS - LANE, v.shape[0]), v_ref.dtype)
    for g in range(GQA_KV_HEADS):
        vg = v_t[g * hd:(g + 1) * hd]
        v_ref[0, 0, g * VT_ROWS:g * VT_ROWS + hd, :] = vg
        v_ref[0, 0, g * VT_ROWS + hd:g * VT_ROWS + LANE, :] = vg
        v_ref[0, 0, g * VT_ROWS + LANE:(g + 1) * VT_ROWS, :] = ones


def _gqa_prep(pg, g_q, g_k, bd, tab_q, tab_k, *, tm):
    bsz, t_all, _ = pg.shape
    row = lambda w: pl.BlockSpec((1, tm, w), lambda b, t: (b, t, 0))
    tab = pl.BlockSpec((3, tm, LANE), lambda b, t: (0, t, 0))
    kw = GQA_KV_HEADS * 2 * LANE
    vt_rows = GQA_KV_HEADS * VT_ROWS
    assert tm == VT_TILE
    return pl.pallas_call(
        _gqa_prep_kernel,
        grid=(bsz, t_all // tm),
        in_specs=[row(GQA_IN), _const_spec(g_q.shape), _const_spec(g_k.shape), _const_spec(bd.shape), tab, tab],
        out_specs=[row(GQA_OUT), row(kw), pl.BlockSpec((1, 1, vt_rows, tm), lambda b, t: (b, t, 0, 0))],
        out_shape=[jax.ShapeDtypeStruct((bsz, t_all, GQA_OUT), BF16),
                   jax.ShapeDtypeStruct((bsz, t_all, kw), BF16),
                   jax.ShapeDtypeStruct((bsz, t_all // tm, vt_rows, tm), BF16)],
        compiler_params=_cparams(("parallel", "parallel")),
        name="gqa_prep",
    )(pg, g_q, g_k, bd, tab_q, tab_k)


def _attn_kernel(q_ref, k_ref, vt_ref, o_ref, m_ref, acc_ref, a0_ref, a1_ref, p0_ref, p1_ref, s0_ref, s1_ref,
                 *, pairs, tq, tk, n_lat, ctx_len, ctx_queries):
    cb = ATTN_COL_BLOCK
    vr = VT_ROWS
    s_bufs, p_bufs, a_bufs = (s0_ref, s1_ref), (p0_ref, p1_ref), (a0_ref, a1_ref)

    def scores(r0, size, par):
        for pi, (_, units) in enumerate(pairs):
            for ui, (qb, kb) in enumerate(units):
                s_bufs[par][pi, :size, ui * tq:(ui + 1) * tq] = lax.dot_general(
                    k_ref[0, pl.ds(r0, size), kb * LANE:(kb + 1) * LANE],
                    q_ref[0, :, qb * LANE:(qb + 1) * LANE],
                    (((1,), (1,)), ((), ())), preferred_element_type=F32)

    def softmax(size, par):
        for pi in range(len(pairs)):
            for c in range(0, 2 * tq, cb):
                s = s_bufs[par][pi, :size, c:c + cb]
                m_prev = m_ref[pi, :, c:c + cb]
                m_next = jnp.maximum(m_prev, jnp.max(s, axis=0, keepdims=True))
                p_bufs[par][pi, :size, c:c + cb] = jnp.exp2((s - m_next).astype(BF16))
                a_bufs[par][pi, :, c:c + cb] = jnp.exp2(m_prev - m_next)
                m_ref[pi, :, c:c + cb] = m_next

    def values(u0, size, par):
        for pi, (vb, _) in enumerate(pairs):
            vt = jnp.concatenate([vt_ref[0, u0 + i, vb * vr:(vb + 1) * vr, :]
                                  for i in range(size // VT_TILE)], axis=1)
            pv = jnp.dot(vt, p_bufs[par][pi, :size, :], preferred_element_type=F32)
            acc_ref[pi] = a_bufs[par][pi] * acc_ref[pi] + pv

    m_ref[...] = jnp.full(m_ref.shape, -jnp.inf, F32)
    acc_ref[...] = jnp.zeros(acc_ref.shape, F32)
    tku = tk // VT_TILE
    ctx_u = n_lat // VT_TILE
    scores(n_lat, ctx_len, 0)
    if ctx_queries:
        softmax(ctx_len, 0)
        values(ctx_u, ctx_len, 0)
    else:
        n = n_lat // tk
        scores(0, tk, 1)
        softmax(ctx_len, 0)
        scores(tk, tk, 0)
        softmax(tk, 1)
        values(ctx_u, ctx_len, 0)

        def body(jj, carry):
            r = pl.multiple_of(jj * (2 * tk), 2 * tk)
            u = jj * (2 * tku)
            scores(r + 2 * tk, tk, 1)
            softmax(tk, 0)
            values(u, tk, 1)
            scores(r + 3 * tk, tk, 0)
            softmax(tk, 1)
            values(u + tku, tk, 0)
            return carry

        lax.fori_loop(0, (n - 2) // 2, body, 0)
        softmax(tk, 0)
        values((n - 2) * tku, tk, 1)
        values((n - 1) * tku, tk, 0)
    row = lax.broadcasted_iota(jnp.int32, (LANE, tq), 0)
    for pi in range(len(pairs)):
        acc = acc_ref[pi]
        o_t = acc[:LANE] / acc[LANE:LANE + 1]
        o_t = jnp.where(row < LANE // 2, o_t[:, :tq], o_t[:, tq:])
        o_ref[0, :, pi * LANE:(pi + 1) * LANE] = o_t.T.astype(o_ref.dtype)


def _attention(q, k, vt, *, groups, q_w, k_w, vt_rows, pairs, n_lat, ctx_queries, tq, tk):
    bsz, t_all, _ = q.shape
    ctx_len = t_all - n_lat
    out_w = len(pairs) * LANE
    rows = ctx_len if ctx_queries else n_lat
    q_tile0 = n_lat // tq if ctx_queries else 0
    assert rows % tq == 0 and n_lat % tq == 0 and n_lat % (2 * tk) == 0 and ctx_len <= tk
    assert (2 * tq) % ATTN_COL_BLOCK == 0 and tk % VT_TILE == 0 and ctx_len % VT_TILE == 0
    state = pltpu.VMEM((len(pairs), 1, 2 * tq), F32)
    acc = pltpu.VMEM((len(pairs), VT_ROWS, 2 * tq), F32)
    pbuf = pltpu.VMEM((len(pairs), tk, 2 * tq), BF16)
    sbuf = pltpu.VMEM((len(pairs), tk, 2 * tq), F32)
    return pl.pallas_call(
        functools.partial(_attn_kernel, pairs=pairs, tq=tq, tk=tk, n_lat=n_lat, ctx_len=ctx_len,
                          ctx_queries=ctx_queries),
        grid=(bsz, groups, rows // tq),
        in_specs=[
            pl.BlockSpec((1, tq, q_w), lambda b, g, t: (b, t + q_tile0, g)),
            pl.BlockSpec((1, t_all, k_w), lambda b, g, t: (b, 0, g)),
            pl.BlockSpec((1, t_all // VT_TILE, vt_rows, VT_TILE), lambda b, g, t: (b, 0, g, 0)),
        ],
        out_specs=pl.BlockSpec((1, tq, out_w), lambda b, g, t: (b, t, g)),
        out_shape=jax.ShapeDtypeStruct((bsz, rows, groups * out_w), BF16),
        scratch_shapes=[state, acc, state, state, pbuf, pbuf, sbuf, sbuf],
        compiler_params=_cparams(("parallel", "parallel", "arbitrary")),
        name="attention_ctx" if ctx_queries else "attention",
    )(q, k, vt)


def _ssd_chunk_index(i, n_chunks, ctx_chunks, reverse):
    if reverse:
        return n_chunks - 1 - i
    lat_chunks = n_chunks - ctx_chunks
    return jnp.where(i < ctx_chunks, lat_chunks + i, i - ctx_chunks)


def _ssd_kernel(*refs, reverse, n_chunks, ctx_chunks):
    if reverse:
        (xbc_ref, prev_ref, next_ref, dt_ref, cw_ref, cb_ref, dtb_ref, a_ref, tri_ref, ehd_ref,
         yf_ref, z_ref, dsk_ref, g_ref, y_ref, h_ref) = refs
    else:
        (xbc_ref, prev_ref, next_ref, dt_ref, cw_ref, cb_ref, dtb_ref, a_ref, tri_ref, ehd_ref,
         y_ref, h_ref) = refs
    q = SSD_CHUNK
    i = pl.program_id(1)
    ci = _ssd_chunk_index(i, n_chunks, ctx_chunks, reverse)

    @pl.when(i == 0)
    def _():
        h_ref[...] = jnp.zeros_like(h_ref)

    x = xbc_ref[0]
    lat_chunks = n_chunks - ctx_chunks
    at_start = jnp.logical_or(ci == 0, ci == lat_chunks)
    at_end = jnp.logical_or(ci == lat_chunks - 1, ci == n_chunks - 1)
    prev_row = jnp.where(at_start, 0.0, prev_ref[0, 7:8, :])
    next_row = jnp.where(at_end, 0.0, next_ref[0, 0:1, :])
    row = lax.broadcasted_iota(jnp.int32, (q, 1), 0)
    x_prev = jnp.where(row == 0, prev_row, pltpu.roll(x, 1, axis=0))
    x_next = jnp.where(row == q - 1, next_row, pltpu.roll(x, q - 1, axis=0))
    u = cw_ref[0:1, :] * x_prev + cw_ref[1:2, :] * x + cw_ref[2:3, :] * x_next + cb_ref[...]
    u = u * jax.nn.sigmoid(u)
    xs = u[:, :SSD_INNER]
    gs = SSD_STATE
    bm = [u[:, SSD_INNER + g * gs:SSD_INNER + (g + 1) * gs] for g in range(SSD_GROUPS)]
    cm = [u[:, SSD_INNER + (SSD_GROUPS + g) * gs:SSD_INNER + (SSD_GROUPS + g + 1) * gs]
          for g in range(SSD_GROUPS)]

    dt = jax.nn.softplus(dt_ref[0] + dtb_ref[...])
    da = dt * a_ref[...]
    a_cum = _tri_cumsum(tri_ref, da)
    end = 0 if reverse else q - 1
    a_tot = a_cum[end:end + 1, :]
    e_cum = jnp.exp(a_cum)
    dte = jnp.exp(a_tot - a_cum)
    dt_x = _split_dot(dt, ehd_ref[...], 2)
    ec_x = _split_dot(e_cum, ehd_ref[...], 2)
    dte_x = _split_dot(dte, ehd_ref[...], 2)
    xdt = xs * dt_x
    xdt_b = xdt.astype(BF16)
    xst_b = (xdt * dte_x).astype(BF16)
    a_cum_t = a_cum.T

    kk = lax.broadcasted_iota(jnp.int32, (q, q), 1)
    qq = lax.broadcasted_iota(jnp.int32, (q, q), 0)
    in_order = (kk >= qq) if reverse else (kk <= qq)
    lane = lax.broadcasted_iota(jnp.int32, (q, LANE), 1)
    off = SSD_HEADS if reverse else 0
    hpg = SSD_HEADS // SSD_GROUPS
    gw = hpg * SSD_HEAD_DIM
    y_parts = []
    for g in range(SSD_GROUPS):
        bm_b = bm[g].astype(BF16)
        cm_b = cm[g].astype(BF16)
        scores = lax.dot_general(cm_b, bm_b, (((1,), (1,)), ((), ())), preferred_element_type=F32)
        h_prev = h_ref[g]
        y_off = jnp.dot(cm_b, h_prev.astype(BF16), preferred_element_type=F32) * ec_x[:, g * gw:(g + 1) * gw]
        pair_out = []
        for j in range(hpg // 2):
            xp = xdt_b[:, g * gw + j * LANE:g * gw + (j + 1) * LANE]
            res = []
            for e in range(2):
                hd = g * hpg + 2 * j + e
                a_q = a_cum[:, off + hd:off + hd + 1]
                a_k = a_cum_t[off + hd:off + hd + 1, :]
                decay = jnp.exp(jnp.where(in_order, a_q - a_k, -jnp.inf))
                pmat = (scores * decay).astype(BF16)
                res.append(jnp.dot(pmat, xp, preferred_element_type=F32))
            pair_out.append(jnp.where(lane < SSD_HEAD_DIM, res[0], res[1]))
        y_parts.append(jnp.concatenate(pair_out, axis=1) + y_off)
        st = jnp.dot(bm[g].T.astype(BF16), xst_b[:, g * gw:(g + 1) * gw], preferred_element_type=F32)
        h_ref[g] = h_prev * ec_x[end:end + 1, g * gw:(g + 1) * gw] + st
    y = jnp.concatenate(y_parts, axis=1)

    if reverse:
        yt = yf_ref[0] + y + dsk_ref[...] * xs
        zz = z_ref[0]
        yt = yt * (zz * jax.nn.sigmoid(zz))
        yt = yt * lax.rsqrt(jnp.mean(yt * yt, axis=-1, keepdims=True) + NORM_EPS) * g_ref[...]
        y_ref[0] = yt.astype(y_ref.dtype)
    else:
        y_ref[0] = y


def _tri_cumsum(tri_ref, da):
    acc = None
    rem = da
    for _ in range(3):
        piece = rem.astype(BF16)
        term = jnp.dot(tri_ref[...], piece, preferred_element_type=F32)
        acc = term if acc is None else acc + term
        rem = rem - piece.astype(F32)
    return acc


def _ssd_direction(xbc, dt, conv_w, conv_b, dt_bias, a_row, tri, ehd, extra, *, ctx_len, reverse):
    bsz, t_all, cch = xbc.shape
    q = SSD_CHUNK
    n_chunks = t_all // q
    ctx_chunks = ctx_len // q
    halo = 8
    hb = q // halo
    n_halo = t_all // halo
    cidx = lambda i: _ssd_chunk_index(i, n_chunks, ctx_chunks, reverse)
    chunk = lambda w: pl.BlockSpec((1, q, w), lambda b, i: (b, cidx(i), 0))
    in_specs = [
        chunk(cch),
        pl.BlockSpec((1, halo, cch), lambda b, i: (b, jnp.maximum(cidx(i) * hb - 1, 0), 0)),
        pl.BlockSpec((1, halo, cch), lambda b, i: (b, jnp.minimum((cidx(i) + 1) * hb, n_halo - 1), 0)),
        chunk(LANE),
        _const_spec(conv_w.shape), _const_spec(conv_b.shape), _const_spec(dt_bias.shape),
        _const_spec(a_row.shape), _const_spec(tri.shape), _const_spec(ehd.shape),
    ]
    args = [xbc, xbc, xbc, dt, conv_w, conv_b, dt_bias, a_row, tri, ehd]
    if reverse:
        y_f, z, d_x, g_ssd = extra
        in_specs += [chunk(SSD_INNER), chunk(SSD_INNER), _const_spec(d_x.shape), _const_spec(g_ssd.shape)]
        args += [y_f, z, d_x, g_ssd]
    out_dtype = BF16 if reverse else F32
    hpg = SSD_HEADS // SSD_GROUPS
    return pl.pallas_call(
        functools.partial(_ssd_kernel, reverse=reverse, n_chunks=n_chunks, ctx_chunks=ctx_chunks),
        grid=(bsz, n_chunks),
        in_specs=in_specs,
        out_specs=chunk(SSD_INNER),
        out_shape=jax.ShapeDtypeStruct((bsz, t_all, SSD_INNER), out_dtype),
        scratch_shapes=[pltpu.VMEM((SSD_GROUPS, SSD_STATE, hpg * SSD_HEAD_DIM), F32)],
        compiler_params=_cparams(("parallel", "arbitrary")),
        name="ssd_bwd" if reverse else "ssd_fwd",
    )(*args)


def _post_kernel(x_ref, ym_ref, ys_ref, yg_ref, gate_ref, mod_ref, wo_ref, wi_ref, wf_ref,
                 l1g_ref, l1b_ref, l2g_ref, l2b_ref, o_ref, act_ref, *, tm, n_lat, th):
    is_ctx = _is_ctx_rows(tm, n_lat)
    d = D_MODEL
    mix = None
    row0 = 0
    for j, y_ref in enumerate((ym_ref, ys_ref, yg_ref)):
        kw = y_ref.shape[-1]
        br = jnp.dot(y_ref[0], wo_ref[row0:row0 + kw, :], preferred_element_type=F32)
        br = gate_ref[0, :, j * d:(j + 1) * d].astype(F32) * br
        mix = br if mix is None else mix + br
        row0 += kw
    g1 = _mod_rows(mod_ref, 2 * d, d, is_ctx)
    x1 = _layer_norm(DEEPNORM_ALPHA * x_ref[0] + g1 * mix) * l1g_ref[...] + l1b_ref[...]

    sh2 = _mod_rows(mod_ref, 3 * d, d, is_ctx)
    sc2 = _mod_rows(mod_ref, 4 * d, d, is_ctx)
    g2 = _mod_rows(mod_ref, 5 * d, d, is_ctx)
    h2 = (_layer_norm(x1) * (1.0 + sc2) + sh2).astype(BF16)
    hid = FFN_HIDDEN
    for c0 in range(0, hid, th):
        gt = jnp.dot(h2, wi_ref[:, c0:c0 + th], preferred_element_type=F32)
        up = jnp.dot(h2, wi_ref[:, hid + c0:hid + c0 + th], preferred_element_type=F32)
        act_ref[:, c0:c0 + th] = (gt * jax.nn.sigmoid(gt) * up).astype(BF16)
    fx = jnp.dot(act_ref[...], wf_ref[...], preferred_element_type=F32)
    o_ref[0] = _layer_norm(DEEPNORM_ALPHA * x1 + g2 * fx) * l2g_ref[...] + l2b_ref[...]


def _post_mixer(xa, y_mla, y_ssd, y_gqa, gate, mod, w_out, w_ffn_in, w_ffn_out, l1g, l1b, l2g, l2b,
                *, n_lat, tm, with_ctx):
    bsz, t_all, d = xa.shape
    row = lambda w: pl.BlockSpec((1, tm, w), lambda b, t: (b, t, 0))
    vec = _const_spec((1, d))
    t_out = t_all if with_ctx else n_lat
    return pl.pallas_call(
        functools.partial(_post_kernel, tm=tm, n_lat=n_lat, th=256),
        grid=(bsz, t_out // tm),
        in_specs=[row(d), row(MLA_OUT), row(SSD_INNER), row(GQA_OUT), row(GATE_IN), _const_spec(mod.shape),
                  _const_spec(w_out.shape), _const_spec(w_ffn_in.shape), _const_spec(w_ffn_out.shape),
                  vec, vec, vec, vec],
        out_specs=pl.BlockSpec((1, tm, d), lambda b, t: (b, t, 0)),
        out_shape=jax.ShapeDtypeStruct((bsz, t_out, d), F32),
        scratch_shapes=[pltpu.VMEM((tm, FFN_HIDDEN), BF16)],
        compiler_params=_cparams(("parallel", "parallel")),
        name="post_mixer",
    )(xa, y_mla, y_ssd, y_gqa, gate, mod, w_out, w_ffn_in, w_ffn_out, l1g, l1b, l2g, l2b)


def _rope_tables(n_lat, ctx_len, rot_dim, lane_dim, lane_off, scale):
    rows = n_lat // GRID_W
    row = jnp.repeat(jnp.arange(rows), GRID_W)
    col = jnp.tile(jnp.arange(GRID_W), rows)
    n_freq = rot_dim // 4
    inv_freq = ROPE_THETA ** (-jnp.arange(n_freq, dtype=F32) / n_freq)
    ang = jnp.concatenate([row[:, None] * inv_freq, col[:, None] * inv_freq], axis=-1)
    cos, sin = jnp.cos(ang), jnp.sin(ang)
    half = rot_dim // 2
    lane = np.arange(LANE)
    rel = (lane % lane_dim) - lane_off
    is_rot = (rel >= 0) & (rel < rot_dim)
    idx = np.where(is_rot, rel % half, 0)
    first = is_rot & (rel < half)
    second = is_rot & (rel >= half)
    c = jnp.where(is_rot[None, :], cos[:, idx], 1.0)
    s1 = jnp.where(first[None, :], -sin[:, idx], 0.0)
    s2 = jnp.where(second[None, :], sin[:, idx], 0.0)
    lat = jnp.stack([c, s1, s2])
    ctx = jnp.stack([jnp.ones((ctx_len, LANE), F32), jnp.zeros((ctx_len, LANE), F32),
                     jnp.zeros((ctx_len, LANE), F32)])
    return (jnp.concatenate([lat, ctx], axis=1) * scale).astype(F32)


def _pad_cols(w, width, offset=0):
    out = jnp.zeros((w.shape[0], width), w.dtype)
    return out.at[:, offset:offset + w.shape[1]].set(w)


def _layout_w_in(w):
    c = 0
    q_lat = w[:, c:c + MLA_Q_RANK]; c += MLA_Q_RANK
    kv_lat = w[:, c:c + MLA_KV_RANK]; c += MLA_KV_RANK
    k_rope = _pad_cols(w[:, c:c + MLA_ROPE], LANE, MLA_NOPE); c += MLA_ROPE
    z = w[:, c:c + SSD_INNER]; c += SSD_INNER
    xbc = w[:, c:c + SSD_CONV_CH]; c += SSD_CONV_CH
    dt = _pad_cols(w[:, c:c + 2 * SSD_HEADS], LANE); c += 2 * SSD_HEADS
    rest = w[:, c:]
    return jnp.concatenate([q_lat, kv_lat, k_rope, z, xbc, dt, rest], axis=1).astype(BF16)


def _layout_w_uq(w):
    w = w.reshape(w.shape[0], MLA_HEADS, MLA_QK)
    w = jnp.pad(w, ((0, 0), (0, 0), (0, LANE - MLA_QK)))
    return w.reshape(w.shape[0], MLA_HEADS * LANE).astype(BF16)


def _layout_w_ukv(w):
    w = w.reshape(w.shape[0], MLA_HEADS, MLA_NOPE + MLA_V)
    k = jnp.pad(w[:, :, :MLA_NOPE], ((0, 0), (0, 0), (0, LANE - MLA_NOPE)))
    v = w[:, :, MLA_NOPE:]
    return jnp.concatenate([k.reshape(w.shape[0], -1), v.reshape(w.shape[0], -1)], axis=1).astype(BF16)


def _lane_row(vec, width=LANE):
    return _pad_cols(vec.reshape(1, -1).astype(F32), width)


def kernel(x, c, ctx, c_ctx, w_mod, b_mod, w_in, b_gate, w_uq, g_q_mla, w_ukv, g_kv_mla, conv_w, conv_b,
           a_log, dt_bias, d_skip, g_ssd, g_q_gqa, g_k_gqa, w_out, ln1_g, ln1_b, w_ffn_in, w_ffn_out,
           ln2_g, ln2_b):
    bsz, n_lat, d = x.shape
    ctx_len = ctx.shape[1]
    depth = w_mod.shape[0]
    assert d == D_MODEL and bsz <= MOD_ROWS // 2
    tm = ROW_TILE
    assert ctx_len % tm == 0 and n_lat % tm == 0

    cvec = jnp.zeros((MOD_ROWS, d), F32).at[:bsz].set(c).at[MOD_ROWS // 2].set(c_ctx)
    mod_all = _modulation(cvec, w_mod, b_mod)

    mla_scale = MLA_QK ** -0.5 * LOG2E
    gqa_scale = GQA_HEAD_DIM ** -0.5 * LOG2E
    tab_mq = _rope_tables(n_lat, ctx_len, MLA_ROPE, LANE, MLA_NOPE, mla_scale)
    tab_mk = _rope_tables(n_lat, ctx_len, MLA_ROPE, LANE, MLA_NOPE, 1.0)
    tab_gq = _rope_tables(n_lat, ctx_len, GQA_HEAD_DIM, GQA_HEAD_DIM, 0, gqa_scale)
    tab_gk = _rope_tables(n_lat, ctx_len, GQA_HEAD_DIM, GQA_HEAD_DIM, 0, 1.0)

    qn = SSD_CHUNK
    r = np.arange(qn)
    tri_f = jnp.asarray((r[None, :] <= r[:, None]).astype(np.float32), BF16)
    tri_b = jnp.asarray((r[None, :] >= r[:, None]).astype(np.float32), BF16)
    lane1k = np.arange(SSD_INNER) // SSD_HEAD_DIM
    ehd_f = jnp.asarray((np.arange(LANE)[:, None] == lane1k[None, :]).astype(np.float32), BF16)
    ehd_b = jnp.asarray((np.arange(LANE)[:, None] == (lane1k[None, :] + SSD_HEADS)).astype(np.float32), BF16)
    blk = np.arange(GQA_OUT) // GQA_HEAD_DIM
    bd = jnp.asarray((blk[:, None] == blk[None, :]).astype(np.float32), BF16)

    mla_attn = functools.partial(_attention, groups=MLA_HEADS // 4, q_w=4 * LANE, k_w=4 * LANE,
                                 vt_rows=2 * VT_ROWS,
                                 pairs=((0, ((0, 0), (1, 1))), (1, ((2, 2), (3, 3)))), n_lat=n_lat)
    gqa_attn = functools.partial(_attention, groups=GQA_KV_HEADS, q_w=2 * LANE, k_w=2 * LANE, vt_rows=VT_ROWS,
                                 pairs=((0, ((0, 0), (0, 1))), (0, ((1, 0), (1, 1)))), n_lat=n_lat)

    def attend(fn, q, k, v, with_ctx):
        y = fn(q, k, v, ctx_queries=False, tq=ATTN_TQ, tk=ATTN_TK)
        if with_ctx:
            y_c = fn(q, k, v, ctx_queries=True, tq=ctx_len, tk=ATTN_TK)
            y = jnp.concatenate([y, y_c], axis=1)
        return y

    xa = jnp.concatenate([x, ctx], axis=1)
    for i in range(depth):
        with_ctx = i < depth - 1
        mod = mod_all[i]
        w_inp = _layout_w_in(w_in[i])
        pm, z, xbc, dtr, pg, gate = _in_projection(xa, mod, w_inp, b_gate[i].reshape(1, -1),
                                                   n_lat=n_lat, tm=tm)
        q_m, k_m, v_m = _mla_prep(pm, g_q_mla[i].reshape(1, -1), g_kv_mla[i].reshape(1, -1),
                                  _layout_w_uq(w_uq[i]), _layout_w_ukv(w_ukv[i]), tab_mq, tab_mk, tm=tm)
        y_mla = attend(mla_attn, q_m, k_m, v_m, with_ctx)
        q_g, k_g, v_g = _gqa_prep(pg, jnp.tile(g_q_gqa[i], GQA_HEADS).reshape(1, -1),
                                  jnp.tile(g_k_gqa[i], GQA_KV_HEADS).reshape(1, -1), bd, tab_gq, tab_gk, tm=tm)
        y_gqa = attend(gqa_attn, q_g, k_g, v_g, with_ctx)
        a = -jnp.exp(a_log[i].astype(F32))
        a_row = _lane_row(a.reshape(-1))
        dtb_row = _lane_row(dt_bias[i].reshape(-1))
        cb = conv_b[i].reshape(1, -1)
        y_f = _ssd_direction(xbc, dtr, conv_w[i], cb, dtb_row, a_row, tri_f, ehd_f, None,
                             ctx_len=ctx_len, reverse=False)
        d_x = jnp.repeat(d_skip[i], SSD_HEAD_DIM).reshape(1, -1)
        y_ssd = _ssd_direction(xbc, dtr, conv_w[i], cb, dtb_row, a_row, tri_b, ehd_b,
                               (y_f, z, d_x, g_ssd[i].reshape(1, -1)), ctx_len=ctx_len, reverse=True)
        xa = _post_mixer(xa, y_mla, y_ssd, y_gqa, gate, mod, w_out[i].astype(BF16),
                         w_ffn_in[i].astype(BF16), w_ffn_out[i].astype(BF16),
                         ln1_g[i].reshape(1, -1), ln1_b[i].reshape(1, -1),
                         ln2_g[i].reshape(1, -1), ln2_b[i].reshape(1, -1), n_lat=n_lat, tm=tm,
                         with_ctx=with_ctx)
    return xa
```

```python
import functools
import math

import jax
import jax.numpy as jnp
import numpy as np
from jax import lax
from jax.experimental import pallas as pl
from jax.experimental.pallas import tpu as pltpu

F32 = jnp.float32
BF16 = jnp.bfloat16

D_MODEL = 1024
DEPTH = 2
GRID_W = 64
ROPE_THETA = 10000.0
NORM_EPS = 1e-6

MLA_HEADS = 8
MLA_Q_RANK = 384
MLA_KV_RANK = 256
MLA_NOPE = 64
MLA_ROPE = 32
MLA_V = 64
MLA_QK = MLA_NOPE + MLA_ROPE

SSD_HEADS = 16
SSD_HEAD_DIM = 64
SSD_INNER = SSD_HEADS * SSD_HEAD_DIM
SSD_GROUPS = 2
SSD_STATE = 128
SSD_CHUNK = 128
SSD_CONV_CH = SSD_INNER + 2 * SSD_GROUPS * SSD_STATE

GQA_HEADS = 8
GQA_KV_HEADS = 2
GQA_HEAD_DIM = 64

MLA_IN = MLA_Q_RANK + MLA_KV_RANK + MLA_ROPE
SSD_IN = SSD_INNER + SSD_CONV_CH + 2 * SSD_HEADS
GQA_IN = (GQA_HEADS + 2 * GQA_KV_HEADS) * GQA_HEAD_DIM
GATE_IN = 3 * D_MODEL
MLA_OUT = MLA_HEADS * MLA_V
GQA_OUT = GQA_HEADS * GQA_HEAD_DIM
FFN_HIDDEN = -(-8 * D_MODEL // (3 * 256)) * 256
DEEPNORM_ALPHA = (2 * DEPTH) ** 0.25

LANE = 128
MOD_ROWS = 8
LOG2E = 1.4426950408889634

C_PM = 0
W_PM = MLA_Q_RANK + MLA_KV_RANK + LANE
C_Z = C_PM + W_PM
C_XBC = C_Z + SSD_INNER
C_DT = C_XBC + SSD_CONV_CH
C_PG = C_DT + LANE
C_GATE = C_PG + GQA_IN
W_INP = C_GATE + GATE_IN

VMEM_LIMIT = 56 * 1024 * 1024
ROW_TILE = 256
ATTN_TQ = 512
ATTN_TK = 512
ATTN_COL_BLOCK = 256
ATTN_KEY_BLOCK = 64
VT_TILE = ROW_TILE
VT_ROWS = 144


def _cparams(sem):
    return pltpu.CompilerParams(dimension_semantics=sem, vmem_limit_bytes=VMEM_LIMIT)


def _const_spec(shape):
    nd = len(shape)
    return pl.BlockSpec(shape, lambda *_: (0,) * nd, pipeline_mode=pl.Buffered(1))


def _layer_norm(x):
    mu = jnp.mean(x, axis=-1, keepdims=True)
    xc = x - mu
    var = jnp.mean(xc * xc, axis=-1, keepdims=True)
    return xc * lax.rsqrt(var + NORM_EPS)


def _split_dot(x, m01, parts):
    acc = None
    rem = x
    for _ in range(parts):
        piece = rem.astype(BF16)
        term = jnp.dot(piece, m01, preferred_element_type=F32)
        acc = term if acc is None else acc + term
        rem = rem - piece.astype(F32)
    return acc


def _mod_rows(mod_ref, col, width, is_ctx):
    b = pl.program_id(0)
    lat = mod_ref[pl.ds(b, 1), col:col + width]
    ctx = mod_ref[MOD_ROWS // 2:MOD_ROWS // 2 + 1, col:col + width]
    return jnp.where(is_ctx, ctx, lat)


def _is_ctx_rows(tm, n_lat):
    t = pl.program_id(1)
    rows = t * tm + lax.broadcasted_iota(jnp.int32, (tm, 1), 0)
    return rows >= n_lat


def _mod_kernel(c_ref, w_ref, b_ref, o_ref):
    c = c_ref[...]
    s = (c * jax.nn.sigmoid(c)).astype(BF16)
    o_ref[0] = jnp.dot(s, w_ref[0].astype(BF16), preferred_element_type=F32) + b_ref[0]


def _modulation(cvec, w_mod, b_mod):
    depth, d, n = w_mod.shape
    tn = 1024
    return pl.pallas_call(
        _mod_kernel,
        grid=(depth, n // tn),
        in_specs=[
            pl.BlockSpec((MOD_ROWS, d), lambda l, j: (0, 0)),
            pl.BlockSpec((1, d, tn), lambda l, j: (l, 0, j)),
            pl.BlockSpec((1, 1, tn), lambda l, j: (l, 0, j)),
        ],
        out_specs=pl.BlockSpec((1, MOD_ROWS, tn), lambda l, j: (l, 0, j)),
        out_shape=jax.ShapeDtypeStruct((depth, MOD_ROWS, n), F32),
        compiler_params=_cparams(("arbitrary", "arbitrary")),
        name="modulation",
    )(cvec, w_mod, b_mod.reshape(depth, 1, n))


def _inproj_kernel(x_ref, mod_ref, w_ref, bg_ref, pm_ref, z_ref, xbc_ref, dt_ref, pg_ref, gate_ref,
                   *, tm, n_lat):
    is_ctx = _is_ctx_rows(tm, n_lat)
    d = D_MODEL
    sh = _mod_rows(mod_ref, 0, d, is_ctx)
    sc = _mod_rows(mod_ref, d, d, is_ctx)
    h = (_layer_norm(x_ref[0]) * (1.0 + sc) + sh).astype(BF16)

    def proj(c0, width):
        return jnp.dot(h, w_ref[:, c0:c0 + width], preferred_element_type=F32)

    pm_ref[0] = proj(C_PM, W_PM)
    z_ref[0] = proj(C_Z, SSD_INNER)
    xbc_ref[0] = proj(C_XBC, SSD_CONV_CH)
    dt_ref[0] = proj(C_DT, LANE)
    pg_ref[0] = proj(C_PG, GQA_IN)
    for j in range(3):
        c0 = j * D_MODEL
        logits = proj(C_GATE + c0, D_MODEL) + bg_ref[:, c0:c0 + D_MODEL]
        gate_ref[0, :, c0:c0 + D_MODEL] = jax.nn.sigmoid(logits).astype(gate_ref.dtype)


def _in_projection(xa, mod, w_inp, b_gate, *, n_lat, tm):
    bsz, t_all, d = xa.shape
    row = lambda w: pl.BlockSpec((1, tm, w), lambda b, t: (b, t, 0))
    out_w = (W_PM, SSD_INNER, SSD_CONV_CH, LANE, GQA_IN, GATE_IN)
    out_dt = (F32, F32, F32, F32, F32, BF16)
    return pl.pallas_call(
        functools.partial(_inproj_kernel, tm=tm, n_lat=n_lat),
        grid=(bsz, t_all // tm),
        in_specs=[row(d), _const_spec(mod.shape), _const_spec(w_inp.shape), _const_spec(b_gate.shape)],
        out_specs=[row(w) for w in out_w],
        out_shape=[jax.ShapeDtypeStruct((bsz, t_all, w), dt) for w, dt in zip(out_w, out_dt)],
        compiler_params=_cparams(("parallel", "parallel")),
        name="in_projection",
    )(xa, mod, w_inp, b_gate)


def _rope_lanes(x, c, s_fwd, s_bwd, half):
    n = x.shape[-1]
    return x * c + pltpu.roll(x, n - half, axis=1) * s_fwd + pltpu.roll(x, half, axis=1) * s_bwd


def _mla_prep_kernel(pm_ref, gq_ref, gkv_ref, wuq_ref, wukv_ref, tq_ref, tk_ref, q_ref, k_ref, v_ref):
    pm = pm_ref[0]
    pq = pm[:, :MLA_Q_RANK]
    ckv = pm[:, MLA_Q_RANK:MLA_Q_RANK + MLA_KV_RANK]
    kr = pm[:, MLA_Q_RANK + MLA_KV_RANK:]

    def rms(x, g):
        return x * lax.rsqrt(jnp.mean(x * x, axis=-1, keepdims=True) + NORM_EPS) * g

    q = jnp.dot(rms(pq, gq_ref[...]).astype(BF16), wuq_ref[...], preferred_element_type=F32)
    kv = jnp.dot(rms(ckv, gkv_ref[...]).astype(BF16), wukv_ref[...], preferred_element_type=F32)
    half = MLA_ROPE // 2
    kr = _rope_lanes(kr, tk_ref[0], tk_ref[1], tk_ref[2], half)
    cq, sq1, sq2 = tq_ref[0], tq_ref[1], tq_ref[2]
    for hd in range(MLA_HEADS):
        sl = slice(hd * LANE, (hd + 1) * LANE)
        q_ref[0, :, sl] = _rope_lanes(q[:, sl], cq, sq1, sq2, half).astype(q_ref.dtype)
        k_ref[0, :, sl] = (kv[:, sl] + kr).astype(k_ref.dtype)
    ones = jnp.ones((VT_ROWS - LANE, pm.shape[0]), v_ref.dtype)
    for j in range(MLA_HEADS // 2):
        c0 = MLA_HEADS * LANE + j * LANE
        v_ref[0, 0, j * VT_ROWS:j * VT_ROWS + LANE, :] = kv[:, c0:c0 + LANE].T.astype(v_ref.dtype)
        v_ref[0, 0, j * VT_ROWS + LANE:(j + 1) * VT_ROWS, :] = ones


def _mla_prep(pm, g_q, g_kv, wuq_p, wukv_p, tab_q, tab_k, *, tm):
    bsz, t_all, _ = pm.shape
    row = lambda w: pl.BlockSpec((1, tm, w), lambda b, t: (b, t, 0))
    tab = pl.BlockSpec((3, tm, LANE), lambda b, t: (0, t, 0))
    hw = MLA_HEADS * LANE
    vt_rows = (MLA_HEADS // 2) * VT_ROWS
    assert tm == VT_TILE
    return pl.pallas_call(
        _mla_prep_kernel,
        grid=(bsz, t_all // tm),
        in_specs=[row(W_PM), _const_spec(g_q.shape), _const_spec(g_kv.shape),
                  _const_spec(wuq_p.shape), _const_spec(wukv_p.shape), tab, tab],
        out_specs=[row(hw), row(hw), pl.BlockSpec((1, 1, vt_rows, tm), lambda b, t: (b, t, 0, 0))],
        out_shape=[jax.ShapeDtypeStruct((bsz, t_all, hw), BF16),
                   jax.ShapeDtypeStruct((bsz, t_all, hw), BF16),
                   jax.ShapeDtypeStruct((bsz, t_all // tm, vt_rows, tm), BF16)],
        compiler_params=_cparams(("parallel", "parallel")),
        name="mla_prep",
    )(pm, g_q, g_kv, wuq_p, wukv_p, tab_q, tab_k)


def _gqa_prep_kernel(pg_ref, gq_ref, gk_ref, bd_ref, tq_ref, tk_ref, q_ref, k_ref, v_ref):
    pg = pg_ref[0]
    hd = GQA_HEAD_DIM
    q = pg[:, :GQA_OUT]
    k = pg[:, GQA_OUT:GQA_OUT + LANE]
    v = pg[:, GQA_OUT + LANE:]
    half = hd // 2

    def head_rms(x, g):
        w = x.shape[-1]
        ms = _split_dot(x * x, bd_ref[:w, :w], 2) * (1.0 / hd)
        return x * lax.rsqrt(ms + NORM_EPS) * g

    qn = head_rms(q, gq_ref[...])
    for j in range(GQA_OUT // LANE):
        sl = slice(j * LANE, (j + 1) * LANE)
        q_ref[0, :, sl] = _rope_lanes(qn[:, sl], tq_ref[0], tq_ref[1], tq_ref[2], half).astype(q_ref.dtype)
    kn = head_rms(k, gk_ref[...])
    kn = _rope_lanes(kn, tk_ref[0], tk_ref[1], tk_ref[2], half)
    lane = lax.broadcasted_iota(jnp.int32, kn.shape, 1)
    low = lane < hd
    kn_sw = pltpu.roll(kn, hd, axis=1)
    zero = jnp.zeros_like(kn)
    k_ref[0, :, 0 * LANE:1 * LANE] = jnp.where(low, kn, zero).astype(k_ref.dtype)
    k_ref[0, :, 1 * LANE:2 * LANE] = jnp.where(low, zero, kn_sw).astype(k_ref.dtype)
    k_ref[0, :, 2 * LANE:3 * LANE] = jnp.where(low, kn_sw, zero).astype(k_ref.dtype)
    k_ref[0, :, 3 * LANE:4 * LANE] = jnp.where(low, zero, kn).astype(k_ref.dtype)
    v_t = v.T.astype(v_ref.dtype)
    ones = jnp.ones((VT_ROWS - LANE, v.shape[0]), v_ref.dtype)
    for g in range(GQA_KV_HEADS):
        vg = v_t[g * hd:(g + 1) * hd]
        v_ref[0, 0, g * VT_ROWS:g * VT_ROWS + hd, :] = vg
        v_ref[0, 0, g * VT_ROWS + hd:g * VT_ROWS + LANE, :] = vg
        v_ref[0, 0, g * VT_ROWS + LANE:(g + 1) * VT_ROWS, :] = ones


def _gqa_prep(pg, g_q, g_k, bd, tab_q, tab_k, *, tm):
    bsz, t_all, _ = pg.shape
    row = lambda w: pl.BlockSpec((1, tm, w), lambda b, t: (b, t, 0))
    tab = pl.BlockSpec((3, tm, LANE), lambda b, t: (0, t, 0))
    kw = GQA_KV_HEADS * 2 * LANE
    vt_rows = GQA_KV_HEADS * VT_ROWS
    assert tm == VT_TILE
    return pl.pallas_call(
        _gqa_prep_kernel,
        grid=(bsz, t_all // tm),
        in_specs=[row(GQA_IN), _const_spec(g_q.shape), _const_spec(g_k.shape), _const_spec(bd.shape), tab, tab],
        out_specs=[row(GQA_OUT), row(kw), pl.BlockSpec((1, 1, vt_rows, tm), lambda b, t: (b, t, 0, 0))],
        out_shape=[jax.ShapeDtypeStruct((bsz, t_all, GQA_OUT), BF16),
                   jax.ShapeDtypeStruct((bsz, t_all, kw), BF16),
                   jax.ShapeDtypeStruct((bsz, t_all // tm, vt_rows, tm), BF16)],
        compiler_params=_cparams(("parallel", "parallel")),
        name="gqa_prep",
    )(pg, g_q, g_k, bd, tab_q, tab_k)


def _attn_kernel(q_ref, k_ref, vt_ref, o_ref, m_ref, acc_ref, a0_ref, a1_ref, c0_ref, c1_ref,
                 p0_ref, p1_ref, s0_ref, s1_ref, *, pairs, tq, tk, n_lat, ctx_len, ctx_queries):
    cb = ATTN_COL_BLOCK
    vr = VT_ROWS
    s_bufs, p_bufs, a_bufs, c_bufs = (s0_ref, s1_ref), (p0_ref, p1_ref), (a0_ref, a1_ref), (c0_ref, c1_ref)

    def scores(r0, size, par):
        for pi, (_, units) in enumerate(pairs):
            for ui, (qb, kb) in enumerate(units):
                s = lax.dot_general(
                    k_ref[0, pl.ds(r0, size), kb * LANE:(kb + 1) * LANE],
                    q_ref[0, :, qb * LANE:(qb + 1) * LANE],
                    (((1,), (1,)), ((), ())), preferred_element_type=F32)
                s_bufs[par][pi, :size, ui * tq:(ui + 1) * tq] = s
                c_bufs[par][pi, :, ui * tq:(ui + 1) * tq] = jnp.max(s, axis=0, keepdims=True)

    def softmax(size, par):
        for pi in range(len(pairs)):
            for c in range(0, 2 * tq, cb):
                m_prev = m_ref[pi, :, c:c + cb]
                m_next = jnp.maximum(m_prev, c_bufs[par][pi, :, c:c + cb])
                for r in range(0, size, ATTN_KEY_BLOCK):
                    blk = s_bufs[par][pi, r:r + ATTN_KEY_BLOCK, c:c + cb]
                    p_bufs[par][pi, r:r + ATTN_KEY_BLOCK, c:c + cb] = jnp.exp2((blk - m_next).astype(BF16))
                a_bufs[par][pi, :, c:c + cb] = jnp.exp2(m_prev - m_next)
                m_ref[pi, :, c:c + cb] = m_next

    def values(u0, size, par):
        for pi, (vb, _) in enumerate(pairs):
            vt = jnp.concatenate([vt_ref[0, u0 + i, vb * vr:(vb + 1) * vr, :]
                                  for i in range(size // VT_TILE)], axis=1)
            pv = jnp.dot(vt, p_bufs[par][pi, :size, :], preferred_element_type=F32)
            acc_ref[pi] = a_bufs[par][pi] * acc_ref[pi] + pv

    m_ref[...] = jnp.full(m_ref.shape, -jnp.inf, F32)
    acc_ref[...] = jnp.zeros(acc_ref.shape, F32)
    tku = tk // VT_TILE
    ctx_u = n_lat // VT_TILE
    scores(n_lat, ctx_len, 0)
    if ctx_queries:
        softmax(ctx_len, 0)
        values(ctx_u, ctx_len, 0)
    else:
        n = n_lat // tk
        scores(0, tk, 1)
        softmax(ctx_len, 0)
        scores(tk, tk, 0)
        softmax(tk, 1)
        values(ctx_u, ctx_len, 0)

        def body(jj, carry):
            r = pl.multiple_of(jj * (2 * tk), 2 * tk)
            u = jj * (2 * tku)
            scores(r + 2 * tk, tk, 1)
            softmax(tk, 0)
            values(u, tk, 1)
            scores(r + 3 * tk, tk, 0)
            softmax(tk, 1)
            values(u + tku, tk, 0)
            return carry

        lax.fori_loop(0, (n - 2) // 2, body, 0)
        softmax(tk, 0)
        values((n - 2) * tku, tk, 1)
        values((n - 1) * tku, tk, 0)
    row = lax.broadcasted_iota(jnp.int32, (LANE, tq), 0)
    for pi in range(len(pairs)):
        acc = acc_ref[pi]
        o_t = acc[:LANE] / acc[LANE:LANE + 1]
        o_t = jnp.where(row < LANE // 2, o_t[:, :tq], o_t[:, tq:])
        o_ref[0, :, pi * LANE:(pi + 1) * LANE] = o_t.T.astype(o_ref.dtype)


def _attention(q, k, vt, *, groups, q_w, k_w, vt_rows, pairs, n_lat, ctx_queries, tq, tk):
    bsz, t_all, _ = q.shape
    ctx_len = t_all - n_lat
    out_w = len(pairs) * LANE
    rows = ctx_len if ctx_queries else n_lat
    q_tile0 = n_lat // tq if ctx_queries else 0
    assert rows % tq == 0 and n_lat % tq == 0 and n_lat % (2 * tk) == 0 and ctx_len <= tk
    assert (2 * tq) % ATTN_COL_BLOCK == 0 and tk % VT_TILE == 0 and ctx_len % VT_TILE == 0
    state = pltpu.VMEM((len(pairs), 1, 2 * tq), F32)
    acc = pltpu.VMEM((len(pairs), VT_ROWS, 2 * tq), F32)
    pbuf = pltpu.VMEM((len(pairs), tk, 2 * tq), BF16)
    sbuf = pltpu.VMEM((len(pairs), tk, 2 * tq), F32)
    return pl.pallas_call(
        functools.partial(_attn_kernel, pairs=pairs, tq=tq, tk=tk, n_lat=n_lat, ctx_len=ctx_len,
                          ctx_queries=ctx_queries),
        grid=(bsz, groups, rows // tq),
        in_specs=[
            pl.BlockSpec((1, tq, q_w), lambda b, g, t: (b, t + q_tile0, g)),
            pl.BlockSpec((1, t_all, k_w), lambda b, g, t: (b, 0, g)),
            pl.BlockSpec((1, t_all // VT_TILE, vt_rows, VT_TILE), lambda b, g, t: (b, 0, g, 0)),
        ],
        out_specs=pl.BlockSpec((1, tq, out_w), lambda b, g, t: (b, t, g)),
        out_shape=jax.ShapeDtypeStruct((bsz, rows, groups * out_w), BF16),
        scratch_shapes=[state, acc, state, state, state, state, pbuf, pbuf, sbuf, sbuf],
        compiler_params=_cparams(("parallel", "parallel", "arbitrary")),
        name="attention_ctx" if ctx_queries else "attention",
    )(q, k, vt)


def _ssd_chunk_index(i, n_chunks, ctx_chunks, reverse):
    if reverse:
        return n_chunks - 1 - i
    lat_chunks = n_chunks - ctx_chunks
    return jnp.where(i < ctx_chunks, lat_chunks + i, i - ctx_chunks)


def _ssd_kernel(*refs, reverse, n_chunks, ctx_chunks):
    if reverse:
        (xbc_ref, prev_ref, next_ref, dt_ref, cw_ref, cb_ref, dtb_ref, a_ref, tri_ref, ehd_ref,
         yf_ref, z_ref, dsk_ref, g_ref, y_ref, h_ref) = refs
    else:
        (xbc_ref, prev_ref, next_ref, dt_ref, cw_ref, cb_ref, dtb_ref, a_ref, tri_ref, ehd_ref,
         y_ref, h_ref) = refs
    q = SSD_CHUNK
    i = pl.program_id(1)
    ci = _ssd_chunk_index(i, n_chunks, ctx_chunks, reverse)

    @pl.when(i == 0)
    def _():
        h_ref[...] = jnp.zeros_like(h_ref)

    x = xbc_ref[0]
    lat_chunks = n_chunks - ctx_chunks
    at_start = jnp.logical_or(ci == 0, ci == lat_chunks)
    at_end = jnp.logical_or(ci == lat_chunks - 1, ci == n_chunks - 1)
    prev_row = jnp.where(at_start, 0.0, prev_ref[0, 7:8, :])
    next_row = jnp.where(at_end, 0.0, next_ref[0, 0:1, :])
    row = lax.broadcasted_iota(jnp.int32, (q, 1), 0)
    x_prev = jnp.where(row == 0, prev_row, pltpu.roll(x, 1, axis=0))
    x_next = jnp.where(row == q - 1, next_row, pltpu.roll(x, q - 1, axis=0))
    u = cw_ref[0:1, :] * x_prev + cw_ref[1:2, :] * x + cw_ref[2:3, :] * x_next + cb_ref[...]
    u = u * jax.nn.sigmoid(u)
    xs = u[:, :SSD_INNER]
    gs = SSD_STATE
    bm = [u[:, SSD_INNER + g * gs:SSD_INNER + (g + 1) * gs] for g in range(SSD_GROUPS)]
    cm = [u[:, SSD_INNER + (SSD_GROUPS + g) * gs:SSD_INNER + (SSD_GROUPS + g + 1) * gs]
          for g in range(SSD_GROUPS)]

    dt = jax.nn.softplus(dt_ref[0] + dtb_ref[...])
    da = dt * a_ref[...]
    a_cum = _tri_cumsum(tri_ref, da)
    end = 0 if reverse else q - 1
    a_tot = a_cum[end:end + 1, :]
    e_cum = jnp.exp(a_cum)
    dte = jnp.exp(a_tot - a_cum)
    dt_x = _split_dot(dt, ehd_ref[...], 2)
    ec_x = _split_dot(e_cum, ehd_ref[...], 2)
    dte_x = _split_dot(dte, ehd_ref[...], 2)
    xdt = xs * dt_x
    xdt_b = xdt.astype(BF16)
    xst_b = (xdt * dte_x).astype(BF16)
    a_cum_t = a_cum.T

    kk = lax.broadcasted_iota(jnp.int32, (q, q), 1)
    qq = lax.broadcasted_iota(jnp.int32, (q, q), 0)
    in_order = (kk >= qq) if reverse else (kk <= qq)
    lane = lax.broadcasted_iota(jnp.int32, (q, LANE), 1)
    off = SSD_HEADS if reverse else 0
    hpg = SSD_HEADS // SSD_GROUPS
    gw = hpg * SSD_HEAD_DIM
    y_parts = []
    for g in range(SSD_GROUPS):
        bm_b = bm[g].astype(BF16)
        cm_b = cm[g].astype(BF16)
        scores = lax.dot_general(cm_b, bm_b, (((1,), (1,)), ((), ())), preferred_element_type=F32)
        h_prev = h_ref[g]
        y_off = jnp.dot(cm_b, h_prev.astype(BF16), preferred_element_type=F32) * ec_x[:, g * gw:(g + 1) * gw]
        pair_out = []
        for j in range(hpg // 2):
            xp = xdt_b[:, g * gw + j * LANE:g * gw + (j + 1) * LANE]
            res = []
            for e in range(2):
                hd = g * hpg + 2 * j + e
                a_q = a_cum[:, off + hd:off + hd + 1]
                a_k = a_cum_t[off + hd:off + hd + 1, :]
                decay = jnp.exp(jnp.where(in_order, a_q - a_k, -jnp.inf))
                pmat = (scores * decay).astype(BF16)
                res.append(jnp.dot(pmat, xp, preferred_element_type=F32))
            pair_out.append(jnp.where(lane < SSD_HEAD_DIM, res[0], res[1]))
        y_parts.append(jnp.concatenate(pair_out, axis=1) + y_off)
        st = jnp.dot(bm[g].T.astype(BF16), xst_b[:, g * gw:(g + 1) * gw], preferred_element_type=F32)
        h_ref[g] = h_prev * ec_x[end:end + 1, g * gw:(g + 1) * gw] + st
    y = jnp.concatenate(y_parts, axis=1)

    if reverse:
        yt = yf_ref[0] + y + dsk_ref[...] * xs
        zz = z_ref[0]
        yt = yt * (zz * jax.nn.sigmoid(zz))
        yt = yt * lax.rsqrt(jnp.mean(yt * yt, axis=-1, keepdims=True) + NORM_EPS) * g_ref[...]
        y_ref[0] = yt.astype(y_ref.dtype)
    else:
        y_ref[0] = y


def _tri_cumsum(tri_ref, da):
    acc = None
    rem = da
    for _ in range(3):
        piece = rem.astype(BF16)
        term = jnp.dot(tri_ref[...], piece, preferred_element_type=F32)
        acc = term if acc is None else acc + term
        rem = rem - piece.astype(F32)
    return acc


def _ssd_direction(xbc, dt, conv_w, conv_b, dt_bias, a_row, tri, ehd, extra, *, ctx_len, reverse):
    bsz, t_all, cch = xbc.shape
    q = SSD_CHUNK
    n_chunks = t_all // q
    ctx_chunks = ctx_len // q
    halo = 8
    hb = q // halo
    n_halo = t_all // halo
    cidx = lambda i: _ssd_chunk_index(i, n_chunks, ctx_chunks, reverse)
    chunk = lambda w: pl.BlockSpec((1, q, w), lambda b, i: (b, cidx(i), 0))
    in_specs = [
        chunk(cch),
        pl.BlockSpec((1, halo, cch), lambda b, i: (b, jnp.maximum(cidx(i) * hb - 1, 0), 0)),
        pl.BlockSpec((1, halo, cch), lambda b, i: (b, jnp.minimum((cidx(i) + 1) * hb, n_halo - 1), 0)),
        chunk(LANE),
        _const_spec(conv_w.shape), _const_spec(conv_b.shape), _const_spec(dt_bias.shape),
        _const_spec(a_row.shape), _const_spec(tri.shape), _const_spec(ehd.shape),
    ]
    args = [xbc, xbc, xbc, dt, conv_w, conv_b, dt_bias, a_row, tri, ehd]
    if reverse:
        y_f, z, d_x, g_ssd = extra
        in_specs += [chunk(SSD_INNER), chunk(SSD_INNER), _const_spec(d_x.shape), _const_spec(g_ssd.shape)]
        args += [y_f, z, d_x, g_ssd]
    out_dtype = BF16 if reverse else F32
    hpg = SSD_HEADS // SSD_GROUPS
    return pl.pallas_call(
        functools.partial(_ssd_kernel, reverse=reverse, n_chunks=n_chunks, ctx_chunks=ctx_chunks),
        grid=(bsz, n_chunks),
        in_specs=in_specs,
        out_specs=chunk(SSD_INNER),
        out_shape=jax.ShapeDtypeStruct((bsz, t_all, SSD_INNER), out_dtype),
        scratch_shapes=[pltpu.VMEM((SSD_GROUPS, SSD_STATE, hpg * SSD_HEAD_DIM), F32)],
        compiler_params=_cparams(("parallel", "arbitrary")),
        name="ssd_bwd" if reverse else "ssd_fwd",
    )(*args)


def _post_kernel(x_ref, ym_ref, ys_ref, yg_ref, gate_ref, mod_ref, wo_ref, wi_ref, wf_ref,
                 l1g_ref, l1b_ref, l2g_ref, l2b_ref, o_ref, act_ref, *, tm, n_lat, th):
    is_ctx = _is_ctx_rows(tm, n_lat)
    d = D_MODEL
    mix = None
    row0 = 0
    for j, y_ref in enumerate((ym_ref, ys_ref, yg_ref)):
        kw = y_ref.shape[-1]
        br = jnp.dot(y_ref[0], wo_ref[row0:row0 + kw, :], preferred_element_type=F32)
        br = gate_ref[0, :, j * d:(j + 1) * d].astype(F32) * br
        mix = br if mix is None else mix + br
        row0 += kw
    g1 = _mod_rows(mod_ref, 2 * d, d, is_ctx)
    x1 = _layer_norm(DEEPNORM_ALPHA * x_ref[0] + g1 * mix) * l1g_ref[...] + l1b_ref[...]

    sh2 = _mod_rows(mod_ref, 3 * d, d, is_ctx)
    sc2 = _mod_rows(mod_ref, 4 * d, d, is_ctx)
    g2 = _mod_rows(mod_ref, 5 * d, d, is_ctx)
    h2 = (_layer_norm(x1) * (1.0 + sc2) + sh2).astype(BF16)
    hid = FFN_HIDDEN
    for c0 in range(0, hid, th):
        gt = jnp.dot(h2, wi_ref[:, c0:c0 + th], preferred_element_type=F32)
        up = jnp.dot(h2, wi_ref[:, hid + c0:hid + c0 + th], preferred_element_type=F32)
        act_ref[:, c0:c0 + th] = (gt * jax.nn.sigmoid(gt) * up).astype(BF16)
    fx = jnp.dot(act_ref[...], wf_ref[...], preferred_element_type=F32)
    o_ref[0] = _layer_norm(DEEPNORM_ALPHA * x1 + g2 * fx) * l2g_ref[...] + l2b_ref[...]


def _post_mixer(xa, y_mla, y_ssd, y_gqa, gate, mod, w_out, w_ffn_in, w_ffn_out, l1g, l1b, l2g, l2b,
                *, n_lat, tm, with_ctx):
    bsz, t_all, d = xa.shape
    row = lambda w: pl.BlockSpec((1, tm, w), lambda b, t: (b, t, 0))
    vec = _const_spec((1, d))
    t_out = t_all if with_ctx else n_lat
    return pl.pallas_call(
        functools.partial(_post_kernel, tm=tm, n_lat=n_lat, th=256),
        grid=(bsz, t_out // tm),
        in_specs=[row(d), row(MLA_OUT), row(SSD_INNER), row(GQA_OUT), row(GATE_IN), _const_spec(mod.shape),
                  _const_spec(w_out.shape), _const_spec(w_ffn_in.shape), _const_spec(w_ffn_out.shape),
                  vec, vec, vec, vec],
        out_specs=pl.BlockSpec((1, tm, d), lambda b, t: (b, t, 0)),
        out_shape=jax.ShapeDtypeStruct((bsz, t_out, d), F32),
        scratch_shapes=[pltpu.VMEM((tm, FFN_HIDDEN), BF16)],
        compiler_params=_cparams(("parallel", "parallel")),
        name="post_mixer",
    )(xa, y_mla, y_ssd, y_gqa, gate, mod, w_out, w_ffn_in, w_ffn_out, l1g, l1b, l2g, l2b)


def _rope_tables(n_lat, ctx_len, rot_dim, lane_dim, lane_off, scale):
    rows = n_lat // GRID_W
    row = jnp.repeat(jnp.arange(rows), GRID_W)
    col = jnp.tile(jnp.arange(GRID_W), rows)
    n_freq = rot_dim // 4
    inv_freq = ROPE_THETA ** (-jnp.arange(n_freq, dtype=F32) / n_freq)
    ang = jnp.concatenate([row[:, None] * inv_freq, col[:, None] * inv_freq], axis=-1)
    cos, sin = jnp.cos(ang), jnp.sin(ang)
    half = rot_dim // 2
    lane = np.arange(LANE)
    rel = (lane % lane_dim) - lane_off
    is_rot = (rel >= 0) & (rel < rot_dim)
    idx = np.where(is_rot, rel % half, 0)
    first = is_rot & (rel < half)
    second = is_rot & (rel >= half)
    c = jnp.where(is_rot[None, :], cos[:, idx], 1.0)
    s1 = jnp.where(first[None, :], -sin[:, idx], 0.0)
    s2 = jnp.where(second[None, :], sin[:, idx], 0.0)
    lat = jnp.stack([c, s1, s2])
    ctx = jnp.stack([jnp.ones((ctx_len, LANE), F32), jnp.zeros((ctx_len, LANE), F32),
                     jnp.zeros((ctx_len, LANE), F32)])
    return (jnp.concatenate([lat, ctx], axis=1) * scale).astype(F32)


def _pad_cols(w, width, offset=0):
    out = jnp.zeros((w.shape[0], width), w.dtype)
    return out.at[:, offset:offset + w.shape[1]].set(w)


def _layout_w_in(w):
    c = 0
    q_lat = w[:, c:c + MLA_Q_RANK]; c += MLA_Q_RANK
    kv_lat = w[:, c:c + MLA_KV_RANK]; c += MLA_KV_RANK
    k_rope = _pad_cols(w[:, c:c + MLA_ROPE], LANE, MLA_NOPE); c += MLA_ROPE
    z = w[:, c:c + SSD_INNER]; c += SSD_INNER
    xbc = w[:, c:c + SSD_CONV_CH]; c += SSD_CONV_CH
    dt = _pad_cols(w[:, c:c + 2 * SSD_HEADS], LANE); c += 2 * SSD_HEADS
    rest = w[:, c:]
    return jnp.concatenate([q_lat, kv_lat, k_rope, z, xbc, dt, rest], axis=1).astype(BF16)


def _layout_w_uq(w):
    w = w.reshape(w.shape[0], MLA_HEADS, MLA_QK)
    w = jnp.pad(w, ((0, 0), (0, 0), (0, LANE - MLA_QK)))
    return w.reshape(w.shape[0], MLA_HEADS * LANE).astype(BF16)


def _layout_w_ukv(w):
    w = w.reshape(w.shape[0], MLA_HEADS, MLA_NOPE + MLA_V)
    k = jnp.pad(w[:, :, :MLA_NOPE], ((0, 0), (0, 0), (0, LANE - MLA_NOPE)))
    v = w[:, :, MLA_NOPE:]
    return jnp.concatenate([k.reshape(w.shape[0], -1), v.reshape(w.shape[0], -1)], axis=1).astype(BF16)


def _lane_row(vec, width=LANE):
    return _pad_cols(vec.reshape(1, -1).astype(F32), width)


def kernel(x, c, ctx, c_ctx, w_mod, b_mod, w_in, b_gate, w_uq, g_q_mla, w_ukv, g_kv_mla, conv_w, conv_b,
           a_log, dt_bias, d_skip, g_ssd, g_q_gqa, g_k_gqa, w_out, ln1_g, ln1_b, w_ffn_in, w_ffn_out,
           ln2_g, ln2_b):
    bsz, n_lat, d = x.shape
    ctx_len = ctx.shape[1]
    depth = w_mod.shape[0]
    assert d == D_MODEL and bsz <= MOD_ROWS // 2
    tm = ROW_TILE
    assert ctx_len % tm == 0 and n_lat % tm == 0

    cvec = jnp.zeros((MOD_ROWS, d), F32).at[:bsz].set(c).at[MOD_ROWS // 2].set(c_ctx)
    mod_all = _modulation(cvec, w_mod, b_mod)

    mla_scale = MLA_QK ** -0.5 * LOG2E
    gqa_scale = GQA_HEAD_DIM ** -0.5 * LOG2E
    tab_mq = _rope_tables(n_lat, ctx_len, MLA_ROPE, LANE, MLA_NOPE, mla_scale)
    tab_mk = _rope_tables(n_lat, ctx_len, MLA_ROPE, LANE, MLA_NOPE, 1.0)
    tab_gq = _rope_tables(n_lat, ctx_len, GQA_HEAD_DIM, GQA_HEAD_DIM, 0, gqa_scale)
    tab_gk = _rope_tables(n_lat, ctx_len, GQA_HEAD_DIM, GQA_HEAD_DIM, 0, 1.0)

    qn = SSD_CHUNK
    r = np.arange(qn)
    tri_f = jnp.asarray((r[None, :] <= r[:, None]).astype(np.float32), BF16)
    tri_b = jnp.asarray((r[None, :] >= r[:, None]).astype(np.float32), BF16)
    lane1k = np.arange(SSD_INNER) // SSD_HEAD_DIM
    ehd_f = jnp.asarray((np.arange(LANE)[:, None] == lane1k[None, :]).astype(np.float32), BF16)
    ehd_b = jnp.asarray((np.arange(LANE)[:, None] == (lane1k[None, :] + SSD_HEADS)).astype(np.float32), BF16)
    blk = np.arange(GQA_OUT) // GQA_HEAD_DIM
    bd = jnp.asarray((blk[:, None] == blk[None, :]).astype(np.float32), BF16)

    mla_attn = functools.partial(_attention, groups=MLA_HEADS // 4, q_w=4 * LANE, k_w=4 * LANE,
                                 vt_rows=2 * VT_ROWS,
                                 pairs=((0, ((0, 0), (1, 1))), (1, ((2, 2), (3, 3)))), n_lat=n_lat)
    gqa_attn = functools.partial(_attention, groups=GQA_KV_HEADS, q_w=2 * LANE, k_w=2 * LANE, vt_rows=VT_ROWS,
                                 pairs=((0, ((0, 0), (0, 1))), (0, ((1, 0), (1, 1)))), n_lat=n_lat)

    def attend(fn, q, k, v, with_ctx):
        y = fn(q, k, v, ctx_queries=False, tq=ATTN_TQ, tk=ATTN_TK)
        if with_ctx:
            y_c = fn(q, k, v, ctx_queries=True, tq=ctx_len, tk=ATTN_TK)
            y = jnp.concatenate([y, y_c], axis=1)
        return y

    xa = jnp.concatenate([x, ctx], axis=1)
    for i in range(depth):
        with_ctx = i < depth - 1
        mod = mod_all[i]
        w_inp = _layout_w_in(w_in[i])
        pm, z, xbc, dtr, pg, gate = _in_projection(xa, mod, w_inp, b_gate[i].reshape(1, -1),
                                                   n_lat=n_lat, tm=tm)
        q_m, k_m, v_m = _mla_prep(pm, g_q_mla[i].reshape(1, -1), g_kv_mla[i].reshape(1, -1),
                                  _layout_w_uq(w_uq[i]), _layout_w_ukv(w_ukv[i]), tab_mq, tab_mk, tm=tm)
        y_mla = attend(mla_attn, q_m, k_m, v_m, with_ctx)
        q_g, k_g, v_g = _gqa_prep(pg, jnp.tile(g_q_gqa[i], GQA_HEADS).reshape(1, -1),
                                  jnp.tile(g_k_gqa[i], GQA_KV_HEADS).reshape(1, -1), bd, tab_gq, tab_gk, tm=tm)
        y_gqa = attend(gqa_attn, q_g, k_g, v_g, with_ctx)
        a = -jnp.exp(a_log[i].astype(F32))
        a_row = _lane_row(a.reshape(-1))
        dtb_row = _lane_row(dt_bias[i].reshape(-1))
        cb = conv_b[i].reshape(1, -1)
        y_f = _ssd_direction(xbc, dtr, conv_w[i], cb, dtb_row, a_row, tri_f, ehd_f, None,
                             ctx_len=ctx_len, reverse=False)
        d_x = jnp.repeat(d_skip[i], SSD_HEAD_DIM).reshape(1, -1)
        y_ssd = _ssd_direction(xbc, dtr, conv_w[i], cb, dtb_row, a_row, tri_b, ehd_b,
                               (y_f, z, d_x, g_ssd[i].reshape(1, -1)), ctx_len=ctx_len, reverse=True)
        xa = _post_mixer(xa, y_mla, y_ssd, y_gqa, gate, mod, w_out[i].astype(BF16),
                         w_ffn_in[i].astype(BF16), w_ffn_out[i].astype(BF16),
                         ln1_g[i].reshape(1, -1), ln1_b[i].reshape(1, -1),
                         ln2_g[i].reshape(1, -1), ln2_b[i].reshape(1, -1), n_lat=n_lat, tm=tm,
                         with_ctx=with_ctx)
    return xa
```

```python
import functools
import math

import jax
import jax.numpy as jnp
import numpy as np
from jax import lax
from jax.experimental import pallas as pl
from jax.experimental.pallas import tpu as pltpu

F32 = jnp.float32
BF16 = jnp.bfloat16

D_MODEL = 1024
DEPTH = 2
GRID_W = 64
ROPE_THETA = 10000.0
NORM_EPS = 1e-6

MLA_HEADS = 8
MLA_Q_RANK = 384
MLA_KV_RANK = 256
MLA_NOPE = 64
MLA_ROPE = 32
MLA_V = 64
MLA_QK = MLA_NOPE + MLA_ROPE

SSD_HEADS = 16
SSD_HEAD_DIM = 64
SSD_INNER = SSD_HEADS * SSD_HEAD_DIM
SSD_GROUPS = 2
SSD_STATE = 128
SSD_CHUNK = 128
SSD_CONV_CH = SSD_INNER + 2 * SSD_GROUPS * SSD_STATE

GQA_HEADS = 8
GQA_KV_HEADS = 2
GQA_HEAD_DIM = 64

MLA_IN = MLA_Q_RANK + MLA_KV_RANK + MLA_ROPE
SSD_IN = SSD_INNER + SSD_CONV_CH + 2 * SSD_HEADS
GQA_IN = (GQA_HEADS + 2 * GQA_KV_HEADS) * GQA_HEAD_DIM
GATE_IN = 3 * D_MODEL
MLA_OUT = MLA_HEADS * MLA_V
GQA_OUT = GQA_HEADS * GQA_HEAD_DIM
FFN_HIDDEN = -(-8 * D_MODEL // (3 * 256)) * 256
DEEPNORM_ALPHA = (2 * DEPTH) ** 0.25

LANE = 128
MOD_ROWS = 8
LOG2E = 1.4426950408889634

C_PM = 0
W_PM = MLA_Q_RANK + MLA_KV_RANK + LANE
C_Z = C_PM + W_PM
C_XBC = C_Z + SSD_INNER
C_DT = C_XBC + SSD_CONV_CH
C_PG = C_DT + LANE
C_GATE = C_PG + GQA_IN
W_INP = C_GATE + GATE_IN

VMEM_LIMIT = 56 * 1024 * 1024
ROW_TILE = 256
ATTN_TQ = 512
ATTN_TK = 512
ATTN_COL_BLOCK = 256
ATTN_KEY_BLOCK = 64
VT_TILE = ROW_TILE
VT_ROWS = 144


def _cparams(sem):
    return pltpu.CompilerParams(dimension_semantics=sem, vmem_limit_bytes=VMEM_LIMIT)


def _const_spec(shape):
    nd = len(shape)
    return pl.BlockSpec(shape, lambda *_: (0,) * nd, pipeline_mode=pl.Buffered(1))


def _layer_norm(x):
    mu = jnp.mean(x, axis=-1, keepdims=True)
    xc = x - mu
    var = jnp.mean(xc * xc, axis=-1, keepdims=True)
    return xc * lax.rsqrt(var + NORM_EPS)


def _split_dot(x, m01, parts):
    acc = None
    rem = x
    for _ in range(parts):
        piece = rem.astype(BF16)
        term = jnp.dot(piece, m01, preferred_element_type=F32)
        acc = term if acc is None else acc + term
        rem = rem - piece.astype(F32)
    return acc


def _mod_rows(mod_ref, col, width, is_ctx):
    b = pl.program_id(0)
    lat = mod_ref[pl.ds(b, 1), col:col + width]
    ctx = mod_ref[MOD_ROWS // 2:MOD_ROWS // 2 + 1, col:col + width]
    return jnp.where(is_ctx, ctx, lat)


def _is_ctx_rows(tm, n_lat):
    t = pl.program_id(1)
    rows = t * tm + lax.broadcasted_iota(jnp.int32, (tm, 1), 0)
    return rows >= n_lat


def _mod_kernel(c_ref, w_ref, b_ref, o_ref):
    c = c_ref[...]
    s = (c * jax.nn.sigmoid(c)).astype(BF16)
    o_ref[0] = jnp.dot(s, w_ref[0].astype(BF16), preferred_element_type=F32) + b_ref[0]


def _modulation(cvec, w_mod, b_mod):
    depth, d, n = w_mod.shape
    tn = 1024
    return pl.pallas_call(
        _mod_kernel,
        grid=(depth, n // tn),
        in_specs=[
            pl.BlockSpec((MOD_ROWS, d), lambda l, j: (0, 0)),
            pl.BlockSpec((1, d, tn), lambda l, j: (l, 0, j)),
            pl.BlockSpec((1, 1, tn), lambda l, j: (l, 0, j)),
        ],
        out_specs=pl.BlockSpec((1, MOD_ROWS, tn), lambda l, j: (l, 0, j)),
        out_shape=jax.ShapeDtypeStruct((depth, MOD_ROWS, n), F32),
        compiler_params=_cparams(("arbitrary", "arbitrary")),
        name="modulation",
    )(cvec, w_mod, b_mod.reshape(depth, 1, n))


def _rope_lanes(x, c, s_fwd, s_bwd, half):
    n = x.shape[-1]
    return x * c + pltpu.roll(x, n - half, axis=1) * s_fwd + pltpu.roll(x, half, axis=1) * s_bwd


def _mla_prep(pm, gq_ref, gkv_ref, wuq_ref, wukv_ref, tq_ref, tk_ref, q_ref, k_ref, v_ref):
    pq = pm[:, :MLA_Q_RANK]
    ckv = pm[:, MLA_Q_RANK:MLA_Q_RANK + MLA_KV_RANK]
    kr = pm[:, MLA_Q_RANK + MLA_KV_RANK:]

    def rms(x, g):
        return x * lax.rsqrt(jnp.mean(x * x, axis=-1, keepdims=True) + NORM_EPS) * g

    q = jnp.dot(rms(pq, gq_ref[...]).astype(BF16), wuq_ref[...], preferred_element_type=F32)
    kv = jnp.dot(rms(ckv, gkv_ref[...]).astype(BF16), wukv_ref[...], preferred_element_type=F32)
    half = MLA_ROPE // 2
    kr = _rope_lanes(kr, tk_ref[0], tk_ref[1], tk_ref[2], half)
    cq, sq1, sq2 = tq_ref[0], tq_ref[1], tq_ref[2]
    for hd in range(MLA_HEADS):
        sl = slice(hd * LANE, (hd + 1) * LANE)
        q_ref[0, :, sl] = _rope_lanes(q[:, sl], cq, sq1, sq2, half).astype(q_ref.dtype)
        k_ref[0, :, sl] = (kv[:, sl] + kr).astype(k_ref.dtype)
    ones = jnp.ones((VT_ROWS - LANE, pm.shape[0]), v_ref.dtype)
    for j in range(MLA_HEADS // 2):
        c0 = MLA_HEADS * LANE + j * LANE
        v_ref[0, 0, j * VT_ROWS:j * VT_ROWS + LANE, :] = kv[:, c0:c0 + LANE].T.astype(v_ref.dtype)
        v_ref[0, 0, j * VT_ROWS + LANE:(j + 1) * VT_ROWS, :] = ones


def _gqa_prep(pg, gq_ref, gk_ref, bd_ref, tq_ref, tk_ref, q_ref, k_ref, v_ref):
    hd = GQA_HEAD_DIM
    q = pg[:, :GQA_OUT]
    k = pg[:, GQA_OUT:GQA_OUT + LANE]
    v = pg[:, GQA_OUT + LANE:]
    half = hd // 2

    def head_rms(x, g):
        w = x.shape[-1]
        ms = _split_dot(x * x, bd_ref[:w, :w], 2) * (1.0 / hd)
        return x * lax.rsqrt(ms + NORM_EPS) * g

    qn = head_rms(q, gq_ref[...])
    for j in range(GQA_OUT // LANE):
        sl = slice(j * LANE, (j + 1) * LANE)
        q_ref[0, :, sl] = _rope_lanes(qn[:, sl], tq_ref[0], tq_ref[1], tq_ref[2], half).astype(q_ref.dtype)
    kn = head_rms(k, gk_ref[...])
    kn = _rope_lanes(kn, tk_ref[0], tk_ref[1], tk_ref[2], half)
    lane = lax.broadcasted_iota(jnp.int32, kn.shape, 1)
    low = lane < hd
    kn_sw = pltpu.roll(kn, hd, axis=1)
    zero = jnp.zeros_like(kn)
    k_ref[0, :, 0 * LANE:1 * LANE] = jnp.where(low, kn, zero).astype(k_ref.dtype)
    k_ref[0, :, 1 * LANE:2 * LANE] = jnp.where(low, zero, kn_sw).astype(k_ref.dtype)
    k_ref[0, :, 2 * LANE:3 * LANE] = jnp.where(low, kn_sw, zero).astype(k_ref.dtype)
    k_ref[0, :, 3 * LANE:4 * LANE] = jnp.where(low, zero, kn).astype(k_ref.dtype)
    v_t = v.T.astype(v_ref.dtype)
    ones = jnp.ones((VT_ROWS - LANE, v.shape[0]), v_ref.dtype)
    for g in range(GQA_KV_HEADS):
        vg = v_t[g * hd:(g + 1) * hd]
        v_ref[0, 0, g * VT_ROWS:g * VT_ROWS + hd, :] = vg
        v_ref[0, 0, g * VT_ROWS + hd:g * VT_ROWS + LANE, :] = vg
        v_ref[0, 0, g * VT_ROWS + LANE:(g + 1) * VT_ROWS, :] = ones


def _inproj_kernel(x_ref, mod_ref, w_ref, bg_ref,
                   gqm_ref, gkvm_ref, wuq_ref, wukv_ref, tmq_ref, tmk_ref,
                   gqg_ref, gkg_ref, bd_ref, tgq_ref, tgk_ref,
                   z_ref, xbc_ref, dt_ref, gate_ref, qm_ref, km_ref, vm_ref, qg_ref, kg_ref, vg_ref,
                   *, tm, n_lat):
    is_ctx = _is_ctx_rows(tm, n_lat)
    d = D_MODEL
    sh = _mod_rows(mod_ref, 0, d, is_ctx)
    sc = _mod_rows(mod_ref, d, d, is_ctx)
    h = (_layer_norm(x_ref[0]) * (1.0 + sc) + sh).astype(BF16)

    def proj(c0, width):
        return jnp.dot(h, w_ref[:, c0:c0 + width], preferred_element_type=F32)

    _mla_prep(proj(C_PM, W_PM), gqm_ref, gkvm_ref, wuq_ref, wukv_ref, tmq_ref, tmk_ref, qm_ref, km_ref, vm_ref)
    _gqa_prep(proj(C_PG, GQA_IN), gqg_ref, gkg_ref, bd_ref, tgq_ref, tgk_ref, qg_ref, kg_ref, vg_ref)
    z_ref[0] = proj(C_Z, SSD_INNER)
    xbc_ref[0] = proj(C_XBC, SSD_CONV_CH)
    dt_ref[0] = proj(C_DT, LANE)
    for j in range(3):
        c0 = j * D_MODEL
        logits = proj(C_GATE + c0, D_MODEL) + bg_ref[:, c0:c0 + D_MODEL]
        gate_ref[0, :, c0:c0 + D_MODEL] = jax.nn.sigmoid(logits).astype(gate_ref.dtype)


def _in_projection(xa, mod, w_inp, b_gate, mla_consts, gqa_consts, *, n_lat, tm):
    bsz, t_all, d = xa.shape
    assert tm == VT_TILE
    row = lambda w: pl.BlockSpec((1, tm, w), lambda b, t: (b, t, 0))
    tab = pl.BlockSpec((3, tm, LANE), lambda b, t: (0, t, 0))
    vt_spec = lambda rows: pl.BlockSpec((1, 1, rows, tm), lambda b, t: (b, t, 0, 0))
    g_qm, g_kvm, wuq_p, wukv_p, tab_mq, tab_mk = mla_consts
    g_qg, g_kg, bd, tab_gq, tab_gk = gqa_consts
    hw = MLA_HEADS * LANE
    kw = GQA_KV_HEADS * 2 * LANE
    vm_rows = (MLA_HEADS // 2) * VT_ROWS
    vg_rows = GQA_KV_HEADS * VT_ROWS
    seq = lambda w, dt: jax.ShapeDtypeStruct((bsz, t_all, w), dt)
    vts = lambda rows: jax.ShapeDtypeStruct((bsz, t_all // tm, rows, tm), BF16)
    consts = (mod, w_inp, b_gate, g_qm, g_kvm, wuq_p, wukv_p)
    return pl.pallas_call(
        functools.partial(_inproj_kernel, tm=tm, n_lat=n_lat),
        grid=(bsz, t_all // tm),
        in_specs=[row(d)] + [_const_spec(a.shape) for a in consts] + [tab, tab]
                 + [_const_spec(a.shape) for a in (g_qg, g_kg, bd)] + [tab, tab],
        out_specs=[row(SSD_INNER), row(SSD_CONV_CH), row(LANE), row(GATE_IN),
                   row(hw), row(hw), vt_spec(vm_rows), row(GQA_OUT), row(kw), vt_spec(vg_rows)],
        out_shape=[seq(SSD_INNER, F32), seq(SSD_CONV_CH, F32), seq(LANE, F32), seq(GATE_IN, BF16),
                   seq(hw, BF16), seq(hw, BF16), vts(vm_rows), seq(GQA_OUT, BF16), seq(kw, BF16), vts(vg_rows)],
        compiler_params=_cparams(("parallel", "parallel")),
        name="in_projection",
    )(xa, *consts, tab_mq, tab_mk, g_qg, g_kg, bd, tab_gq, tab_gk)


def _attn_kernel(q_ref, k_ref, vt_ref, o_ref, m_ref, acc_ref, a0_ref, a1_ref, c0_ref, c1_ref,
                 p0_ref, p1_ref, s0_ref, s1_ref, *, pairs, tq, tk, n_lat, ctx_len, ctx_queries):
    cb = ATTN_COL_BLOCK
    vr = VT_ROWS
    s_bufs, p_bufs, a_bufs, c_bufs = (s0_ref, s1_ref), (p0_ref, p1_ref), (a0_ref, a1_ref), (c0_ref, c1_ref)

    def scores(r0, size, par):
        for pi, (_, units) in enumerate(pairs):
            for ui, (qb, kb) in enumerate(units):
                s = lax.dot_general(
                    k_ref[0, pl.ds(r0, size), kb * LANE:(kb + 1) * LANE],
                    q_ref[0, :, qb * LANE:(qb + 1) * LANE],
                    (((1,), (1,)), ((), ())), preferred_element_type=F32)
                s_bufs[par][pi, :size, ui * tq:(ui + 1) * tq] = s
                c_bufs[par][pi, :, ui * tq:(ui + 1) * tq] = jnp.max(s, axis=0, keepdims=True)

    def softmax(size, par):
        for pi in range(len(pairs)):
            for c in range(0, 2 * tq, cb):
                m_prev = m_ref[pi, :, c:c + cb]
                m_next = jnp.maximum(m_prev, c_bufs[par][pi, :, c:c + cb])
                for r in range(0, size, ATTN_KEY_BLOCK):
                    blk = s_bufs[par][pi, r:r + ATTN_KEY_BLOCK, c:c + cb]
                    p_bufs[par][pi, r:r + ATTN_KEY_BLOCK, c:c + cb] = jnp.exp2((blk - m_next).astype(BF16))
                a_bufs[par][pi, :, c:c + cb] = jnp.exp2(m_prev - m_next)
                m_ref[pi, :, c:c + cb] = m_next

    def values(u0, size, par):
        for pi, (vb, _) in enumerate(pairs):
            vt = jnp.concatenate([vt_ref[0, u0 + i, vb * vr:(vb + 1) * vr, :]
                                  for i in range(size // VT_TILE)], axis=1)
            pv = jnp.dot(vt, p_bufs[par][pi, :size, :], preferred_element_type=F32)
            acc_ref[pi] = a_bufs[par][pi] * acc_ref[pi] + pv

    m_ref[...] = jnp.full(m_ref.shape, -jnp.inf, F32)
    acc_ref[...] = jnp.zeros(acc_ref.shape, F32)
    tku = tk // VT_TILE
    ctx_u = n_lat // VT_TILE
    scores(n_lat, ctx_len, 0)
    if ctx_queries:
        softmax(ctx_len, 0)
        values(ctx_u, ctx_len, 0)
    else:
        n = n_lat // tk
        scores(0, tk, 1)
        softmax(ctx_len, 0)
        scores(tk, tk, 0)
        softmax(tk, 1)
        values(ctx_u, ctx_len, 0)

        def body(jj, carry):
            r = pl.multiple_of(jj * (2 * tk), 2 * tk)
            u = jj * (2 * tku)
            scores(r + 2 * tk, tk, 1)
            softmax(tk, 0)
            values(u, tk, 1)
            scores(r + 3 * tk, tk, 0)
            softmax(tk, 1)
            values(u + tku, tk, 0)
            return carry

        lax.fori_loop(0, (n - 2) // 2, body, 0)
        softmax(tk, 0)
        values((n - 2) * tku, tk, 1)
        values((n - 1) * tku, tk, 0)
    row = lax.broadcasted_iota(jnp.int32, (LANE, tq), 0)
    for pi in range(len(pairs)):
        acc = acc_ref[pi]
        o_t = acc[:LANE] / acc[LANE:LANE + 1]
        o_t = jnp.where(row < LANE // 2, o_t[:, :tq], o_t[:, tq:])
        o_ref[0, :, pi * LANE:(pi + 1) * LANE] = o_t.T.astype(o_ref.dtype)


def _attention(q, k, vt, *, groups, q_w, k_w, vt_rows, pairs, n_lat, ctx_queries, tq, tk):
    bsz, t_all, _ = q.shape
    ctx_len = t_all - n_lat
    out_w = len(pairs) * LANE
    rows = ctx_len if ctx_queries else n_lat
    q_tile0 = n_lat // tq if ctx_queries else 0
    assert rows % tq == 0 and n_lat % tq == 0 and n_lat % (2 * tk) == 0 and ctx_len <= tk
    assert (2 * tq) % ATTN_COL_BLOCK == 0 and tk % VT_TILE == 0 and ctx_len % VT_TILE == 0
    state = pltpu.VMEM((len(pairs), 1, 2 * tq), F32)
    acc = pltpu.VMEM((len(pairs), VT_ROWS, 2 * tq), F32)
    pbuf = pltpu.VMEM((len(pairs), tk, 2 * tq), BF16)
    sbuf = pltpu.VMEM((len(pairs), tk, 2 * tq), F32)
    return pl.pallas_call(
        functools.partial(_attn_kernel, pairs=pairs, tq=tq, tk=tk, n_lat=n_lat, ctx_len=ctx_len,
                          ctx_queries=ctx_queries),
        grid=(bsz, groups, rows // tq),
        in_specs=[
            pl.BlockSpec((1, tq, q_w), lambda b, g, t: (b, t + q_tile0, g)),
            pl.BlockSpec((1, t_all, k_w), lambda b, g, t: (b, 0, g)),
            pl.BlockSpec((1, t_all // VT_TILE, vt_rows, VT_TILE), lambda b, g, t: (b, 0, g, 0)),
        ],
        out_specs=pl.BlockSpec((1, tq, out_w), lambda b, g, t: (b, t, g)),
        out_shape=jax.ShapeDtypeStruct((bsz, rows, groups * out_w), BF16),
        scratch_shapes=[state, acc, state, state, state, state, pbuf, pbuf, sbuf, sbuf],
        compiler_params=_cparams(("parallel", "parallel", "arbitrary")),
        name="attention_ctx" if ctx_queries else "attention",
    )(q, k, vt)


def _ssd_chunk_index(i, n_chunks, ctx_chunks, reverse):
    if reverse:
        return n_chunks - 1 - i
    lat_chunks = n_chunks - ctx_chunks
    return jnp.where(i < ctx_chunks, lat_chunks + i, i - ctx_chunks)


def _ssd_kernel(*refs, reverse, n_chunks, ctx_chunks):
    if reverse:
        (xbc_ref, prev_ref, next_ref, dt_ref, cw_ref, cb_ref, dtb_ref, a_ref, tri_ref, ehd_ref,
         yf_ref, z_ref, dsk_ref, g_ref, y_ref, h_ref) = refs
    else:
        (xbc_ref, prev_ref, next_ref, dt_ref, cw_ref, cb_ref, dtb_ref, a_ref, tri_ref, ehd_ref,
         y_ref, h_ref) = refs
    q = SSD_CHUNK
    i = pl.program_id(1)
    ci = _ssd_chunk_index(i, n_chunks, ctx_chunks, reverse)

    @pl.when(i == 0)
    def _():
        h_ref[...] = jnp.zeros_like(h_ref)

    x = xbc_ref[0]
    lat_chunks = n_chunks - ctx_chunks
    at_start = jnp.logical_or(ci == 0, ci == lat_chunks)
    at_end = jnp.logical_or(ci == lat_chunks - 1, ci == n_chunks - 1)
    prev_row = jnp.where(at_start, 0.0, prev_ref[0, 7:8, :])
    next_row = jnp.where(at_end, 0.0, next_ref[0, 0:1, :])
    row = lax.broadcasted_iota(jnp.int32, (q, 1), 0)
    x_prev = jnp.where(row == 0, prev_row, pltpu.roll(x, 1, axis=0))
    x_next = jnp.where(row == q - 1, next_row, pltpu.roll(x, q - 1, axis=0))
    u = cw_ref[0:1, :] * x_prev + cw_ref[1:2, :] * x + cw_ref[2:3, :] * x_next + cb_ref[...]
    u = u * jax.nn.sigmoid(u)
    xs = u[:, :SSD_INNER]
    gs = SSD_STATE
    bm = [u[:, SSD_INNER + g * gs:SSD_INNER + (g + 1) * gs] for g in range(SSD_GROUPS)]
    cm = [u[:, SSD_INNER + (SSD_GROUPS + g) * gs:SSD_INNER + (SSD_GROUPS + g + 1) * gs]
          for g in range(SSD_GROUPS)]

    dt = jax.nn.softplus(dt_ref[0] + dtb_ref[...])
    da = dt * a_ref[...]
    a_cum = _tri_cumsum(tri_ref, da)
    end = 0 if reverse else q - 1
    a_tot = a_cum[end:end + 1, :]
    e_cum = jnp.exp(a_cum)
    dte = jnp.exp(a_tot - a_cum)
    dt_x = _split_dot(dt, ehd_ref[...], 2)
    ec_x = _split_dot(e_cum, ehd_ref[...], 2)
    dte_x = _split_dot(dte, ehd_ref[...], 2)
    xdt = xs * dt_x
    xdt_b = xdt.astype(BF16)
    xst_b = (xdt * dte_x).astype(BF16)
    a_cum_t = a_cum.T

    kk = lax.broadcasted_iota(jnp.int32, (q, q), 1)
    qq = lax.broadcasted_iota(jnp.int32, (q, q), 0)
    in_order = (kk >= qq) if reverse else (kk <= qq)
    lane = lax.broadcasted_iota(jnp.int32, (q, LANE), 1)
    off = SSD_HEADS if reverse else 0
    hpg = SSD_HEADS // SSD_GROUPS
    gw = hpg * SSD_HEAD_DIM
    y_parts = []
    for g in range(SSD_GROUPS):
        bm_b = bm[g].astype(BF16)
        cm_b = cm[g].astype(BF16)
        scores = lax.dot_general(cm_b, bm_b, (((1,), (1,)), ((), ())), preferred_element_type=F32)
        h_prev = h_ref[g]
        y_off = jnp.dot(cm_b, h_prev.astype(BF16), preferred_element_type=F32) * ec_x[:, g * gw:(g + 1) * gw]
        pair_out = []
        for j in range(hpg // 2):
            xp = xdt_b[:, g * gw + j * LANE:g * gw + (j + 1) * LANE]
            res = []
            for e in range(2):
                hd = g * hpg + 2 * j + e
                a_q = a_cum[:, off + hd:off + hd + 1]
                a_k = a_cum_t[off + hd:off + hd + 1, :]
                decay = jnp.exp(jnp.where(in_order, a_q - a_k, -jnp.inf))
                pmat = (scores * decay).astype(BF16)
                res.append(jnp.dot(pmat, xp, preferred_element_type=F32))
            pair_out.append(jnp.where(lane < SSD_HEAD_DIM, res[0], res[1]))
        y_parts.append(jnp.concatenate(pair_out, axis=1) + y_off)
        st = jnp.dot(bm[g].T.astype(BF16), xst_b[:, g * gw:(g + 1) * gw], preferred_element_type=F32)
        h_ref[g] = h_prev * ec_x[end:end + 1, g * gw:(g + 1) * gw] + st
    y = jnp.concatenate(y_parts, axis=1)

    if reverse:
        yt = yf_ref[0] + y + dsk_ref[...] * xs
        zz = z_ref[0]
        yt = yt * (zz * jax.nn.sigmoid(zz))
        yt = yt * lax.rsqrt(jnp.mean(yt * yt, axis=-1, keepdims=True) + NORM_EPS) * g_ref[...]
        y_ref[0] = yt.astype(y_ref.dtype)
    else:
        y_ref[0] = y


def _tri_cumsum(tri_ref, da):
    acc = None
    rem = da
    for _ in range(3):
        piece = rem.astype(BF16)
        term = jnp.dot(tri_ref[...], piece, preferred_element_type=F32)
        acc = term if acc is None else acc + term
        rem = rem - piece.astype(F32)
    return acc


def _ssd_direction(xbc, dt, conv_w, conv_b, dt_bias, a_row, tri, ehd, extra, *, ctx_len, reverse):
    bsz, t_all, cch = xbc.shape
    q = SSD_CHUNK
    n_chunks = t_all // q
    ctx_chunks = ctx_len // q
    halo = 8
    hb = q // halo
    n_halo = t_all // halo
    cidx = lambda i: _ssd_chunk_index(i, n_chunks, ctx_chunks, reverse)
    chunk = lambda w: pl.BlockSpec((1, q, w), lambda b, i: (b, cidx(i), 0))
    in_specs = [
        chunk(cch),
        pl.BlockSpec((1, halo, cch), lambda b, i: (b, jnp.maximum(cidx(i) * hb - 1, 0), 0)),
        pl.BlockSpec((1, halo, cch), lambda b, i: (b, jnp.minimum((cidx(i) + 1) * hb, n_halo - 1), 0)),
        chunk(LANE),
        _const_spec(conv_w.shape), _const_spec(conv_b.shape), _const_spec(dt_bias.shape),
        _const_spec(a_row.shape), _const_spec(tri.shape), _const_spec(ehd.shape),
    ]
    args = [xbc, xbc, xbc, dt, conv_w, conv_b, dt_bias, a_row, tri, ehd]
    if reverse:
        y_f, z, d_x, g_ssd = extra
        in_specs += [chunk(SSD_INNER), chunk(SSD_INNER), _const_spec(d_x.shape), _const_spec(g_ssd.shape)]
        args += [y_f, z, d_x, g_ssd]
    out_dtype = BF16 if reverse else F32
    hpg = SSD_HEADS // SSD_GROUPS
    return pl.pallas_call(
        functools.partial(_ssd_kernel, reverse=reverse, n_chunks=n_chunks, ctx_chunks=ctx_chunks),
        grid=(bsz, n_chunks),
        in_specs=in_specs,
        out_specs=chunk(SSD_INNER),
        out_shape=jax.ShapeDtypeStruct((bsz, t_all, SSD_INNER), out_dtype),
        scratch_shapes=[pltpu.VMEM((SSD_GROUPS, SSD_STATE, hpg * SSD_HEAD_DIM), F32)],
        compiler_params=_cparams(("parallel", "arbitrary")),
        name="ssd_bwd" if reverse else "ssd_fwd",
    )(*args)


def _post_kernel(x_ref, ym_ref, ys_ref, yg_ref, gate_ref, mod_ref, wo_ref, wi_ref, wf_ref,
                 l1g_ref, l1b_ref, l2g_ref, l2b_ref, o_ref, act_ref, *, tm, n_lat, th):
    is_ctx = _is_ctx_rows(tm, n_lat)
    d = D_MODEL
    mix = None
    row0 = 0
    for j, y_ref in enumerate((ym_ref, ys_ref, yg_ref)):
        kw = y_ref.shape[-1]
        br = jnp.dot(y_ref[0], wo_ref[row0:row0 + kw, :], preferred_element_type=F32)
        br = gate_ref[0, :, j * d:(j + 1) * d].astype(F32) * br
        mix = br if mix is None else mix + br
        row0 += kw
    g1 = _mod_rows(mod_ref, 2 * d, d, is_ctx)
    x1 = _layer_norm(DEEPNORM_ALPHA * x_ref[0] + g1 * mix) * l1g_ref[...] + l1b_ref[...]

    sh2 = _mod_rows(mod_ref, 3 * d, d, is_ctx)
    sc2 = _mod_rows(mod_ref, 4 * d, d, is_ctx)
    g2 = _mod_rows(mod_ref, 5 * d, d, is_ctx)
    h2 = (_layer_norm(x1) * (1.0 + sc2) + sh2).astype(BF16)
    hid = FFN_HIDDEN
    for c0 in range(0, hid, th):
        gt = jnp.dot(h2, wi_ref[:, c0:c0 + th], preferred_element_type=F32)
        up = jnp.dot(h2, wi_ref[:, hid + c0:hid + c0 + th], preferred_element_type=F32)
        act_ref[:, c0:c0 + th] = (gt * jax.nn.sigmoid(gt) * up).astype(BF16)
    fx = jnp.dot(act_ref[...], wf_ref[...], preferred_element_type=F32)
    o_ref[0] = _layer_norm(DEEPNORM_ALPHA * x1 + g2 * fx) * l2g_ref[...] + l2b_ref[...]


def _post_mixer(xa, y_mla, y_ssd, y_gqa, gate, mod, w_out, w_ffn_in, w_ffn_out, l1g, l1b, l2g, l2b,
                *, n_lat, tm, with_ctx):
    bsz, t_all, d = xa.shape
    row = lambda w: pl.BlockSpec((1, tm, w), lambda b, t: (b, t, 0))
    vec = _const_spec((1, d))
    t_out = t_all if with_ctx else n_lat
    return pl.pallas_call(
        functools.partial(_post_kernel, tm=tm, n_lat=n_lat, th=256),
        grid=(bsz, t_out // tm),
        in_specs=[row(d), row(MLA_OUT), row(SSD_INNER), row(GQA_OUT), row(GATE_IN), _const_spec(mod.shape),
                  _const_spec(w_out.shape), _const_spec(w_ffn_in.shape), _const_spec(w_ffn_out.shape),
                  vec, vec, vec, vec],
        out_specs=pl.BlockSpec((1, tm, d), lambda b, t: (b, t, 0)),
        out_shape=jax.ShapeDtypeStruct((bsz, t_out, d), F32),
        scratch_shapes=[pltpu.VMEM((tm, FFN_HIDDEN), BF16)],
        compiler_params=_cparams(("parallel", "parallel")),
        name="post_mixer",
    )(xa, y_mla, y_ssd, y_gqa, gate, mod, w_out, w_ffn_in, w_ffn_out, l1g, l1b, l2g, l2b)


def _rope_tables(n_lat, ctx_len, rot_dim, lane_dim, lane_off, scale):
    rows = n_lat // GRID_W
    row = jnp.repeat(jnp.arange(rows), GRID_W)
    col = jnp.tile(jnp.arange(GRID_W), rows)
    n_freq = rot_dim // 4
    inv_freq = ROPE_THETA ** (-jnp.arange(n_freq, dtype=F32) / n_freq)
    ang = jnp.concatenate([row[:, None] * inv_freq, col[:, None] * inv_freq], axis=-1)
    cos, sin = jnp.cos(ang), jnp.sin(ang)
    half = rot_dim // 2
    lane = np.arange(LANE)
    rel = (lane % lane_dim) - lane_off
    is_rot = (rel >= 0) & (rel < rot_dim)
    idx = np.where(is_rot, rel % half, 0)
    first = is_rot & (rel < half)
    second = is_rot & (rel >= half)
    c = jnp.where(is_rot[None, :], cos[:, idx], 1.0)
    s1 = jnp.where(first[None, :], -sin[:, idx], 0.0)
    s2 = jnp.where(second[None, :], sin[:, idx], 0.0)
    lat = jnp.stack([c, s1, s2])
    ctx = jnp.stack([jnp.ones((ctx_len, LANE), F32), jnp.zeros((ctx_len, LANE), F32),
                     jnp.zeros((ctx_len, LANE), F32)])
    return (jnp.concatenate([lat, ctx], axis=1) * scale).astype(F32)


def _pad_cols(w, width, offset=0):
    out = jnp.zeros((w.shape[0], width), w.dtype)
    return out.at[:, offset:offset + w.shape[1]].set(w)


def _layout_w_in(w):
    c = 0
    q_lat = w[:, c:c + MLA_Q_RANK]; c += MLA_Q_RANK
    kv_lat = w[:, c:c + MLA_KV_RANK]; c += MLA_KV_RANK
    k_rope = _pad_cols(w[:, c:c + MLA_ROPE], LANE, MLA_NOPE); c += MLA_ROPE
    z = w[:, c:c + SSD_INNER]; c += SSD_INNER
    xbc = w[:, c:c + SSD_CONV_CH]; c += SSD_CONV_CH
    dt = _pad_cols(w[:, c:c + 2 * SSD_HEADS], LANE); c += 2 * SSD_HEADS
    rest = w[:, c:]
    return jnp.concatenate([q_lat, kv_lat, k_rope, z, xbc, dt, rest], axis=1).astype(BF16)


def _layout_w_uq(w):
    w = w.reshape(w.shape[0], MLA_HEADS, MLA_QK)
    w = jnp.pad(w, ((0, 0), (0, 0), (0, LANE - MLA_QK)))
    return w.reshape(w.shape[0], MLA_HEADS * LANE).astype(BF16)


def _layout_w_ukv(w):
    w = w.reshape(w.shape[0], MLA_HEADS, MLA_NOPE + MLA_V)
    k = jnp.pad(w[:, :, :MLA_NOPE], ((0, 0), (0, 0), (0, LANE - MLA_NOPE)))
    v = w[:, :, MLA_NOPE:]
    return jnp.concatenate([k.reshape(w.shape[0], -1), v.reshape(w.shape[0], -1)], axis=1).astype(BF16)


def _lane_row(vec, width=LANE):
    return _pad_cols(vec.reshape(1, -1).astype(F32), width)


def kernel(x, c, ctx, c_ctx, w_mod, b_mod, w_in, b_gate, w_uq, g_q_mla, w_ukv, g_kv_mla, conv_w, conv_b,
           a_log, dt_bias, d_skip, g_ssd, g_q_gqa, g_k_gqa, w_out, ln1_g, ln1_b, w_ffn_in, w_ffn_out,
           ln2_g, ln2_b):
    bsz, n_lat, d = x.shape
    ctx_len = ctx.shape[1]
    depth = w_mod.shape[0]
    assert d == D_MODEL and bsz <= MOD_ROWS // 2
    tm = ROW_TILE
    assert ctx_len % tm == 0 and n_lat % tm == 0

    cvec = jnp.zeros((MOD_ROWS, d), F32).at[:bsz].set(c).at[MOD_ROWS // 2].set(c_ctx)
    mod_all = _modulation(cvec, w_mod, b_mod)

    mla_scale = MLA_QK ** -0.5 * LOG2E
    gqa_scale = GQA_HEAD_DIM ** -0.5 * LOG2E
    tab_mq = _rope_tables(n_lat, ctx_len, MLA_ROPE, LANE, MLA_NOPE, mla_scale)
    tab_mk = _rope_tables(n_lat, ctx_len, MLA_ROPE, LANE, MLA_NOPE, 1.0)
    tab_gq = _rope_tables(n_lat, ctx_len, GQA_HEAD_DIM, GQA_HEAD_DIM, 0, gqa_scale)
    tab_gk = _rope_tables(n_lat, ctx_len, GQA_HEAD_DIM, GQA_HEAD_DIM, 0, 1.0)

    qn = SSD_CHUNK
    r = np.arange(qn)
    tri_f = jnp.asarray((r[None, :] <= r[:, None]).astype(np.float32), BF16)
    tri_b = jnp.asarray((r[None, :] >= r[:, None]).astype(np.float32), BF16)
    lane1k = np.arange(SSD_INNER) // SSD_HEAD_DIM
    ehd_f = jnp.asarray((np.arange(LANE)[:, None] == lane1k[None, :]).astype(np.float32), BF16)
    ehd_b = jnp.asarray((np.arange(LANE)[:, None] == (lane1k[None, :] + SSD_HEADS)).astype(np.float32), BF16)
    blk = np.arange(GQA_OUT) // GQA_HEAD_DIM
    bd = jnp.asarray((blk[:, None] == blk[None, :]).astype(np.float32), BF16)

    mla_attn = functools.partial(_attention, groups=MLA_HEADS // 4, q_w=4 * LANE, k_w=4 * LANE,
                                 vt_rows=2 * VT_ROWS,
                                 pairs=((0, ((0, 0), (1, 1))), (1, ((2, 2), (3, 3)))), n_lat=n_lat)
    gqa_attn = functools.partial(_attention, groups=GQA_KV_HEADS, q_w=2 * LANE, k_w=2 * LANE, vt_rows=VT_ROWS,
                                 pairs=((0, ((0, 0), (0, 1))), (0, ((1, 0), (1, 1)))), n_lat=n_lat)

    def attend(fn, q, k, v, with_ctx):
        y = fn(q, k, v, ctx_queries=False, tq=ATTN_TQ, tk=ATTN_TK)
        if with_ctx:
            y_c = fn(q, k, v, ctx_queries=True, tq=ctx_len, tk=ATTN_TK)
            y = jnp.concatenate([y, y_c], axis=1)
        return y

    xa = jnp.concatenate([x, ctx], axis=1)
    for i in range(depth):
        with_ctx = i < depth - 1
        mod = mod_all[i]
        w_inp = _layout_w_in(w_in[i])
        mla_consts = (g_q_mla[i].reshape(1, -1), g_kv_mla[i].reshape(1, -1),
                      _layout_w_uq(w_uq[i]), _layout_w_ukv(w_ukv[i]), tab_mq, tab_mk)
        gqa_consts = (jnp.tile(g_q_gqa[i], GQA_HEADS).reshape(1, -1),
                      jnp.tile(g_k_gqa[i], GQA_KV_HEADS).reshape(1, -1), bd, tab_gq, tab_gk)
        z, xbc, dtr, gate, q_m, k_m, v_m, q_g, k_g, v_g = _in_projection(
            xa, mod, w_inp, b_gate[i].reshape(1, -1), mla_consts, gqa_consts, n_lat=n_lat, tm=tm)
        y_mla = attend(mla_attn, q_m, k_m, v_m, with_ctx)
        y_gqa = attend(gqa_attn, q_g, k_g, v_g, with_ctx)
        a = -jnp.exp(a_log[i].astype(F32))
        a_row = _lane_row(a.reshape(-1))
        dtb_row = _lane_row(dt_bias[i].reshape(-1))
        cb = conv_b[i].reshape(1, -1)
        y_f = _ssd_direction(xbc, dtr, conv_w[i], cb, dtb_row, a_row, tri_f, ehd_f, None,
                             ctx_len=ctx_len, reverse=False)
        d_x = jnp.repeat(d_skip[i], SSD_HEAD_DIM).reshape(1, -1)
        y_ssd = _ssd_direction(xbc, dtr, conv_w[i], cb, dtb_row, a_row, tri_b, ehd_b,
                               (y_f, z, d_x, g_ssd[i].reshape(1, -1)), ctx_len=ctx_len, reverse=True)
        xa = _post_mixer(xa, y_mla, y_ssd, y_gqa, gate, mod, w_out[i].astype(BF16),
                         w_ffn_in[i].astype(BF16), w_ffn_out[i].astype(BF16),
                         ln1_g[i].reshape(1, -1), ln1_b[i].reshape(1, -1),
                         ln2_g[i].reshape(1, -1), ln2_b[i].reshape(1, -1), n_lat=n_lat, tm=tm,
                         with_ctx=with_ctx)
    return xa
```

```python
import functools
import math

import jax
import jax.numpy as jnp
import numpy as np
from jax import lax
from jax.experimental import pallas as pl
from jax.experimental.pallas import tpu as pltpu

F32 = jnp.float32
BF16 = jnp.bfloat16

D_MODEL = 1024
DEPTH = 2
GRID_W = 64
ROPE_THETA = 10000.0
NORM_EPS = 1e-6

MLA_HEADS = 8
MLA_Q_RANK = 384
MLA_KV_RANK = 256
MLA_NOPE = 64
MLA_ROPE = 32
MLA_V = 64
MLA_QK = MLA_NOPE + MLA_ROPE

SSD_HEADS = 16
SSD_HEAD_DIM = 64
SSD_INNER = SSD_HEADS * SSD_HEAD_DIM
SSD_GROUPS = 2
SSD_STATE = 128
SSD_CHUNK = 128
SSD_CONV_CH = SSD_INNER + 2 * SSD_GROUPS * SSD_STATE

GQA_HEADS = 8
GQA_KV_HEADS = 2
GQA_HEAD_DIM = 64

MLA_IN = MLA_Q_RANK + MLA_KV_RANK + MLA_ROPE
SSD_IN = SSD_INNER + SSD_CONV_CH + 2 * SSD_HEADS
GQA_IN = (GQA_HEADS + 2 * GQA_KV_HEADS) * GQA_HEAD_DIM
GATE_IN = 3 * D_MODEL
MLA_OUT = MLA_HEADS * MLA_V
GQA_OUT = GQA_HEADS * GQA_HEAD_DIM
FFN_HIDDEN = -(-8 * D_MODEL // (3 * 256)) * 256
DEEPNORM_ALPHA = (2 * DEPTH) ** 0.25

LANE = 128
MOD_ROWS = 8
LOG2E = 1.4426950408889634

C_PM = 0
W_PM = MLA_Q_RANK + MLA_KV_RANK + LANE
C_Z = C_PM + W_PM
C_XBC = C_Z + SSD_INNER
C_DT = C_XBC + SSD_CONV_CH
C_PG = C_DT + LANE
C_GATE = C_PG + GQA_IN
W_INP = C_GATE + GATE_IN

VMEM_LIMIT = 56 * 1024 * 1024
ROW_TILE = 256
LATENT_ROW_TILE = 512
ATTN_TQ = 512
ATTN_TK = 512
ATTN_COL_BLOCK = 256
ATTN_KEY_BLOCK = 64
VT_TILE = ROW_TILE
VT_ROWS = 144


def _cparams(sem):
    return pltpu.CompilerParams(dimension_semantics=sem, vmem_limit_bytes=VMEM_LIMIT)


def _const_spec(shape):
    nd = len(shape)
    return pl.BlockSpec(shape, lambda *_: (0,) * nd, pipeline_mode=pl.Buffered(1))


def _layer_norm(x):
    mu = jnp.mean(x, axis=-1, keepdims=True)
    xc = x - mu
    var = jnp.mean(xc * xc, axis=-1, keepdims=True)
    return xc * lax.rsqrt(var + NORM_EPS)


def _split_dot(x, m01, parts):
    pieces = []
    rem = x
    for _ in range(parts):
        piece = rem.astype(BF16)
        pieces.append(piece)
        rem = rem - piece.astype(F32)
    return jnp.dot(jnp.concatenate(pieces, axis=1), jnp.concatenate([m01] * parts, axis=0),
                   preferred_element_type=F32)


def _mod_rows(mod_ref, col, width, is_ctx):
    b = pl.program_id(0)
    lat = mod_ref[pl.ds(b, 1), col:col + width]
    ctx = mod_ref[MOD_ROWS // 2:MOD_ROWS // 2 + 1, col:col + width]
    return jnp.where(is_ctx, ctx, lat)


def _is_ctx_rows(tm, n_lat):
    t = pl.program_id(1)
    rows = t * tm + lax.broadcasted_iota(jnp.int32, (tm, 1), 0)
    return rows >= n_lat


def _mod_kernel(c_ref, w_ref, b_ref, o_ref):
    c = c_ref[...]
    s = (c * jax.nn.sigmoid(c)).astype(BF16)
    o_ref[0] = jnp.dot(s, w_ref[0].astype(BF16), preferred_element_type=F32) + b_ref[0]


def _modulation(cvec, w_mod, b_mod):
    depth, d, n = w_mod.shape
    tn = 1024
    return pl.pallas_call(
        _mod_kernel,
        grid=(depth, n // tn),
        in_specs=[
            pl.BlockSpec((MOD_ROWS, d), lambda l, j: (0, 0)),
            pl.BlockSpec((1, d, tn), lambda l, j: (l, 0, j)),
            pl.BlockSpec((1, 1, tn), lambda l, j: (l, 0, j)),
        ],
        out_specs=pl.BlockSpec((1, MOD_ROWS, tn), lambda l, j: (l, 0, j)),
        out_shape=jax.ShapeDtypeStruct((depth, MOD_ROWS, n), F32),
        compiler_params=_cparams(("arbitrary", "arbitrary")),
        name="modulation",
    )(cvec, w_mod, b_mod.reshape(depth, 1, n))


def _rope_lanes(x, c, s_fwd, s_bwd, half):
    n = x.shape[-1]
    return x * c + pltpu.roll(x, n - half, axis=1) * s_fwd + pltpu.roll(x, half, axis=1) * s_bwd


def _mla_prep(pm, gq_ref, gkv_ref, wuq_ref, wukv_ref, tq_ref, tk_ref, q_ref, k_ref, v_ref):
    pq = pm[:, :MLA_Q_RANK]
    ckv = pm[:, MLA_Q_RANK:MLA_Q_RANK + MLA_KV_RANK]
    kr = pm[:, MLA_Q_RANK + MLA_KV_RANK:]

    def rms(x, g):
        return x * lax.rsqrt(jnp.mean(x * x, axis=-1, keepdims=True) + NORM_EPS) * g

    q = jnp.dot(rms(pq, gq_ref[...]).astype(BF16), wuq_ref[...], preferred_element_type=F32)
    kv = jnp.dot(rms(ckv, gkv_ref[...]).astype(BF16), wukv_ref[...], preferred_element_type=F32)
    half = MLA_ROPE // 2
    kr = _rope_lanes(kr, tk_ref[0], tk_ref[1], tk_ref[2], half)
    cq, sq1, sq2 = tq_ref[0], tq_ref[1], tq_ref[2]
    for hd in range(MLA_HEADS):
        sl = slice(hd * LANE, (hd + 1) * LANE)
        q_ref[0, :, sl] = _rope_lanes(q[:, sl], cq, sq1, sq2, half).astype(q_ref.dtype)
        k_ref[0, :, sl] = (kv[:, sl] + kr).astype(k_ref.dtype)
    ones = jnp.ones((VT_ROWS - LANE, pm.shape[0]), v_ref.dtype)
    for j in range(MLA_HEADS // 2):
        c0 = MLA_HEADS * LANE + j * LANE
        v_ref[0, 0, j * VT_ROWS:j * VT_ROWS + LANE, :] = kv[:, c0:c0 + LANE].T.astype(v_ref.dtype)
        v_ref[0, 0, j * VT_ROWS + LANE:(j + 1) * VT_ROWS, :] = ones


def _gqa_prep(pg, gq_ref, gk_ref, bd_ref, tq_ref, tk_ref, q_ref, k_ref, v_ref):
    hd = GQA_HEAD_DIM
    q = pg[:, :GQA_OUT]
    k = pg[:, GQA_OUT:GQA_OUT + LANE]
    v = pg[:, GQA_OUT + LANE:]
    half = hd // 2

    def head_rms(x, g):
        w = x.shape[-1]
        ms = _split_dot(x * x, bd_ref[:w, :w], 2) * (1.0 / hd)
        return x * lax.rsqrt(ms + NORM_EPS) * g

    qn = head_rms(q, gq_ref[...])
    for j in range(GQA_OUT // LANE):
        sl = slice(j * LANE, (j + 1) * LANE)
        q_ref[0, :, sl] = _rope_lanes(qn[:, sl], tq_ref[0], tq_ref[1], tq_ref[2], half).astype(q_ref.dtype)
    kn = head_rms(k, gk_ref[...])
    kn = _rope_lanes(kn, tk_ref[0], tk_ref[1], tk_ref[2], half)
    lane = lax.broadcasted_iota(jnp.int32, kn.shape, 1)
    low = lane < hd
    kn_sw = pltpu.roll(kn, hd, axis=1)
    zero = jnp.zeros_like(kn)
    k_ref[0, :, 0 * LANE:1 * LANE] = jnp.where(low, kn, zero).astype(k_ref.dtype)
    k_ref[0, :, 1 * LANE:2 * LANE] = jnp.where(low, zero, kn_sw).astype(k_ref.dtype)
    k_ref[0, :, 2 * LANE:3 * LANE] = jnp.where(low, kn_sw, zero).astype(k_ref.dtype)
    k_ref[0, :, 3 * LANE:4 * LANE] = jnp.where(low, zero, kn).astype(k_ref.dtype)
    v_t = v.T.astype(v_ref.dtype)
    ones = jnp.ones((VT_ROWS - LANE, v.shape[0]), v_ref.dtype)
    for g in range(GQA_KV_HEADS):
        vg = v_t[g * hd:(g + 1) * hd]
        v_ref[0, 0, g * VT_ROWS:g * VT_ROWS + hd, :] = vg
        v_ref[0, 0, g * VT_ROWS + hd:g * VT_ROWS + LANE, :] = vg
        v_ref[0, 0, g * VT_ROWS + LANE:(g + 1) * VT_ROWS, :] = ones


def _inproj_kernel(x_ref, mod_ref, w_ref, bg_ref,
                   gqm_ref, gkvm_ref, wuq_ref, wukv_ref, tmq_ref, tmk_ref,
                   gqg_ref, gkg_ref, bd_ref, tgq_ref, tgk_ref,
                   z_ref, xbc_ref, dt_ref, gate_ref, qm_ref, km_ref, vm_ref, qg_ref, kg_ref, vg_ref,
                   *, tm, n_lat):
    is_ctx = _is_ctx_rows(tm, n_lat)
    d = D_MODEL
    sh = _mod_rows(mod_ref, 0, d, is_ctx)
    sc = _mod_rows(mod_ref, d, d, is_ctx)
    h = (_layer_norm(x_ref[0]) * (1.0 + sc) + sh).astype(BF16)

    def proj(c0, width):
        return jnp.dot(h, w_ref[:, c0:c0 + width], preferred_element_type=F32)

    _mla_prep(proj(C_PM, W_PM), gqm_ref, gkvm_ref, wuq_ref, wukv_ref, tmq_ref, tmk_ref, qm_ref, km_ref, vm_ref)
    _gqa_prep(proj(C_PG, GQA_IN), gqg_ref, gkg_ref, bd_ref, tgq_ref, tgk_ref, qg_ref, kg_ref, vg_ref)
    z_ref[0] = proj(C_Z, SSD_INNER)
    xbc_ref[0] = proj(C_XBC, SSD_CONV_CH)
    dt_ref[0] = proj(C_DT, LANE)
    for j in range(3):
        c0 = j * D_MODEL
        logits = proj(C_GATE + c0, D_MODEL) + bg_ref[:, c0:c0 + D_MODEL]
        gate_ref[0, :, c0:c0 + D_MODEL] = jax.nn.sigmoid(logits).astype(gate_ref.dtype)


def _in_projection(xa, mod, w_inp, b_gate, mla_consts, gqa_consts, *, n_lat, tm):
    bsz, t_all, d = xa.shape
    assert tm == VT_TILE
    row = lambda w: pl.BlockSpec((1, tm, w), lambda b, t: (b, t, 0))
    tab = pl.BlockSpec((3, tm, LANE), lambda b, t: (0, t, 0))
    vt_spec = lambda rows: pl.BlockSpec((1, 1, rows, tm), lambda b, t: (b, t, 0, 0))
    g_qm, g_kvm, wuq_p, wukv_p, tab_mq, tab_mk = mla_consts
    g_qg, g_kg, bd, tab_gq, tab_gk = gqa_consts
    hw = MLA_HEADS * LANE
    kw = GQA_KV_HEADS * 2 * LANE
    vm_rows = (MLA_HEADS // 2) * VT_ROWS
    vg_rows = GQA_KV_HEADS * VT_ROWS
    seq = lambda w, dt: jax.ShapeDtypeStruct((bsz, t_all, w), dt)
    vts = lambda rows: jax.ShapeDtypeStruct((bsz, t_all // tm, rows, tm), BF16)
    consts = (mod, w_inp, b_gate, g_qm, g_kvm, wuq_p, wukv_p)
    return pl.pallas_call(
        functools.partial(_inproj_kernel, tm=tm, n_lat=n_lat),
        grid=(bsz, t_all // tm),
        in_specs=[row(d)] + [_const_spec(a.shape) for a in consts] + [tab, tab]
                 + [_const_spec(a.shape) for a in (g_qg, g_kg, bd)] + [tab, tab],
        out_specs=[row(SSD_INNER), row(SSD_CONV_CH), row(LANE), row(GATE_IN),
                   row(hw), row(hw), vt_spec(vm_rows), row(GQA_OUT), row(kw), vt_spec(vg_rows)],
        out_shape=[seq(SSD_INNER, F32), seq(SSD_CONV_CH, F32), seq(LANE, F32), seq(GATE_IN, BF16),
                   seq(hw, BF16), seq(hw, BF16), vts(vm_rows), seq(GQA_OUT, BF16), seq(kw, BF16), vts(vg_rows)],
        compiler_params=_cparams(("parallel", "parallel")),
        name="in_projection",
    )(xa, *consts, tab_mq, tab_mk, g_qg, g_kg, bd, tab_gq, tab_gk)


def _attn_kernel(q_ref, k_ref, vt_ref, o_ref, m_ref, acc_ref, a0_ref, a1_ref, c0_ref, c1_ref,
                 p0_ref, p1_ref, s0_ref, s1_ref, *, pairs, tq, tk, n_lat, ctx_len, ctx_queries):
    cb = ATTN_COL_BLOCK
    vr = VT_ROWS
    s_bufs, p_bufs, a_bufs, c_bufs = (s0_ref, s1_ref), (p0_ref, p1_ref), (a0_ref, a1_ref), (c0_ref, c1_ref)

    def scores(r0, size, par):
        for pi, (_, units) in enumerate(pairs):
            for ui, (qb, kb) in enumerate(units):
                s = lax.dot_general(
                    k_ref[0, pl.ds(r0, size), kb * LANE:(kb + 1) * LANE],
                    q_ref[0, :, qb * LANE:(qb + 1) * LANE],
                    (((1,), (1,)), ((), ())), preferred_element_type=F32)
                s_bufs[par][pi, :size, ui * tq:(ui + 1) * tq] = s
                c_bufs[par][pi, :, ui * tq:(ui + 1) * tq] = jnp.max(s, axis=0, keepdims=True)

    def softmax(size, par):
        for pi in range(len(pairs)):
            for c in range(0, 2 * tq, cb):
                m_prev = m_ref[pi, :, c:c + cb]
                m_next = jnp.maximum(m_prev, c_bufs[par][pi, :, c:c + cb])
                for r in range(0, size, ATTN_KEY_BLOCK):
                    blk = s_bufs[par][pi, r:r + ATTN_KEY_BLOCK, c:c + cb]
                    p_bufs[par][pi, r:r + ATTN_KEY_BLOCK, c:c + cb] = jnp.exp2(blk - m_next).astype(BF16)
                a_bufs[par][pi, :, c:c + cb] = jnp.exp2(m_prev - m_next)
                m_ref[pi, :, c:c + cb] = m_next

    def values(u0, size, par):
        for pi, (vb, _) in enumerate(pairs):
            vt = jnp.concatenate([vt_ref[0, u0 + i, vb * vr:(vb + 1) * vr, :]
                                  for i in range(size // VT_TILE)], axis=1)
            pv = jnp.dot(vt, p_bufs[par][pi, :size, :], preferred_element_type=F32)
            acc_ref[pi] = a_bufs[par][pi] * acc_ref[pi] + pv

    m_ref[...] = jnp.full(m_ref.shape, -jnp.inf, F32)
    acc_ref[...] = jnp.zeros(acc_ref.shape, F32)
    tku = tk // VT_TILE
    ctx_u = n_lat // VT_TILE
    scores(n_lat, ctx_len, 0)
    if ctx_queries:
        softmax(ctx_len, 0)
        values(ctx_u, ctx_len, 0)
    else:
        n = n_lat // tk
        scores(0, tk, 1)
        softmax(ctx_len, 0)
        scores(tk, tk, 0)
        softmax(tk, 1)
        values(ctx_u, ctx_len, 0)

        def body(jj, carry):
            r = pl.multiple_of(jj * (2 * tk), 2 * tk)
            u = jj * (2 * tku)
            scores(r + 2 * tk, tk, 1)
            softmax(tk, 0)
            values(u, tk, 1)
            scores(r + 3 * tk, tk, 0)
            softmax(tk, 1)
            values(u + tku, tk, 0)
            return carry

        lax.fori_loop(0, (n - 2) // 2, body, 0)
        softmax(tk, 0)
        values((n - 2) * tku, tk, 1)
        values((n - 1) * tku, tk, 0)
    row = lax.broadcasted_iota(jnp.int32, (LANE, tq), 0)
    for pi in range(len(pairs)):
        acc = acc_ref[pi]
        o_t = acc[:LANE] / acc[LANE:LANE + 1]
        o_t = jnp.where(row < LANE // 2, o_t[:, :tq], o_t[:, tq:])
        o_ref[0, :, pi * LANE:(pi + 1) * LANE] = o_t.T.astype(o_ref.dtype)


def _attention(q, k, vt, *, groups, q_w, k_w, vt_rows, pairs, n_lat, ctx_queries, tq, tk):
    bsz, t_all, _ = k.shape
    ctx_len = t_all - n_lat
    out_w = len(pairs) * LANE
    rows = ctx_len if ctx_queries else n_lat
    q_tile0 = n_lat // tq if ctx_queries else 0
    assert rows % tq == 0 and n_lat % tq == 0 and n_lat % (2 * tk) == 0 and ctx_len <= tk
    assert (2 * tq) % ATTN_COL_BLOCK == 0 and tk % VT_TILE == 0 and ctx_len % VT_TILE == 0
    state = pltpu.VMEM((len(pairs), 1, 2 * tq), F32)
    acc = pltpu.VMEM((len(pairs), VT_ROWS, 2 * tq), F32)
    pbuf = pltpu.VMEM((len(pairs), tk, 2 * tq), BF16)
    sbuf = pltpu.VMEM((len(pairs), tk, 2 * tq), F32)
    return pl.pallas_call(
        functools.partial(_attn_kernel, pairs=pairs, tq=tq, tk=tk, n_lat=n_lat, ctx_len=ctx_len,
                          ctx_queries=ctx_queries),
        grid=(bsz, groups, rows // tq),
        in_specs=[
            pl.BlockSpec((1, tq, q_w), lambda b, g, t: (b, t + q_tile0, g)),
            pl.BlockSpec((1, t_all, k_w), lambda b, g, t: (b, 0, g)),
            pl.BlockSpec((1, t_all // VT_TILE, vt_rows, VT_TILE), lambda b, g, t: (b, 0, g, 0)),
        ],
        out_specs=pl.BlockSpec((1, tq, out_w), lambda b, g, t: (b, t, g)),
        out_shape=jax.ShapeDtypeStruct((bsz, rows, groups * out_w), BF16),
        scratch_shapes=[state, acc, state, state, state, state, pbuf, pbuf, sbuf, sbuf],
        compiler_params=_cparams(("parallel", "parallel", "arbitrary")),
        name="attention_ctx" if ctx_queries else "attention",
    )(q, k, vt)


def _ssd_chunk_index(i, n_chunks, ctx_chunks, reverse):
    if reverse:
        return n_chunks - 1 - i
    lat_chunks = n_chunks - ctx_chunks
    return jnp.where(i < ctx_chunks, lat_chunks + i, i - ctx_chunks)


def _ssd_kernel(*refs, reverse, n_chunks, ctx_chunks):
    if reverse:
        (u_ref, dt_ref, dtb_ref, a_ref, tri_ref, ehd_ref,
         yf_ref, z_ref, dsk_ref, g_ref, y_ref, h_ref) = refs
    else:
        (xbc_ref, prev_ref, next_ref, dt_ref, cw_ref, cb_ref, dtb_ref, a_ref, tri_ref, ehd_ref,
         y_ref, uo_ref, h_ref) = refs
    q = SSD_CHUNK
    i = pl.program_id(1)
    ci = _ssd_chunk_index(i, n_chunks, ctx_chunks, reverse)

    @pl.when(i == 0)
    def _():
        h_ref[...] = jnp.zeros_like(h_ref)

    if reverse:
        u = u_ref[0]
    else:
        x = xbc_ref[0]
        lat_chunks = n_chunks - ctx_chunks
        at_start = jnp.logical_or(ci == 0, ci == lat_chunks)
        at_end = jnp.logical_or(ci == lat_chunks - 1, ci == n_chunks - 1)
        prev_row = jnp.where(at_start, 0.0, prev_ref[0, 7:8, :])
        next_row = jnp.where(at_end, 0.0, next_ref[0, 0:1, :])
        row = lax.broadcasted_iota(jnp.int32, (q, 1), 0)
        x_prev = jnp.where(row == 0, prev_row, pltpu.roll(x, 1, axis=0))
        x_next = jnp.where(row == q - 1, next_row, pltpu.roll(x, q - 1, axis=0))
        u = cw_ref[0:1, :] * x_prev + cw_ref[1:2, :] * x + cw_ref[2:3, :] * x_next + cb_ref[...]
        u = u * jax.nn.sigmoid(u)
        uo_ref[0] = u
    xs = u[:, :SSD_INNER]
    gs = SSD_STATE
    bm = [u[:, SSD_INNER + g * gs:SSD_INNER + (g + 1) * gs] for g in range(SSD_GROUPS)]
    cm = [u[:, SSD_INNER + (SSD_GROUPS + g) * gs:SSD_INNER + (SSD_GROUPS + g + 1) * gs]
          for g in range(SSD_GROUPS)]

    dt = jax.nn.softplus(dt_ref[0] + dtb_ref[...])
    da = dt * a_ref[...]
    a_cum = _tri_cumsum(tri_ref, da)
    end = 0 if reverse else q - 1
    a_tot = a_cum[end:end + 1, :]
    e_cum = jnp.exp(a_cum)
    dte = jnp.exp(a_tot - a_cum)
    dt_x = _split_dot(dt, ehd_ref[...], 2)
    ec_x = _split_dot(e_cum, ehd_ref[...], 2)
    dte_x = _split_dot(dte, ehd_ref[...], 2)
    xdt = xs * dt_x
    xdt_b = xdt.astype(BF16)
    xst_b = (xdt * dte_x).astype(BF16)
    a_cum_t = a_cum.T

    kk = lax.broadcasted_iota(jnp.int32, (q, q), 1)
    qq = lax.broadcasted_iota(jnp.int32, (q, q), 0)
    in_order = (kk >= qq) if reverse else (kk <= qq)
    lane = lax.broadcasted_iota(jnp.int32, (q, LANE), 1)
    off = SSD_HEADS if reverse else 0
    hpg = SSD_HEADS // SSD_GROUPS
    gw = hpg * SSD_HEAD_DIM
    y_parts = []
    for g in range(SSD_GROUPS):
        bm_b = bm[g].astype(BF16)
        cm_b = cm[g].astype(BF16)
        scores = lax.dot_general(cm_b, bm_b, (((1,), (1,)), ((), ())), preferred_element_type=F32)
        h_prev = h_ref[g]
        y_off = jnp.dot(cm_b, h_prev.astype(BF16), preferred_element_type=F32) * ec_x[:, g * gw:(g + 1) * gw]
        pair_out = []
        for j in range(hpg // 2):
            xp = xdt_b[:, g * gw + j * LANE:g * gw + (j + 1) * LANE]
            res = []
            for e in range(2):
                hd = g * hpg + 2 * j + e
                a_q = a_cum[:, off + hd:off + hd + 1]
                a_k = a_cum_t[off + hd:off + hd + 1, :]
                decay = jnp.exp(jnp.where(in_order, a_q - a_k, -jnp.inf))
                pmat = (scores * decay).astype(BF16)
                res.append(jnp.dot(pmat, xp, preferred_element_type=F32))
            pair_out.append(jnp.where(lane < SSD_HEAD_DIM, res[0], res[1]))
        y_parts.append(jnp.concatenate(pair_out, axis=1) + y_off)
        st = jnp.dot(bm[g].T.astype(BF16), xst_b[:, g * gw:(g + 1) * gw], preferred_element_type=F32)
        h_ref[g] = h_prev * ec_x[end:end + 1, g * gw:(g + 1) * gw] + st
    y = jnp.concatenate(y_parts, axis=1)

    if reverse:
        yt = yf_ref[0] + y + dsk_ref[...] * xs
        zz = z_ref[0]
        yt = yt * (zz * jax.nn.sigmoid(zz))
        yt = yt * lax.rsqrt(jnp.mean(yt * yt, axis=-1, keepdims=True) + NORM_EPS) * g_ref[...]
        y_ref[0] = yt.astype(y_ref.dtype)
    else:
        y_ref[0] = y


def _tri_cumsum(tri_ref, da):
    acc = None
    rem = da
    for _ in range(3):
        piece = rem.astype(BF16)
        term = jnp.dot(tri_ref[...], piece, preferred_element_type=F32)
        acc = term if acc is None else acc + term
        rem = rem - piece.astype(F32)
    return acc


def _ssd_direction(xbc, dt, conv, dt_bias, a_row, tri, ehd, extra, *, ctx_len, reverse):
    bsz, t_all, cch = xbc.shape
    q = SSD_CHUNK
    n_chunks = t_all // q
    ctx_chunks = ctx_len // q
    halo = 8
    hb = q // halo
    n_halo = t_all // halo
    cidx = lambda i: _ssd_chunk_index(i, n_chunks, ctx_chunks, reverse)
    chunk = lambda w: pl.BlockSpec((1, q, w), lambda b, i: (b, cidx(i), 0))
    tail = [_const_spec(dt_bias.shape), _const_spec(a_row.shape), _const_spec(tri.shape), _const_spec(ehd.shape)]
    if reverse:
        y_f, z, d_x, g_ssd = extra
        in_specs = [chunk(cch), chunk(LANE)] + tail + [
            chunk(SSD_INNER), chunk(SSD_INNER), _const_spec(d_x.shape), _const_spec(g_ssd.shape)]
        args = [xbc, dt, dt_bias, a_row, tri, ehd, y_f, z, d_x, g_ssd]
        out_specs = chunk(SSD_INNER)
        out_shape = jax.ShapeDtypeStruct((bsz, t_all, SSD_INNER), BF16)
    else:
        conv_w, conv_b = conv
        in_specs = [
            chunk(cch),
            pl.BlockSpec((1, halo, cch), lambda b, i: (b, jnp.maximum(cidx(i) * hb - 1, 0), 0)),
            pl.BlockSpec((1, halo, cch), lambda b, i: (b, jnp.minimum((cidx(i) + 1) * hb, n_halo - 1), 0)),
            chunk(LANE), _const_spec(conv_w.shape), _const_spec(conv_b.shape)] + tail
        args = [xbc, xbc, xbc, dt, conv_w, conv_b, dt_bias, a_row, tri, ehd]
        out_specs = [chunk(SSD_INNER), chunk(cch)]
        out_shape = [jax.ShapeDtypeStruct((bsz, t_all, SSD_INNER), F32),
                     jax.ShapeDtypeStruct((bsz, t_all, cch), F32)]
    hpg = SSD_HEADS // SSD_GROUPS
    return pl.pallas_call(
        functools.partial(_ssd_kernel, reverse=reverse, n_chunks=n_chunks, ctx_chunks=ctx_chunks),
        grid=(bsz, n_chunks),
        in_specs=in_specs,
        out_specs=out_specs,
        out_shape=out_shape,
        scratch_shapes=[pltpu.VMEM((SSD_GROUPS, SSD_STATE, hpg * SSD_HEAD_DIM), F32)],
        compiler_params=_cparams(("parallel", "arbitrary")),
        name="ssd_bwd" if reverse else "ssd_fwd",
    )(*args)


def _post_kernel(x_ref, ym_ref, ys_ref, yg_ref, gate_ref, mod_ref, wo_ref, wi_ref, wf_ref,
                 l1g_ref, l1b_ref, l2g_ref, l2b_ref, o_ref, act_ref, *, tm, n_lat, th):
    is_ctx = _is_ctx_rows(tm, n_lat)
    d = D_MODEL
    mix = None
    row0 = 0
    for j, y_ref in enumerate((ym_ref, ys_ref, yg_ref)):
        kw = y_ref.shape[-1]
        br = jnp.dot(y_ref[0], wo_ref[row0:row0 + kw, :], preferred_element_type=F32)
        br = gate_ref[0, :, j * d:(j + 1) * d].astype(F32) * br
        mix = br if mix is None else mix + br
        row0 += kw
    g1 = _mod_rows(mod_ref, 2 * d, d, is_ctx)
    x1 = _layer_norm(DEEPNORM_ALPHA * x_ref[0] + g1 * mix) * l1g_ref[...] + l1b_ref[...]

    sh2 = _mod_rows(mod_ref, 3 * d, d, is_ctx)
    sc2 = _mod_rows(mod_ref, 4 * d, d, is_ctx)
    g2 = _mod_rows(mod_ref, 5 * d, d, is_ctx)
    h2 = (_layer_norm(x1) * (1.0 + sc2) + sh2).astype(BF16)
    hid = FFN_HIDDEN
    for c0 in range(0, hid, th):
        gt = jnp.dot(h2, wi_ref[:, c0:c0 + th], preferred_element_type=F32)
        up = jnp.dot(h2, wi_ref[:, hid + c0:hid + c0 + th], preferred_element_type=F32)
        act_ref[:, c0:c0 + th] = (gt * jax.nn.sigmoid(gt) * up).astype(BF16)
    fx = jnp.dot(act_ref[...], wf_ref[...], preferred_element_type=F32)
    o_ref[0] = _layer_norm(DEEPNORM_ALPHA * x1 + g2 * fx) * l2g_ref[...] + l2b_ref[...]


def _post_mixer(xa, y_mla, y_ssd, y_gqa, gate, mod, w_out, w_ffn_in, w_ffn_out, l1g, l1b, l2g, l2b,
                *, n_lat, tm, with_ctx):
    bsz, t_all, d = xa.shape
    row = lambda w: pl.BlockSpec((1, tm, w), lambda b, t: (b, t, 0))
    vec = _const_spec((1, d))
    t_out = t_all if with_ctx else n_lat
    return pl.pallas_call(
        functools.partial(_post_kernel, tm=tm, n_lat=n_lat, th=256),
        grid=(bsz, t_out // tm),
        in_specs=[row(d), row(MLA_OUT), row(SSD_INNER), row(GQA_OUT), row(GATE_IN), _const_spec(mod.shape),
                  _const_spec(w_out.shape), _const_spec(w_ffn_in.shape), _const_spec(w_ffn_out.shape),
                  vec, vec, vec, vec],
        out_specs=pl.BlockSpec((1, tm, d), lambda b, t: (b, t, 0)),
        out_shape=jax.ShapeDtypeStruct((bsz, t_out, d), F32),
        scratch_shapes=[pltpu.VMEM((tm, FFN_HIDDEN), BF16)],
        compiler_params=_cparams(("parallel", "parallel")),
        name="post_mixer",
    )(xa, y_mla, y_ssd, y_gqa, gate, mod, w_out, w_ffn_in, w_ffn_out, l1g, l1b, l2g, l2b)


def _rope_tables(n_lat, ctx_len, rot_dim, lane_dim, lane_off, scale):
    rows = n_lat // GRID_W
    row = jnp.repeat(jnp.arange(rows), GRID_W)
    col = jnp.tile(jnp.arange(GRID_W), rows)
    n_freq = rot_dim // 4
    inv_freq = ROPE_THETA ** (-jnp.arange(n_freq, dtype=F32) / n_freq)
    ang = jnp.concatenate([row[:, None] * inv_freq, col[:, None] * inv_freq], axis=-1)
    cos, sin = jnp.cos(ang), jnp.sin(ang)
    half = rot_dim // 2
    lane = np.arange(LANE)
    rel = (lane % lane_dim) - lane_off
    is_rot = (rel >= 0) & (rel < rot_dim)
    idx = np.where(is_rot, rel % half, 0)
    first = is_rot & (rel < half)
    second = is_rot & (rel >= half)
    c = jnp.where(is_rot[None, :], cos[:, idx], 1.0)
    s1 = jnp.where(first[None, :], -sin[:, idx], 0.0)
    s2 = jnp.where(second[None, :], sin[:, idx], 0.0)
    lat = jnp.stack([c, s1, s2])
    ctx = jnp.stack([jnp.ones((ctx_len, LANE), F32), jnp.zeros((ctx_len, LANE), F32),
                     jnp.zeros((ctx_len, LANE), F32)])
    return (jnp.concatenate([lat, ctx], axis=1) * scale).astype(F32)


def _pad_cols(w, width, offset=0):
    out = jnp.zeros((w.shape[0], width), w.dtype)
    return out.at[:, offset:offset + w.shape[1]].set(w)


def _layout_w_in(w):
    c = 0
    q_lat = w[:, c:c + MLA_Q_RANK]; c += MLA_Q_RANK
    kv_lat = w[:, c:c + MLA_KV_RANK]; c += MLA_KV_RANK
    k_rope = _pad_cols(w[:, c:c + MLA_ROPE], LANE, MLA_NOPE); c += MLA_ROPE
    z = w[:, c:c + SSD_INNER]; c += SSD_INNER
    xbc = w[:, c:c + SSD_CONV_CH]; c += SSD_CONV_CH
    dt = _pad_cols(w[:, c:c + 2 * SSD_HEADS], LANE); c += 2 * SSD_HEADS
    rest = w[:, c:]
    return jnp.concatenate([q_lat, kv_lat, k_rope, z, xbc, dt, rest], axis=1).astype(BF16)


def _layout_w_uq(w):
    w = w.reshape(w.shape[0], MLA_HEADS, MLA_QK)
    w = jnp.pad(w, ((0, 0), (0, 0), (0, LANE - MLA_QK)))
    return w.reshape(w.shape[0], MLA_HEADS * LANE).astype(BF16)


def _layout_w_ukv(w):
    w = w.reshape(w.shape[0], MLA_HEADS, MLA_NOPE + MLA_V)
    k = jnp.pad(w[:, :, :MLA_NOPE], ((0, 0), (0, 0), (0, LANE - MLA_NOPE)))
    v = w[:, :, MLA_NOPE:]
    return jnp.concatenate([k.reshape(w.shape[0], -1), v.reshape(w.shape[0], -1)], axis=1).astype(BF16)


def _lane_row(vec, width=LANE):
    return _pad_cols(vec.reshape(1, -1).astype(F32), width)


def kernel(x, c, ctx, c_ctx, w_mod, b_mod, w_in, b_gate, w_uq, g_q_mla, w_ukv, g_kv_mla, conv_w, conv_b,
           a_log, dt_bias, d_skip, g_ssd, g_q_gqa, g_k_gqa, w_out, ln1_g, ln1_b, w_ffn_in, w_ffn_out,
           ln2_g, ln2_b):
    bsz, n_lat, d = x.shape
    ctx_len = ctx.shape[1]
    depth = w_mod.shape[0]
    assert d == D_MODEL and bsz <= MOD_ROWS // 2
    tm = ROW_TILE
    assert ctx_len % tm == 0 and n_lat % tm == 0

    cvec = jnp.zeros((MOD_ROWS, d), F32).at[:bsz].set(c).at[MOD_ROWS // 2].set(c_ctx)
    mod_all = _modulation(cvec, w_mod, b_mod)

    mla_scale = MLA_QK ** -0.5 * LOG2E
    gqa_scale = GQA_HEAD_DIM ** -0.5 * LOG2E
    tab_mq = _rope_tables(n_lat, ctx_len, MLA_ROPE, LANE, MLA_NOPE, mla_scale)
    tab_mk = _rope_tables(n_lat, ctx_len, MLA_ROPE, LANE, MLA_NOPE, 1.0)
    tab_gq = _rope_tables(n_lat, ctx_len, GQA_HEAD_DIM, GQA_HEAD_DIM, 0, gqa_scale)
    tab_gk = _rope_tables(n_lat, ctx_len, GQA_HEAD_DIM, GQA_HEAD_DIM, 0, 1.0)

    qn = SSD_CHUNK
    r = np.arange(qn)
    tri_f = jnp.asarray((r[None, :] <= r[:, None]).astype(np.float32), BF16)
    tri_b = jnp.asarray((r[None, :] >= r[:, None]).astype(np.float32), BF16)
    lane1k = np.arange(SSD_INNER) // SSD_HEAD_DIM
    ehd_f = jnp.asarray((np.arange(LANE)[:, None] == lane1k[None, :]).astype(np.float32), BF16)
    ehd_b = jnp.asarray((np.arange(LANE)[:, None] == (lane1k[None, :] + SSD_HEADS)).astype(np.float32), BF16)
    blk = np.arange(GQA_OUT) // GQA_HEAD_DIM
    bd = jnp.asarray((blk[:, None] == blk[None, :]).astype(np.float32), BF16)

    mla_attn = functools.partial(_attention, groups=MLA_HEADS // 4, q_w=4 * LANE, k_w=4 * LANE,
                                 vt_rows=2 * VT_ROWS,
                                 pairs=((0, ((0, 0), (1, 1))), (1, ((2, 2), (3, 3)))), n_lat=n_lat)
    gqa_attn = functools.partial(_attention, groups=GQA_KV_HEADS, q_w=2 * LANE, k_w=2 * LANE, vt_rows=VT_ROWS,
                                 pairs=((0, ((0, 0), (0, 1))), (0, ((1, 0), (1, 1)))), n_lat=n_lat)

    def attend(fn, q, k, v, with_ctx):
        y = fn(q, k, v, ctx_queries=False, tq=ATTN_TQ, tk=ATTN_TK)
        if with_ctx:
            y_c = fn(q, k, v, ctx_queries=True, tq=ctx_len, tk=ATTN_TK)
            y = jnp.concatenate([y, y_c], axis=1)
        return y

    xa = jnp.concatenate([x, ctx], axis=1)
    for i in range(depth):
        with_ctx = i < depth - 1
        mod = mod_all[i]
        w_inp = _layout_w_in(w_in[i])
        mla_consts = (g_q_mla[i].reshape(1, -1), g_kv_mla[i].reshape(1, -1),
                      _layout_w_uq(w_uq[i]), _layout_w_ukv(w_ukv[i]), tab_mq, tab_mk)
        gqa_consts = (jnp.tile(g_q_gqa[i], GQA_HEADS).reshape(1, -1),
                      jnp.tile(g_k_gqa[i], GQA_KV_HEADS).reshape(1, -1), bd, tab_gq, tab_gk)
        z, xbc, dtr, gate, q_m, k_m, v_m, q_g, k_g, v_g = _in_projection(
            xa, mod, w_inp, b_gate[i].reshape(1, -1), mla_consts, gqa_consts, n_lat=n_lat, tm=tm)
        y_mla = attend(mla_attn, q_m, k_m, v_m, with_ctx)
        y_gqa = attend(gqa_attn, q_g, k_g, v_g, with_ctx)
        a = -jnp.exp(a_log[i].astype(F32))
        a_row = _lane_row(a.reshape(-1))
        dtb_row = _lane_row(dt_bias[i].reshape(-1))
        y_f, u_act = _ssd_direction(xbc, dtr, (conv_w[i], conv_b[i].reshape(1, -1)), dtb_row, a_row,
                                    tri_f, ehd_f, None, ctx_len=ctx_len, reverse=False)
        d_x = jnp.repeat(d_skip[i], SSD_HEAD_DIM).reshape(1, -1)
        y_ssd = _ssd_direction(u_act, dtr, None, dtb_row, a_row, tri_b, ehd_b,
                               (y_f, z, d_x, g_ssd[i].reshape(1, -1)), ctx_len=ctx_len, reverse=True)
        xa = _post_mixer(xa, y_mla, y_ssd, y_gqa, gate, mod, w_out[i].astype(BF16),
                         w_ffn_in[i].astype(BF16), w_ffn_out[i].astype(BF16),
                         ln1_g[i].reshape(1, -1), ln1_b[i].reshape(1, -1),
                         ln2_g[i].reshape(1, -1), ln2_b[i].reshape(1, -1), n_lat=n_lat,
                         tm=tm if with_ctx else LATENT_ROW_TILE, with_ctx=with_ctx)
    return xa
```

```python
import functools
import math

import jax
import jax.numpy as jnp
import numpy as np
from jax import lax
from jax.experimental import pallas as pl
from jax.experimental.pallas import tpu as pltpu

F32 = jnp.float32
BF16 = jnp.bfloat16

D_MODEL = 1024
DEPTH = 2
GRID_W = 64
ROPE_THETA = 10000.0
NORM_EPS = 1e-6

MLA_HEADS = 8
MLA_Q_RANK = 384
MLA_KV_RANK = 256
MLA_NOPE = 64
MLA_ROPE = 32
MLA_V = 64
MLA_QK = MLA_NOPE + MLA_ROPE

SSD_HEADS = 16
SSD_HEAD_DIM = 64
SSD_INNER = SSD_HEADS * SSD_HEAD_DIM
SSD_GROUPS = 2
SSD_STATE = 128
SSD_CHUNK = 128
SSD_CONV_CH = SSD_INNER + 2 * SSD_GROUPS * SSD_STATE

GQA_HEADS = 8
GQA_KV_HEADS = 2
GQA_HEAD_DIM = 64

MLA_IN = MLA_Q_RANK + MLA_KV_RANK + MLA_ROPE
SSD_IN = SSD_INNER + SSD_CONV_CH + 2 * SSD_HEADS
GQA_IN = (GQA_HEADS + 2 * GQA_KV_HEADS) * GQA_HEAD_DIM
GATE_IN = 3 * D_MODEL
MLA_OUT = MLA_HEADS * MLA_V
GQA_OUT = GQA_HEADS * GQA_HEAD_DIM
FFN_HIDDEN = -(-8 * D_MODEL // (3 * 256)) * 256
DEEPNORM_ALPHA = (2 * DEPTH) ** 0.25

LANE = 128
MOD_ROWS = 8
LOG2E = 1.4426950408889634

C_PM = 0
W_PM = MLA_Q_RANK + MLA_KV_RANK + LANE
C_Z = C_PM + W_PM
C_XBC = C_Z + SSD_INNER
C_DT = C_XBC + SSD_CONV_CH
C_PG = C_DT + LANE
C_GATE = C_PG + GQA_IN
W_INP = C_GATE + GATE_IN

VMEM_LIMIT = 56 * 1024 * 1024
ROW_TILE = 256
LATENT_ROW_TILE = 512
ATTN_TQ = 512
ATTN_TK = 512
ATTN_COL_BLOCK = 256
ATTN_KEY_BLOCK = 64
SSD_BLOCK = 2 * SSD_CHUNK
VT_TILE = ROW_TILE
VT_ROWS = 144


def _cparams(sem):
    return pltpu.CompilerParams(dimension_semantics=sem, vmem_limit_bytes=VMEM_LIMIT)


def _const_spec(shape):
    nd = len(shape)
    return pl.BlockSpec(shape, lambda *_: (0,) * nd, pipeline_mode=pl.Buffered(1))


def _layer_norm(x):
    mu = jnp.mean(x, axis=-1, keepdims=True)
    xc = x - mu
    var = jnp.mean(xc * xc, axis=-1, keepdims=True)
    return xc * lax.rsqrt(var + NORM_EPS)


def _split_dot(x, m01, parts):
    pieces = []
    rem = x
    for _ in range(parts):
        piece = rem.astype(BF16)
        pieces.append(piece)
        rem = rem - piece.astype(F32)
    return jnp.dot(jnp.concatenate(pieces, axis=1), jnp.concatenate([m01] * parts, axis=0),
                   preferred_element_type=F32)


def _mod_rows(mod_ref, col, width, is_ctx):
    b = pl.program_id(0)
    lat = mod_ref[pl.ds(b, 1), col:col + width]
    ctx = mod_ref[MOD_ROWS // 2:MOD_ROWS // 2 + 1, col:col + width]
    return jnp.where(is_ctx, ctx, lat)


def _is_ctx_rows(tm, n_lat):
    t = pl.program_id(1)
    rows = t * tm + lax.broadcasted_iota(jnp.int32, (tm, 1), 0)
    return rows >= n_lat


def _mod_kernel(c_ref, w_ref, b_ref, o_ref):
    c = c_ref[...]
    s = (c * jax.nn.sigmoid(c)).astype(BF16)
    o_ref[0] = jnp.dot(s, w_ref[0].astype(BF16), preferred_element_type=F32) + b_ref[0]


def _modulation(cvec, w_mod, b_mod):
    depth, d, n = w_mod.shape
    tn = 1024
    return pl.pallas_call(
        _mod_kernel,
        grid=(depth, n // tn),
        in_specs=[
            pl.BlockSpec((MOD_ROWS, d), lambda l, j: (0, 0)),
            pl.BlockSpec((1, d, tn), lambda l, j: (l, 0, j)),
            pl.BlockSpec((1, 1, tn), lambda l, j: (l, 0, j)),
        ],
        out_specs=pl.BlockSpec((1, MOD_ROWS, tn), lambda l, j: (l, 0, j)),
        out_shape=jax.ShapeDtypeStruct((depth, MOD_ROWS, n), F32),
        compiler_params=_cparams(("arbitrary", "arbitrary")),
        name="modulation",
    )(cvec, w_mod, b_mod.reshape(depth, 1, n))


def _rope_lanes(x, c, s_fwd, s_bwd, half):
    n = x.shape[-1]
    return x * c + pltpu.roll(x, n - half, axis=1) * s_fwd + pltpu.roll(x, half, axis=1) * s_bwd


def _mla_prep(pm, gq_ref, gkv_ref, wuq_ref, wukv_ref, tq_ref, tk_ref, q_ref, k_ref, v_ref):
    pq = pm[:, :MLA_Q_RANK]
    ckv = pm[:, MLA_Q_RANK:MLA_Q_RANK + MLA_KV_RANK]
    kr = pm[:, MLA_Q_RANK + MLA_KV_RANK:]

    def rms(x, g):
        return x * lax.rsqrt(jnp.mean(x * x, axis=-1, keepdims=True) + NORM_EPS) * g

    q = jnp.dot(rms(pq, gq_ref[...]).astype(BF16), wuq_ref[...], preferred_element_type=F32)
    kv = jnp.dot(rms(ckv, gkv_ref[...]).astype(BF16), wukv_ref[...], preferred_element_type=F32)
    half = MLA_ROPE // 2
    kr = _rope_lanes(kr, tk_ref[0], tk_ref[1], tk_ref[2], half)
    cq, sq1, sq2 = tq_ref[0], tq_ref[1], tq_ref[2]
    for hd in range(MLA_HEADS):
        sl = slice(hd * LANE, (hd + 1) * LANE)
        q_ref[0, :, sl] = _rope_lanes(q[:, sl], cq, sq1, sq2, half).astype(q_ref.dtype)
        k_ref[0, :, sl] = (kv[:, sl] + kr).astype(k_ref.dtype)
    ones = jnp.ones((VT_ROWS - LANE, pm.shape[0]), v_ref.dtype)
    for j in range(MLA_HEADS // 2):
        c0 = MLA_HEADS * LANE + j * LANE
        v_ref[0, 0, j * VT_ROWS:j * VT_ROWS + LANE, :] = kv[:, c0:c0 + LANE].T.astype(v_ref.dtype)
        v_ref[0, 0, j * VT_ROWS + LANE:(j + 1) * VT_ROWS, :] = ones


def _gqa_prep(pg, gq_ref, gk_ref, bd_ref, tq_ref, tk_ref, q_ref, k_ref, v_ref):
    hd = GQA_HEAD_DIM
    q = pg[:, :GQA_OUT]
    k = pg[:, GQA_OUT:GQA_OUT + LANE]
    v = pg[:, GQA_OUT + LANE:]
    half = hd // 2

    def head_rms(x, g):
        w = x.shape[-1]
        ms = _split_dot(x * x, bd_ref[:w, :w], 2) * (1.0 / hd)
        return x * lax.rsqrt(ms + NORM_EPS) * g

    qn = head_rms(q, gq_ref[...])
    for j in range(GQA_OUT // LANE):
        sl = slice(j * LANE, (j + 1) * LANE)
        q_ref[0, :, sl] = _rope_lanes(qn[:, sl], tq_ref[0], tq_ref[1], tq_ref[2], half).astype(q_ref.dtype)
    kn = head_rms(k, gk_ref[...])
    kn = _rope_lanes(kn, tk_ref[0], tk_ref[1], tk_ref[2], half)
    lane = lax.broadcasted_iota(jnp.int32, kn.shape, 1)
    low = lane < hd
    kn_sw = pltpu.roll(kn, hd, axis=1)
    zero = jnp.zeros_like(kn)
    k_ref[0, :, 0 * LANE:1 * LANE] = jnp.where(low, kn, zero).astype(k_ref.dtype)
    k_ref[0, :, 1 * LANE:2 * LANE] = jnp.where(low, zero, kn_sw).astype(k_ref.dtype)
    k_ref[0, :, 2 * LANE:3 * LANE] = jnp.where(low, kn_sw, zero).astype(k_ref.dtype)
    k_ref[0, :, 3 * LANE:4 * LANE] = jnp.where(low, zero, kn).astype(k_ref.dtype)
    v_t = v.T.astype(v_ref.dtype)
    ones = jnp.ones((VT_ROWS - LANE, v.shape[0]), v_ref.dtype)
    for g in range(GQA_KV_HEADS):
        vg = v_t[g * hd:(g + 1) * hd]
        v_ref[0, 0, g * VT_ROWS:g * VT_ROWS + hd, :] = vg
        v_ref[0, 0, g * VT_ROWS + hd:g * VT_ROWS + LANE, :] = vg
        v_ref[0, 0, g * VT_ROWS + LANE:(g + 1) * VT_ROWS, :] = ones


def _inproj_kernel(x_ref, mod_ref, w_ref, bg_ref,
                   gqm_ref, gkvm_ref, wuq_ref, wukv_ref, tmq_ref, tmk_ref,
                   gqg_ref, gkg_ref, bd_ref, tgq_ref, tgk_ref,
                   z_ref, xbc_ref, dt_ref, gate_ref, qm_ref, km_ref, vm_ref, qg_ref, kg_ref, vg_ref,
                   *, tm, n_lat):
    is_ctx = _is_ctx_rows(tm, n_lat)
    d = D_MODEL
    sh = _mod_rows(mod_ref, 0, d, is_ctx)
    sc = _mod_rows(mod_ref, d, d, is_ctx)
    h = (_layer_norm(x_ref[0]) * (1.0 + sc) + sh).astype(BF16)

    def proj(c0, width):
        return jnp.dot(h, w_ref[:, c0:c0 + width], preferred_element_type=F32)

    _mla_prep(proj(C_PM, W_PM), gqm_ref, gkvm_ref, wuq_ref, wukv_ref, tmq_ref, tmk_ref, qm_ref, km_ref, vm_ref)
    _gqa_prep(proj(C_PG, GQA_IN), gqg_ref, gkg_ref, bd_ref, tgq_ref, tgk_ref, qg_ref, kg_ref, vg_ref)
    z_ref[0] = proj(C_Z, SSD_INNER)
    xbc_ref[0] = proj(C_XBC, SSD_CONV_CH)
    dt_ref[0] = proj(C_DT, LANE)
    for j in range(3):
        c0 = j * D_MODEL
        logits = proj(C_GATE + c0, D_MODEL) + bg_ref[:, c0:c0 + D_MODEL]
        gate_ref[0, :, c0:c0 + D_MODEL] = jax.nn.sigmoid(logits).astype(gate_ref.dtype)


def _in_projection(xa, mod, w_inp, b_gate, mla_consts, gqa_consts, *, n_lat, tm):
    bsz, t_all, d = xa.shape
    assert tm == VT_TILE
    row = lambda w: pl.BlockSpec((1, tm, w), lambda b, t: (b, t, 0))
    tab = pl.BlockSpec((3, tm, LANE), lambda b, t: (0, t, 0))
    vt_spec = lambda rows: pl.BlockSpec((1, 1, rows, tm), lambda b, t: (b, t, 0, 0))
    g_qm, g_kvm, wuq_p, wukv_p, tab_mq, tab_mk = mla_consts
    g_qg, g_kg, bd, tab_gq, tab_gk = gqa_consts
    hw = MLA_HEADS * LANE
    kw = GQA_KV_HEADS * 2 * LANE
    vm_rows = (MLA_HEADS // 2) * VT_ROWS
    vg_rows = GQA_KV_HEADS * VT_ROWS
    seq = lambda w, dt: jax.ShapeDtypeStruct((bsz, t_all, w), dt)
    vts = lambda rows: jax.ShapeDtypeStruct((bsz, t_all // tm, rows, tm), BF16)
    consts = (mod, w_inp, b_gate, g_qm, g_kvm, wuq_p, wukv_p)
    return pl.pallas_call(
        functools.partial(_inproj_kernel, tm=tm, n_lat=n_lat),
        grid=(bsz, t_all // tm),
        in_specs=[row(d)] + [_const_spec(a.shape) for a in consts] + [tab, tab]
                 + [_const_spec(a.shape) for a in (g_qg, g_kg, bd)] + [tab, tab],
        out_specs=[row(SSD_INNER), row(SSD_CONV_CH), row(LANE), row(GATE_IN),
                   row(hw), row(hw), vt_spec(vm_rows), row(GQA_OUT), row(kw), vt_spec(vg_rows)],
        out_shape=[seq(SSD_INNER, F32), seq(SSD_CONV_CH, F32), seq(LANE, F32), seq(GATE_IN, BF16),
                   seq(hw, BF16), seq(hw, BF16), vts(vm_rows), seq(GQA_OUT, BF16), seq(kw, BF16), vts(vg_rows)],
        compiler_params=_cparams(("parallel", "parallel")),
        name="in_projection",
    )(xa, *consts, tab_mq, tab_mk, g_qg, g_kg, bd, tab_gq, tab_gk)


def _attn_kernel(q_ref, k_ref, vt_ref, o_ref, m_ref, acc_ref, a0_ref, a1_ref, c0_ref, c1_ref,
                 p0_ref, p1_ref, s0_ref, s1_ref, *, pairs, tq, tk, n_lat, ctx_len, ctx_queries):
    cb = ATTN_COL_BLOCK
    vr = VT_ROWS
    s_bufs, p_bufs, a_bufs, c_bufs = (s0_ref, s1_ref), (p0_ref, p1_ref), (a0_ref, a1_ref), (c0_ref, c1_ref)

    def scores(r0, size, par):
        for pi, (_, units) in enumerate(pairs):
            for ui, (qb, kb) in enumerate(units):
                s = lax.dot_general(
                    k_ref[0, pl.ds(r0, size), kb * LANE:(kb + 1) * LANE],
                    q_ref[0, :, qb * LANE:(qb + 1) * LANE],
                    (((1,), (1,)), ((), ())), preferred_element_type=F32)
                s_bufs[par][pi, :size, ui * tq:(ui + 1) * tq] = s
                c_bufs[par][pi, :, ui * tq:(ui + 1) * tq] = jnp.max(s, axis=0, keepdims=True)

    def softmax(size, par):
        for pi in range(len(pairs)):
            for c in range(0, 2 * tq, cb):
                m_prev = m_ref[pi, :, c:c + cb]
                m_next = jnp.maximum(m_prev, c_bufs[par][pi, :, c:c + cb])
                for r in range(0, size, ATTN_KEY_BLOCK):
                    blk = s_bufs[par][pi, r:r + ATTN_KEY_BLOCK, c:c + cb]
                    p_bufs[par][pi, r:r + ATTN_KEY_BLOCK, c:c + cb] = jnp.exp2(blk - m_next).astype(BF16)
                a_bufs[par][pi, :, c:c + cb] = jnp.exp2(m_prev - m_next)
                m_ref[pi, :, c:c + cb] = m_next

    def values(u0, size, par):
        for pi, (vb, _) in enumerate(pairs):
            vt = jnp.concatenate([vt_ref[0, u0 + i, vb * vr:(vb + 1) * vr, :]
                                  for i in range(size // VT_TILE)], axis=1)
            pv = jnp.dot(vt, p_bufs[par][pi, :size, :], preferred_element_type=F32)
            acc_ref[pi] = a_bufs[par][pi] * acc_ref[pi] + pv

    m_ref[...] = jnp.full(m_ref.shape, -jnp.inf, F32)
    acc_ref[...] = jnp.zeros(acc_ref.shape, F32)
    tku = tk // VT_TILE
    ctx_u = n_lat // VT_TILE
    scores(n_lat, ctx_len, 0)
    if ctx_queries:
        softmax(ctx_len, 0)
        values(ctx_u, ctx_len, 0)
    else:
        n = n_lat // tk
        scores(0, tk, 1)
        softmax(ctx_len, 0)
        scores(tk, tk, 0)
        softmax(tk, 1)
        values(ctx_u, ctx_len, 0)

        def body(jj, carry):
            r = pl.multiple_of(jj * (2 * tk), 2 * tk)
            u = jj * (2 * tku)
            scores(r + 2 * tk, tk, 1)
            softmax(tk, 0)
            values(u, tk, 1)
            scores(r + 3 * tk, tk, 0)
            softmax(tk, 1)
            values(u + tku, tk, 0)
            return carry

        lax.fori_loop(0, (n - 2) // 2, body, 0)
        softmax(tk, 0)
        values((n - 2) * tku, tk, 1)
        values((n - 1) * tku, tk, 0)
    row = lax.broadcasted_iota(jnp.int32, (LANE, tq), 0)
    for pi in range(len(pairs)):
        acc = acc_ref[pi]
        o_t = acc[:LANE] / acc[LANE:LANE + 1]
        o_t = jnp.where(row < LANE // 2, o_t[:, :tq], o_t[:, tq:])
        o_ref[0, :, pi * LANE:(pi + 1) * LANE] = o_t.T.astype(o_ref.dtype)


def _attention(q, k, vt, *, groups, q_w, k_w, vt_rows, pairs, n_lat, ctx_queries, tq, tk):
    bsz, t_all, _ = k.shape
    ctx_len = t_all - n_lat
    out_w = len(pairs) * LANE
    rows = ctx_len if ctx_queries else n_lat
    q_tile0 = n_lat // tq if ctx_queries else 0
    assert rows % tq == 0 and n_lat % tq == 0 and n_lat % (2 * tk) == 0 and ctx_len <= tk
    assert (2 * tq) % ATTN_COL_BLOCK == 0 and tk % VT_TILE == 0 and ctx_len % VT_TILE == 0
    state = pltpu.VMEM((len(pairs), 1, 2 * tq), F32)
    acc = pltpu.VMEM((len(pairs), VT_ROWS, 2 * tq), F32)
    pbuf = pltpu.VMEM((len(pairs), tk, 2 * tq), BF16)
    sbuf = pltpu.VMEM((len(pairs), tk, 2 * tq), F32)
    return pl.pallas_call(
        functools.partial(_attn_kernel, pairs=pairs, tq=tq, tk=tk, n_lat=n_lat, ctx_len=ctx_len,
                          ctx_queries=ctx_queries),
        grid=(bsz, groups, rows // tq),
        in_specs=[
            pl.BlockSpec((1, tq, q_w), lambda b, g, t: (b, t + q_tile0, g)),
            pl.BlockSpec((1, t_all, k_w), lambda b, g, t: (b, 0, g)),
            pl.BlockSpec((1, t_all // VT_TILE, vt_rows, VT_TILE), lambda b, g, t: (b, 0, g, 0)),
        ],
        out_specs=pl.BlockSpec((1, tq, out_w), lambda b, g, t: (b, t, g)),
        out_shape=jax.ShapeDtypeStruct((bsz, rows, groups * out_w), BF16),
        scratch_shapes=[state, acc, state, state, state, state, pbuf, pbuf, sbuf, sbuf],
        compiler_params=_cparams(("parallel", "parallel", "arbitrary")),
        name="attention_ctx" if ctx_queries else "attention",
    )(q, k, vt)


def _ssd_chunk_index(i, n_chunks, ctx_chunks, reverse):
    if reverse:
        return n_chunks - 1 - i
    lat_chunks = n_chunks - ctx_chunks
    return jnp.where(i < ctx_chunks, lat_chunks + i, i - ctx_chunks)


def _ssd_kernel(*refs, reverse, n_chunks, ctx_chunks):
    if reverse:
        (u_ref, dt_ref, dtb_ref, a_ref, tri_ref, ehd_ref,
         yf_ref, z_ref, dsk_ref, g_ref, y_ref, h_ref) = refs
    else:
        (xbc_ref, prev_ref, next_ref, dt_ref, cw_ref, cb_ref, dtb_ref, a_ref, tri_ref, ehd_ref,
         y_ref, uo_ref, h_ref) = refs
    q = SSD_CHUNK
    rows = SSD_BLOCK
    i = pl.program_id(1)
    ci = _ssd_chunk_index(i, n_chunks, ctx_chunks, reverse)

    @pl.when(i == 0)
    def _():
        h_ref[...] = jnp.zeros_like(h_ref)

    if reverse:
        u_blk = u_ref[0]
    else:
        x = xbc_ref[0]
        lat_chunks = n_chunks - ctx_chunks
        at_start = jnp.logical_or(ci == 0, ci == lat_chunks)
        at_end = jnp.logical_or(ci == lat_chunks - 1, ci == n_chunks - 1)
        prev_row = jnp.where(at_start, 0.0, prev_ref[0, 7:8, :])
        next_row = jnp.where(at_end, 0.0, next_ref[0, 0:1, :])
        row = lax.broadcasted_iota(jnp.int32, (rows, 1), 0)
        x_prev = jnp.where(row == 0, prev_row, pltpu.roll(x, 1, axis=0))
        x_next = jnp.where(row == rows - 1, next_row, pltpu.roll(x, rows - 1, axis=0))
        u_blk = cw_ref[0:1, :] * x_prev + cw_ref[1:2, :] * x + cw_ref[2:3, :] * x_next + cb_ref[...]
        u_blk = u_blk * jax.nn.sigmoid(u_blk)
        uo_ref[0] = u_blk

    subs = range(rows // q)
    for sub in (reversed(subs) if reverse else subs):
        sl = slice(sub * q, (sub + 1) * q)
        y = _ssd_chunk(u_blk[sl], dt_ref[0, sl, :], dtb_ref, a_ref, tri_ref, ehd_ref, h_ref, reverse)
        if reverse:
            yt = yf_ref[0, sl, :] + y + dsk_ref[...] * u_blk[sl, :SSD_INNER]
            zz = z_ref[0, sl, :]
            yt = yt * (zz * jax.nn.sigmoid(zz))
            yt = yt * lax.rsqrt(jnp.mean(yt * yt, axis=-1, keepdims=True) + NORM_EPS) * g_ref[...]
            y_ref[0, sl, :] = yt.astype(y_ref.dtype)
        else:
            y_ref[0, sl, :] = y


def _ssd_chunk(u, dt_raw, dtb_ref, a_ref, tri_ref, ehd_ref, h_ref, reverse):
    q = SSD_CHUNK
    xs = u[:, :SSD_INNER]
    gs = SSD_STATE
    bm = [u[:, SSD_INNER + g * gs:SSD_INNER + (g + 1) * gs] for g in range(SSD_GROUPS)]
    cm = [u[:, SSD_INNER + (SSD_GROUPS + g) * gs:SSD_INNER + (SSD_GROUPS + g + 1) * gs]
          for g in range(SSD_GROUPS)]

    dt = jax.nn.softplus(dt_raw + dtb_ref[...])
    da = dt * a_ref[...]
    a_cum = _tri_cumsum(tri_ref, da)
    end = 0 if reverse else q - 1
    a_tot = a_cum[end:end + 1, :]
    e_cum = jnp.exp(a_cum)
    dte = jnp.exp(a_tot - a_cum)
    dt_x = _split_dot(dt, ehd_ref[...], 2)
    ec_x = _split_dot(e_cum, ehd_ref[...], 2)
    dte_x = _split_dot(dte, ehd_ref[...], 2)
    xdt = xs * dt_x
    xdt_b = xdt.astype(BF16)
    xst_b = (xdt * dte_x).astype(BF16)
    a_cum_t = a_cum.T

    kk = lax.broadcasted_iota(jnp.int32, (q, q), 1)
    qq = lax.broadcasted_iota(jnp.int32, (q, q), 0)
    in_order = (kk >= qq) if reverse else (kk <= qq)
    lane = lax.broadcasted_iota(jnp.int32, (q, LANE), 1)
    off = SSD_HEADS if reverse else 0
    hpg = SSD_HEADS // SSD_GROUPS
    gw = hpg * SSD_HEAD_DIM
    y_parts = []
    for g in range(SSD_GROUPS):
        bm_b = bm[g].astype(BF16)
        cm_b = cm[g].astype(BF16)
        scores = lax.dot_general(cm_b, bm_b, (((1,), (1,)), ((), ())), preferred_element_type=F32)
        h_prev = h_ref[g]
        y_off = jnp.dot(cm_b, h_prev.astype(BF16), preferred_element_type=F32) * ec_x[:, g * gw:(g + 1) * gw]
        pair_out = []
        for j in range(hpg // 2):
            xp = xdt_b[:, g * gw + j * LANE:g * gw + (j + 1) * LANE]
            res = []
            for e in range(2):
                hd = g * hpg + 2 * j + e
                a_q = a_cum[:, off + hd:off + hd + 1]
                a_k = a_cum_t[off + hd:off + hd + 1, :]
                decay = jnp.exp(jnp.where(in_order, a_q - a_k, -jnp.inf))
                pmat = (scores * decay).astype(BF16)
                res.append(jnp.dot(pmat, xp, preferred_element_type=F32))
            pair_out.append(jnp.where(lane < SSD_HEAD_DIM, res[0], res[1]))
        y_parts.append(jnp.concatenate(pair_out, axis=1) + y_off)
        st = jnp.dot(bm[g].T.astype(BF16), xst_b[:, g * gw:(g + 1) * gw], preferred_element_type=F32)
        h_ref[g] = h_prev * ec_x[end:end + 1, g * gw:(g + 1) * gw] + st
    return jnp.concatenate(y_parts, axis=1)


def _tri_cumsum(tri_ref, da):
    acc = None
    rem = da
    for _ in range(3):
        piece = rem.astype(BF16)
        term = jnp.dot(tri_ref[...], piece, preferred_element_type=F32)
        acc = term if acc is None else acc + term
        rem = rem - piece.astype(F32)
    return acc


def _ssd_direction(xbc, dt, conv, dt_bias, a_row, tri, ehd, extra, *, ctx_len, reverse):
    bsz, t_all, cch = xbc.shape
    q = SSD_BLOCK
    assert t_all % q == 0 and ctx_len % q == 0
    n_chunks = t_all // q
    ctx_chunks = ctx_len // q
    halo = 8
    hb = q // halo
    n_halo = t_all // halo
    cidx = lambda i: _ssd_chunk_index(i, n_chunks, ctx_chunks, reverse)
    chunk = lambda w: pl.BlockSpec((1, q, w), lambda b, i: (b, cidx(i), 0))
    tail = [_const_spec(dt_bias.shape), _const_spec(a_row.shape), _const_spec(tri.shape), _const_spec(ehd.shape)]
    if reverse:
        y_f, z, d_x, g_ssd = extra
        in_specs = [chunk(cch), chunk(LANE)] + tail + [
            chunk(SSD_INNER), chunk(SSD_INNER), _const_spec(d_x.shape), _const_spec(g_ssd.shape)]
        args = [xbc, dt, dt_bias, a_row, tri, ehd, y_f, z, d_x, g_ssd]
        out_specs = chunk(SSD_INNER)
        out_shape = jax.ShapeDtypeStruct((bsz, t_all, SSD_INNER), BF16)
    else:
        conv_w, conv_b = conv
        in_specs = [
            chunk(cch),
            pl.BlockSpec((1, halo, cch), lambda b, i: (b, jnp.maximum(cidx(i) * hb - 1, 0), 0)),
            pl.BlockSpec((1, halo, cch), lambda b, i: (b, jnp.minimum((cidx(i) + 1) * hb, n_halo - 1), 0)),
            chunk(LANE), _const_spec(conv_w.shape), _const_spec(conv_b.shape)] + tail
        args = [xbc, xbc, xbc, dt, conv_w, conv_b, dt_bias, a_row, tri, ehd]
        out_specs = [chunk(SSD_INNER), chunk(cch)]
        out_shape = [jax.ShapeDtypeStruct((bsz, t_all, SSD_INNER), F32),
                     jax.ShapeDtypeStruct((bsz, t_all, cch), F32)]
    hpg = SSD_HEADS // SSD_GROUPS
    return pl.pallas_call(
        functools.partial(_ssd_kernel, reverse=reverse, n_chunks=n_chunks, ctx_chunks=ctx_chunks),
        grid=(bsz, n_chunks),
        in_specs=in_specs,
        out_specs=out_specs,
        out_shape=out_shape,
        scratch_shapes=[pltpu.VMEM((SSD_GROUPS, SSD_STATE, hpg * SSD_HEAD_DIM), F32)],
        compiler_params=_cparams(("parallel", "arbitrary")),
        name="ssd_bwd" if reverse else "ssd_fwd",
    )(*args)


def _post_kernel(x_ref, ym_ref, ys_ref, yg_ref, gate_ref, mod_ref, wo_ref, wi_ref, wf_ref,
                 l1g_ref, l1b_ref, l2g_ref, l2b_ref, o_ref, act_ref, *, tm, n_lat, th):
    is_ctx = _is_ctx_rows(tm, n_lat)
    d = D_MODEL
    mix = None
    row0 = 0
    for j, y_ref in enumerate((ym_ref, ys_ref, yg_ref)):
        kw = y_ref.shape[-1]
        br = jnp.dot(y_ref[0], wo_ref[row0:row0 + kw, :], preferred_element_type=F32)
        br = gate_ref[0, :, j * d:(j + 1) * d].astype(F32) * br
        mix = br if mix is None else mix + br
        row0 += kw
    g1 = _mod_rows(mod_ref, 2 * d, d, is_ctx)
    x1 = _layer_norm(DEEPNORM_ALPHA * x_ref[0] + g1 * mix) * l1g_ref[...] + l1b_ref[...]

    sh2 = _mod_rows(mod_ref, 3 * d, d, is_ctx)
    sc2 = _mod_rows(mod_ref, 4 * d, d, is_ctx)
    g2 = _mod_rows(mod_ref, 5 * d, d, is_ctx)
    h2 = (_layer_norm(x1) * (1.0 + sc2) + sh2).astype(BF16)
    hid = FFN_HIDDEN
    for c0 in range(0, hid, th):
        gt = jnp.dot(h2, wi_ref[:, c0:c0 + th], preferred_element_type=F32)
        up = jnp.dot(h2, wi_ref[:, hid + c0:hid + c0 + th], preferred_element_type=F32)
        act_ref[:, c0:c0 + th] = (gt * jax.nn.sigmoid(gt) * up).astype(BF16)
    fx = jnp.dot(act_ref[...], wf_ref[...], preferred_element_type=F32)
    o_ref[0] = _layer_norm(DEEPNORM_ALPHA * x1 + g2 * fx) * l2g_ref[...] + l2b_ref[...]


def _post_mixer(xa, y_mla, y_ssd, y_gqa, gate, mod, w_out, w_ffn_in, w_ffn_out, l1g, l1b, l2g, l2b,
                *, n_lat, tm, with_ctx):
    bsz, t_all, d = xa.shape
    row = lambda w: pl.BlockSpec((1, tm, w), lambda b, t: (b, t, 0))
    vec = _const_spec((1, d))
    t_out = t_all if with_ctx else n_lat
    return pl.pallas_call(
        functools.partial(_post_kernel, tm=tm, n_lat=n_lat, th=256),
        grid=(bsz, t_out // tm),
        in_specs=[row(d), row(MLA_OUT), row(SSD_INNER), row(GQA_OUT), row(GATE_IN), _const_spec(mod.shape),
                  _const_spec(w_out.shape), _const_spec(w_ffn_in.shape), _const_spec(w_ffn_out.shape),
                  vec, vec, vec, vec],
        out_specs=pl.BlockSpec((1, tm, d), lambda b, t: (b, t, 0)),
        out_shape=jax.ShapeDtypeStruct((bsz, t_out, d), F32),
        scratch_shapes=[pltpu.VMEM((tm, FFN_HIDDEN), BF16)],
        compiler_params=_cparams(("parallel", "parallel")),
        name="post_mixer",
    )(xa, y_mla, y_ssd, y_gqa, gate, mod, w_out, w_ffn_in, w_ffn_out, l1g, l1b, l2g, l2b)


def _rope_tables(n_lat, ctx_len, rot_dim, lane_dim, lane_off, scale):
    rows = n_lat // GRID_W
    row = jnp.repeat(jnp.arange(rows), GRID_W)
    col = jnp.tile(jnp.arange(GRID_W), rows)
    n_freq = rot_dim // 4
    inv_freq = ROPE_THETA ** (-jnp.arange(n_freq, dtype=F32) / n_freq)
    ang = jnp.concatenate([row[:, None] * inv_freq, col[:, None] * inv_freq], axis=-1)
    cos, sin = jnp.cos(ang), jnp.sin(ang)
    half = rot_dim // 2
    lane = np.arange(LANE)
    rel = (lane % lane_dim) - lane_off
    is_rot = (rel >= 0) & (rel < rot_dim)
    idx = np.where(is_rot, rel % half, 0)
    first = is_rot & (rel < half)
    second = is_rot & (rel >= half)
    c = jnp.where(is_rot[None, :], cos[:, idx], 1.0)
    s1 = jnp.where(first[None, :], -sin[:, idx], 0.0)
    s2 = jnp.where(second[None, :], sin[:, idx], 0.0)
    lat = jnp.stack([c, s1, s2])
    ctx = jnp.stack([jnp.ones((ctx_len, LANE), F32), jnp.zeros((ctx_len, LANE), F32),
                     jnp.zeros((ctx_len, LANE), F32)])
    return (jnp.concatenate([lat, ctx], axis=1) * scale).astype(F32)


def _pad_cols(w, width, offset=0):
    out = jnp.zeros((w.shape[0], width), w.dtype)
    return out.at[:, offset:offset + w.shape[1]].set(w)


def _layout_w_in(w):
    c = 0
    q_lat = w[:, c:c + MLA_Q_RANK]; c += MLA_Q_RANK
    kv_lat = w[:, c:c + MLA_KV_RANK]; c += MLA_KV_RANK
    k_rope = _pad_cols(w[:, c:c + MLA_ROPE], LANE, MLA_NOPE); c += MLA_ROPE
    z = w[:, c:c + SSD_INNER]; c += SSD_INNER
    xbc = w[:, c:c + SSD_CONV_CH]; c += SSD_CONV_CH
    dt = _pad_cols(w[:, c:c + 2 * SSD_HEADS], LANE); c += 2 * SSD_HEADS
    rest = w[:, c:]
    return jnp.concatenate([q_lat, kv_lat, k_rope, z, xbc, dt, rest], axis=1).astype(BF16)


def _layout_w_uq(w):
    w = w.reshape(w.shape[0], MLA_HEADS, MLA_QK)
    w = jnp.pad(w, ((0, 0), (0, 0), (0, LANE - MLA_QK)))
    return w.reshape(w.shape[0], MLA_HEADS * LANE).astype(BF16)


def _layout_w_ukv(w):
    w = w.reshape(w.shape[0], MLA_HEADS, MLA_NOPE + MLA_V)
    k = jnp.pad(w[:, :, :MLA_NOPE], ((0, 0), (0, 0), (0, LANE - MLA_NOPE)))
    v = w[:, :, MLA_NOPE:]
    return jnp.concatenate([k.reshape(w.shape[0], -1), v.reshape(w.shape[0], -1)], axis=1).astype(BF16)


def _lane_row(vec, width=LANE):
    return _pad_cols(vec.reshape(1, -1).astype(F32), width)


def kernel(x, c, ctx, c_ctx, w_mod, b_mod, w_in, b_gate, w_uq, g_q_mla, w_ukv, g_kv_mla, conv_w, conv_b,
           a_log, dt_bias, d_skip, g_ssd, g_q_gqa, g_k_gqa, w_out, ln1_g, ln1_b, w_ffn_in, w_ffn_out,
           ln2_g, ln2_b):
    bsz, n_lat, d = x.shape
    ctx_len = ctx.shape[1]
    depth = w_mod.shape[0]
    assert d == D_MODEL and bsz <= MOD_ROWS // 2
    tm = ROW_TILE
    assert ctx_len % tm == 0 and n_lat % tm == 0

    cvec = jnp.zeros((MOD_ROWS, d), F32).at[:bsz].set(c).at[MOD_ROWS // 2].set(c_ctx)
    mod_all = _modulation(cvec, w_mod, b_mod)

    mla_scale = MLA_QK ** -0.5 * LOG2E
    gqa_scale = GQA_HEAD_DIM ** -0.5 * LOG2E
    tab_mq = _rope_tables(n_lat, ctx_len, MLA_ROPE, LANE, MLA_NOPE, mla_scale)
    tab_mk = _rope_tables(n_lat, ctx_len, MLA_ROPE, LANE, MLA_NOPE, 1.0)
    tab_gq = _rope_tables(n_lat, ctx_len, GQA_HEAD_DIM, GQA_HEAD_DIM, 0, gqa_scale)
    tab_gk = _rope_tables(n_lat, ctx_len, GQA_HEAD_DIM, GQA_HEAD_DIM, 0, 1.0)

    qn = SSD_CHUNK
    r = np.arange(qn)
    tri_f = jnp.asarray((r[None, :] <= r[:, None]).astype(np.float32), BF16)
    tri_b = jnp.asarray((r[None, :] >= r[:, None]).astype(np.float32), BF16)
    lane1k = np.arange(SSD_INNER) // SSD_HEAD_DIM
    ehd_f = jnp.asarray((np.arange(LANE)[:, None] == lane1k[None, :]).astype(np.float32), BF16)
    ehd_b = jnp.asarray((np.arange(LANE)[:, None] == (lane1k[None, :] + SSD_HEADS)).astype(np.float32), BF16)
    blk = np.arange(GQA_OUT) // GQA_HEAD_DIM
    bd = jnp.asarray((blk[:, None] == blk[None, :]).astype(np.float32), BF16)

    mla_attn = functools.partial(_attention, groups=MLA_HEADS // 4, q_w=4 * LANE, k_w=4 * LANE,
                                 vt_rows=2 * VT_ROWS,
                                 pairs=((0, ((0, 0), (1, 1))), (1, ((2, 2), (3, 3)))), n_lat=n_lat)
    gqa_attn = functools.partial(_attention, groups=GQA_KV_HEADS, q_w=2 * LANE, k_w=2 * LANE, vt_rows=VT_ROWS,
                                 pairs=((0, ((0, 0), (0, 1))), (0, ((1, 0), (1, 1)))), n_lat=n_lat)

    def attend(fn, q, k, v, with_ctx):
        y = fn(q, k, v, ctx_queries=False, tq=ATTN_TQ, tk=ATTN_TK)
        if with_ctx:
            y_c = fn(q, k, v, ctx_queries=True, tq=ctx_len, tk=ATTN_TK)
            y = jnp.concatenate([y, y_c], axis=1)
        return y

    xa = jnp.concatenate([x, ctx], axis=1)
    for i in range(depth):
        with_ctx = i < depth - 1
        mod = mod_all[i]
        w_inp = _layout_w_in(w_in[i])
        mla_consts = (g_q_mla[i].reshape(1, -1), g_kv_mla[i].reshape(1, -1),
                      _layout_w_uq(w_uq[i]), _layout_w_ukv(w_ukv[i]), tab_mq, tab_mk)
        gqa_consts = (jnp.tile(g_q_gqa[i], GQA_HEADS).reshape(1, -1),
                      jnp.tile(g_k_gqa[i], GQA_KV_HEADS).reshape(1, -1), bd, tab_gq, tab_gk)
        z, xbc, dtr, gate, q_m, k_m, v_m, q_g, k_g, v_g = _in_projection(
            xa, mod, w_inp, b_gate[i].reshape(1, -1), mla_consts, gqa_consts, n_lat=n_lat, tm=tm)
        y_mla = attend(mla_attn, q_m, k_m, v_m, with_ctx)
        y_gqa = attend(gqa_attn, q_g, k_g, v_g, with_ctx)
        a = -jnp.exp(a_log[i].astype(F32))
        a_row = _lane_row(a.reshape(-1))
        dtb_row = _lane_row(dt_bias[i].reshape(-1))
        y_f, u_act = _ssd_direction(xbc, dtr, (conv_w[i], conv_b[i].reshape(1, -1)), dtb_row, a_row,
                                    tri_f, ehd_f, None, ctx_len=ctx_len, reverse=False)
        d_x = jnp.repeat(d_skip[i], SSD_HEAD_DIM).reshape(1, -1)
        y_ssd = _ssd_direction(u_act, dtr, None, dtb_row, a_row, tri_b, ehd_b,
                               (y_f, z, d_x, g_ssd[i].reshape(1, -1)), ctx_len=ctx_len, reverse=True)
        xa = _post_mixer(xa, y_mla, y_ssd, y_gqa, gate, mod, w_out[i].astype(BF16),
                         w_ffn_in[i].astype(BF16), w_ffn_out[i].astype(BF16),
                         ln1_g[i].reshape(1, -1), ln1_b[i].reshape(1, -1),
                         ln2_g[i].reshape(1, -1), ln2_b[i].reshape(1, -1), n_lat=n_lat,
                         tm=tm if with_ctx else LATENT_ROW_TILE, with_ctx=with_ctx)
    return xa
```

```python
import functools
import math

import jax
import jax.numpy as jnp
import numpy as np
from jax import lax
from jax.experimental import pallas as pl
from jax.experimental.pallas import tpu as pltpu

F32 = jnp.float32
BF16 = jnp.bfloat16

D_MODEL = 1024
DEPTH = 2
GRID_W = 64
ROPE_THETA = 10000.0
NORM_EPS = 1e-6

MLA_HEADS = 8
MLA_Q_RANK = 384
MLA_KV_RANK = 256
MLA_NOPE = 64
MLA_ROPE = 32
MLA_V = 64
MLA_QK = MLA_NOPE + MLA_ROPE

SSD_HEADS = 16
SSD_HEAD_DIM = 64
SSD_INNER = SSD_HEADS * SSD_HEAD_DIM
SSD_GROUPS = 2
SSD_STATE = 128
SSD_CHUNK = 128
SSD_CONV_CH = SSD_INNER + 2 * SSD_GROUPS * SSD_STATE

GQA_HEADS = 8
GQA_KV_HEADS = 2
GQA_HEAD_DIM = 64

MLA_IN = MLA_Q_RANK + MLA_KV_RANK + MLA_ROPE
SSD_IN = SSD_INNER + SSD_CONV_CH + 2 * SSD_HEADS
GQA_IN = (GQA_HEADS + 2 * GQA_KV_HEADS) * GQA_HEAD_DIM
GATE_IN = 3 * D_MODEL
MLA_OUT = MLA_HEADS * MLA_V
GQA_OUT = GQA_HEADS * GQA_HEAD_DIM
FFN_HIDDEN = -(-8 * D_MODEL // (3 * 256)) * 256
DEEPNORM_ALPHA = (2 * DEPTH) ** 0.25

LANE = 128
MOD_ROWS = 8
LOG2E = 1.4426950408889634

C_PM = 0
W_PM = MLA_Q_RANK + MLA_KV_RANK + LANE
C_Z = C_PM + W_PM
C_XBC = C_Z + SSD_INNER
C_DT = C_XBC + SSD_CONV_CH
C_PG = C_DT + LANE
C_GATE = C_PG + GQA_IN
W_INP = C_GATE + GATE_IN

VMEM_LIMIT = 56 * 1024 * 1024
ROW_TILE = 256
LATENT_ROW_TILE = 512
ATTN_TQ = 1024
ATTN_TK = 512
ATTN_COL_BLOCK = 256
ATTN_KEY_BLOCK = 64
SSD_BLOCK = 2 * SSD_CHUNK
VT_TILE = ROW_TILE
VT_ROWS = 144


def _cparams(sem):
    return pltpu.CompilerParams(dimension_semantics=sem, vmem_limit_bytes=VMEM_LIMIT)


def _const_spec(shape):
    nd = len(shape)
    return pl.BlockSpec(shape, lambda *_: (0,) * nd, pipeline_mode=pl.Buffered(1))


def _layer_norm(x):
    mu = jnp.mean(x, axis=-1, keepdims=True)
    xc = x - mu
    var = jnp.mean(xc * xc, axis=-1, keepdims=True)
    return xc * lax.rsqrt(var + NORM_EPS)


def _split_dot(x, m01, parts):
    pieces = []
    rem = x
    for _ in range(parts):
        piece = rem.astype(BF16)
        pieces.append(piece)
        rem = rem - piece.astype(F32)
    return jnp.dot(jnp.concatenate(pieces, axis=1), jnp.concatenate([m01] * parts, axis=0),
                   preferred_element_type=F32)


def _mod_rows(mod_ref, col, width, is_ctx):
    b = pl.program_id(0)
    lat = mod_ref[pl.ds(b, 1), col:col + width]
    ctx = mod_ref[MOD_ROWS // 2:MOD_ROWS // 2 + 1, col:col + width]
    return jnp.where(is_ctx, ctx, lat)


def _is_ctx_rows(tm, n_lat):
    t = pl.program_id(1)
    rows = t * tm + lax.broadcasted_iota(jnp.int32, (tm, 1), 0)
    return rows >= n_lat


def _mod_kernel(c_ref, w_ref, b_ref, o_ref):
    c = c_ref[...]
    s = (c * jax.nn.sigmoid(c)).astype(BF16)
    o_ref[0] = jnp.dot(s, w_ref[0].astype(BF16), preferred_element_type=F32) + b_ref[0]


def _modulation(cvec, w_mod, b_mod):
    depth, d, n = w_mod.shape
    tn = 1024
    return pl.pallas_call(
        _mod_kernel,
        grid=(depth, n // tn),
        in_specs=[
            pl.BlockSpec((MOD_ROWS, d), lambda l, j: (0, 0)),
            pl.BlockSpec((1, d, tn), lambda l, j: (l, 0, j)),
            pl.BlockSpec((1, 1, tn), lambda l, j: (l, 0, j)),
        ],
        out_specs=pl.BlockSpec((1, MOD_ROWS, tn), lambda l, j: (l, 0, j)),
        out_shape=jax.ShapeDtypeStruct((depth, MOD_ROWS, n), F32),
        compiler_params=_cparams(("arbitrary", "arbitrary")),
        name="modulation",
    )(cvec, w_mod, b_mod.reshape(depth, 1, n))


def _rope_lanes(x, c, s_fwd, s_bwd, half):
    n = x.shape[-1]
    return x * c + pltpu.roll(x, n - half, axis=1) * s_fwd + pltpu.roll(x, half, axis=1) * s_bwd


def _mla_prep(pm, gq_ref, gkv_ref, wuq_ref, wukv_ref, tq_ref, tk_ref, q_ref, k_ref, v_ref):
    pq = pm[:, :MLA_Q_RANK]
    ckv = pm[:, MLA_Q_RANK:MLA_Q_RANK + MLA_KV_RANK]
    kr = pm[:, MLA_Q_RANK + MLA_KV_RANK:]

    def rms(x, g):
        return x * lax.rsqrt(jnp.mean(x * x, axis=-1, keepdims=True) + NORM_EPS) * g

    q = jnp.dot(rms(pq, gq_ref[...]).astype(BF16), wuq_ref[...], preferred_element_type=F32)
    kv = jnp.dot(rms(ckv, gkv_ref[...]).astype(BF16), wukv_ref[...], preferred_element_type=F32)
    half = MLA_ROPE // 2
    kr = _rope_lanes(kr, tk_ref[0], tk_ref[1], tk_ref[2], half)
    cq, sq1, sq2 = tq_ref[0], tq_ref[1], tq_ref[2]
    for hd in range(MLA_HEADS):
        sl = slice(hd * LANE, (hd + 1) * LANE)
        q_ref[0, :, sl] = _rope_lanes(q[:, sl], cq, sq1, sq2, half).astype(q_ref.dtype)
        k_ref[0, :, sl] = (kv[:, sl] + kr).astype(k_ref.dtype)
    ones = jnp.ones((VT_ROWS - LANE, pm.shape[0]), v_ref.dtype)
    for j in range(MLA_HEADS // 2):
        c0 = MLA_HEADS * LANE + j * LANE
        v_ref[0, 0, j * VT_ROWS:j * VT_ROWS + LANE, :] = kv[:, c0:c0 + LANE].T.astype(v_ref.dtype)
        v_ref[0, 0, j * VT_ROWS + LANE:(j + 1) * VT_ROWS, :] = ones


def _gqa_prep(pg, gq_ref, gk_ref, bd_ref, tq_ref, tk_ref, q_ref, k_ref, v_ref):
    hd = GQA_HEAD_DIM
    q = pg[:, :GQA_OUT]
    k = pg[:, GQA_OUT:GQA_OUT + LANE]
    v = pg[:, GQA_OUT + LANE:]
    half = hd // 2

    def head_rms(x, g):
        w = x.shape[-1]
        ms = _split_dot(x * x, bd_ref[:w, :w], 2) * (1.0 / hd)
        return x * lax.rsqrt(ms + NORM_EPS) * g

    qn = head_rms(q, gq_ref[...])
    for j in range(GQA_OUT // LANE):
        sl = slice(j * LANE, (j + 1) * LANE)
        q_ref[0, :, sl] = _rope_lanes(qn[:, sl], tq_ref[0], tq_ref[1], tq_ref[2], half).astype(q_ref.dtype)
    kn = head_rms(k, gk_ref[...])
    kn = _rope_lanes(kn, tk_ref[0], tk_ref[1], tk_ref[2], half)
    lane = lax.broadcasted_iota(jnp.int32, kn.shape, 1)
    low = lane < hd
    kn_sw = pltpu.roll(kn, hd, axis=1)
    zero = jnp.zeros_like(kn)
    k_ref[0, :, 0 * LANE:1 * LANE] = jnp.where(low, kn, zero).astype(k_ref.dtype)
    k_ref[0, :, 1 * LANE:2 * LANE] = jnp.where(low, zero, kn_sw).astype(k_ref.dtype)
    k_ref[0, :, 2 * LANE:3 * LANE] = jnp.where(low, kn_sw, zero).astype(k_ref.dtype)
    k_ref[0, :, 3 * LANE:4 * LANE] = jnp.where(low, zero, kn).astype(k_ref.dtype)
    v_t = v.T.astype(v_ref.dtype)
    ones = jnp.ones((VT_ROWS - LANE, v.shape[0]), v_ref.dtype)
    for g in range(GQA_KV_HEADS):
        vg = v_t[g * hd:(g + 1) * hd]
        v_ref[0, 0, g * VT_ROWS:g * VT_ROWS + hd, :] = vg
        v_ref[0, 0, g * VT_ROWS + hd:g * VT_ROWS + LANE, :] = vg
        v_ref[0, 0, g * VT_ROWS + LANE:(g + 1) * VT_ROWS, :] = ones


def _inproj_kernel(x_ref, mod_ref, w_ref, bg_ref,
                   gqm_ref, gkvm_ref, wuq_ref, wukv_ref, tmq_ref, tmk_ref,
                   gqg_ref, gkg_ref, bd_ref, tgq_ref, tgk_ref,
                   z_ref, xbc_ref, dt_ref, gate_ref, qm_ref, km_ref, vm_ref, qg_ref, kg_ref, vg_ref,
                   *, tm, n_lat):
    is_ctx = _is_ctx_rows(tm, n_lat)
    d = D_MODEL
    sh = _mod_rows(mod_ref, 0, d, is_ctx)
    sc = _mod_rows(mod_ref, d, d, is_ctx)
    h = (_layer_norm(x_ref[0]) * (1.0 + sc) + sh).astype(BF16)

    def proj(c0, width):
        return jnp.dot(h, w_ref[:, c0:c0 + width], preferred_element_type=F32)

    _mla_prep(proj(C_PM, W_PM), gqm_ref, gkvm_ref, wuq_ref, wukv_ref, tmq_ref, tmk_ref, qm_ref, km_ref, vm_ref)
    _gqa_prep(proj(C_PG, GQA_IN), gqg_ref, gkg_ref, bd_ref, tgq_ref, tgk_ref, qg_ref, kg_ref, vg_ref)
    z_ref[0] = proj(C_Z, SSD_INNER)
    xbc_ref[0] = proj(C_XBC, SSD_CONV_CH)
    dt_ref[0] = proj(C_DT, LANE)
    for j in range(3):
        c0 = j * D_MODEL
        logits = proj(C_GATE + c0, D_MODEL) + bg_ref[:, c0:c0 + D_MODEL]
        gate_ref[0, :, c0:c0 + D_MODEL] = jax.nn.sigmoid(logits).astype(gate_ref.dtype)


def _in_projection(xa, mod, w_inp, b_gate, mla_consts, gqa_consts, *, n_lat, tm):
    bsz, t_all, d = xa.shape
    assert tm == VT_TILE
    row = lambda w: pl.BlockSpec((1, tm, w), lambda b, t: (b, t, 0))
    tab = pl.BlockSpec((3, tm, LANE), lambda b, t: (0, t, 0))
    vt_spec = lambda rows: pl.BlockSpec((1, 1, rows, tm), lambda b, t: (b, t, 0, 0))
    g_qm, g_kvm, wuq_p, wukv_p, tab_mq, tab_mk = mla_consts
    g_qg, g_kg, bd, tab_gq, tab_gk = gqa_consts
    hw = MLA_HEADS * LANE
    kw = GQA_KV_HEADS * 2 * LANE
    vm_rows = (MLA_HEADS // 2) * VT_ROWS
    vg_rows = GQA_KV_HEADS * VT_ROWS
    seq = lambda w, dt: jax.ShapeDtypeStruct((bsz, t_all, w), dt)
    vts = lambda rows: jax.ShapeDtypeStruct((bsz, t_all // tm, rows, tm), BF16)
    consts = (mod, w_inp, b_gate, g_qm, g_kvm, wuq_p, wukv_p)
    return pl.pallas_call(
        functools.partial(_inproj_kernel, tm=tm, n_lat=n_lat),
        grid=(bsz, t_all // tm),
        in_specs=[row(d)] + [_const_spec(a.shape) for a in consts] + [tab, tab]
                 + [_const_spec(a.shape) for a in (g_qg, g_kg, bd)] + [tab, tab],
        out_specs=[row(SSD_INNER), row(SSD_CONV_CH), row(LANE), row(GATE_IN),
                   row(hw), row(hw), vt_spec(vm_rows), row(GQA_OUT), row(kw), vt_spec(vg_rows)],
        out_shape=[seq(SSD_INNER, F32), seq(SSD_CONV_CH, F32), seq(LANE, F32), seq(GATE_IN, BF16),
                   seq(hw, BF16), seq(hw, BF16), vts(vm_rows), seq(GQA_OUT, BF16), seq(kw, BF16), vts(vg_rows)],
        compiler_params=_cparams(("parallel", "parallel")),
        name="in_projection",
    )(xa, *consts, tab_mq, tab_mk, g_qg, g_kg, bd, tab_gq, tab_gk)


def _attn_kernel(q_ref, k_ref, vt_ref, o_ref, m_ref, acc_ref, a0_ref, a1_ref, c0_ref, c1_ref,
                 p0_ref, p1_ref, s0_ref, s1_ref, *, pairs, tq, tk, n_lat, ctx_len, ctx_queries):
    cb = ATTN_COL_BLOCK
    vr = VT_ROWS
    s_bufs, p_bufs, a_bufs, c_bufs = (s0_ref, s1_ref), (p0_ref, p1_ref), (a0_ref, a1_ref), (c0_ref, c1_ref)

    def scores(r0, size, par):
        for pi, (_, units) in enumerate(pairs):
            for ui, (qb, kb) in enumerate(units):
                s = lax.dot_general(
                    k_ref[0, pl.ds(r0, size), kb * LANE:(kb + 1) * LANE],
                    q_ref[0, :, qb * LANE:(qb + 1) * LANE],
                    (((1,), (1,)), ((), ())), preferred_element_type=F32)
                s_bufs[par][pi, :size, ui * tq:(ui + 1) * tq] = s
                c_bufs[par][pi, :, ui * tq:(ui + 1) * tq] = jnp.max(s, axis=0, keepdims=True)

    def softmax(size, par):
        for pi in range(len(pairs)):
            for c in range(0, 2 * tq, cb):
                m_prev = m_ref[pi, :, c:c + cb]
                m_next = jnp.maximum(m_prev, c_bufs[par][pi, :, c:c + cb])
                for r in range(0, size, ATTN_KEY_BLOCK):
                    blk = s_bufs[par][pi, r:r + ATTN_KEY_BLOCK, c:c + cb]
                    p_bufs[par][pi, r:r + ATTN_KEY_BLOCK, c:c + cb] = jnp.exp2(blk - m_next).astype(BF16)
                a_bufs[par][pi, :, c:c + cb] = jnp.exp2(m_prev - m_next)
                m_ref[pi, :, c:c + cb] = m_next

    def values(u0, size, par):
        for pi, (vb, _) in enumerate(pairs):
            vt = jnp.concatenate([vt_ref[0, u0 + i, vb * vr:(vb + 1) * vr, :]
                                  for i in range(size // VT_TILE)], axis=1)
            pv = jnp.dot(vt, p_bufs[par][pi, :size, :], preferred_element_type=F32)
            acc_ref[pi] = a_bufs[par][pi] * acc_ref[pi] + pv

    m_ref[...] = jnp.full(m_ref.shape, -jnp.inf, F32)
    acc_ref[...] = jnp.zeros(acc_ref.shape, F32)
    tku = tk // VT_TILE
    ctx_u = n_lat // VT_TILE
    scores(n_lat, ctx_len, 0)
    if ctx_queries:
        softmax(ctx_len, 0)
        values(ctx_u, ctx_len, 0)
    else:
        n = n_lat // tk
        scores(0, tk, 1)
        softmax(ctx_len, 0)
        scores(tk, tk, 0)
        softmax(tk, 1)
        values(ctx_u, ctx_len, 0)

        def body(jj, carry):
            r = pl.multiple_of(jj * (2 * tk), 2 * tk)
            u = jj * (2 * tku)
            scores(r + 2 * tk, tk, 1)
            softmax(tk, 0)
            values(u, tk, 1)
            scores(r + 3 * tk, tk, 0)
            softmax(tk, 1)
            values(u + tku, tk, 0)
            return carry

        lax.fori_loop(0, (n - 2) // 2, body, 0)
        softmax(tk, 0)
        values((n - 2) * tku, tk, 1)
        values((n - 1) * tku, tk, 0)
    row = lax.broadcasted_iota(jnp.int32, (LANE, tq), 0)
    for pi in range(len(pairs)):
        acc = acc_ref[pi]
        o_t = acc[:LANE] / acc[LANE:LANE + 1]
        o_t = jnp.where(row < LANE // 2, o_t[:, :tq], o_t[:, tq:])
        o_ref[0, :, pi * LANE:(pi + 1) * LANE] = o_t.T.astype(o_ref.dtype)


def _attention(q, k, vt, *, groups, q_w, k_w, vt_rows, pairs, n_lat, ctx_queries, tq, tk):
    bsz, t_all, _ = k.shape
    ctx_len = t_all - n_lat
    out_w = len(pairs) * LANE
    rows = ctx_len if ctx_queries else n_lat
    q_tile0 = n_lat // tq if ctx_queries else 0
    assert rows % tq == 0 and n_lat % tq == 0 and n_lat % (2 * tk) == 0 and ctx_len <= tk
    assert (2 * tq) % ATTN_COL_BLOCK == 0 and tk % VT_TILE == 0 and ctx_len % VT_TILE == 0
    state = pltpu.VMEM((len(pairs), 1, 2 * tq), F32)
    acc = pltpu.VMEM((len(pairs), VT_ROWS, 2 * tq), F32)
    pbuf = pltpu.VMEM((len(pairs), tk, 2 * tq), BF16)
    sbuf = pltpu.VMEM((len(pairs), tk, 2 * tq), F32)
    return pl.pallas_call(
        functools.partial(_attn_kernel, pairs=pairs, tq=tq, tk=tk, n_lat=n_lat, ctx_len=ctx_len,
                          ctx_queries=ctx_queries),
        grid=(bsz, groups, rows // tq),
        in_specs=[
            pl.BlockSpec((1, tq, q_w), lambda b, g, t: (b, t + q_tile0, g)),
            pl.BlockSpec((1, t_all, k_w), lambda b, g, t: (b, 0, g)),
            pl.BlockSpec((1, t_all // VT_TILE, vt_rows, VT_TILE), lambda b, g, t: (b, 0, g, 0)),
        ],
        out_specs=pl.BlockSpec((1, tq, out_w), lambda b, g, t: (b, t, g)),
        out_shape=jax.ShapeDtypeStruct((bsz, rows, groups * out_w), BF16),
        scratch_shapes=[state, acc, state, state, state, state, pbuf, pbuf, sbuf, sbuf],
        compiler_params=_cparams(("parallel", "parallel", "arbitrary")),
        name="attention_ctx" if ctx_queries else "attention",
    )(q, k, vt)


def _ssd_chunk_index(i, n_chunks, ctx_chunks, reverse):
    if reverse:
        return n_chunks - 1 - i
    lat_chunks = n_chunks - ctx_chunks
    return jnp.where(i < ctx_chunks, lat_chunks + i, i - ctx_chunks)


def _ssd_kernel(*refs, reverse, n_chunks, ctx_chunks):
    if reverse:
        (u_ref, dt_ref, dtb_ref, a_ref, tri_ref, ehd_ref,
         yf_ref, z_ref, dsk_ref, g_ref, y_ref, h_ref) = refs
    else:
        (xbc_ref, prev_ref, next_ref, dt_ref, cw_ref, cb_ref, dtb_ref, a_ref, tri_ref, ehd_ref,
         y_ref, uo_ref, h_ref) = refs
    q = SSD_CHUNK
    rows = SSD_BLOCK
    i = pl.program_id(1)
    ci = _ssd_chunk_index(i, n_chunks, ctx_chunks, reverse)

    @pl.when(i == 0)
    def _():
        h_ref[...] = jnp.zeros_like(h_ref)

    if reverse:
        u_blk = u_ref[0]
    else:
        x = xbc_ref[0]
        lat_chunks = n_chunks - ctx_chunks
        at_start = jnp.logical_or(ci == 0, ci == lat_chunks)
        at_end = jnp.logical_or(ci == lat_chunks - 1, ci == n_chunks - 1)
        prev_row = jnp.where(at_start, 0.0, prev_ref[0, 7:8, :])
        next_row = jnp.where(at_end, 0.0, next_ref[0, 0:1, :])
        row = lax.broadcasted_iota(jnp.int32, (rows, 1), 0)
        x_prev = jnp.where(row == 0, prev_row, pltpu.roll(x, 1, axis=0))
        x_next = jnp.where(row == rows - 1, next_row, pltpu.roll(x, rows - 1, axis=0))
        u_blk = cw_ref[0:1, :] * x_prev + cw_ref[1:2, :] * x + cw_ref[2:3, :] * x_next + cb_ref[...]
        u_blk = u_blk * jax.nn.sigmoid(u_blk)
        uo_ref[0] = u_blk

    subs = range(rows // q)
    for sub in (reversed(subs) if reverse else subs):
        sl = slice(sub * q, (sub + 1) * q)
        y = _ssd_chunk(u_blk[sl], dt_ref[0, sl, :], dtb_ref, a_ref, tri_ref, ehd_ref, h_ref, reverse)
        if reverse:
            yt = yf_ref[0, sl, :] + y + dsk_ref[...] * u_blk[sl, :SSD_INNER]
            zz = z_ref[0, sl, :]
            yt = yt * (zz * jax.nn.sigmoid(zz))
            yt = yt * lax.rsqrt(jnp.mean(yt * yt, axis=-1, keepdims=True) + NORM_EPS) * g_ref[...]
            y_ref[0, sl, :] = yt.astype(y_ref.dtype)
        else:
            y_ref[0, sl, :] = y


def _ssd_chunk(u, dt_raw, dtb_ref, a_ref, tri_ref, ehd_ref, h_ref, reverse):
    q = SSD_CHUNK
    xs = u[:, :SSD_INNER]
    gs = SSD_STATE
    bm = [u[:, SSD_INNER + g * gs:SSD_INNER + (g + 1) * gs] for g in range(SSD_GROUPS)]
    cm = [u[:, SSD_INNER + (SSD_GROUPS + g) * gs:SSD_INNER + (SSD_GROUPS + g + 1) * gs]
          for g in range(SSD_GROUPS)]

    dt = jax.nn.softplus(dt_raw + dtb_ref[...])
    da = dt * a_ref[...]
    a_cum = _tri_cumsum(tri_ref, da)
    end = 0 if reverse else q - 1
    a_tot = a_cum[end:end + 1, :]
    e_cum = jnp.exp(a_cum)
    dte = jnp.exp(a_tot - a_cum)
    dt_x = _split_dot(dt, ehd_ref[...], 2)
    ec_x = _split_dot(e_cum, ehd_ref[...], 2)
    dte_x = _split_dot(dte, ehd_ref[...], 2)
    xdt = xs * dt_x
    xdt_b = xdt.astype(BF16)
    xst_b = (xdt * dte_x).astype(BF16)
    a_cum_t = a_cum.T

    kk = lax.broadcasted_iota(jnp.int32, (q, q), 1)
    qq = lax.broadcasted_iota(jnp.int32, (q, q), 0)
    in_order = (kk >= qq) if reverse else (kk <= qq)
    lane = lax.broadcasted_iota(jnp.int32, (q, LANE), 1)
    off = SSD_HEADS if reverse else 0
    hpg = SSD_HEADS // SSD_GROUPS
    gw = hpg * SSD_HEAD_DIM
    y_parts = []
    for g in range(SSD_GROUPS):
        bm_b = bm[g].astype(BF16)
        cm_b = cm[g].astype(BF16)
        scores = lax.dot_general(cm_b, bm_b, (((1,), (1,)), ((), ())), preferred_element_type=F32)
        h_prev = h_ref[g]
        y_off = jnp.dot(cm_b, h_prev.astype(BF16), preferred_element_type=F32) * ec_x[:, g * gw:(g + 1) * gw]
        pair_out = []
        for j in range(hpg // 2):
            xp = xdt_b[:, g * gw + j * LANE:g * gw + (j + 1) * LANE]
            res = []
            for e in range(2):
                hd = g * hpg + 2 * j + e
                a_q = a_cum[:, off + hd:off + hd + 1]
                a_k = a_cum_t[off + hd:off + hd + 1, :]
                decay = jnp.exp(jnp.where(in_order, a_q - a_k, -jnp.inf))
                pmat = (scores * decay).astype(BF16)
                res.append(jnp.dot(pmat, xp, preferred_element_type=F32))
            pair_out.append(jnp.where(lane < SSD_HEAD_DIM, res[0], res[1]))
        y_parts.append(jnp.concatenate(pair_out, axis=1) + y_off)
        st = jnp.dot(bm[g].T.astype(BF16), xst_b[:, g * gw:(g + 1) * gw], preferred_element_type=F32)
        h_ref[g] = h_prev * ec_x[end:end + 1, g * gw:(g + 1) * gw] + st
    return jnp.concatenate(y_parts, axis=1)


def _tri_cumsum(tri_ref, da):
    acc = None
    rem = da
    for _ in range(3):
        piece = rem.astype(BF16)
        term = jnp.dot(tri_ref[...], piece, preferred_element_type=F32)
        acc = term if acc is None else acc + term
        rem = rem - piece.astype(F32)
    return acc


def _ssd_direction(xbc, dt, conv, dt_bias, a_row, tri, ehd, extra, *, ctx_len, reverse):
    bsz, t_all, cch = xbc.shape
    q = SSD_BLOCK
    assert t_all % q == 0 and ctx_len % q == 0
    n_chunks = t_all // q
    ctx_chunks = ctx_len // q
    halo = 8
    hb = q // halo
    n_halo = t_all // halo
    cidx = lambda i: _ssd_chunk_index(i, n_chunks, ctx_chunks, reverse)
    chunk = lambda w: pl.BlockSpec((1, q, w), lambda b, i: (b, cidx(i), 0))
    tail = [_const_spec(dt_bias.shape), _const_spec(a_row.shape), _const_spec(tri.shape), _const_spec(ehd.shape)]
    if reverse:
        y_f, z, d_x, g_ssd = extra
        in_specs = [chunk(cch), chunk(LANE)] + tail + [
            chunk(SSD_INNER), chunk(SSD_INNER), _const_spec(d_x.shape), _const_spec(g_ssd.shape)]
        args = [xbc, dt, dt_bias, a_row, tri, ehd, y_f, z, d_x, g_ssd]
        out_specs = chunk(SSD_INNER)
        out_shape = jax.ShapeDtypeStruct((bsz, t_all, SSD_INNER), BF16)
    else:
        conv_w, conv_b = conv
        in_specs = [
            chunk(cch),
            pl.BlockSpec((1, halo, cch), lambda b, i: (b, jnp.maximum(cidx(i) * hb - 1, 0), 0)),
            pl.BlockSpec((1, halo, cch), lambda b, i: (b, jnp.minimum((cidx(i) + 1) * hb, n_halo - 1), 0)),
            chunk(LANE), _const_spec(conv_w.shape), _const_spec(conv_b.shape)] + tail
        args = [xbc, xbc, xbc, dt, conv_w, conv_b, dt_bias, a_row, tri, ehd]
        out_specs = [chunk(SSD_INNER), chunk(cch)]
        out_shape = [jax.ShapeDtypeStruct((bsz, t_all, SSD_INNER), F32),
                     jax.ShapeDtypeStruct((bsz, t_all, cch), F32)]
    hpg = SSD_HEADS // SSD_GROUPS
    return pl.pallas_call(
        functools.partial(_ssd_kernel, reverse=reverse, n_chunks=n_chunks, ctx_chunks=ctx_chunks),
        grid=(bsz, n_chunks),
        in_specs=in_specs,
        out_specs=out_specs,
        out_shape=out_shape,
        scratch_shapes=[pltpu.VMEM((SSD_GROUPS, SSD_STATE, hpg * SSD_HEAD_DIM), F32)],
        compiler_params=_cparams(("parallel", "arbitrary")),
        name="ssd_bwd" if reverse else "ssd_fwd",
    )(*args)


def _post_kernel(x_ref, ym_ref, ys_ref, yg_ref, gate_ref, mod_ref, wo_ref, wi_ref, wf_ref,
                 l1g_ref, l1b_ref, l2g_ref, l2b_ref, o_ref, act_ref, *, tm, n_lat, th):
    is_ctx = _is_ctx_rows(tm, n_lat)
    d = D_MODEL
    mix = None
    row0 = 0
    for j, y_ref in enumerate((ym_ref, ys_ref, yg_ref)):
        kw = y_ref.shape[-1]
        br = jnp.dot(y_ref[0], wo_ref[row0:row0 + kw, :], preferred_element_type=F32)
        br = gate_ref[0, :, j * d:(j + 1) * d].astype(F32) * br
        mix = br if mix is None else mix + br
        row0 += kw
    g1 = _mod_rows(mod_ref, 2 * d, d, is_ctx)
    x1 = _layer_norm(DEEPNORM_ALPHA * x_ref[0] + g1 * mix) * l1g_ref[...] + l1b_ref[...]

    sh2 = _mod_rows(mod_ref, 3 * d, d, is_ctx)
    sc2 = _mod_rows(mod_ref, 4 * d, d, is_ctx)
    g2 = _mod_rows(mod_ref, 5 * d, d, is_ctx)
    h2 = (_layer_norm(x1) * (1.0 + sc2) + sh2).astype(BF16)
    hid = FFN_HIDDEN
    for c0 in range(0, hid, th):
        gt = jnp.dot(h2, wi_ref[:, c0:c0 + th], preferred_element_type=F32)
        up = jnp.dot(h2, wi_ref[:, hid + c0:hid + c0 + th], preferred_element_type=F32)
        act_ref[:, c0:c0 + th] = (gt * jax.nn.sigmoid(gt) * up).astype(BF16)
    fx = jnp.dot(act_ref[...], wf_ref[...], preferred_element_type=F32)
    o_ref[0] = _layer_norm(DEEPNORM_ALPHA * x1 + g2 * fx) * l2g_ref[...] + l2b_ref[...]


def _post_mixer(xa, y_mla, y_ssd, y_gqa, gate, mod, w_out, w_ffn_in, w_ffn_out, l1g, l1b, l2g, l2b,
                *, n_lat, tm, with_ctx):
    bsz, t_all, d = xa.shape
    row = lambda w: pl.BlockSpec((1, tm, w), lambda b, t: (b, t, 0))
    vec = _const_spec((1, d))
    t_out = t_all if with_ctx else n_lat
    return pl.pallas_call(
        functools.partial(_post_kernel, tm=tm, n_lat=n_lat, th=256),
        grid=(bsz, t_out // tm),
        in_specs=[row(d), row(MLA_OUT), row(SSD_INNER), row(GQA_OUT), row(GATE_IN), _const_spec(mod.shape),
                  _const_spec(w_out.shape), _const_spec(w_ffn_in.shape), _const_spec(w_ffn_out.shape),
                  vec, vec, vec, vec],
        out_specs=pl.BlockSpec((1, tm, d), lambda b, t: (b, t, 0)),
        out_shape=jax.ShapeDtypeStruct((bsz, t_out, d), F32),
        scratch_shapes=[pltpu.VMEM((tm, FFN_HIDDEN), BF16)],
        compiler_params=_cparams(("parallel", "parallel")),
        name="post_mixer",
    )(xa, y_mla, y_ssd, y_gqa, gate, mod, w_out, w_ffn_in, w_ffn_out, l1g, l1b, l2g, l2b)


def _rope_tables(n_lat, ctx_len, rot_dim, lane_dim, lane_off, scale):
    rows = n_lat // GRID_W
    row = jnp.repeat(jnp.arange(rows), GRID_W)
    col = jnp.tile(jnp.arange(GRID_W), rows)
    n_freq = rot_dim // 4
    inv_freq = ROPE_THETA ** (-jnp.arange(n_freq, dtype=F32) / n_freq)
    ang = jnp.concatenate([row[:, None] * inv_freq, col[:, None] * inv_freq], axis=-1)
    cos, sin = jnp.cos(ang), jnp.sin(ang)
    half = rot_dim // 2
    lane = np.arange(LANE)
    rel = (lane % lane_dim) - lane_off
    is_rot = (rel >= 0) & (rel < rot_dim)
    idx = np.where(is_rot, rel % half, 0)
    first = is_rot & (rel < half)
    second = is_rot & (rel >= half)
    c = jnp.where(is_rot[None, :], cos[:, idx], 1.0)
    s1 = jnp.where(first[None, :], -sin[:, idx], 0.0)
    s2 = jnp.where(second[None, :], sin[:, idx], 0.0)
    lat = jnp.stack([c, s1, s2])
    ctx = jnp.stack([jnp.ones((ctx_len, LANE), F32), jnp.zeros((ctx_len, LANE), F32),
                     jnp.zeros((ctx_len, LANE), F32)])
    return (jnp.concatenate([lat, ctx], axis=1) * scale).astype(F32)


def _pad_cols(w, width, offset=0):
    out = jnp.zeros((w.shape[0], width), w.dtype)
    return out.at[:, offset:offset + w.shape[1]].set(w)


def _layout_w_in(w):
    c = 0
    q_lat = w[:, c:c + MLA_Q_RANK]; c += MLA_Q_RANK
    kv_lat = w[:, c:c + MLA_KV_RANK]; c += MLA_KV_RANK
    k_rope = _pad_cols(w[:, c:c + MLA_ROPE], LANE, MLA_NOPE); c += MLA_ROPE
    z = w[:, c:c + SSD_INNER]; c += SSD_INNER
    xbc = w[:, c:c + SSD_CONV_CH]; c += SSD_CONV_CH
    dt = _pad_cols(w[:, c:c + 2 * SSD_HEADS], LANE); c += 2 * SSD_HEADS
    rest = w[:, c:]
    return jnp.concatenate([q_lat, kv_lat, k_rope, z, xbc, dt, rest], axis=1).astype(BF16)


def _layout_w_uq(w):
    w = w.reshape(w.shape[0], MLA_HEADS, MLA_QK)
    w = jnp.pad(w, ((0, 0), (0, 0), (0, LANE - MLA_QK)))
    return w.reshape(w.shape[0], MLA_HEADS * LANE).astype(BF16)


def _layout_w_ukv(w):
    w = w.reshape(w.shape[0], MLA_HEADS, MLA_NOPE + MLA_V)
    k = jnp.pad(w[:, :, :MLA_NOPE], ((0, 0), (0, 0), (0, LANE - MLA_NOPE)))
    v = w[:, :, MLA_NOPE:]
    return jnp.concatenate([k.reshape(w.shape[0], -1), v.reshape(w.shape[0], -1)], axis=1).astype(BF16)


def _lane_row(vec, width=LANE):
    return _pad_cols(vec.reshape(1, -1).astype(F32), width)


def kernel(x, c, ctx, c_ctx, w_mod, b_mod, w_in, b_gate, w_uq, g_q_mla, w_ukv, g_kv_mla, conv_w, conv_b,
           a_log, dt_bias, d_skip, g_ssd, g_q_gqa, g_k_gqa, w_out, ln1_g, ln1_b, w_ffn_in, w_ffn_out,
           ln2_g, ln2_b):
    bsz, n_lat, d = x.shape
    ctx_len = ctx.shape[1]
    depth = w_mod.shape[0]
    assert d == D_MODEL and bsz <= MOD_ROWS // 2
    tm = ROW_TILE
    assert ctx_len % tm == 0 and n_lat % tm == 0

    cvec = jnp.zeros((MOD_ROWS, d), F32).at[:bsz].set(c).at[MOD_ROWS // 2].set(c_ctx)
    mod_all = _modulation(cvec, w_mod, b_mod)

    mla_scale = MLA_QK ** -0.5 * LOG2E
    gqa_scale = GQA_HEAD_DIM ** -0.5 * LOG2E
    tab_mq = _rope_tables(n_lat, ctx_len, MLA_ROPE, LANE, MLA_NOPE, mla_scale)
    tab_mk = _rope_tables(n_lat, ctx_len, MLA_ROPE, LANE, MLA_NOPE, 1.0)
    tab_gq = _rope_tables(n_lat, ctx_len, GQA_HEAD_DIM, GQA_HEAD_DIM, 0, gqa_scale)
    tab_gk = _rope_tables(n_lat, ctx_len, GQA_HEAD_DIM, GQA_HEAD_DIM, 0, 1.0)

    qn = SSD_CHUNK
    r = np.arange(qn)
    tri_f = jnp.asarray((r[None, :] <= r[:, None]).astype(np.float32), BF16)
    tri_b = jnp.asarray((r[None, :] >= r[:, None]).astype(np.float32), BF16)
    lane1k = np.arange(SSD_INNER) // SSD_HEAD_DIM
    ehd_f = jnp.asarray((np.arange(LANE)[:, None] == lane1k[None, :]).astype(np.float32), BF16)
    ehd_b = jnp.asarray((np.arange(LANE)[:, None] == (lane1k[None, :] + SSD_HEADS)).astype(np.float32), BF16)
    blk = np.arange(GQA_OUT) // GQA_HEAD_DIM
    bd = jnp.asarray((blk[:, None] == blk[None, :]).astype(np.float32), BF16)

    mla_attn = functools.partial(_attention, groups=MLA_HEADS // 4, q_w=4 * LANE, k_w=4 * LANE,
                                 vt_rows=2 * VT_ROWS,
                                 pairs=((0, ((0, 0), (1, 1))), (1, ((2, 2), (3, 3)))), n_lat=n_lat)
    gqa_attn = functools.partial(_attention, groups=GQA_KV_HEADS, q_w=2 * LANE, k_w=2 * LANE, vt_rows=VT_ROWS,
                                 pairs=((0, ((0, 0), (0, 1))), (0, ((1, 0), (1, 1)))), n_lat=n_lat)

    def attend(fn, q, k, v, with_ctx):
        y = fn(q, k, v, ctx_queries=False, tq=ATTN_TQ, tk=ATTN_TK)
        if with_ctx:
            y_c = fn(q, k, v, ctx_queries=True, tq=ctx_len, tk=ATTN_TK)
            y = jnp.concatenate([y, y_c], axis=1)
        return y

    xa = jnp.concatenate([x, ctx], axis=1)
    for i in range(depth):
        with_ctx = i < depth - 1
        mod = mod_all[i]
        w_inp = _layout_w_in(w_in[i])
        mla_consts = (g_q_mla[i].reshape(1, -1), g_kv_mla[i].reshape(1, -1),
                      _layout_w_uq(w_uq[i]), _layout_w_ukv(w_ukv[i]), tab_mq, tab_mk)
        gqa_consts = (jnp.tile(g_q_gqa[i], GQA_HEADS).reshape(1, -1),
                      jnp.tile(g_k_gqa[i], GQA_KV_HEADS).reshape(1, -1), bd, tab_gq, tab_gk)
        z, xbc, dtr, gate, q_m, k_m, v_m, q_g, k_g, v_g = _in_projection(
            xa, mod, w_inp, b_gate[i].reshape(1, -1), mla_consts, gqa_consts, n_lat=n_lat, tm=tm)
        y_mla = attend(mla_attn, q_m, k_m, v_m, with_ctx)
        y_gqa = attend(gqa_attn, q_g, k_g, v_g, with_ctx)
        a = -jnp.exp(a_log[i].astype(F32))
        a_row = _lane_row(a.reshape(-1))
        dtb_row = _lane_row(dt_bias[i].reshape(-1))
        y_f, u_act = _ssd_direction(xbc, dtr, (conv_w[i], conv_b[i].reshape(1, -1)), dtb_row, a_row,
                                    tri_f, ehd_f, None, ctx_len=ctx_len, reverse=False)
        d_x = jnp.repeat(d_skip[i], SSD_HEAD_DIM).reshape(1, -1)
        y_ssd = _ssd_direction(u_act, dtr, None, dtb_row, a_row, tri_b, ehd_b,
                               (y_f, z, d_x, g_ssd[i].reshape(1, -1)), ctx_len=ctx_len, reverse=True)
        xa = _post_mixer(xa, y_mla, y_ssd, y_gqa, gate, mod, w_out[i].astype(BF16),
                         w_ffn_in[i].astype(BF16), w_ffn_out[i].astype(BF16),
                         ln1_g[i].reshape(1, -1), ln1_b[i].reshape(1, -1),
                         ln2_g[i].reshape(1, -1), ln2_b[i].reshape(1, -1), n_lat=n_lat,
                         tm=tm if with_ctx else LATENT_ROW_TILE, with_ctx=with_ctx)
    return xa
```

```python
import functools
import math

import jax
import jax.numpy as jnp
import numpy as np
from jax import lax
from jax.experimental import pallas as pl
from jax.experimental.pallas import tpu as pltpu

F32 = jnp.float32
BF16 = jnp.bfloat16

D_MODEL = 1024
DEPTH = 2
GRID_W = 64
ROPE_THETA = 10000.0
NORM_EPS = 1e-6

MLA_HEADS = 8
MLA_Q_RANK = 384
MLA_KV_RANK = 256
MLA_NOPE = 64
MLA_ROPE = 32
MLA_V = 64
MLA_QK = MLA_NOPE + MLA_ROPE

SSD_HEADS = 16
SSD_HEAD_DIM = 64
SSD_INNER = SSD_HEADS * SSD_HEAD_DIM
SSD_GROUPS = 2
SSD_STATE = 128
SSD_CHUNK = 128
SSD_CONV_CH = SSD_INNER + 2 * SSD_GROUPS * SSD_STATE

GQA_HEADS = 8
GQA_KV_HEADS = 2
GQA_HEAD_DIM = 64

MLA_IN = MLA_Q_RANK + MLA_KV_RANK + MLA_ROPE
SSD_IN = SSD_INNER + SSD_CONV_CH + 2 * SSD_HEADS
GQA_IN = (GQA_HEADS + 2 * GQA_KV_HEADS) * GQA_HEAD_DIM
GATE_IN = 3 * D_MODEL
MLA_OUT = MLA_HEADS * MLA_V
GQA_OUT = GQA_HEADS * GQA_HEAD_DIM
FFN_HIDDEN = -(-8 * D_MODEL // (3 * 256)) * 256
DEEPNORM_ALPHA = (2 * DEPTH) ** 0.25

LANE = 128
MOD_ROWS = 8
LOG2E = 1.4426950408889634

C_PM = 0
W_PM = MLA_Q_RANK + MLA_KV_RANK + LANE
C_Z = C_PM + W_PM
C_XBC = C_Z + SSD_INNER
C_DT = C_XBC + SSD_CONV_CH
C_PG = C_DT + LANE
C_GATE = C_PG + GQA_IN
W_INP = C_GATE + GATE_IN

VMEM_LIMIT = 56 * 1024 * 1024
ROW_TILE = 256
LATENT_ROW_TILE = 512
ATTN_TQ = 1024
ATTN_TK = 512
ATTN_COL_BLOCK = 256
ATTN_KEY_BLOCK = 64
SSD_BLOCK = 2 * SSD_CHUNK
VT_TILE = ROW_TILE
VT_ROWS = 144


def _cparams(sem):
    return pltpu.CompilerParams(dimension_semantics=sem, vmem_limit_bytes=VMEM_LIMIT)


def _const_spec(shape):
    nd = len(shape)
    return pl.BlockSpec(shape, lambda *_: (0,) * nd, pipeline_mode=pl.Buffered(1))


def _layer_norm(x):
    mu = jnp.mean(x, axis=-1, keepdims=True)
    xc = x - mu
    var = jnp.mean(xc * xc, axis=-1, keepdims=True)
    return xc * lax.rsqrt(var + NORM_EPS)


def _split_dot(x, m01, parts):
    pieces = []
    rem = x
    for _ in range(parts):
        piece = rem.astype(BF16)
        pieces.append(piece)
        rem = rem - piece.astype(F32)
    return jnp.dot(jnp.concatenate(pieces, axis=1), jnp.concatenate([m01] * parts, axis=0),
                   preferred_element_type=F32)


def _mod_rows(mod_ref, col, width, is_ctx):
    b = pl.program_id(0)
    lat = mod_ref[pl.ds(b, 1), col:col + width]
    ctx = mod_ref[MOD_ROWS // 2:MOD_ROWS // 2 + 1, col:col + width]
    return jnp.where(is_ctx, ctx, lat)


def _is_ctx_rows(tm, n_lat):
    t = pl.program_id(1)
    rows = t * tm + lax.broadcasted_iota(jnp.int32, (tm, 1), 0)
    return rows >= n_lat


def _mod_kernel(c_ref, w_ref, b_ref, o_ref):
    c = c_ref[...]
    s = (c * jax.nn.sigmoid(c)).astype(BF16)
    o_ref[0] = jnp.dot(s, w_ref[0].astype(BF16), preferred_element_type=F32) + b_ref[0]


def _modulation(cvec, w_mod, b_mod):
    depth, d, n = w_mod.shape
    tn = 1024
    return pl.pallas_call(
        _mod_kernel,
        grid=(depth, n // tn),
        in_specs=[
            pl.BlockSpec((MOD_ROWS, d), lambda l, j: (0, 0)),
            pl.BlockSpec((1, d, tn), lambda l, j: (l, 0, j)),
            pl.BlockSpec((1, 1, tn), lambda l, j: (l, 0, j)),
        ],
        out_specs=pl.BlockSpec((1, MOD_ROWS, tn), lambda l, j: (l, 0, j)),
        out_shape=jax.ShapeDtypeStruct((depth, MOD_ROWS, n), F32),
        compiler_params=_cparams(("arbitrary", "arbitrary")),
        name="modulation",
    )(cvec, w_mod, b_mod.reshape(depth, 1, n))


def _rope_lanes(x, c, s_fwd, s_bwd, half):
    n = x.shape[-1]
    return x * c + pltpu.roll(x, n - half, axis=1) * s_fwd + pltpu.roll(x, half, axis=1) * s_bwd


def _mla_prep(pm, gq_ref, gkv_ref, wuq_ref, wukv_ref, tq_ref, tk_ref, q_ref, k_ref, v_ref):
    pq = pm[:, :MLA_Q_RANK]
    ckv = pm[:, MLA_Q_RANK:MLA_Q_RANK + MLA_KV_RANK]
    kr = pm[:, MLA_Q_RANK + MLA_KV_RANK:]

    def rms(x, g):
        return x * lax.rsqrt(jnp.mean(x * x, axis=-1, keepdims=True) + NORM_EPS) * g

    q = jnp.dot(rms(pq, gq_ref[...]).astype(BF16), wuq_ref[...], preferred_element_type=F32)
    kv = jnp.dot(rms(ckv, gkv_ref[...]).astype(BF16), wukv_ref[...], preferred_element_type=F32)
    half = MLA_ROPE // 2
    kr = _rope_lanes(kr, tk_ref[0], tk_ref[1], tk_ref[2], half)
    cq, sq1, sq2 = tq_ref[0], tq_ref[1], tq_ref[2]
    for hd in range(MLA_HEADS):
        sl = slice(hd * LANE, (hd + 1) * LANE)
        q_ref[0, :, sl] = _rope_lanes(q[:, sl], cq, sq1, sq2, half).astype(q_ref.dtype)
        k_ref[0, :, sl] = (kv[:, sl] + kr).astype(k_ref.dtype)
    ones = jnp.ones((VT_ROWS - LANE, pm.shape[0]), v_ref.dtype)
    for j in range(MLA_HEADS // 2):
        c0 = MLA_HEADS * LANE + j * LANE
        v_ref[0, 0, j * VT_ROWS:j * VT_ROWS + LANE, :] = kv[:, c0:c0 + LANE].T.astype(v_ref.dtype)
        v_ref[0, 0, j * VT_ROWS + LANE:(j + 1) * VT_ROWS, :] = ones


def _gqa_prep(pg, gq_ref, gk_ref, bd_ref, tq_ref, tk_ref, q_ref, k_ref, v_ref):
    hd = GQA_HEAD_DIM
    q = pg[:, :GQA_OUT]
    k = pg[:, GQA_OUT:GQA_OUT + LANE]
    v = pg[:, GQA_OUT + LANE:]
    half = hd // 2

    def head_rms(x, g):
        w = x.shape[-1]
        ms = _split_dot(x * x, bd_ref[:w, :w], 2) * (1.0 / hd)
        return x * lax.rsqrt(ms + NORM_EPS) * g

    qn = head_rms(q, gq_ref[...])
    for j in range(GQA_OUT // LANE):
        sl = slice(j * LANE, (j + 1) * LANE)
        q_ref[0, :, sl] = _rope_lanes(qn[:, sl], tq_ref[0], tq_ref[1], tq_ref[2], half).astype(q_ref.dtype)
    kn = head_rms(k, gk_ref[...])
    kn = _rope_lanes(kn, tk_ref[0], tk_ref[1], tk_ref[2], half)
    lane = lax.broadcasted_iota(jnp.int32, kn.shape, 1)
    low = lane < hd
    kn_sw = pltpu.roll(kn, hd, axis=1)
    zero = jnp.zeros_like(kn)
    k_ref[0, :, 0 * LANE:1 * LANE] = jnp.where(low, kn, zero).astype(k_ref.dtype)
    k_ref[0, :, 1 * LANE:2 * LANE] = jnp.where(low, zero, kn_sw).astype(k_ref.dtype)
    k_ref[0, :, 2 * LANE:3 * LANE] = jnp.where(low, kn_sw, zero).astype(k_ref.dtype)
    k_ref[0, :, 3 * LANE:4 * LANE] = jnp.where(low, zero, kn).astype(k_ref.dtype)
    v_t = v.T.astype(v_ref.dtype)
    ones = jnp.ones((VT_ROWS - LANE, v.shape[0]), v_ref.dtype)
    for g in range(GQA_KV_HEADS):
        vg = v_t[g * hd:(g + 1) * hd]
        v_ref[0, 0, g * VT_ROWS:g * VT_ROWS + hd, :] = vg
        v_ref[0, 0, g * VT_ROWS + hd:g * VT_ROWS + LANE, :] = vg
        v_ref[0, 0, g * VT_ROWS + LANE:(g + 1) * VT_ROWS, :] = ones


def _inproj_kernel(x_ref, mod_ref, w_ref, bg_ref,
                   gqm_ref, gkvm_ref, wuq_ref, wukv_ref, tmq_ref, tmk_ref,
                   gqg_ref, gkg_ref, bd_ref, tgq_ref, tgk_ref,
                   z_ref, xbc_ref, dt_ref, gate_ref, qm_ref, km_ref, vm_ref, qg_ref, kg_ref, vg_ref,
                   *, tm, n_lat):
    is_ctx = _is_ctx_rows(tm, n_lat)
    d = D_MODEL
    sh = _mod_rows(mod_ref, 0, d, is_ctx)
    sc = _mod_rows(mod_ref, d, d, is_ctx)
    h = (_layer_norm(x_ref[0]) * (1.0 + sc) + sh).astype(BF16)

    def proj(c0, width):
        return jnp.dot(h, w_ref[:, c0:c0 + width], preferred_element_type=F32)

    _mla_prep(proj(C_PM, W_PM), gqm_ref, gkvm_ref, wuq_ref, wukv_ref, tmq_ref, tmk_ref, qm_ref, km_ref, vm_ref)
    _gqa_prep(proj(C_PG, GQA_IN), gqg_ref, gkg_ref, bd_ref, tgq_ref, tgk_ref, qg_ref, kg_ref, vg_ref)
    z_ref[0] = proj(C_Z, SSD_INNER)
    xbc_ref[0] = proj(C_XBC, SSD_CONV_CH)
    dt_ref[0] = proj(C_DT, LANE)
    for j in range(3):
        c0 = j * D_MODEL
        logits = proj(C_GATE + c0, D_MODEL) + bg_ref[:, c0:c0 + D_MODEL]
        gate_ref[0, :, c0:c0 + D_MODEL] = jax.nn.sigmoid(logits).astype(gate_ref.dtype)


def _in_projection(xa, mod, w_inp, b_gate, mla_consts, gqa_consts, *, n_lat, tm):
    bsz, t_all, d = xa.shape
    assert tm == VT_TILE
    row = lambda w: pl.BlockSpec((1, tm, w), lambda b, t: (b, t, 0))
    tab = pl.BlockSpec((3, tm, LANE), lambda b, t: (0, t, 0))
    vt_spec = lambda rows: pl.BlockSpec((1, 1, rows, tm), lambda b, t: (b, t, 0, 0))
    g_qm, g_kvm, wuq_p, wukv_p, tab_mq, tab_mk = mla_consts
    g_qg, g_kg, bd, tab_gq, tab_gk = gqa_consts
    hw = MLA_HEADS * LANE
    kw = GQA_KV_HEADS * 2 * LANE
    vm_rows = (MLA_HEADS // 2) * VT_ROWS
    vg_rows = GQA_KV_HEADS * VT_ROWS
    seq = lambda w, dt: jax.ShapeDtypeStruct((bsz, t_all, w), dt)
    vts = lambda rows: jax.ShapeDtypeStruct((bsz, t_all // tm, rows, tm), BF16)
    consts = (mod, w_inp, b_gate, g_qm, g_kvm, wuq_p, wukv_p)
    return pl.pallas_call(
        functools.partial(_inproj_kernel, tm=tm, n_lat=n_lat),
        grid=(bsz, t_all // tm),
        in_specs=[row(d)] + [_const_spec(a.shape) for a in consts] + [tab, tab]
                 + [_const_spec(a.shape) for a in (g_qg, g_kg, bd)] + [tab, tab],
        out_specs=[row(SSD_INNER), row(SSD_CONV_CH), row(LANE), row(GATE_IN),
                   row(hw), row(hw), vt_spec(vm_rows), row(GQA_OUT), row(kw), vt_spec(vg_rows)],
        out_shape=[seq(SSD_INNER, F32), seq(SSD_CONV_CH, F32), seq(LANE, F32), seq(GATE_IN, BF16),
                   seq(hw, BF16), seq(hw, BF16), vts(vm_rows), seq(GQA_OUT, BF16), seq(kw, BF16), vts(vg_rows)],
        compiler_params=_cparams(("parallel", "parallel")),
        name="in_projection",
    )(xa, *consts, tab_mq, tab_mk, g_qg, g_kg, bd, tab_gq, tab_gk)


def _attn_kernel(q_ref, k_ref, vt_ref, o_ref, m_ref, acc_ref, a0_ref, a1_ref, c0_ref, c1_ref,
                 p0_ref, p1_ref, s0_ref, s1_ref, *, pairs, tq, tk, n_lat, ctx_len, ctx_queries):
    cb = ATTN_COL_BLOCK
    vr = VT_ROWS
    s_bufs, p_bufs, a_bufs, c_bufs = (s0_ref, s1_ref), (p0_ref, p1_ref), (a0_ref, a1_ref), (c0_ref, c1_ref)

    def scores(r0, size, par):
        for pi, (_, units) in enumerate(pairs):
            for ui, (qb, kb) in enumerate(units):
                s = lax.dot_general(
                    k_ref[0, pl.ds(r0, size), kb * LANE:(kb + 1) * LANE],
                    q_ref[0, :, qb * LANE:(qb + 1) * LANE],
                    (((1,), (1,)), ((), ())), preferred_element_type=F32)
                s_bufs[par][pi, :size, ui * tq:(ui + 1) * tq] = s
                c_bufs[par][pi, :, ui * tq:(ui + 1) * tq] = jnp.max(s, axis=0, keepdims=True)

    def softmax(size, par):
        for pi in range(len(pairs)):
            for c in range(0, 2 * tq, cb):
                m_prev = m_ref[pi, :, c:c + cb]
                m_next = jnp.maximum(m_prev, c_bufs[par][pi, :, c:c + cb])
                for r in range(0, size, ATTN_KEY_BLOCK):
                    blk = s_bufs[par][pi, r:r + ATTN_KEY_BLOCK, c:c + cb]
                    p_bufs[par][pi, r:r + ATTN_KEY_BLOCK, c:c + cb] = jnp.exp2(blk - m_next).astype(BF16)
                a_bufs[par][pi, :, c:c + cb] = jnp.exp2(m_prev - m_next)
                m_ref[pi, :, c:c + cb] = m_next

    def values(u0, size, par):
        for pi, (vb, _) in enumerate(pairs):
            vt = jnp.concatenate([vt_ref[0, u0 + i, vb * vr:(vb + 1) * vr, :]
                                  for i in range(size // VT_TILE)], axis=1)
            pv = jnp.dot(vt, p_bufs[par][pi, :size, :], preferred_element_type=F32)
            acc_ref[pi] = a_bufs[par][pi] * acc_ref[pi] + pv

    m_ref[...] = jnp.full(m_ref.shape, -jnp.inf, F32)
    acc_ref[...] = jnp.zeros(acc_ref.shape, F32)
    tku = tk // VT_TILE
    ctx_u = n_lat // VT_TILE
    scores(n_lat, ctx_len, 0)
    if ctx_queries:
        softmax(ctx_len, 0)
        values(ctx_u, ctx_len, 0)
    else:
        n = n_lat // tk
        scores(0, tk, 1)
        softmax(ctx_len, 0)
        scores(tk, tk, 0)
        softmax(tk, 1)
        values(ctx_u, ctx_len, 0)

        def body(jj, carry):
            r = pl.multiple_of(jj * (2 * tk), 2 * tk)
            u = jj * (2 * tku)
            scores(r + 2 * tk, tk, 1)
            softmax(tk, 0)
            values(u, tk, 1)
            scores(r + 3 * tk, tk, 0)
            softmax(tk, 1)
            values(u + tku, tk, 0)
            return carry

        lax.fori_loop(0, (n - 2) // 2, body, 0)
        softmax(tk, 0)
        values((n - 2) * tku, tk, 1)
        values((n - 1) * tku, tk, 0)
    row = lax.broadcasted_iota(jnp.int32, (LANE, tq), 0)
    for pi in range(len(pairs)):
        acc = acc_ref[pi]
        o_t = acc[:LANE] / acc[LANE:LANE + 1]
        o_t = jnp.where(row < LANE // 2, o_t[:, :tq], o_t[:, tq:])
        o_ref[0, :, pi * LANE:(pi + 1) * LANE] = o_t.T.astype(o_ref.dtype)


def _attention(q, k, vt, *, groups, q_w, k_w, vt_rows, pairs, n_lat, ctx_queries, tq, tk):
    bsz, t_all, _ = k.shape
    ctx_len = t_all - n_lat
    out_w = len(pairs) * LANE
    rows = ctx_len if ctx_queries else n_lat
    q_tile0 = n_lat // tq if ctx_queries else 0
    assert rows % tq == 0 and n_lat % tq == 0 and n_lat % (2 * tk) == 0 and ctx_len <= tk
    assert (2 * tq) % ATTN_COL_BLOCK == 0 and tk % VT_TILE == 0 and ctx_len % VT_TILE == 0
    state = pltpu.VMEM((len(pairs), 1, 2 * tq), F32)
    acc = pltpu.VMEM((len(pairs), VT_ROWS, 2 * tq), F32)
    pbuf = pltpu.VMEM((len(pairs), tk, 2 * tq), BF16)
    sbuf = pltpu.VMEM((len(pairs), tk, 2 * tq), F32)
    return pl.pallas_call(
        functools.partial(_attn_kernel, pairs=pairs, tq=tq, tk=tk, n_lat=n_lat, ctx_len=ctx_len,
                          ctx_queries=ctx_queries),
        grid=(bsz, groups, rows // tq),
        in_specs=[
            pl.BlockSpec((1, tq, q_w), lambda b, g, t: (b, t + q_tile0, g)),
            pl.BlockSpec((1, t_all, k_w), lambda b, g, t: (b, 0, g)),
            pl.BlockSpec((1, t_all // VT_TILE, vt_rows, VT_TILE), lambda b, g, t: (b, 0, g, 0)),
        ],
        out_specs=pl.BlockSpec((1, tq, out_w), lambda b, g, t: (b, t, g)),
        out_shape=jax.ShapeDtypeStruct((bsz, rows, groups * out_w), BF16),
        scratch_shapes=[state, acc, state, state, state, state, pbuf, pbuf, sbuf, sbuf],
        compiler_params=_cparams(("parallel", "parallel", "arbitrary")),
        name="attention_ctx" if ctx_queries else "attention",
    )(q, k, vt)


def _ssd_chunk_index(i, n_chunks, ctx_chunks, reverse):
    if reverse:
        return n_chunks - 1 - i
    lat_chunks = n_chunks - ctx_chunks
    return jnp.where(i < ctx_chunks, lat_chunks + i, i - ctx_chunks)


def _ssd_kernel(*refs, reverse, n_chunks, ctx_chunks):
    if reverse:
        (u_ref, dt_ref, dtb_ref, a_ref, tri_ref, ehd_ref,
         yf_ref, z_ref, dsk_ref, g_ref, y_ref, h_ref) = refs
    else:
        (xbc_ref, prev_ref, next_ref, dt_ref, cw_ref, cb_ref, dtb_ref, a_ref, tri_ref, ehd_ref,
         y_ref, uo_ref, h_ref) = refs
    q = SSD_CHUNK
    rows = SSD_BLOCK
    i = pl.program_id(1)
    ci = _ssd_chunk_index(i, n_chunks, ctx_chunks, reverse)

    @pl.when(i == 0)
    def _():
        h_ref[...] = jnp.zeros_like(h_ref)

    if reverse:
        u_blk = u_ref[0]
    else:
        x = xbc_ref[0]
        lat_chunks = n_chunks - ctx_chunks
        at_start = jnp.logical_or(ci == 0, ci == lat_chunks)
        at_end = jnp.logical_or(ci == lat_chunks - 1, ci == n_chunks - 1)
        prev_row = jnp.where(at_start, 0.0, prev_ref[0, 7:8, :])
        next_row = jnp.where(at_end, 0.0, next_ref[0, 0:1, :])
        row = lax.broadcasted_iota(jnp.int32, (rows, 1), 0)
        x_prev = jnp.where(row == 0, prev_row, pltpu.roll(x, 1, axis=0))
        x_next = jnp.where(row == rows - 1, next_row, pltpu.roll(x, rows - 1, axis=0))
        u_blk = cw_ref[0:1, :] * x_prev + cw_ref[1:2, :] * x + cw_ref[2:3, :] * x_next + cb_ref[...]
        u_blk = u_blk * jax.nn.sigmoid(u_blk)
        uo_ref[0] = u_blk

    subs = range(rows // q)
    for sub in (reversed(subs) if reverse else subs):
        sl = slice(sub * q, (sub + 1) * q)
        y = _ssd_chunk(u_blk[sl], dt_ref[0, sl, :], dtb_ref, a_ref, tri_ref, ehd_ref, h_ref, reverse)
        if reverse:
            yt = yf_ref[0, sl, :] + y + dsk_ref[...] * u_blk[sl, :SSD_INNER]
            zz = z_ref[0, sl, :]
            yt = yt * (zz * jax.nn.sigmoid(zz))
            yt = yt * lax.rsqrt(jnp.mean(yt * yt, axis=-1, keepdims=True) + NORM_EPS) * g_ref[...]
            y_ref[0, sl, :] = yt.astype(y_ref.dtype)
        else:
            y_ref[0, sl, :] = y


def _ssd_chunk(u, dt_raw, dtb_ref, a_ref, tri_ref, ehd_ref, h_ref, reverse):
    q = SSD_CHUNK
    xs = u[:, :SSD_INNER]
    gs = SSD_STATE
    bm = [u[:, SSD_INNER + g * gs:SSD_INNER + (g + 1) * gs] for g in range(SSD_GROUPS)]
    cm = [u[:, SSD_INNER + (SSD_GROUPS + g) * gs:SSD_INNER + (SSD_GROUPS + g + 1) * gs]
          for g in range(SSD_GROUPS)]

    dt = jax.nn.softplus(dt_raw + dtb_ref[...])
    da = dt * a_ref[...]
    a_cum = _tri_cumsum(tri_ref, da)
    end = 0 if reverse else q - 1
    a_tot = a_cum[end:end + 1, :]
    e_cum = jnp.exp(a_cum)
    dte = jnp.exp(a_tot - a_cum)
    dt_x = _split_dot(dt, ehd_ref[...], 2)
    ec_x = _split_dot(e_cum, ehd_ref[...], 2)
    dte_x = _split_dot(dte, ehd_ref[...], 2)
    xdt = xs * dt_x
    xdt_b = xdt.astype(BF16)
    xst_b = (xdt * dte_x).astype(BF16)
    a_cum_t = a_cum.T

    kk = lax.broadcasted_iota(jnp.int32, (q, q), 1)
    qq = lax.broadcasted_iota(jnp.int32, (q, q), 0)
    in_order = (kk >= qq) if reverse else (kk <= qq)
    lane = lax.broadcasted_iota(jnp.int32, (q, LANE), 1)
    off = SSD_HEADS if reverse else 0
    hpg = SSD_HEADS // SSD_GROUPS
    gw = hpg * SSD_HEAD_DIM
    y_parts = []
    for g in range(SSD_GROUPS):
        bm_b = bm[g].astype(BF16)
        cm_b = cm[g].astype(BF16)
        scores = lax.dot_general(cm_b, bm_b, (((1,), (1,)), ((), ())), preferred_element_type=F32)
        h_prev = h_ref[g]
        y_off = jnp.dot(cm_b, h_prev.astype(BF16), preferred_element_type=F32) * ec_x[:, g * gw:(g + 1) * gw]
        pair_out = []
        for j in range(hpg // 2):
            xp = xdt_b[:, g * gw + j * LANE:g * gw + (j + 1) * LANE]
            res = []
            for e in range(2):
                hd = g * hpg + 2 * j + e
                a_q = a_cum[:, off + hd:off + hd + 1]
                a_k = a_cum_t[off + hd:off + hd + 1, :]
                decay = jnp.exp(jnp.where(in_order, a_q - a_k, -jnp.inf))
                pmat = (scores * decay).astype(BF16)
                res.append(jnp.dot(pmat, xp, preferred_element_type=F32))
            pair_out.append(jnp.where(lane < SSD_HEAD_DIM, res[0], res[1]))
        y_parts.append(jnp.concatenate(pair_out, axis=1) + y_off)
        st = jnp.dot(bm[g].T.astype(BF16), xst_b[:, g * gw:(g + 1) * gw], preferred_element_type=F32)
        h_ref[g] = h_prev * ec_x[end:end + 1, g * gw:(g + 1) * gw] + st
    return jnp.concatenate(y_parts, axis=1)


def _tri_cumsum(tri_ref, da):
    acc = None
    rem = da
    for _ in range(3):
        piece = rem.astype(BF16)
        term = jnp.dot(tri_ref[...], piece, preferred_element_type=F32)
        acc = term if acc is None else acc + term
        rem = rem - piece.astype(F32)
    return acc


def _ssd_direction(xbc, dt, conv, dt_bias, a_row, tri, ehd, extra, *, ctx_len, reverse):
    bsz, t_all, cch = xbc.shape
    q = SSD_BLOCK
    assert t_all % q == 0 and ctx_len % q == 0
    n_chunks = t_all // q
    ctx_chunks = ctx_len // q
    halo = 8
    hb = q // halo
    n_halo = t_all // halo
    cidx = lambda i: _ssd_chunk_index(i, n_chunks, ctx_chunks, reverse)
    chunk = lambda w: pl.BlockSpec((1, q, w), lambda b, i: (b, cidx(i), 0))
    tail = [_const_spec(dt_bias.shape), _const_spec(a_row.shape), _const_spec(tri.shape), _const_spec(ehd.shape)]
    if reverse:
        y_f, z, d_x, g_ssd = extra
        in_specs = [chunk(cch), chunk(LANE)] + tail + [
            chunk(SSD_INNER), chunk(SSD_INNER), _const_spec(d_x.shape), _const_spec(g_ssd.shape)]
        args = [xbc, dt, dt_bias, a_row, tri, ehd, y_f, z, d_x, g_ssd]
        out_specs = chunk(SSD_INNER)
        out_shape = jax.ShapeDtypeStruct((bsz, t_all, SSD_INNER), BF16)
    else:
        conv_w, conv_b = conv
        in_specs = [
            chunk(cch),
            pl.BlockSpec((1, halo, cch), lambda b, i: (b, jnp.maximum(cidx(i) * hb - 1, 0), 0)),
            pl.BlockSpec((1, halo, cch), lambda b, i: (b, jnp.minimum((cidx(i) + 1) * hb, n_halo - 1), 0)),
            chunk(LANE), _const_spec(conv_w.shape), _const_spec(conv_b.shape)] + tail
        args = [xbc, xbc, xbc, dt, conv_w, conv_b, dt_bias, a_row, tri, ehd]
        out_specs = [chunk(SSD_INNER), chunk(cch)]
        out_shape = [jax.ShapeDtypeStruct((bsz, t_all, SSD_INNER), F32),
                     jax.ShapeDtypeStruct((bsz, t_all, cch), F32)]
    hpg = SSD_HEADS // SSD_GROUPS
    return pl.pallas_call(
        functools.partial(_ssd_kernel, reverse=reverse, n_chunks=n_chunks, ctx_chunks=ctx_chunks),
        grid=(bsz, n_chunks),
        in_specs=in_specs,
        out_specs=out_specs,
        out_shape=out_shape,
        scratch_shapes=[pltpu.VMEM((SSD_GROUPS, SSD_STATE, hpg * SSD_HEAD_DIM), F32)],
        compiler_params=_cparams(("parallel", "arbitrary")),
        name="ssd_bwd" if reverse else "ssd_fwd",
    )(*args)


def _post_kernel(x_ref, ym_ref, ys_ref, yg_ref, gate_ref, mod_ref, wo_ref, wi_ref, wf_ref,
                 l1g_ref, l1b_ref, l2g_ref, l2b_ref, o_ref, act_ref, *, tm, n_lat, th):
    is_ctx = _is_ctx_rows(tm, n_lat)
    d = D_MODEL
    mix = None
    row0 = 0
    for j, y_ref in enumerate((ym_ref, ys_ref, yg_ref)):
        kw = y_ref.shape[-1]
        br = jnp.dot(y_ref[0], wo_ref[row0:row0 + kw, :], preferred_element_type=F32)
        br = gate_ref[0, :, j * d:(j + 1) * d].astype(F32) * br
        mix = br if mix is None else mix + br
        row0 += kw
    g1 = _mod_rows(mod_ref, 2 * d, d, is_ctx)
    x1 = _layer_norm(DEEPNORM_ALPHA * x_ref[0] + g1 * mix) * l1g_ref[...] + l1b_ref[...]

    sh2 = _mod_rows(mod_ref, 3 * d, d, is_ctx)
    sc2 = _mod_rows(mod_ref, 4 * d, d, is_ctx)
    g2 = _mod_rows(mod_ref, 5 * d, d, is_ctx)
    h2 = (_layer_norm(x1) * (1.0 + sc2) + sh2).astype(BF16)
    hid = FFN_HIDDEN
    for c0 in range(0, hid, th):
        gt = jnp.dot(h2, wi_ref[:, c0:c0 + th], preferred_element_type=F32)
        up = jnp.dot(h2, wi_ref[:, hid + c0:hid + c0 + th], preferred_element_type=F32)
        act_ref[:, c0:c0 + th] = (gt * jax.nn.sigmoid(gt) * up).astype(BF16)
    fx = jnp.dot(act_ref[...], wf_ref[...], preferred_element_type=F32)
    o_ref[0] = _layer_norm(DEEPNORM_ALPHA * x1 + g2 * fx) * l2g_ref[...] + l2b_ref[...]


def _post_mixer(xa, y_mla, y_ssd, y_gqa, gate, mod, w_out, w_ffn_in, w_ffn_out, l1g, l1b, l2g, l2b,
                *, n_lat, tm, with_ctx):
    bsz, t_all, d = xa.shape
    row = lambda w: pl.BlockSpec((1, tm, w), lambda b, t: (b, t, 0))
    vec = _const_spec((1, d))
    t_out = t_all if with_ctx else n_lat
    return pl.pallas_call(
        functools.partial(_post_kernel, tm=tm, n_lat=n_lat, th=256),
        grid=(bsz, t_out // tm),
        in_specs=[row(d), row(MLA_OUT), row(SSD_INNER), row(GQA_OUT), row(GATE_IN), _const_spec(mod.shape),
                  _const_spec(w_out.shape), _const_spec(w_ffn_in.shape), _const_spec(w_ffn_out.shape),
                  vec, vec, vec, vec],
        out_specs=pl.BlockSpec((1, tm, d), lambda b, t: (b, t, 0)),
        out_shape=jax.ShapeDtypeStruct((bsz, t_out, d), F32),
        scratch_shapes=[pltpu.VMEM((tm, FFN_HIDDEN), BF16)],
        compiler_params=_cparams(("parallel", "parallel")),
        name="post_mixer",
    )(xa, y_mla, y_ssd, y_gqa, gate, mod, w_out, w_ffn_in, w_ffn_out, l1g, l1b, l2g, l2b)


def _rope_tables(n_lat, ctx_len, rot_dim, lane_dim, lane_off, scale):
    rows = n_lat // GRID_W
    row = np.repeat(np.arange(rows), GRID_W)
    col = np.tile(np.arange(GRID_W), rows)
    n_freq = rot_dim // 4
    inv_freq = ROPE_THETA ** (-np.arange(n_freq, dtype=np.float64) / n_freq)
    ang = np.concatenate([row[:, None] * inv_freq, col[:, None] * inv_freq], axis=-1)
    cos, sin = np.cos(ang), np.sin(ang)
    half = rot_dim // 2
    lane = np.arange(LANE)
    rel = (lane % lane_dim) - lane_off
    is_rot = (rel >= 0) & (rel < rot_dim)
    idx = np.where(is_rot, rel % half, 0)
    first = is_rot & (rel < half)
    second = is_rot & (rel >= half)
    c = np.where(is_rot[None, :], cos[:, idx], 1.0)
    s1 = np.where(first[None, :], -sin[:, idx], 0.0)
    s2 = np.where(second[None, :], sin[:, idx], 0.0)
    lat = np.stack([c, s1, s2])
    ctx = np.stack([np.ones((ctx_len, LANE)), np.zeros((ctx_len, LANE)), np.zeros((ctx_len, LANE))])
    return jnp.asarray((np.concatenate([lat, ctx], axis=1) * scale).astype(np.float32))


def _pad_cols(w, width, offset=0):
    return jnp.pad(w, ((0, 0), (offset, width - offset - w.shape[1])))


def _layout_w_in(w):
    c = 0
    q_lat = w[:, c:c + MLA_Q_RANK]; c += MLA_Q_RANK
    kv_lat = w[:, c:c + MLA_KV_RANK]; c += MLA_KV_RANK
    k_rope = _pad_cols(w[:, c:c + MLA_ROPE], LANE, MLA_NOPE); c += MLA_ROPE
    z = w[:, c:c + SSD_INNER]; c += SSD_INNER
    xbc = w[:, c:c + SSD_CONV_CH]; c += SSD_CONV_CH
    dt = _pad_cols(w[:, c:c + 2 * SSD_HEADS], LANE); c += 2 * SSD_HEADS
    rest = w[:, c:]
    return jnp.concatenate([q_lat, kv_lat, k_rope, z, xbc, dt, rest], axis=1).astype(BF16)


def _layout_w_uq(w):
    w = w.reshape(w.shape[0], MLA_HEADS, MLA_QK)
    w = jnp.pad(w, ((0, 0), (0, 0), (0, LANE - MLA_QK)))
    return w.reshape(w.shape[0], MLA_HEADS * LANE).astype(BF16)


def _layout_w_ukv(w):
    w = w.reshape(w.shape[0], MLA_HEADS, MLA_NOPE + MLA_V)
    k = jnp.pad(w[:, :, :MLA_NOPE], ((0, 0), (0, 0), (0, LANE - MLA_NOPE)))
    v = w[:, :, MLA_NOPE:]
    return jnp.concatenate([k.reshape(w.shape[0], -1), v.reshape(w.shape[0], -1)], axis=1).astype(BF16)


def _lane_row(vec, width=LANE):
    return _pad_cols(vec.reshape(1, -1).astype(F32), width)


def kernel(x, c, ctx, c_ctx, w_mod, b_mod, w_in, b_gate, w_uq, g_q_mla, w_ukv, g_kv_mla, conv_w, conv_b,
           a_log, dt_bias, d_skip, g_ssd, g_q_gqa, g_k_gqa, w_out, ln1_g, ln1_b, w_ffn_in, w_ffn_out,
           ln2_g, ln2_b):
    bsz, n_lat, d = x.shape
    ctx_len = ctx.shape[1]
    depth = w_mod.shape[0]
    assert d == D_MODEL and bsz <= MOD_ROWS // 2
    tm = ROW_TILE
    assert ctx_len % tm == 0 and n_lat % tm == 0

    cvec = jnp.zeros((MOD_ROWS, d), F32).at[:bsz].set(c).at[MOD_ROWS // 2].set(c_ctx)
    mod_all = _modulation(cvec, w_mod, b_mod)

    mla_scale = MLA_QK ** -0.5 * LOG2E
    gqa_scale = GQA_HEAD_DIM ** -0.5 * LOG2E
    tab_mq = _rope_tables(n_lat, ctx_len, MLA_ROPE, LANE, MLA_NOPE, mla_scale)
    tab_mk = _rope_tables(n_lat, ctx_len, MLA_ROPE, LANE, MLA_NOPE, 1.0)
    tab_gq = _rope_tables(n_lat, ctx_len, GQA_HEAD_DIM, GQA_HEAD_DIM, 0, gqa_scale)
    tab_gk = _rope_tables(n_lat, ctx_len, GQA_HEAD_DIM, GQA_HEAD_DIM, 0, 1.0)

    qn = SSD_CHUNK
    r = np.arange(qn)
    tri_f = jnp.asarray((r[None, :] <= r[:, None]).astype(np.float32), BF16)
    tri_b = jnp.asarray((r[None, :] >= r[:, None]).astype(np.float32), BF16)
    lane1k = np.arange(SSD_INNER) // SSD_HEAD_DIM
    ehd_f = jnp.asarray((np.arange(LANE)[:, None] == lane1k[None, :]).astype(np.float32), BF16)
    ehd_b = jnp.asarray((np.arange(LANE)[:, None] == (lane1k[None, :] + SSD_HEADS)).astype(np.float32), BF16)
    blk = np.arange(GQA_OUT) // GQA_HEAD_DIM
    bd = jnp.asarray((blk[:, None] == blk[None, :]).astype(np.float32), BF16)

    mla_attn = functools.partial(_attention, groups=MLA_HEADS // 4, q_w=4 * LANE, k_w=4 * LANE,
                                 vt_rows=2 * VT_ROWS,
                                 pairs=((0, ((0, 0), (1, 1))), (1, ((2, 2), (3, 3)))), n_lat=n_lat)
    gqa_attn = functools.partial(_attention, groups=GQA_KV_HEADS, q_w=2 * LANE, k_w=2 * LANE, vt_rows=VT_ROWS,
                                 pairs=((0, ((0, 0), (0, 1))), (0, ((1, 0), (1, 1)))), n_lat=n_lat)

    def attend(fn, q, k, v, with_ctx):
        y = fn(q, k, v, ctx_queries=False, tq=ATTN_TQ, tk=ATTN_TK)
        if with_ctx:
            y_c = fn(q, k, v, ctx_queries=True, tq=ctx_len, tk=ATTN_TK)
            y = jnp.concatenate([y, y_c], axis=1)
        return y

    xa = jnp.concatenate([x, ctx], axis=1)
    for i in range(depth):
        with_ctx = i < depth - 1
        mod = mod_all[i]
        w_inp = _layout_w_in(w_in[i])
        mla_consts = (g_q_mla[i].reshape(1, -1), g_kv_mla[i].reshape(1, -1),
                      _layout_w_uq(w_uq[i]), _layout_w_ukv(w_ukv[i]), tab_mq, tab_mk)
        gqa_consts = (jnp.tile(g_q_gqa[i], GQA_HEADS).reshape(1, -1),
                      jnp.tile(g_k_gqa[i], GQA_KV_HEADS).reshape(1, -1), bd, tab_gq, tab_gk)
        z, xbc, dtr, gate, q_m, k_m, v_m, q_g, k_g, v_g = _in_projection(
            xa, mod, w_inp, b_gate[i].reshape(1, -1), mla_consts, gqa_consts, n_lat=n_lat, tm=tm)
        y_mla = attend(mla_attn, q_m, k_m, v_m, with_ctx)
        y_gqa = attend(gqa_attn, q_g, k_g, v_g, with_ctx)
        a = -jnp.exp(a_log[i].astype(F32))
        a_row = _lane_row(a.reshape(-1))
        dtb_row = _lane_row(dt_bias[i].reshape(-1))
        y_f, u_act = _ssd_direction(xbc, dtr, (conv_w[i], conv_b[i].reshape(1, -1)), dtb_row, a_row,
                                    tri_f, ehd_f, None, ctx_len=ctx_len, reverse=False)
        d_x = jnp.repeat(d_skip[i], SSD_HEAD_DIM).reshape(1, -1)
        y_ssd = _ssd_direction(u_act, dtr, None, dtb_row, a_row, tri_b, ehd_b,
                               (y_f, z, d_x, g_ssd[i].reshape(1, -1)), ctx_len=ctx_len, reverse=True)
        xa = _post_mixer(xa, y_mla, y_ssd, y_gqa, gate, mod, w_out[i].astype(BF16),
                         w_ffn_in[i].astype(BF16), w_ffn_out[i].astype(BF16),
                         ln1_g[i].reshape(1, -1), ln1_b[i].reshape(1, -1),
                         ln2_g[i].reshape(1, -1), ln2_b[i].reshape(1, -1), n_lat=n_lat,
                         tm=tm if with_ctx else LATENT_ROW_TILE, with_ctx=with_ctx)
    return xa
```

```python
import functools
import math

import jax
import jax.numpy as jnp
import numpy as np
from jax import lax
from jax.experimental import pallas as pl
from jax.experimental.pallas import tpu as pltpu

F32 = jnp.float32
BF16 = jnp.bfloat16

D_MODEL = 1024
DEPTH = 2
GRID_W = 64
ROPE_THETA = 10000.0
NORM_EPS = 1e-6

MLA_HEADS = 8
MLA_Q_RANK = 384
MLA_KV_RANK = 256
MLA_NOPE = 64
MLA_ROPE = 32
MLA_V = 64
MLA_QK = MLA_NOPE + MLA_ROPE

SSD_HEADS = 16
SSD_HEAD_DIM = 64
SSD_INNER = SSD_HEADS * SSD_HEAD_DIM
SSD_GROUPS = 2
SSD_STATE = 128
SSD_CHUNK = 128
SSD_CONV_CH = SSD_INNER + 2 * SSD_GROUPS * SSD_STATE

GQA_HEADS = 8
GQA_KV_HEADS = 2
GQA_HEAD_DIM = 64

MLA_IN = MLA_Q_RANK + MLA_KV_RANK + MLA_ROPE
SSD_IN = SSD_INNER + SSD_CONV_CH + 2 * SSD_HEADS
GQA_IN = (GQA_HEADS + 2 * GQA_KV_HEADS) * GQA_HEAD_DIM
GATE_IN = 3 * D_MODEL
MLA_OUT = MLA_HEADS * MLA_V
GQA_OUT = GQA_HEADS * GQA_HEAD_DIM
FFN_HIDDEN = -(-8 * D_MODEL // (3 * 256)) * 256
DEEPNORM_ALPHA = (2 * DEPTH) ** 0.25

LANE = 128
MOD_ROWS = 8
LOG2E = 1.4426950408889634

C_PM = 0
W_PM = MLA_Q_RANK + MLA_KV_RANK + LANE
C_Z = C_PM + W_PM
C_XBC = C_Z + SSD_INNER
C_DT = C_XBC + SSD_CONV_CH
C_PG = C_DT + LANE
C_GATE = C_PG + GQA_IN
W_INP = C_GATE + GATE_IN

VMEM_LIMIT = 56 * 1024 * 1024
ROW_TILE = 256
LATENT_ROW_TILE = 512
ATTN_TQ = 1024
ATTN_TK = 512
ATTN_COL_BLOCK = 256
ATTN_KEY_BLOCK = 64
SSD_BLOCK = 2 * SSD_CHUNK
VT_TILE = ROW_TILE
VT_ROWS = 144


def _cparams(sem):
    return pltpu.CompilerParams(dimension_semantics=sem, vmem_limit_bytes=VMEM_LIMIT)


def _const_spec(shape):
    nd = len(shape)
    return pl.BlockSpec(shape, lambda *_: (0,) * nd, pipeline_mode=pl.Buffered(1))


def _layer_norm(x):
    mu = jnp.mean(x, axis=-1, keepdims=True)
    xc = x - mu
    var = jnp.mean(xc * xc, axis=-1, keepdims=True)
    return xc * lax.rsqrt(var + NORM_EPS)


def _split_dot(x, m01, parts):
    pieces = []
    rem = x
    for _ in range(parts):
        piece = rem.astype(BF16)
        pieces.append(piece)
        rem = rem - piece.astype(F32)
    return jnp.dot(jnp.concatenate(pieces, axis=1), jnp.concatenate([m01] * parts, axis=0),
                   preferred_element_type=F32)


def _mod_rows(mod_ref, col, width, is_ctx):
    b = pl.program_id(0)
    lat = mod_ref[pl.ds(b, 1), col:col + width]
    ctx = mod_ref[MOD_ROWS // 2:MOD_ROWS // 2 + 1, col:col + width]
    return jnp.where(is_ctx, ctx, lat)


def _is_ctx_rows(tm, n_lat):
    t = pl.program_id(1)
    rows = t * tm + lax.broadcasted_iota(jnp.int32, (tm, 1), 0)
    return rows >= n_lat


def _mod_kernel(c_ref, w_ref, b_ref, o_ref):
    c = c_ref[...]
    s = (c * jax.nn.sigmoid(c)).astype(BF16)
    o_ref[0] = jnp.dot(s, w_ref[0].astype(BF16), preferred_element_type=F32) + b_ref[0]


def _modulation(cvec, w_mod, b_mod):
    depth, d, n = w_mod.shape
    tn = 1024
    return pl.pallas_call(
        _mod_kernel,
        grid=(depth, n // tn),
        in_specs=[
            pl.BlockSpec((MOD_ROWS, d), lambda l, j: (0, 0)),
            pl.BlockSpec((1, d, tn), lambda l, j: (l, 0, j)),
            pl.BlockSpec((1, 1, tn), lambda l, j: (l, 0, j)),
        ],
        out_specs=pl.BlockSpec((1, MOD_ROWS, tn), lambda l, j: (l, 0, j)),
        out_shape=jax.ShapeDtypeStruct((depth, MOD_ROWS, n), F32),
        compiler_params=_cparams(("arbitrary", "arbitrary")),
        name="modulation",
    )(cvec, w_mod, b_mod.reshape(depth, 1, n))


def _rope_lanes(x, c, s_fwd, s_bwd, half):
    n = x.shape[-1]
    return x * c + pltpu.roll(x, n - half, axis=1) * s_fwd + pltpu.roll(x, half, axis=1) * s_bwd


def _mla_prep(pm, gq_ref, gkv_ref, wuq_ref, wukv_ref, tq_ref, tk_ref, q_ref, k_ref, v_ref):
    pq = pm[:, :MLA_Q_RANK]
    ckv = pm[:, MLA_Q_RANK:MLA_Q_RANK + MLA_KV_RANK]
    kr = pm[:, MLA_Q_RANK + MLA_KV_RANK:]

    def rms(x, g):
        return x * lax.rsqrt(jnp.mean(x * x, axis=-1, keepdims=True) + NORM_EPS) * g

    q = jnp.dot(rms(pq, gq_ref[...]).astype(BF16), wuq_ref[...], preferred_element_type=F32)
    kv = jnp.dot(rms(ckv, gkv_ref[...]).astype(BF16), wukv_ref[...], preferred_element_type=F32)
    half = MLA_ROPE // 2
    kr = _rope_lanes(kr, tk_ref[0], tk_ref[1], tk_ref[2], half)
    cq, sq1, sq2 = tq_ref[0], tq_ref[1], tq_ref[2]
    for hd in range(MLA_HEADS):
        sl = slice(hd * LANE, (hd + 1) * LANE)
        q_ref[0, :, sl] = _rope_lanes(q[:, sl], cq, sq1, sq2, half).astype(q_ref.dtype)
        k_ref[0, :, sl] = (kv[:, sl] + kr).astype(k_ref.dtype)
    ones = jnp.ones((VT_ROWS - LANE, pm.shape[0]), v_ref.dtype)
    for j in range(MLA_HEADS // 2):
        c0 = MLA_HEADS * LANE + j * LANE
        v_ref[0, 0, j * VT_ROWS:j * VT_ROWS + LANE, :] = kv[:, c0:c0 + LANE].T.astype(v_ref.dtype)
        v_ref[0, 0, j * VT_ROWS + LANE:(j + 1) * VT_ROWS, :] = ones


def _gqa_prep(pg, gq_ref, gk_ref, bd_ref, tq_ref, tk_ref, q_ref, k_ref, v_ref):
    hd = GQA_HEAD_DIM
    q = pg[:, :GQA_OUT]
    k = pg[:, GQA_OUT:GQA_OUT + LANE]
    v = pg[:, GQA_OUT + LANE:]
    half = hd // 2

    def head_rms(x, g):
        w = x.shape[-1]
        ms = _split_dot(x * x, bd_ref[:w, :w], 2) * (1.0 / hd)
        return x * lax.rsqrt(ms + NORM_EPS) * g

    qn = head_rms(q, gq_ref[...])
    for j in range(GQA_OUT // LANE):
        sl = slice(j * LANE, (j + 1) * LANE)
        q_ref[0, :, sl] = _rope_lanes(qn[:, sl], tq_ref[0], tq_ref[1], tq_ref[2], half).astype(q_ref.dtype)
    kn = head_rms(k, gk_ref[...])
    kn = _rope_lanes(kn, tk_ref[0], tk_ref[1], tk_ref[2], half)
    lane = lax.broadcasted_iota(jnp.int32, kn.shape, 1)
    low = lane < hd
    kn_sw = pltpu.roll(kn, hd, axis=1)
    zero = jnp.zeros_like(kn)
    k_ref[0, :, 0 * LANE:1 * LANE] = jnp.where(low, kn, zero).astype(k_ref.dtype)
    k_ref[0, :, 1 * LANE:2 * LANE] = jnp.where(low, zero, kn_sw).astype(k_ref.dtype)
    k_ref[0, :, 2 * LANE:3 * LANE] = jnp.where(low, kn_sw, zero).astype(k_ref.dtype)
    k_ref[0, :, 3 * LANE:4 * LANE] = jnp.where(low, zero, kn).astype(k_ref.dtype)
    v_t = v.T.astype(v_ref.dtype)
    ones = jnp.ones((VT_ROWS - LANE, v.shape[0]), v_ref.dtype)
    for g in range(GQA_KV_HEADS):
        vg = v_t[g * hd:(g + 1) * hd]
        v_ref[0, 0, g * VT_ROWS:g * VT_ROWS + hd, :] = vg
        v_ref[0, 0, g * VT_ROWS + hd:g * VT_ROWS + LANE, :] = vg
        v_ref[0, 0, g * VT_ROWS + LANE:(g + 1) * VT_ROWS, :] = ones


def _split_row_specs(tm, width, n_lat, ctx_row0):
    lat_tiles = n_lat // tm
    assert ctx_row0 % tm == 0
    ctx_block0 = ctx_row0 // tm
    return [pl.BlockSpec((1, tm, width), lambda b, t: (b, jnp.minimum(t, lat_tiles - 1), 0)),
            pl.BlockSpec((1, tm, width), lambda b, t: (b, ctx_block0 + jnp.maximum(t - lat_tiles, 0), 0))]


def _pick_rows(lat_ref, ctx_ref, tm, n_lat):
    return jnp.where(pl.program_id(1) * tm >= n_lat, ctx_ref[0], lat_ref[0])


def _inproj_kernel(xl_ref, xc_ref, mod_ref, w_ref, bg_ref,
                   gqm_ref, gkvm_ref, wuq_ref, wukv_ref, tmq_ref, tmk_ref,
                   gqg_ref, gkg_ref, bd_ref, tgq_ref, tgk_ref,
                   z_ref, xbc_ref, dt_ref, gate_ref, qm_ref, km_ref, vm_ref, qg_ref, kg_ref, vg_ref,
                   *, tm, n_lat):
    is_ctx = _is_ctx_rows(tm, n_lat)
    d = D_MODEL
    sh = _mod_rows(mod_ref, 0, d, is_ctx)
    sc = _mod_rows(mod_ref, d, d, is_ctx)
    h = (_layer_norm(_pick_rows(xl_ref, xc_ref, tm, n_lat)) * (1.0 + sc) + sh).astype(BF16)

    def proj(c0, width):
        return jnp.dot(h, w_ref[:, c0:c0 + width], preferred_element_type=F32)

    _mla_prep(proj(C_PM, W_PM), gqm_ref, gkvm_ref, wuq_ref, wukv_ref, tmq_ref, tmk_ref, qm_ref, km_ref, vm_ref)
    _gqa_prep(proj(C_PG, GQA_IN), gqg_ref, gkg_ref, bd_ref, tgq_ref, tgk_ref, qg_ref, kg_ref, vg_ref)
    z_ref[0] = proj(C_Z, SSD_INNER)
    xbc_ref[0] = proj(C_XBC, SSD_CONV_CH)
    dt_ref[0] = proj(C_DT, LANE)
    for j in range(3):
        c0 = j * D_MODEL
        logits = proj(C_GATE + c0, D_MODEL) + bg_ref[:, c0:c0 + D_MODEL]
        gate_ref[0, :, c0:c0 + D_MODEL] = jax.nn.sigmoid(logits).astype(gate_ref.dtype)


def _in_projection(x_lat, x_ctx, ctx_row0, t_all, mod, w_inp, b_gate, mla_consts, gqa_consts, *, n_lat, tm):
    bsz, _, d = x_lat.shape
    assert tm == VT_TILE
    row = lambda w: pl.BlockSpec((1, tm, w), lambda b, t: (b, t, 0))
    tab = pl.BlockSpec((3, tm, LANE), lambda b, t: (0, t, 0))
    vt_spec = lambda rows: pl.BlockSpec((1, 1, rows, tm), lambda b, t: (b, t, 0, 0))
    g_qm, g_kvm, wuq_p, wukv_p, tab_mq, tab_mk = mla_consts
    g_qg, g_kg, bd, tab_gq, tab_gk = gqa_consts
    hw = MLA_HEADS * LANE
    kw = GQA_KV_HEADS * 2 * LANE
    vm_rows = (MLA_HEADS // 2) * VT_ROWS
    vg_rows = GQA_KV_HEADS * VT_ROWS
    seq = lambda w, dt: jax.ShapeDtypeStruct((bsz, t_all, w), dt)
    vts = lambda rows: jax.ShapeDtypeStruct((bsz, t_all // tm, rows, tm), BF16)
    consts = (mod, w_inp, b_gate, g_qm, g_kvm, wuq_p, wukv_p)
    return pl.pallas_call(
        functools.partial(_inproj_kernel, tm=tm, n_lat=n_lat),
        grid=(bsz, t_all // tm),
        in_specs=_split_row_specs(tm, d, n_lat, ctx_row0) + [_const_spec(a.shape) for a in consts] + [tab, tab]
                 + [_const_spec(a.shape) for a in (g_qg, g_kg, bd)] + [tab, tab],
        out_specs=[row(SSD_INNER), row(SSD_CONV_CH), row(LANE), row(GATE_IN),
                   row(hw), row(hw), vt_spec(vm_rows), row(GQA_OUT), row(kw), vt_spec(vg_rows)],
        out_shape=[seq(SSD_INNER, F32), seq(SSD_CONV_CH, F32), seq(LANE, F32), seq(GATE_IN, BF16),
                   seq(hw, BF16), seq(hw, BF16), vts(vm_rows), seq(GQA_OUT, BF16), seq(kw, BF16), vts(vg_rows)],
        compiler_params=_cparams(("parallel", "parallel")),
        name="in_projection",
    )(x_lat, x_ctx, *consts, tab_mq, tab_mk, g_qg, g_kg, bd, tab_gq, tab_gk)


def _attn_kernel(q_ref, k_ref, vt_ref, o_ref, m_ref, acc_ref, a0_ref, a1_ref, c0_ref, c1_ref,
                 p0_ref, p1_ref, s0_ref, s1_ref, *, pairs, tq, tk, n_lat, ctx_len, ctx_queries):
    cb = ATTN_COL_BLOCK
    vr = VT_ROWS
    s_bufs, p_bufs, a_bufs, c_bufs = (s0_ref, s1_ref), (p0_ref, p1_ref), (a0_ref, a1_ref), (c0_ref, c1_ref)

    def scores(r0, size, par):
        for pi, (_, units) in enumerate(pairs):
            for ui, (qb, kb) in enumerate(units):
                s = lax.dot_general(
                    k_ref[0, pl.ds(r0, size), kb * LANE:(kb + 1) * LANE],
                    q_ref[0, :, qb * LANE:(qb + 1) * LANE],
                    (((1,), (1,)), ((), ())), preferred_element_type=F32)
                s_bufs[par][pi, :size, ui * tq:(ui + 1) * tq] = s
                c_bufs[par][pi, :, ui * tq:(ui + 1) * tq] = jnp.max(s, axis=0, keepdims=True)

    def softmax(size, par):
        for pi in range(len(pairs)):
            for c in range(0, 2 * tq, cb):
                m_prev = m_ref[pi, :, c:c + cb]
                m_next = jnp.maximum(m_prev, c_bufs[par][pi, :, c:c + cb])
                for r in range(0, size, ATTN_KEY_BLOCK):
                    blk = s_bufs[par][pi, r:r + ATTN_KEY_BLOCK, c:c + cb]
                    p_bufs[par][pi, r:r + ATTN_KEY_BLOCK, c:c + cb] = jnp.exp2(blk - m_next).astype(BF16)
                a_bufs[par][pi, :, c:c + cb] = jnp.exp2(m_prev - m_next)
                m_ref[pi, :, c:c + cb] = m_next

    def values(u0, size, par):
        for pi, (vb, _) in enumerate(pairs):
            vt = jnp.concatenate([vt_ref[0, u0 + i, vb * vr:(vb + 1) * vr, :]
                                  for i in range(size // VT_TILE)], axis=1)
            pv = jnp.dot(vt, p_bufs[par][pi, :size, :], preferred_element_type=F32)
            acc_ref[pi] = a_bufs[par][pi] * acc_ref[pi] + pv

    m_ref[...] = jnp.full(m_ref.shape, -jnp.inf, F32)
    acc_ref[...] = jnp.zeros(acc_ref.shape, F32)
    tku = tk // VT_TILE
    ctx_u = n_lat // VT_TILE
    scores(n_lat, ctx_len, 0)
    if ctx_queries:
        softmax(ctx_len, 0)
        values(ctx_u, ctx_len, 0)
    else:
        n = n_lat // tk
        scores(0, tk, 1)
        softmax(ctx_len, 0)
        scores(tk, tk, 0)
        softmax(tk, 1)
        values(ctx_u, ctx_len, 0)

        def body(jj, carry):
            r = pl.multiple_of(jj * (2 * tk), 2 * tk)
            u = jj * (2 * tku)
            scores(r + 2 * tk, tk, 1)
            softmax(tk, 0)
            values(u, tk, 1)
            scores(r + 3 * tk, tk, 0)
            softmax(tk, 1)
            values(u + tku, tk, 0)
            return carry

        lax.fori_loop(0, (n - 2) // 2, body, 0)
        softmax(tk, 0)
        values((n - 2) * tku, tk, 1)
        values((n - 1) * tku, tk, 0)
    row = lax.broadcasted_iota(jnp.int32, (LANE, tq), 0)
    for pi in range(len(pairs)):
        acc = acc_ref[pi]
        o_t = acc[:LANE] / acc[LANE:LANE + 1]
        o_t = jnp.where(row < LANE // 2, o_t[:, :tq], o_t[:, tq:])
        o_ref[0, :, pi * LANE:(pi + 1) * LANE] = o_t.T.astype(o_ref.dtype)


def _attention(q, k, vt, *, groups, q_w, k_w, vt_rows, pairs, n_lat, ctx_queries, tq, tk):
    bsz, t_all, _ = k.shape
    ctx_len = t_all - n_lat
    out_w = len(pairs) * LANE
    rows = ctx_len if ctx_queries else n_lat
    q_tile0 = n_lat // tq if ctx_queries else 0
    assert rows % tq == 0 and n_lat % tq == 0 and n_lat % (2 * tk) == 0 and ctx_len <= tk
    assert (2 * tq) % ATTN_COL_BLOCK == 0 and tk % VT_TILE == 0 and ctx_len % VT_TILE == 0
    state = pltpu.VMEM((len(pairs), 1, 2 * tq), F32)
    acc = pltpu.VMEM((len(pairs), VT_ROWS, 2 * tq), F32)
    pbuf = pltpu.VMEM((len(pairs), tk, 2 * tq), BF16)
    sbuf = pltpu.VMEM((len(pairs), tk, 2 * tq), F32)
    return pl.pallas_call(
        functools.partial(_attn_kernel, pairs=pairs, tq=tq, tk=tk, n_lat=n_lat, ctx_len=ctx_len,
                          ctx_queries=ctx_queries),
        grid=(bsz, groups, rows // tq),
        in_specs=[
            pl.BlockSpec((1, tq, q_w), lambda b, g, t: (b, t + q_tile0, g)),
            pl.BlockSpec((1, t_all, k_w), lambda b, g, t: (b, 0, g)),
            pl.BlockSpec((1, t_all // VT_TILE, vt_rows, VT_TILE), lambda b, g, t: (b, 0, g, 0)),
        ],
        out_specs=pl.BlockSpec((1, tq, out_w), lambda b, g, t: (b, t, g)),
        out_shape=jax.ShapeDtypeStruct((bsz, rows, groups * out_w), BF16),
        scratch_shapes=[state, acc, state, state, state, state, pbuf, pbuf, sbuf, sbuf],
        compiler_params=_cparams(("parallel", "parallel", "arbitrary")),
        name="attention_ctx" if ctx_queries else "attention",
    )(q, k, vt)


def _ssd_chunk_index(i, n_chunks, ctx_chunks, reverse):
    if reverse:
        return n_chunks - 1 - i
    lat_chunks = n_chunks - ctx_chunks
    return jnp.where(i < ctx_chunks, lat_chunks + i, i - ctx_chunks)


def _ssd_kernel(*refs, reverse, n_chunks, ctx_chunks):
    if reverse:
        (u_ref, dt_ref, dtb_ref, a_ref, tri_ref, ehd_ref,
         yf_ref, z_ref, dsk_ref, g_ref, y_ref, h_ref) = refs
    else:
        (xbc_ref, prev_ref, next_ref, dt_ref, cw_ref, cb_ref, dtb_ref, a_ref, tri_ref, ehd_ref,
         y_ref, uo_ref, h_ref) = refs
    q = SSD_CHUNK
    rows = SSD_BLOCK
    i = pl.program_id(1)
    ci = _ssd_chunk_index(i, n_chunks, ctx_chunks, reverse)

    @pl.when(i == 0)
    def _():
        h_ref[...] = jnp.zeros_like(h_ref)

    if reverse:
        u_blk = u_ref[0]
    else:
        x = xbc_ref[0]
        lat_chunks = n_chunks - ctx_chunks
        at_start = jnp.logical_or(ci == 0, ci == lat_chunks)
        at_end = jnp.logical_or(ci == lat_chunks - 1, ci == n_chunks - 1)
        prev_row = jnp.where(at_start, 0.0, prev_ref[0, 7:8, :])
        next_row = jnp.where(at_end, 0.0, next_ref[0, 0:1, :])
        row = lax.broadcasted_iota(jnp.int32, (rows, 1), 0)
        x_prev = jnp.where(row == 0, prev_row, pltpu.roll(x, 1, axis=0))
        x_next = jnp.where(row == rows - 1, next_row, pltpu.roll(x, rows - 1, axis=0))
        u_blk = cw_ref[0:1, :] * x_prev + cw_ref[1:2, :] * x + cw_ref[2:3, :] * x_next + cb_ref[...]
        u_blk = u_blk * jax.nn.sigmoid(u_blk)
        uo_ref[0] = u_blk

    subs = range(rows // q)
    for sub in (reversed(subs) if reverse else subs):
        sl = slice(sub * q, (sub + 1) * q)
        y = _ssd_chunk(u_blk[sl], dt_ref[0, sl, :], dtb_ref, a_ref, tri_ref, ehd_ref, h_ref, reverse)
        if reverse:
            yt = yf_ref[0, sl, :] + y + dsk_ref[...] * u_blk[sl, :SSD_INNER]
            zz = z_ref[0, sl, :]
            yt = yt * (zz * jax.nn.sigmoid(zz))
            yt = yt * lax.rsqrt(jnp.mean(yt * yt, axis=-1, keepdims=True) + NORM_EPS) * g_ref[...]
            y_ref[0, sl, :] = yt.astype(y_ref.dtype)
        else:
            y_ref[0, sl, :] = y


def _ssd_chunk(u, dt_raw, dtb_ref, a_ref, tri_ref, ehd_ref, h_ref, reverse):
    q = SSD_CHUNK
    xs = u[:, :SSD_INNER]
    gs = SSD_STATE
    bm = [u[:, SSD_INNER + g * gs:SSD_INNER + (g + 1) * gs] for g in range(SSD_GROUPS)]
    cm = [u[:, SSD_INNER + (SSD_GROUPS + g) * gs:SSD_INNER + (SSD_GROUPS + g + 1) * gs]
          for g in range(SSD_GROUPS)]

    dt = jax.nn.softplus(dt_raw + dtb_ref[...])
    da = dt * a_ref[...]
    a_cum = _tri_cumsum(tri_ref, da)
    end = 0 if reverse else q - 1
    a_tot = a_cum[end:end + 1, :]
    e_cum = jnp.exp(a_cum)
    dte = jnp.exp(a_tot - a_cum)
    dt_x = _split_dot(dt, ehd_ref[...], 2)
    ec_x = _split_dot(e_cum, ehd_ref[...], 2)
    dte_x = _split_dot(dte, ehd_ref[...], 2)
    xdt = xs * dt_x
    xdt_b = xdt.astype(BF16)
    xst_b = (xdt * dte_x).astype(BF16)
    a_cum_t = a_cum.T

    kk = lax.broadcasted_iota(jnp.int32, (q, q), 1)
    qq = lax.broadcasted_iota(jnp.int32, (q, q), 0)
    in_order = (kk >= qq) if reverse else (kk <= qq)
    lane = lax.broadcasted_iota(jnp.int32, (q, LANE), 1)
    off = SSD_HEADS if reverse else 0
    hpg = SSD_HEADS // SSD_GROUPS
    gw = hpg * SSD_HEAD_DIM
    y_parts = []
    for g in range(SSD_GROUPS):
        bm_b = bm[g].astype(BF16)
        cm_b = cm[g].astype(BF16)
        scores = lax.dot_general(cm_b, bm_b, (((1,), (1,)), ((), ())), preferred_element_type=F32)
        h_prev = h_ref[g]
        y_off = jnp.dot(cm_b, h_prev.astype(BF16), preferred_element_type=F32) * ec_x[:, g * gw:(g + 1) * gw]
        pair_out = []
        for j in range(hpg // 2):
            xp = xdt_b[:, g * gw + j * LANE:g * gw + (j + 1) * LANE]
            res = []
            for e in range(2):
                hd = g * hpg + 2 * j + e
                a_q = a_cum[:, off + hd:off + hd + 1]
                a_k = a_cum_t[off + hd:off + hd + 1, :]
                decay = jnp.exp(jnp.where(in_order, a_q - a_k, -jnp.inf))
                pmat = (scores * decay).astype(BF16)
                res.append(jnp.dot(pmat, xp, preferred_element_type=F32))
            pair_out.append(jnp.where(lane < SSD_HEAD_DIM, res[0], res[1]))
        y_parts.append(jnp.concatenate(pair_out, axis=1) + y_off)
        st = jnp.dot(bm[g].T.astype(BF16), xst_b[:, g * gw:(g + 1) * gw], preferred_element_type=F32)
        h_ref[g] = h_prev * ec_x[end:end + 1, g * gw:(g + 1) * gw] + st
    return jnp.concatenate(y_parts, axis=1)


def _tri_cumsum(tri_ref, da):
    acc = None
    rem = da
    for _ in range(3):
        piece = rem.astype(BF16)
        term = jnp.dot(tri_ref[...], piece, preferred_element_type=F32)
        acc = term if acc is None else acc + term
        rem = rem - piece.astype(F32)
    return acc


def _ssd_direction(xbc, dt, conv, dt_bias, a_row, tri, ehd, extra, *, ctx_len, reverse):
    bsz, t_all, cch = xbc.shape
    q = SSD_BLOCK
    assert t_all % q == 0 and ctx_len % q == 0
    n_chunks = t_all // q
    ctx_chunks = ctx_len // q
    halo = 8
    hb = q // halo
    n_halo = t_all // halo
    cidx = lambda i: _ssd_chunk_index(i, n_chunks, ctx_chunks, reverse)
    chunk = lambda w: pl.BlockSpec((1, q, w), lambda b, i: (b, cidx(i), 0))
    tail = [_const_spec(dt_bias.shape), _const_spec(a_row.shape), _const_spec(tri.shape), _const_spec(ehd.shape)]
    if reverse:
        y_f, z, d_x, g_ssd = extra
        in_specs = [chunk(cch), chunk(LANE)] + tail + [
            chunk(SSD_INNER), chunk(SSD_INNER), _const_spec(d_x.shape), _const_spec(g_ssd.shape)]
        args = [xbc, dt, dt_bias, a_row, tri, ehd, y_f, z, d_x, g_ssd]
        out_specs = chunk(SSD_INNER)
        out_shape = jax.ShapeDtypeStruct((bsz, t_all, SSD_INNER), BF16)
    else:
        conv_w, conv_b = conv
        in_specs = [
            chunk(cch),
            pl.BlockSpec((1, halo, cch), lambda b, i: (b, jnp.maximum(cidx(i) * hb - 1, 0), 0)),
            pl.BlockSpec((1, halo, cch), lambda b, i: (b, jnp.minimum((cidx(i) + 1) * hb, n_halo - 1), 0)),
            chunk(LANE), _const_spec(conv_w.shape), _const_spec(conv_b.shape)] + tail
        args = [xbc, xbc, xbc, dt, conv_w, conv_b, dt_bias, a_row, tri, ehd]
        out_specs = [chunk(SSD_INNER), chunk(cch)]
        out_shape = [jax.ShapeDtypeStruct((bsz, t_all, SSD_INNER), F32),
                     jax.ShapeDtypeStruct((bsz, t_all, cch), F32)]
    hpg = SSD_HEADS // SSD_GROUPS
    return pl.pallas_call(
        functools.partial(_ssd_kernel, reverse=reverse, n_chunks=n_chunks, ctx_chunks=ctx_chunks),
        grid=(bsz, n_chunks),
        in_specs=in_specs,
        out_specs=out_specs,
        out_shape=out_shape,
        scratch_shapes=[pltpu.VMEM((SSD_GROUPS, SSD_STATE, hpg * SSD_HEAD_DIM), F32)],
        compiler_params=_cparams(("parallel", "arbitrary")),
        name="ssd_bwd" if reverse else "ssd_fwd",
    )(*args)


def _post_kernel(xl_ref, xc_ref, yml_ref, ymc_ref, ys_ref, ygl_ref, ygc_ref, gate_ref, mod_ref,
                 wo_ref, wi_ref, wf_ref, l1g_ref, l1b_ref, l2g_ref, l2b_ref, o_ref, act_ref, *, tm, n_lat, th):
    is_ctx = _is_ctx_rows(tm, n_lat)
    d = D_MODEL
    mix = None
    row0 = 0
    branches = (_pick_rows(yml_ref, ymc_ref, tm, n_lat), ys_ref[0], _pick_rows(ygl_ref, ygc_ref, tm, n_lat))
    for j, y in enumerate(branches):
        kw = y.shape[-1]
        br = jnp.dot(y, wo_ref[row0:row0 + kw, :], preferred_element_type=F32)
        br = gate_ref[0, :, j * d:(j + 1) * d].astype(F32) * br
        mix = br if mix is None else mix + br
        row0 += kw
    g1 = _mod_rows(mod_ref, 2 * d, d, is_ctx)
    x_in = _pick_rows(xl_ref, xc_ref, tm, n_lat)
    x1 = _layer_norm(DEEPNORM_ALPHA * x_in + g1 * mix) * l1g_ref[...] + l1b_ref[...]

    sh2 = _mod_rows(mod_ref, 3 * d, d, is_ctx)
    sc2 = _mod_rows(mod_ref, 4 * d, d, is_ctx)
    g2 = _mod_rows(mod_ref, 5 * d, d, is_ctx)
    h2 = (_layer_norm(x1) * (1.0 + sc2) + sh2).astype(BF16)
    hid = FFN_HIDDEN
    for c0 in range(0, hid, th):
        gt = jnp.dot(h2, wi_ref[:, c0:c0 + th], preferred_element_type=F32)
        up = jnp.dot(h2, wi_ref[:, hid + c0:hid + c0 + th], preferred_element_type=F32)
        act_ref[:, c0:c0 + th] = (gt * jax.nn.sigmoid(gt) * up).astype(BF16)
    fx = jnp.dot(act_ref[...], wf_ref[...], preferred_element_type=F32)
    o_ref[0] = _layer_norm(DEEPNORM_ALPHA * x1 + g2 * fx) * l2g_ref[...] + l2b_ref[...]


def _post_mixer(x_pair, y_mla_pair, y_ssd, y_gqa_pair, gate, mod, w_out, w_ffn_in, w_ffn_out,
                l1g, l1b, l2g, l2b, *, n_lat, tm, with_ctx):
    bsz, t_all, _ = y_ssd.shape
    d = D_MODEL
    row = lambda w: pl.BlockSpec((1, tm, w), lambda b, t: (b, t, 0))
    vec = _const_spec((1, d))
    t_out = t_all if with_ctx else n_lat
    split = lambda w, triple: _split_row_specs(tm, w, n_lat, triple[2] if with_ctx else 0)
    return pl.pallas_call(
        functools.partial(_post_kernel, tm=tm, n_lat=n_lat, th=256),
        grid=(bsz, t_out // tm),
        in_specs=split(d, x_pair) + split(MLA_OUT, y_mla_pair) + [row(SSD_INNER)] + split(GQA_OUT, y_gqa_pair)
                 + [row(GATE_IN), _const_spec(mod.shape),
                    _const_spec(w_out.shape), _const_spec(w_ffn_in.shape), _const_spec(w_ffn_out.shape),
                    vec, vec, vec, vec],
        out_specs=pl.BlockSpec((1, tm, d), lambda b, t: (b, t, 0)),
        out_shape=jax.ShapeDtypeStruct((bsz, t_out, d), F32),
        scratch_shapes=[pltpu.VMEM((tm, FFN_HIDDEN), BF16)],
        compiler_params=_cparams(("parallel", "parallel")),
        name="post_mixer",
    )(x_pair[0], x_pair[1], y_mla_pair[0], y_mla_pair[1], y_ssd, y_gqa_pair[0], y_gqa_pair[1], gate, mod,
      w_out, w_ffn_in, w_ffn_out, l1g, l1b, l2g, l2b)


def _rope_tables(n_lat, ctx_len, rot_dim, lane_dim, lane_off, scale):
    rows = n_lat // GRID_W
    row = np.repeat(np.arange(rows), GRID_W)
    col = np.tile(np.arange(GRID_W), rows)
    n_freq = rot_dim // 4
    inv_freq = ROPE_THETA ** (-np.arange(n_freq, dtype=np.float64) / n_freq)
    ang = np.concatenate([row[:, None] * inv_freq, col[:, None] * inv_freq], axis=-1)
    cos, sin = np.cos(ang), np.sin(ang)
    half = rot_dim // 2
    lane = np.arange(LANE)
    rel = (lane % lane_dim) - lane_off
    is_rot = (rel >= 0) & (rel < rot_dim)
    idx = np.where(is_rot, rel % half, 0)
    first = is_rot & (rel < half)
    second = is_rot & (rel >= half)
    c = np.where(is_rot[None, :], cos[:, idx], 1.0)
    s1 = np.where(first[None, :], -sin[:, idx], 0.0)
    s2 = np.where(second[None, :], sin[:, idx], 0.0)
    lat = np.stack([c, s1, s2])
    ctx = np.stack([np.ones((ctx_len, LANE)), np.zeros((ctx_len, LANE)), np.zeros((ctx_len, LANE))])
    return jnp.asarray((np.concatenate([lat, ctx], axis=1) * scale).astype(np.float32))


def _pad_cols(w, width, offset=0):
    return jnp.pad(w, ((0, 0), (offset, width - offset - w.shape[1])))


def _layout_w_in(w):
    c = 0
    q_lat = w[:, c:c + MLA_Q_RANK]; c += MLA_Q_RANK
    kv_lat = w[:, c:c + MLA_KV_RANK]; c += MLA_KV_RANK
    k_rope = _pad_cols(w[:, c:c + MLA_ROPE], LANE, MLA_NOPE); c += MLA_ROPE
    z = w[:, c:c + SSD_INNER]; c += SSD_INNER
    xbc = w[:, c:c + SSD_CONV_CH]; c += SSD_CONV_CH
    dt = _pad_cols(w[:, c:c + 2 * SSD_HEADS], LANE); c += 2 * SSD_HEADS
    rest = w[:, c:]
    return jnp.concatenate([q_lat, kv_lat, k_rope, z, xbc, dt, rest], axis=1).astype(BF16)


def _layout_w_uq(w):
    w = w.reshape(w.shape[0], MLA_HEADS, MLA_QK)
    w = jnp.pad(w, ((0, 0), (0, 0), (0, LANE - MLA_QK)))
    return w.reshape(w.shape[0], MLA_HEADS * LANE).astype(BF16)


def _layout_w_ukv(w):
    w = w.reshape(w.shape[0], MLA_HEADS, MLA_NOPE + MLA_V)
    k = jnp.pad(w[:, :, :MLA_NOPE], ((0, 0), (0, 0), (0, LANE - MLA_NOPE)))
    v = w[:, :, MLA_NOPE:]
    return jnp.concatenate([k.reshape(w.shape[0], -1), v.reshape(w.shape[0], -1)], axis=1).astype(BF16)


def _lane_row(vec, width=LANE):
    return _pad_cols(vec.reshape(1, -1).astype(F32), width)


def kernel(x, c, ctx, c_ctx, w_mod, b_mod, w_in, b_gate, w_uq, g_q_mla, w_ukv, g_kv_mla, conv_w, conv_b,
           a_log, dt_bias, d_skip, g_ssd, g_q_gqa, g_k_gqa, w_out, ln1_g, ln1_b, w_ffn_in, w_ffn_out,
           ln2_g, ln2_b):
    bsz, n_lat, d = x.shape
    ctx_len = ctx.shape[1]
    depth = w_mod.shape[0]
    assert d == D_MODEL and bsz <= MOD_ROWS // 2
    tm = ROW_TILE
    assert ctx_len % tm == 0 and n_lat % tm == 0

    cvec = jnp.zeros((MOD_ROWS, d), F32).at[:bsz].set(c).at[MOD_ROWS // 2].set(c_ctx)
    mod_all = _modulation(cvec, w_mod, b_mod)

    mla_scale = MLA_QK ** -0.5 * LOG2E
    gqa_scale = GQA_HEAD_DIM ** -0.5 * LOG2E
    tab_mq = _rope_tables(n_lat, ctx_len, MLA_ROPE, LANE, MLA_NOPE, mla_scale)
    tab_mk = _rope_tables(n_lat, ctx_len, MLA_ROPE, LANE, MLA_NOPE, 1.0)
    tab_gq = _rope_tables(n_lat, ctx_len, GQA_HEAD_DIM, GQA_HEAD_DIM, 0, gqa_scale)
    tab_gk = _rope_tables(n_lat, ctx_len, GQA_HEAD_DIM, GQA_HEAD_DIM, 0, 1.0)

    qn = SSD_CHUNK
    r = np.arange(qn)
    tri_f = jnp.asarray((r[None, :] <= r[:, None]).astype(np.float32), BF16)
    tri_b = jnp.asarray((r[None, :] >= r[:, None]).astype(np.float32), BF16)
    lane1k = np.arange(SSD_INNER) // SSD_HEAD_DIM
    ehd_f = jnp.asarray((np.arange(LANE)[:, None] == lane1k[None, :]).astype(np.float32), BF16)
    ehd_b = jnp.asarray((np.arange(LANE)[:, None] == (lane1k[None, :] + SSD_HEADS)).astype(np.float32), BF16)
    blk = np.arange(GQA_OUT) // GQA_HEAD_DIM
    bd = jnp.asarray((blk[:, None] == blk[None, :]).astype(np.float32), BF16)

    mla_attn = functools.partial(_attention, groups=MLA_HEADS // 4, q_w=4 * LANE, k_w=4 * LANE,
                                 vt_rows=2 * VT_ROWS,
                                 pairs=((0, ((0, 0), (1, 1))), (1, ((2, 2), (3, 3)))), n_lat=n_lat)
    gqa_attn = functools.partial(_attention, groups=GQA_KV_HEADS, q_w=2 * LANE, k_w=2 * LANE, vt_rows=VT_ROWS,
                                 pairs=((0, ((0, 0), (0, 1))), (0, ((1, 0), (1, 1)))), n_lat=n_lat)

    def attend(fn, q, k, v, with_ctx):
        y = fn(q, k, v, ctx_queries=False, tq=ATTN_TQ, tk=ATTN_TK)
        y_c = fn(q, k, v, ctx_queries=True, tq=ctx_len, tk=ATTN_TK) if with_ctx else y
        return (y, y_c, 0)

    t_all = n_lat + ctx_len
    x_pair = (x, ctx, 0)
    for i in range(depth):
        with_ctx = i < depth - 1
        mod = mod_all[i]
        w_inp = _layout_w_in(w_in[i])
        mla_consts = (g_q_mla[i].reshape(1, -1), g_kv_mla[i].reshape(1, -1),
                      _layout_w_uq(w_uq[i]), _layout_w_ukv(w_ukv[i]), tab_mq, tab_mk)
        gqa_consts = (jnp.tile(g_q_gqa[i], GQA_HEADS).reshape(1, -1),
                      jnp.tile(g_k_gqa[i], GQA_KV_HEADS).reshape(1, -1), bd, tab_gq, tab_gk)
        z, xbc, dtr, gate, q_m, k_m, v_m, q_g, k_g, v_g = _in_projection(
            *x_pair, t_all, mod, w_inp, b_gate[i].reshape(1, -1), mla_consts, gqa_consts, n_lat=n_lat, tm=tm)
        y_mla = attend(mla_attn, q_m, k_m, v_m, with_ctx)
        y_gqa = attend(gqa_attn, q_g, k_g, v_g, with_ctx)
        a = -jnp.exp(a_log[i].astype(F32))
        a_row = _lane_row(a.reshape(-1))
        dtb_row = _lane_row(dt_bias[i].reshape(-1))
        y_f, u_act = _ssd_direction(xbc, dtr, (conv_w[i], conv_b[i].reshape(1, -1)), dtb_row, a_row,
                                    tri_f, ehd_f, None, ctx_len=ctx_len, reverse=False)
        d_x = jnp.repeat(d_skip[i], SSD_HEAD_DIM).reshape(1, -1)
        y_ssd = _ssd_direction(u_act, dtr, None, dtb_row, a_row, tri_b, ehd_b,
                               (y_f, z, d_x, g_ssd[i].reshape(1, -1)), ctx_len=ctx_len, reverse=True)
        tm_post = tm if with_ctx else LATENT_ROW_TILE
        xa = _post_mixer(x_pair, y_mla, y_ssd, y_gqa, gate, mod, w_out[i].astype(BF16),
                         w_ffn_in[i].astype(BF16), w_ffn_out[i].astype(BF16),
                         ln1_g[i].reshape(1, -1), ln1_b[i].reshape(1, -1),
                         ln2_g[i].reshape(1, -1), ln2_b[i].reshape(1, -1), n_lat=n_lat,
                         tm=tm_post, with_ctx=with_ctx)
        x_pair = (xa, xa, n_lat)
    return xa
```

```python
import functools

import jax
import jax.numpy as jnp
import numpy as np
from jax import lax
from jax.experimental import pallas as pl
from jax.experimental.pallas import tpu as pltpu

F32 = jnp.float32
BF16 = jnp.bfloat16

D_MODEL = 1024
DEPTH = 2
GRID_W = 64
ROPE_THETA = 10000.0
NORM_EPS = 1e-6

MLA_HEADS = 8
MLA_Q_RANK = 384
MLA_KV_RANK = 256
MLA_NOPE = 64
MLA_ROPE = 32
MLA_V = 64
MLA_QK = MLA_NOPE + MLA_ROPE

SSD_HEADS = 16
SSD_HEAD_DIM = 64
SSD_INNER = SSD_HEADS * SSD_HEAD_DIM
SSD_GROUPS = 2
SSD_STATE = 128
SSD_CHUNK = 128
SSD_CONV_CH = SSD_INNER + 2 * SSD_GROUPS * SSD_STATE

GQA_HEADS = 8
GQA_KV_HEADS = 2
GQA_HEAD_DIM = 64

GQA_IN = (GQA_HEADS + 2 * GQA_KV_HEADS) * GQA_HEAD_DIM
GATE_IN = 3 * D_MODEL
MLA_OUT = MLA_HEADS * MLA_V
GQA_OUT = GQA_HEADS * GQA_HEAD_DIM
FFN_HIDDEN = -(-8 * D_MODEL // (3 * 256)) * 256
DEEPNORM_ALPHA = (2 * DEPTH) ** 0.25

LANE = 128
MOD_ROWS = 8
LOG2E = 1.4426950408889634

C_PM = 0
W_PM = MLA_Q_RANK + MLA_KV_RANK + LANE
C_Z = C_PM + W_PM
C_XBC = C_Z + SSD_INNER
C_DT = C_XBC + SSD_CONV_CH
C_PG = C_DT + LANE
C_GATE = C_PG + GQA_IN
W_INP = C_GATE + GATE_IN

VMEM_LIMIT = 56 * 1024 * 1024
ROW_TILE = 256
LATENT_ROW_TILE = 512
ATTN_TQ = 1024
ATTN_TK = 512
ATTN_COL_BLOCK = 256
ATTN_KEY_BLOCK = 64
SSD_BLOCK = 2 * SSD_CHUNK
VT_TILE = ROW_TILE
VT_ROWS = 144


def _cparams(sem):
    return pltpu.CompilerParams(dimension_semantics=sem, vmem_limit_bytes=VMEM_LIMIT)


def _const_spec(shape):
    nd = len(shape)
    return pl.BlockSpec(shape, lambda *_: (0,) * nd, pipeline_mode=pl.Buffered(1))


def _layer_norm(x):
    mu = jnp.mean(x, axis=-1, keepdims=True)
    xc = x - mu
    var = jnp.mean(xc * xc, axis=-1, keepdims=True)
    return xc * lax.rsqrt(var + NORM_EPS)


def _split_dot(x, m01, parts):
    pieces = []
    rem = x
    for _ in range(parts):
        piece = rem.astype(BF16)
        pieces.append(piece)
        rem = rem - piece.astype(F32)
    return jnp.dot(jnp.concatenate(pieces, axis=1), jnp.concatenate([m01] * parts, axis=0),
                   preferred_element_type=F32)


def _mod_rows(mod_ref, col, width, is_ctx):
    b = pl.program_id(0)
    lat = mod_ref[pl.ds(b, 1), col:col + width]
    ctx = mod_ref[MOD_ROWS // 2:MOD_ROWS // 2 + 1, col:col + width]
    return jnp.where(is_ctx, ctx, lat)


def _is_ctx_rows(tm, n_lat):
    t = pl.program_id(1)
    rows = t * tm + lax.broadcasted_iota(jnp.int32, (tm, 1), 0)
    return rows >= n_lat


def _mod_kernel(c_ref, w_ref, b_ref, o_ref):
    c = c_ref[...]
    s = (c * jax.nn.sigmoid(c)).astype(BF16)
    o_ref[0] = jnp.dot(s, w_ref[0].astype(BF16), preferred_element_type=F32) + b_ref[0]


def _modulation(cvec, w_mod, b_mod):
    depth, d, n = w_mod.shape
    tn = 1024
    return pl.pallas_call(
        _mod_kernel,
        grid=(depth, n // tn),
        in_specs=[
            pl.BlockSpec((MOD_ROWS, d), lambda l, j: (0, 0)),
            pl.BlockSpec((1, d, tn), lambda l, j: (l, 0, j)),
            pl.BlockSpec((1, 1, tn), lambda l, j: (l, 0, j)),
        ],
        out_specs=pl.BlockSpec((1, MOD_ROWS, tn), lambda l, j: (l, 0, j)),
        out_shape=jax.ShapeDtypeStruct((depth, MOD_ROWS, n), F32),
        compiler_params=_cparams(("arbitrary", "arbitrary")),
        name="modulation",
    )(cvec, w_mod, b_mod.reshape(depth, 1, n))


def _rope_lanes(x, c, s_fwd, s_bwd, half):
    n = x.shape[-1]
    return x * c + pltpu.roll(x, n - half, axis=1) * s_fwd + pltpu.roll(x, half, axis=1) * s_bwd


def _mla_prep(pm, gq_ref, gkv_ref, wuq_ref, wukv_ref, tq_ref, tk_ref, q_ref, k_ref, v_ref):
    pq = pm[:, :MLA_Q_RANK]
    ckv = pm[:, MLA_Q_RANK:MLA_Q_RANK + MLA_KV_RANK]
    kr = pm[:, MLA_Q_RANK + MLA_KV_RANK:]

    def rms(x, g):
        return x * lax.rsqrt(jnp.mean(x * x, axis=-1, keepdims=True) + NORM_EPS) * g

    q = jnp.dot(rms(pq, gq_ref[...]).astype(BF16), wuq_ref[...], preferred_element_type=F32)
    kv = jnp.dot(rms(ckv, gkv_ref[...]).astype(BF16), wukv_ref[...], preferred_element_type=F32)
    half = MLA_ROPE // 2
    kr = _rope_lanes(kr, tk_ref[0], tk_ref[1], tk_ref[2], half)
    cq, sq1, sq2 = tq_ref[0], tq_ref[1], tq_ref[2]
    for hd in range(MLA_HEADS):
        sl = slice(hd * LANE, (hd + 1) * LANE)
        q_ref[0, :, sl] = _rope_lanes(q[:, sl], cq, sq1, sq2, half).astype(q_ref.dtype)
        k_ref[0, :, sl] = (kv[:, sl] + kr).astype(k_ref.dtype)
    ones = jnp.ones((VT_ROWS - LANE, pm.shape[0]), v_ref.dtype)
    for j in range(MLA_HEADS // 2):
        c0 = MLA_HEADS * LANE + j * LANE
        v_ref[0, 0, j * VT_ROWS:j * VT_ROWS + LANE, :] = kv[:, c0:c0 + LANE].T.astype(v_ref.dtype)
        v_ref[0, 0, j * VT_ROWS + LANE:(j + 1) * VT_ROWS, :] = ones


def _gqa_prep(pg, gq_ref, gk_ref, bd_ref, tq_ref, tk_ref, q_ref, k_ref, v_ref):
    hd = GQA_HEAD_DIM
    q = pg[:, :GQA_OUT]
    k = pg[:, GQA_OUT:GQA_OUT + LANE]
    v = pg[:, GQA_OUT + LANE:]
    half = hd // 2

    def head_rms(x, g):
        w = x.shape[-1]
        ms = _split_dot(x * x, bd_ref[:w, :w], 2) * (1.0 / hd)
        return x * lax.rsqrt(ms + NORM_EPS) * g

    qn = head_rms(q, gq_ref[...])
    for j in range(GQA_OUT // LANE):
        sl = slice(j * LANE, (j + 1) * LANE)
        q_ref[0, :, sl] = _rope_lanes(qn[:, sl], tq_ref[0], tq_ref[1], tq_ref[2], half).astype(q_ref.dtype)
    kn = head_rms(k, gk_ref[...])
    kn = _rope_lanes(kn, tk_ref[0], tk_ref[1], tk_ref[2], half)
    lane = lax.broadcasted_iota(jnp.int32, kn.shape, 1)
    low = lane < hd
    kn_sw = pltpu.roll(kn, hd, axis=1)
    zero = jnp.zeros_like(kn)
    k_ref[0, :, 0 * LANE:1 * LANE] = jnp.where(low, kn, zero).astype(k_ref.dtype)
    k_ref[0, :, 1 * LANE:2 * LANE] = jnp.where(low, zero, kn_sw).astype(k_ref.dtype)
    k_ref[0, :, 2 * LANE:3 * LANE] = jnp.where(low, kn_sw, zero).astype(k_ref.dtype)
    k_ref[0, :, 3 * LANE:4 * LANE] = jnp.where(low, zero, kn).astype(k_ref.dtype)
    v_t = v.T.astype(v_ref.dtype)
    ones = jnp.ones((VT_ROWS - LANE, v.shape[0]), v_ref.dtype)
    for g in range(GQA_KV_HEADS):
        vg = v_t[g * hd:(g + 1) * hd]
        v_ref[0, 0, g * VT_ROWS:g * VT_ROWS + hd, :] = vg
        v_ref[0, 0, g * VT_ROWS + hd:g * VT_ROWS + LANE, :] = vg
        v_ref[0, 0, g * VT_ROWS + LANE:(g + 1) * VT_ROWS, :] = ones


def _split_row_specs(tm, width, n_lat, ctx_row0):
    lat_tiles = n_lat // tm
    assert ctx_row0 % tm == 0
    ctx_block0 = ctx_row0 // tm
    return [pl.BlockSpec((1, tm, width), lambda b, t: (b, jnp.minimum(t, lat_tiles - 1), 0)),
            pl.BlockSpec((1, tm, width), lambda b, t: (b, ctx_block0 + jnp.maximum(t - lat_tiles, 0), 0))]


def _pick_rows(lat_ref, ctx_ref, tm, n_lat):
    return jnp.where(pl.program_id(1) * tm >= n_lat, ctx_ref[0], lat_ref[0])


def _inproj_kernel(xl_ref, xc_ref, mod_ref, w_ref, bg_ref,
                   gqm_ref, gkvm_ref, wuq_ref, wukv_ref, tmq_ref, tmk_ref,
                   gqg_ref, gkg_ref, bd_ref, tgq_ref, tgk_ref,
                   z_ref, xbc_ref, dt_ref, gate_ref, qm_ref, km_ref, vm_ref, qg_ref, kg_ref, vg_ref,
                   *, tm, n_lat):
    is_ctx = _is_ctx_rows(tm, n_lat)
    d = D_MODEL
    sh = _mod_rows(mod_ref, 0, d, is_ctx)
    sc = _mod_rows(mod_ref, d, d, is_ctx)
    h = (_layer_norm(_pick_rows(xl_ref, xc_ref, tm, n_lat)) * (1.0 + sc) + sh).astype(BF16)

    def proj(c0, width):
        return jnp.dot(h, w_ref[:, c0:c0 + width], preferred_element_type=F32)

    _mla_prep(proj(C_PM, W_PM), gqm_ref, gkvm_ref, wuq_ref, wukv_ref, tmq_ref, tmk_ref, qm_ref, km_ref, vm_ref)
    _gqa_prep(proj(C_PG, GQA_IN), gqg_ref, gkg_ref, bd_ref, tgq_ref, tgk_ref, qg_ref, kg_ref, vg_ref)
    z_ref[0] = proj(C_Z, SSD_INNER)
    xbc_ref[0] = proj(C_XBC, SSD_CONV_CH)
    dt_ref[0] = proj(C_DT, LANE)
    for j in range(3):
        c0 = j * D_MODEL
        logits = proj(C_GATE + c0, D_MODEL) + bg_ref[:, c0:c0 + D_MODEL]
        gate_ref[0, :, c0:c0 + D_MODEL] = jax.nn.sigmoid(logits).astype(gate_ref.dtype)


def _in_projection(x_lat, x_ctx, ctx_row0, t_all, mod, w_inp, b_gate, mla_consts, gqa_consts, *, n_lat, tm):
    bsz, _, d = x_lat.shape
    assert tm == VT_TILE
    row = lambda w: pl.BlockSpec((1, tm, w), lambda b, t: (b, t, 0))
    tab = pl.BlockSpec((3, tm, LANE), lambda b, t: (0, t, 0))
    vt_spec = lambda rows: pl.BlockSpec((1, 1, rows, tm), lambda b, t: (b, t, 0, 0))
    g_qm, g_kvm, wuq_p, wukv_p, tab_mq, tab_mk = mla_consts
    g_qg, g_kg, bd, tab_gq, tab_gk = gqa_consts
    hw = MLA_HEADS * LANE
    kw = GQA_KV_HEADS * 2 * LANE
    vm_rows = (MLA_HEADS // 2) * VT_ROWS
    vg_rows = GQA_KV_HEADS * VT_ROWS
    seq = lambda w, dt: jax.ShapeDtypeStruct((bsz, t_all, w), dt)
    vts = lambda rows: jax.ShapeDtypeStruct((bsz, t_all // tm, rows, tm), BF16)
    consts = (mod, w_inp, b_gate, g_qm, g_kvm, wuq_p, wukv_p)
    return pl.pallas_call(
        functools.partial(_inproj_kernel, tm=tm, n_lat=n_lat),
        grid=(bsz, t_all // tm),
        in_specs=_split_row_specs(tm, d, n_lat, ctx_row0) + [_const_spec(a.shape) for a in consts] + [tab, tab]
                 + [_const_spec(a.shape) for a in (g_qg, g_kg, bd)] + [tab, tab],
        out_specs=[row(SSD_INNER), row(SSD_CONV_CH), row(LANE), row(GATE_IN),
                   row(hw), row(hw), vt_spec(vm_rows), row(GQA_OUT), row(kw), vt_spec(vg_rows)],
        out_shape=[seq(SSD_INNER, F32), seq(SSD_CONV_CH, F32), seq(LANE, F32), seq(GATE_IN, BF16),
                   seq(hw, BF16), seq(hw, BF16), vts(vm_rows), seq(GQA_OUT, BF16), seq(kw, BF16), vts(vg_rows)],
        compiler_params=_cparams(("parallel", "parallel")),
        name="in_projection",
    )(x_lat, x_ctx, *consts, tab_mq, tab_mk, g_qg, g_kg, bd, tab_gq, tab_gk)


def _attn_kernel(q_ref, k_ref, vt_ref, o_ref, m_ref, acc_ref, a0_ref, a1_ref, c0_ref, c1_ref,
                 p0_ref, p1_ref, s0_ref, s1_ref, *, pairs, tq, tk, n_lat, ctx_len, ctx_queries):
    cb = ATTN_COL_BLOCK
    vr = VT_ROWS
    s_bufs, p_bufs, a_bufs, c_bufs = (s0_ref, s1_ref), (p0_ref, p1_ref), (a0_ref, a1_ref), (c0_ref, c1_ref)

    def scores(r0, size, par):
        for pi, (_, units) in enumerate(pairs):
            for ui, (qb, kb) in enumerate(units):
                s = lax.dot_general(
                    k_ref[0, pl.ds(r0, size), kb * LANE:(kb + 1) * LANE],
                    q_ref[0, :, qb * LANE:(qb + 1) * LANE],
                    (((1,), (1,)), ((), ())), preferred_element_type=F32)
                s_bufs[par][pi, :size, ui * tq:(ui + 1) * tq] = s
                c_bufs[par][pi, :, ui * tq:(ui + 1) * tq] = jnp.max(s, axis=0, keepdims=True)

    def softmax(size, par):
        for pi in range(len(pairs)):
            for c in range(0, 2 * tq, cb):
                m_prev = m_ref[pi, :, c:c + cb]
                m_next = jnp.maximum(m_prev, c_bufs[par][pi, :, c:c + cb])
                for r in range(0, size, ATTN_KEY_BLOCK):
                    blk = s_bufs[par][pi, r:r + ATTN_KEY_BLOCK, c:c + cb]
                    p_bufs[par][pi, r:r + ATTN_KEY_BLOCK, c:c + cb] = jnp.exp2(blk - m_next).astype(BF16)
                a_bufs[par][pi, :, c:c + cb] = jnp.exp2(m_prev - m_next)
                m_ref[pi, :, c:c + cb] = m_next

    def values(u0, size, par):
        for pi, (vb, _) in enumerate(pairs):
            vt = jnp.concatenate([vt_ref[0, u0 + i, vb * vr:(vb + 1) * vr, :]
                                  for i in range(size // VT_TILE)], axis=1)
            pv = jnp.dot(vt, p_bufs[par][pi, :size, :], preferred_element_type=F32)
            acc_ref[pi] = a_bufs[par][pi] * acc_ref[pi] + pv

    m_ref[...] = jnp.full(m_ref.shape, -jnp.inf, F32)
    acc_ref[...] = jnp.zeros(acc_ref.shape, F32)
    tku = tk // VT_TILE
    ctx_u = n_lat // VT_TILE
    scores(n_lat, ctx_len, 0)
    if ctx_queries:
        softmax(ctx_len, 0)
        values(ctx_u, ctx_len, 0)
    else:
        n = n_lat // tk
        scores(0, tk, 1)
        softmax(ctx_len, 0)
        scores(tk, tk, 0)
        softmax(tk, 1)
        values(ctx_u, ctx_len, 0)

        def body(jj, carry):
            r = pl.multiple_of(jj * (2 * tk), 2 * tk)
            u = jj * (2 * tku)
            scores(r + 2 * tk, tk, 1)
            softmax(tk, 0)
            values(u, tk, 1)
            scores(r + 3 * tk, tk, 0)
            softmax(tk, 1)
            values(u + tku, tk, 0)
            return carry

        lax.fori_loop(0, (n - 2) // 2, body, 0)
        softmax(tk, 0)
        values((n - 2) * tku, tk, 1)
        values((n - 1) * tku, tk, 0)
    row = lax.broadcasted_iota(jnp.int32, (LANE, tq), 0)
    for pi in range(len(pairs)):
        acc = acc_ref[pi]
        o_t = acc[:LANE] / acc[LANE:LANE + 1]
        o_t = jnp.where(row < LANE // 2, o_t[:, :tq], o_t[:, tq:])
        o_ref[0, :, pi * LANE:(pi + 1) * LANE] = o_t.T.astype(o_ref.dtype)


def _attention(q, k, vt, *, groups, q_w, k_w, vt_rows, pairs, n_lat, ctx_queries, tq, tk):
    bsz, t_all, _ = k.shape
    ctx_len = t_all - n_lat
    out_w = len(pairs) * LANE
    rows = ctx_len if ctx_queries else n_lat
    q_tile0 = n_lat // tq if ctx_queries else 0
    assert rows % tq == 0 and n_lat % tq == 0 and n_lat % (2 * tk) == 0 and ctx_len <= tk
    assert (2 * tq) % ATTN_COL_BLOCK == 0 and tk % VT_TILE == 0 and ctx_len % VT_TILE == 0
    state = pltpu.VMEM((len(pairs), 1, 2 * tq), F32)
    acc = pltpu.VMEM((len(pairs), VT_ROWS, 2 * tq), F32)
    pbuf = pltpu.VMEM((len(pairs), tk, 2 * tq), BF16)
    sbuf = pltpu.VMEM((len(pairs), tk, 2 * tq), F32)
    return pl.pallas_call(
        functools.partial(_attn_kernel, pairs=pairs, tq=tq, tk=tk, n_lat=n_lat, ctx_len=ctx_len,
                          ctx_queries=ctx_queries),
        grid=(bsz, groups, rows // tq),
        in_specs=[
            pl.BlockSpec((1, tq, q_w), lambda b, g, t: (b, t + q_tile0, g)),
            pl.BlockSpec((1, t_all, k_w), lambda b, g, t: (b, 0, g)),
            pl.BlockSpec((1, t_all // VT_TILE, vt_rows, VT_TILE), lambda b, g, t: (b, 0, g, 0)),
        ],
        out_specs=pl.BlockSpec((1, tq, out_w), lambda b, g, t: (b, t, g)),
        out_shape=jax.ShapeDtypeStruct((bsz, rows, groups * out_w), BF16),
        scratch_shapes=[state, acc, state, state, state, state, pbuf, pbuf, sbuf, sbuf],
        compiler_params=_cparams(("parallel", "parallel", "arbitrary")),
        name="attention_ctx" if ctx_queries else "attention",
    )(q, k, vt)


def _ssd_chunk_index(i, n_chunks, ctx_chunks, reverse):
    if reverse:
        return n_chunks - 1 - i
    lat_chunks = n_chunks - ctx_chunks
    return jnp.where(i < ctx_chunks, lat_chunks + i, i - ctx_chunks)


def _ssd_kernel(*refs, reverse, n_chunks, ctx_chunks):
    if reverse:
        (u_ref, dt_ref, dtb_ref, a_ref, tri_ref, ehd_ref,
         yf_ref, z_ref, dsk_ref, g_ref, y_ref, h_ref) = refs
    else:
        (xbc_ref, prev_ref, next_ref, dt_ref, cw_ref, cb_ref, dtb_ref, a_ref, tri_ref, ehd_ref,
         y_ref, uo_ref, h_ref) = refs
    q = SSD_CHUNK
    rows = SSD_BLOCK
    i = pl.program_id(1)
    ci = _ssd_chunk_index(i, n_chunks, ctx_chunks, reverse)

    @pl.when(i == 0)
    def _():
        h_ref[...] = jnp.zeros_like(h_ref)

    if reverse:
        u_blk = u_ref[0]
    else:
        x = xbc_ref[0]
        lat_chunks = n_chunks - ctx_chunks
        at_start = jnp.logical_or(ci == 0, ci == lat_chunks)
        at_end = jnp.logical_or(ci == lat_chunks - 1, ci == n_chunks - 1)
        prev_row = jnp.where(at_start, 0.0, prev_ref[0, 7:8, :])
        next_row = jnp.where(at_end, 0.0, next_ref[0, 0:1, :])
        row = lax.broadcasted_iota(jnp.int32, (rows, 1), 0)
        x_prev = jnp.where(row == 0, prev_row, pltpu.roll(x, 1, axis=0))
        x_next = jnp.where(row == rows - 1, next_row, pltpu.roll(x, rows - 1, axis=0))
        u_blk = cw_ref[0:1, :] * x_prev + cw_ref[1:2, :] * x + cw_ref[2:3, :] * x_next + cb_ref[...]
        u_blk = u_blk * jax.nn.sigmoid(u_blk)
        uo_ref[0] = u_blk

    subs = range(rows // q)
    for sub in (reversed(subs) if reverse else subs):
        sl = slice(sub * q, (sub + 1) * q)
        y = _ssd_chunk(u_blk[sl], dt_ref[0, sl, :], dtb_ref, a_ref, tri_ref, ehd_ref, h_ref, reverse)
        if reverse:
            yt = yf_ref[0, sl, :] + y + dsk_ref[...] * u_blk[sl, :SSD_INNER]
            zz = z_ref[0, sl, :]
            yt = yt * (zz * jax.nn.sigmoid(zz))
            yt = yt * lax.rsqrt(jnp.mean(yt * yt, axis=-1, keepdims=True) + NORM_EPS) * g_ref[...]
            y_ref[0, sl, :] = yt.astype(y_ref.dtype)
        else:
            y_ref[0, sl, :] = y


def _ssd_chunk(u, dt_raw, dtb_ref, a_ref, tri_ref, ehd_ref, h_ref, reverse):
    q = SSD_CHUNK
    xs = u[:, :SSD_INNER]
    gs = SSD_STATE
    bm = [u[:, SSD_INNER + g * gs:SSD_INNER + (g + 1) * gs] for g in range(SSD_GROUPS)]
    cm = [u[:, SSD_INNER + (SSD_GROUPS + g) * gs:SSD_INNER + (SSD_GROUPS + g + 1) * gs]
          for g in range(SSD_GROUPS)]

    dt = jax.nn.softplus(dt_raw + dtb_ref[...])
    da = dt * a_ref[...]
    a_cum = _tri_cumsum(tri_ref, da)
    end = 0 if reverse else q - 1
    a_tot = a_cum[end:end + 1, :]
    e_cum = jnp.exp(a_cum)
    dte = jnp.exp(a_tot - a_cum)
    dt_x = _split_dot(dt, ehd_ref[...], 2)
    ec_x = _split_dot(e_cum, ehd_ref[...], 2)
    dte_x = _split_dot(dte, ehd_ref[...], 2)
    xdt = xs * dt_x
    xdt_b = xdt.astype(BF16)
    xst_b = (xdt * dte_x).astype(BF16)
    a_cum_t = a_cum.T

    kk = lax.broadcasted_iota(jnp.int32, (q, q), 1)
    qq = lax.broadcasted_iota(jnp.int32, (q, q), 0)
    in_order = (kk >= qq) if reverse else (kk <= qq)
    lane = lax.broadcasted_iota(jnp.int32, (q, LANE), 1)
    off = SSD_HEADS if reverse else 0
    hpg = SSD_HEADS // SSD_GROUPS
    gw = hpg * SSD_HEAD_DIM
    y_parts = []
    for g in range(SSD_GROUPS):
        bm_b = bm[g].astype(BF16)
        cm_b = cm[g].astype(BF16)
        scores = lax.dot_general(cm_b, bm_b, (((1,), (1,)), ((), ())), preferred_element_type=F32)
        h_prev = h_ref[g]
        y_off = jnp.dot(cm_b, h_prev.astype(BF16), preferred_element_type=F32) * ec_x[:, g * gw:(g + 1) * gw]
        pair_out = []
        for j in range(hpg // 2):
            xp = xdt_b[:, g * gw + j * LANE:g * gw + (j + 1) * LANE]
            res = []
            for e in range(2):
                hd = g * hpg + 2 * j + e
                a_q = a_cum[:, off + hd:off + hd + 1]
                a_k = a_cum_t[off + hd:off + hd + 1, :]
                decay = jnp.exp(jnp.where(in_order, a_q - a_k, -jnp.inf))
                pmat = (scores * decay).astype(BF16)
                res.append(jnp.dot(pmat, xp, preferred_element_type=F32))
            pair_out.append(jnp.where(lane < SSD_HEAD_DIM, res[0], res[1]))
        y_parts.append(jnp.concatenate(pair_out, axis=1) + y_off)
        st = jnp.dot(bm[g].T.astype(BF16), xst_b[:, g * gw:(g + 1) * gw], preferred_element_type=F32)
        h_ref[g] = h_prev * ec_x[end:end + 1, g * gw:(g + 1) * gw] + st
    return jnp.concatenate(y_parts, axis=1)


def _tri_cumsum(tri_ref, da):
    acc = None
    rem = da
    for _ in range(3):
        piece = rem.astype(BF16)
        term = jnp.dot(tri_ref[...], piece, preferred_element_type=F32)
        acc = term if acc is None else acc + term
        rem = rem - piece.astype(F32)
    return acc


def _ssd_direction(xbc, dt, conv, dt_bias, a_row, tri, ehd, extra, *, ctx_len, reverse):
    bsz, t_all, cch = xbc.shape
    q = SSD_BLOCK
    assert t_all % q == 0 and ctx_len % q == 0
    n_chunks = t_all // q
    ctx_chunks = ctx_len // q
    halo = 8
    hb = q // halo
    n_halo = t_all // halo
    cidx = lambda i: _ssd_chunk_index(i, n_chunks, ctx_chunks, reverse)
    chunk = lambda w: pl.BlockSpec((1, q, w), lambda b, i: (b, cidx(i), 0))
    tail = [_const_spec(dt_bias.shape), _const_spec(a_row.shape), _const_spec(tri.shape), _const_spec(ehd.shape)]
    if reverse:
        y_f, z, d_x, g_ssd = extra
        in_specs = [chunk(cch), chunk(LANE)] + tail + [
            chunk(SSD_INNER), chunk(SSD_INNER), _const_spec(d_x.shape), _const_spec(g_ssd.shape)]
        args = [xbc, dt, dt_bias, a_row, tri, ehd, y_f, z, d_x, g_ssd]
        out_specs = chunk(SSD_INNER)
        out_shape = jax.ShapeDtypeStruct((bsz, t_all, SSD_INNER), BF16)
    else:
        conv_w, conv_b = conv
        in_specs = [
            chunk(cch),
            pl.BlockSpec((1, halo, cch), lambda b, i: (b, jnp.maximum(cidx(i) * hb - 1, 0), 0)),
            pl.BlockSpec((1, halo, cch), lambda b, i: (b, jnp.minimum((cidx(i) + 1) * hb, n_halo - 1), 0)),
            chunk(LANE), _const_spec(conv_w.shape), _const_spec(conv_b.shape)] + tail
        args = [xbc, xbc, xbc, dt, conv_w, conv_b, dt_bias, a_row, tri, ehd]
        out_specs = [chunk(SSD_INNER), chunk(cch)]
        out_shape = [jax.ShapeDtypeStruct((bsz, t_all, SSD_INNER), F32),
                     jax.ShapeDtypeStruct((bsz, t_all, cch), F32)]
    hpg = SSD_HEADS // SSD_GROUPS
    return pl.pallas_call(
        functools.partial(_ssd_kernel, reverse=reverse, n_chunks=n_chunks, ctx_chunks=ctx_chunks),
        grid=(bsz, n_chunks),
        in_specs=in_specs,
        out_specs=out_specs,
        out_shape=out_shape,
        scratch_shapes=[pltpu.VMEM((SSD_GROUPS, SSD_STATE, hpg * SSD_HEAD_DIM), F32)],
        compiler_params=_cparams(("parallel", "arbitrary")),
        name="ssd_bwd" if reverse else "ssd_fwd",
    )(*args)


def _post_kernel(xl_ref, xc_ref, yml_ref, ymc_ref, ys_ref, ygl_ref, ygc_ref, gate_ref, mod_ref,
                 wo_ref, wi_ref, wf_ref, l1g_ref, l1b_ref, l2g_ref, l2b_ref, o_ref, act_ref, *, tm, n_lat, th):
    is_ctx = _is_ctx_rows(tm, n_lat)
    d = D_MODEL
    mix = None
    row0 = 0
    branches = (_pick_rows(yml_ref, ymc_ref, tm, n_lat), ys_ref[0], _pick_rows(ygl_ref, ygc_ref, tm, n_lat))
    for j, y in enumerate(branches):
        kw = y.shape[-1]
        br = jnp.dot(y, wo_ref[row0:row0 + kw, :], preferred_element_type=F32)
        br = gate_ref[0, :, j * d:(j + 1) * d].astype(F32) * br
        mix = br if mix is None else mix + br
        row0 += kw
    g1 = _mod_rows(mod_ref, 2 * d, d, is_ctx)
    x_in = _pick_rows(xl_ref, xc_ref, tm, n_lat)
    x1 = _layer_norm(DEEPNORM_ALPHA * x_in + g1 * mix) * l1g_ref[...] + l1b_ref[...]

    sh2 = _mod_rows(mod_ref, 3 * d, d, is_ctx)
    sc2 = _mod_rows(mod_ref, 4 * d, d, is_ctx)
    g2 = _mod_rows(mod_ref, 5 * d, d, is_ctx)
    h2 = (_layer_norm(x1) * (1.0 + sc2) + sh2).astype(BF16)
    hid = FFN_HIDDEN
    for c0 in range(0, hid, th):
        gt = jnp.dot(h2, wi_ref[:, c0:c0 + th], preferred_element_type=F32)
        up = jnp.dot(h2, wi_ref[:, hid + c0:hid + c0 + th], preferred_element_type=F32)
        act_ref[:, c0:c0 + th] = (gt * jax.nn.sigmoid(gt) * up).astype(BF16)
    fx = jnp.dot(act_ref[...], wf_ref[...], preferred_element_type=F32)
    o_ref[0] = _layer_norm(DEEPNORM_ALPHA * x1 + g2 * fx) * l2g_ref[...] + l2b_ref[...]


def _post_mixer(x_pair, y_mla_pair, y_ssd, y_gqa_pair, gate, mod, w_out, w_ffn_in, w_ffn_out,
                l1g, l1b, l2g, l2b, *, n_lat, tm, with_ctx):
    bsz, t_all, _ = y_ssd.shape
    d = D_MODEL
    row = lambda w: pl.BlockSpec((1, tm, w), lambda b, t: (b, t, 0))
    vec = _const_spec((1, d))
    t_out = t_all if with_ctx else n_lat
    split = lambda w, triple: _split_row_specs(tm, w, n_lat, triple[2] if with_ctx else 0)
    return pl.pallas_call(
        functools.partial(_post_kernel, tm=tm, n_lat=n_lat, th=256),
        grid=(bsz, t_out // tm),
        in_specs=split(d, x_pair) + split(MLA_OUT, y_mla_pair) + [row(SSD_INNER)] + split(GQA_OUT, y_gqa_pair)
                 + [row(GATE_IN), _const_spec(mod.shape),
                    _const_spec(w_out.shape), _const_spec(w_ffn_in.shape), _const_spec(w_ffn_out.shape),
                    vec, vec, vec, vec],
        out_specs=pl.BlockSpec((1, tm, d), lambda b, t: (b, t, 0)),
        out_shape=jax.ShapeDtypeStruct((bsz, t_out, d), F32),
        scratch_shapes=[pltpu.VMEM((tm, FFN_HIDDEN), BF16)],
        compiler_params=_cparams(("parallel", "parallel")),
        name="post_mixer",
    )(x_pair[0], x_pair[1], y_mla_pair[0], y_mla_pair[1], y_ssd, y_gqa_pair[0], y_gqa_pair[1], gate, mod,
      w_out, w_ffn_in, w_ffn_out, l1g, l1b, l2g, l2b)


def _rope_tables(n_lat, ctx_len, rot_dim, lane_dim, lane_off, scale):
    rows = n_lat // GRID_W
    row = np.repeat(np.arange(rows), GRID_W)
    col = np.tile(np.arange(GRID_W), rows)
    n_freq = rot_dim // 4
    inv_freq = ROPE_THETA ** (-np.arange(n_freq, dtype=np.float64) / n_freq)
    ang = np.concatenate([row[:, None] * inv_freq, col[:, None] * inv_freq], axis=-1)
    cos, sin = np.cos(ang), np.sin(ang)
    half = rot_dim // 2
    lane = np.arange(LANE)
    rel = (lane % lane_dim) - lane_off
    is_rot = (rel >= 0) & (rel < rot_dim)
    idx = np.where(is_rot, rel % half, 0)
    first = is_rot & (rel < half)
    second = is_rot & (rel >= half)
    c = np.where(is_rot[None, :], cos[:, idx], 1.0)
    s1 = np.where(first[None, :], -sin[:, idx], 0.0)
    s2 = np.where(second[None, :], sin[:, idx], 0.0)
    lat = np.stack([c, s1, s2])
    ctx = np.stack([np.ones((ctx_len, LANE)), np.zeros((ctx_len, LANE)), np.zeros((ctx_len, LANE))])
    return jnp.asarray((np.concatenate([lat, ctx], axis=1) * scale).astype(np.float32))


def _pad_cols(w, width, offset=0):
    return jnp.pad(w, ((0, 0), (offset, width - offset - w.shape[1])))


def _layout_w_in(w):
    c = 0
    q_lat = w[:, c:c + MLA_Q_RANK]; c += MLA_Q_RANK
    kv_lat = w[:, c:c + MLA_KV_RANK]; c += MLA_KV_RANK
    k_rope = _pad_cols(w[:, c:c + MLA_ROPE], LANE, MLA_NOPE); c += MLA_ROPE
    z = w[:, c:c + SSD_INNER]; c += SSD_INNER
    xbc = w[:, c:c + SSD_CONV_CH]; c += SSD_CONV_CH
    dt = _pad_cols(w[:, c:c + 2 * SSD_HEADS], LANE); c += 2 * SSD_HEADS
    rest = w[:, c:]
    return jnp.concatenate([q_lat, kv_lat, k_rope, z, xbc, dt, rest], axis=1).astype(BF16)


def _layout_w_uq(w):
    w = w.reshape(w.shape[0], MLA_HEADS, MLA_QK)
    w = jnp.pad(w, ((0, 0), (0, 0), (0, LANE - MLA_QK)))
    return w.reshape(w.shape[0], MLA_HEADS * LANE).astype(BF16)


def _layout_w_ukv(w):
    w = w.reshape(w.shape[0], MLA_HEADS, MLA_NOPE + MLA_V)
    k = jnp.pad(w[:, :, :MLA_NOPE], ((0, 0), (0, 0), (0, LANE - MLA_NOPE)))
    v = w[:, :, MLA_NOPE:]
    return jnp.concatenate([k.reshape(w.shape[0], -1), v.reshape(w.shape[0], -1)], axis=1).astype(BF16)


def _lane_row(vec, width=LANE):
    return _pad_cols(vec.reshape(1, -1).astype(F32), width)


def kernel(x, c, ctx, c_ctx, w_mod, b_mod, w_in, b_gate, w_uq, g_q_mla, w_ukv, g_kv_mla, conv_w, conv_b,
           a_log, dt_bias, d_skip, g_ssd, g_q_gqa, g_k_gqa, w_out, ln1_g, ln1_b, w_ffn_in, w_ffn_out,
           ln2_g, ln2_b):
    bsz, n_lat, d = x.shape
    ctx_len = ctx.shape[1]
    depth = w_mod.shape[0]
    assert d == D_MODEL and depth == DEPTH and bsz <= MOD_ROWS // 2
    tm = ROW_TILE
    assert ctx_len % tm == 0 and n_lat % tm == 0

    cvec = jnp.zeros((MOD_ROWS, d), F32).at[:bsz].set(c).at[MOD_ROWS // 2].set(c_ctx)
    mod_all = _modulation(cvec, w_mod, b_mod)

    mla_scale = MLA_QK ** -0.5 * LOG2E
    gqa_scale = GQA_HEAD_DIM ** -0.5 * LOG2E
    tab_mq = _rope_tables(n_lat, ctx_len, MLA_ROPE, LANE, MLA_NOPE, mla_scale)
    tab_mk = _rope_tables(n_lat, ctx_len, MLA_ROPE, LANE, MLA_NOPE, 1.0)
    tab_gq = _rope_tables(n_lat, ctx_len, GQA_HEAD_DIM, GQA_HEAD_DIM, 0, gqa_scale)
    tab_gk = _rope_tables(n_lat, ctx_len, GQA_HEAD_DIM, GQA_HEAD_DIM, 0, 1.0)

    qn = SSD_CHUNK
    r = np.arange(qn)
    tri_f = jnp.asarray((r[None, :] <= r[:, None]).astype(np.float32), BF16)
    tri_b = jnp.asarray((r[None, :] >= r[:, None]).astype(np.float32), BF16)
    lane1k = np.arange(SSD_INNER) // SSD_HEAD_DIM
    ehd_f = jnp.asarray((np.arange(LANE)[:, None] == lane1k[None, :]).astype(np.float32), BF16)
    ehd_b = jnp.asarray((np.arange(LANE)[:, None] == (lane1k[None, :] + SSD_HEADS)).astype(np.float32), BF16)
    blk = np.arange(GQA_OUT) // GQA_HEAD_DIM
    bd = jnp.asarray((blk[:, None] == blk[None, :]).astype(np.float32), BF16)

    mla_attn = functools.partial(_attention, groups=MLA_HEADS // 4, q_w=4 * LANE, k_w=4 * LANE,
                                 vt_rows=2 * VT_ROWS,
                                 pairs=((0, ((0, 0), (1, 1))), (1, ((2, 2), (3, 3)))), n_lat=n_lat)
    gqa_attn = functools.partial(_attention, groups=GQA_KV_HEADS, q_w=2 * LANE, k_w=2 * LANE, vt_rows=VT_ROWS,
                                 pairs=((0, ((0, 0), (0, 1))), (0, ((1, 0), (1, 1)))), n_lat=n_lat)

    def attend(fn, q, k, v, with_ctx):
        y = fn(q, k, v, ctx_queries=False, tq=ATTN_TQ, tk=ATTN_TK)
        y_c = fn(q, k, v, ctx_queries=True, tq=ctx_len, tk=ATTN_TK) if with_ctx else y
        return (y, y_c, 0)

    t_all = n_lat + ctx_len
    x_pair = (x, ctx, 0)
    for i in range(depth):
        with_ctx = i < depth - 1
        mod = mod_all[i]
        w_inp = _layout_w_in(w_in[i])
        mla_consts = (g_q_mla[i].reshape(1, -1), g_kv_mla[i].reshape(1, -1),
                      _layout_w_uq(w_uq[i]), _layout_w_ukv(w_ukv[i]), tab_mq, tab_mk)
        gqa_consts = (jnp.tile(g_q_gqa[i], GQA_HEADS).reshape(1, -1),
                      jnp.tile(g_k_gqa[i], GQA_KV_HEADS).reshape(1, -1), bd, tab_gq, tab_gk)
        z, xbc, dtr, gate, q_m, k_m, v_m, q_g, k_g, v_g = _in_projection(
            *x_pair, t_all, mod, w_inp, b_gate[i].reshape(1, -1), mla_consts, gqa_consts, n_lat=n_lat, tm=tm)
        y_mla = attend(mla_attn, q_m, k_m, v_m, with_ctx)
        y_gqa = attend(gqa_attn, q_g, k_g, v_g, with_ctx)
        a = -jnp.exp(a_log[i].astype(F32))
        a_row = _lane_row(a.reshape(-1))
        dtb_row = _lane_row(dt_bias[i].reshape(-1))
        y_f, u_act = _ssd_direction(xbc, dtr, (conv_w[i], conv_b[i].reshape(1, -1)), dtb_row, a_row,
                                    tri_f, ehd_f, None, ctx_len=ctx_len, reverse=False)
        d_x = jnp.repeat(d_skip[i], SSD_HEAD_DIM).reshape(1, -1)
        y_ssd = _ssd_direction(u_act, dtr, None, dtb_row, a_row, tri_b, ehd_b,
                               (y_f, z, d_x, g_ssd[i].reshape(1, -1)), ctx_len=ctx_len, reverse=True)
        tm_post = tm if with_ctx else LATENT_ROW_TILE
        xa = _post_mixer(x_pair, y_mla, y_ssd, y_gqa, gate, mod, w_out[i].astype(BF16),
                         w_ffn_in[i].astype(BF16), w_ffn_out[i].astype(BF16),
                         ln1_g[i].reshape(1, -1), ln1_b[i].reshape(1, -1),
                         ln2_g[i].reshape(1, -1), ln2_b[i].reshape(1, -1), n_lat=n_lat,
                         tm=tm_post, with_ctx=with_ctx)
        x_pair = (xa, xa, n_lat)
    return xa
```

```python
import functools

import jax
import jax.numpy as jnp
import numpy as np
from jax import lax
from jax.experimental import pallas as pl
from jax.experimental.pallas import tpu as pltpu

F32 = jnp.float32
BF16 = jnp.bfloat16

D_MODEL = 1024
DEPTH = 2
GRID_W = 64
ROPE_THETA = 10000.0
NORM_EPS = 1e-6

MLA_HEADS = 8
MLA_Q_RANK = 384
MLA_KV_RANK = 256
MLA_NOPE = 64
MLA_ROPE = 32
MLA_V = 64
MLA_QK = MLA_NOPE + MLA_ROPE

SSD_HEADS = 16
SSD_HEAD_DIM = 64
SSD_INNER = SSD_HEADS * SSD_HEAD_DIM
SSD_GROUPS = 2
SSD_STATE = 128
SSD_CHUNK = 128
SSD_CONV_CH = SSD_INNER + 2 * SSD_GROUPS * SSD_STATE

GQA_HEADS = 8
GQA_KV_HEADS = 2
GQA_HEAD_DIM = 64

GQA_IN = (GQA_HEADS + 2 * GQA_KV_HEADS) * GQA_HEAD_DIM
GATE_IN = 3 * D_MODEL
MLA_OUT = MLA_HEADS * MLA_V
GQA_OUT = GQA_HEADS * GQA_HEAD_DIM
FFN_HIDDEN = -(-8 * D_MODEL // (3 * 256)) * 256
DEEPNORM_ALPHA = (2 * DEPTH) ** 0.25

LANE = 128
MOD_ROWS = 8
LOG2E = 1.4426950408889634

C_PM = 0
W_PM = MLA_Q_RANK + MLA_KV_RANK + LANE
C_Z = C_PM + W_PM
C_XBC = C_Z + SSD_INNER
C_DT = C_XBC + SSD_CONV_CH
C_PG = C_DT + LANE
C_GATE = C_PG + GQA_IN
W_INP = C_GATE + GATE_IN

VMEM_LIMIT = 56 * 1024 * 1024
ROW_TILE = 256
LATENT_ROW_TILE = 512
ATTN_TQ = 512
ATTN_TK = 512
ATTN_COL_BLOCK = 256
ATTN_KEY_BLOCK = 64
SSD_BLOCK = 2 * SSD_CHUNK
VT_TILE = ROW_TILE
VT_ROWS = 144


def _cparams(sem):
    return pltpu.CompilerParams(dimension_semantics=sem, vmem_limit_bytes=VMEM_LIMIT)


def _const_spec(shape):
    nd = len(shape)
    return pl.BlockSpec(shape, lambda *_: (0,) * nd, pipeline_mode=pl.Buffered(1))


def _layer_norm(x):
    mu = jnp.mean(x, axis=-1, keepdims=True)
    xc = x - mu
    var = jnp.mean(xc * xc, axis=-1, keepdims=True)
    return xc * lax.rsqrt(var + NORM_EPS)


def _split_dot(x, m01, parts):
    pieces = []
    rem = x
    for _ in range(parts):
        piece = rem.astype(BF16)
        pieces.append(piece)
        rem = rem - piece.astype(F32)
    return jnp.dot(jnp.concatenate(pieces, axis=1), jnp.concatenate([m01] * parts, axis=0),
                   preferred_element_type=F32)


def _mod_rows(mod_ref, col, width, is_ctx):
    b = pl.program_id(0)
    lat = mod_ref[pl.ds(b, 1), col:col + width]
    ctx = mod_ref[MOD_ROWS // 2:MOD_ROWS // 2 + 1, col:col + width]
    return jnp.where(is_ctx, ctx, lat)


def _is_ctx_rows(tm, n_lat):
    t = pl.program_id(1)
    rows = t * tm + lax.broadcasted_iota(jnp.int32, (tm, 1), 0)
    return rows >= n_lat


def _mod_kernel(c_ref, w_ref, b_ref, o_ref):
    c = c_ref[...]
    s = (c * jax.nn.sigmoid(c)).astype(BF16)
    o_ref[0] = jnp.dot(s, w_ref[0].astype(BF16), preferred_element_type=F32) + b_ref[0]


def _modulation(cvec, w_mod, b_mod):
    depth, d, n = w_mod.shape
    tn = 1024
    return pl.pallas_call(
        _mod_kernel,
        grid=(depth, n // tn),
        in_specs=[
            pl.BlockSpec((MOD_ROWS, d), lambda l, j: (0, 0)),
            pl.BlockSpec((1, d, tn), lambda l, j: (l, 0, j)),
            pl.BlockSpec((1, 1, tn), lambda l, j: (l, 0, j)),
        ],
        out_specs=pl.BlockSpec((1, MOD_ROWS, tn), lambda l, j: (l, 0, j)),
        out_shape=jax.ShapeDtypeStruct((depth, MOD_ROWS, n), F32),
        compiler_params=_cparams(("arbitrary", "arbitrary")),
        name="modulation",
    )(cvec, w_mod, b_mod.reshape(depth, 1, n))


def _rope_lanes(x, c, s_fwd, s_bwd, half):
    n = x.shape[-1]
    return x * c + pltpu.roll(x, n - half, axis=1) * s_fwd + pltpu.roll(x, half, axis=1) * s_bwd


def _mla_prep(pm, gq_ref, gkv_ref, wuq_ref, wukv_ref, tq_ref, tk_ref, q_ref, k_ref, v_ref):
    pq = pm[:, :MLA_Q_RANK]
    ckv = pm[:, MLA_Q_RANK:MLA_Q_RANK + MLA_KV_RANK]
    kr = pm[:, MLA_Q_RANK + MLA_KV_RANK:]

    def rms(x, g):
        return x * lax.rsqrt(jnp.mean(x * x, axis=-1, keepdims=True) + NORM_EPS) * g

    q = jnp.dot(rms(pq, gq_ref[...]).astype(BF16), wuq_ref[...], preferred_element_type=F32)
    kv = jnp.dot(rms(ckv, gkv_ref[...]).astype(BF16), wukv_ref[...], preferred_element_type=F32)
    half = MLA_ROPE // 2
    kr = _rope_lanes(kr, tk_ref[0], tk_ref[1], tk_ref[2], half)
    cq, sq1, sq2 = tq_ref[0], tq_ref[1], tq_ref[2]
    for hd in range(MLA_HEADS):
        sl = slice(hd * LANE, (hd + 1) * LANE)
        q_ref[0, :, sl] = _rope_lanes(q[:, sl], cq, sq1, sq2, half).astype(q_ref.dtype)
        k_ref[0, :, sl] = (kv[:, sl] + kr).astype(k_ref.dtype)
    ones = jnp.ones((VT_ROWS - LANE, pm.shape[0]), v_ref.dtype)
    for j in range(MLA_HEADS // 2):
        c0 = MLA_HEADS * LANE + j * LANE
        v_ref[0, 0, j * VT_ROWS:j * VT_ROWS + LANE, :] = kv[:, c0:c0 + LANE].T.astype(v_ref.dtype)
        v_ref[0, 0, j * VT_ROWS + LANE:(j + 1) * VT_ROWS, :] = ones


def _gqa_prep(pg, gq_ref, gk_ref, bd_ref, tq_ref, tk_ref, q_ref, k_ref, v_ref):
    hd = GQA_HEAD_DIM
    q = pg[:, :GQA_OUT]
    k = pg[:, GQA_OUT:GQA_OUT + LANE]
    v = pg[:, GQA_OUT + LANE:]
    half = hd // 2

    def head_rms(x, g):
        w = x.shape[-1]
        ms = _split_dot(x * x, bd_ref[:w, :w], 2) * (1.0 / hd)
        return x * lax.rsqrt(ms + NORM_EPS) * g

    qn = head_rms(q, gq_ref[...])
    for j in range(GQA_OUT // LANE):
        sl = slice(j * LANE, (j + 1) * LANE)
        q_ref[0, :, sl] = _rope_lanes(qn[:, sl], tq_ref[0], tq_ref[1], tq_ref[2], half).astype(q_ref.dtype)
    kn = head_rms(k, gk_ref[...])
    kn = _rope_lanes(kn, tk_ref[0], tk_ref[1], tk_ref[2], half)
    lane = lax.broadcasted_iota(jnp.int32, kn.shape, 1)
    low = lane < hd
    kn_sw = pltpu.roll(kn, hd, axis=1)
    zero = jnp.zeros_like(kn)
    k_ref[0, :, 0 * LANE:1 * LANE] = jnp.where(low, kn, zero).astype(k_ref.dtype)
    k_ref[0, :, 1 * LANE:2 * LANE] = jnp.where(low, zero, kn_sw).astype(k_ref.dtype)
    k_ref[0, :, 2 * LANE:3 * LANE] = jnp.where(low, kn_sw, zero).astype(k_ref.dtype)
    k_ref[0, :, 3 * LANE:4 * LANE] = jnp.where(low, zero, kn).astype(k_ref.dtype)
    v_t = v.T.astype(v_ref.dtype)
    ones = jnp.ones((VT_ROWS - LANE, v.shape[0]), v_ref.dtype)
    for g in range(GQA_KV_HEADS):
        vg = v_t[g * hd:(g + 1) * hd]
        v_ref[0, 0, g * VT_ROWS:g * VT_ROWS + hd, :] = vg
        v_ref[0, 0, g * VT_ROWS + hd:g * VT_ROWS + LANE, :] = vg
        v_ref[0, 0, g * VT_ROWS + LANE:(g + 1) * VT_ROWS, :] = ones


def _split_row_specs(tm, width, n_lat, ctx_row0):
    lat_tiles = n_lat // tm
    assert ctx_row0 % tm == 0
    ctx_block0 = ctx_row0 // tm
    return [pl.BlockSpec((1, tm, width), lambda b, t: (b, jnp.minimum(t, lat_tiles - 1), 0)),
            pl.BlockSpec((1, tm, width), lambda b, t: (b, ctx_block0 + jnp.maximum(t - lat_tiles, 0), 0))]


def _pick_rows(lat_ref, ctx_ref, tm, n_lat):
    return jnp.where(pl.program_id(1) * tm >= n_lat, ctx_ref[0], lat_ref[0])


def _inproj_kernel(xl_ref, xc_ref, mod_ref, w_ref, bg_ref,
                   gqm_ref, gkvm_ref, wuq_ref, wukv_ref, tmq_ref, tmk_ref,
                   gqg_ref, gkg_ref, bd_ref, tgq_ref, tgk_ref,
                   z_ref, xbc_ref, dt_ref, gate_ref, qm_ref, km_ref, vm_ref, qg_ref, kg_ref, vg_ref,
                   *, tm, n_lat):
    is_ctx = _is_ctx_rows(tm, n_lat)
    d = D_MODEL
    sh = _mod_rows(mod_ref, 0, d, is_ctx)
    sc = _mod_rows(mod_ref, d, d, is_ctx)
    h = (_layer_norm(_pick_rows(xl_ref, xc_ref, tm, n_lat)) * (1.0 + sc) + sh).astype(BF16)

    def proj(c0, width):
        return jnp.dot(h, w_ref[:, c0:c0 + width], preferred_element_type=F32)

    _mla_prep(proj(C_PM, W_PM), gqm_ref, gkvm_ref, wuq_ref, wukv_ref, tmq_ref, tmk_ref, qm_ref, km_ref, vm_ref)
    _gqa_prep(proj(C_PG, GQA_IN), gqg_ref, gkg_ref, bd_ref, tgq_ref, tgk_ref, qg_ref, kg_ref, vg_ref)
    z_ref[0] = proj(C_Z, SSD_INNER)
    xbc_ref[0] = proj(C_XBC, SSD_CONV_CH)
    dt_ref[0] = proj(C_DT, LANE)
    for j in range(3):
        c0 = j * D_MODEL
        logits = proj(C_GATE + c0, D_MODEL) + bg_ref[:, c0:c0 + D_MODEL]
        gate_ref[0, :, c0:c0 + D_MODEL] = jax.nn.sigmoid(logits).astype(gate_ref.dtype)


def _in_projection(x_lat, x_ctx, ctx_row0, t_all, mod, w_inp, b_gate, mla_consts, gqa_consts, *, n_lat, tm):
    bsz, _, d = x_lat.shape
    assert tm == VT_TILE
    row = lambda w: pl.BlockSpec((1, tm, w), lambda b, t: (b, t, 0))
    tab = pl.BlockSpec((3, tm, LANE), lambda b, t: (0, t, 0))
    vt_spec = lambda rows: pl.BlockSpec((1, 1, rows, tm), lambda b, t: (b, t, 0, 0))
    g_qm, g_kvm, wuq_p, wukv_p, tab_mq, tab_mk = mla_consts
    g_qg, g_kg, bd, tab_gq, tab_gk = gqa_consts
    hw = MLA_HEADS * LANE
    kw = GQA_KV_HEADS * 2 * LANE
    vm_rows = (MLA_HEADS // 2) * VT_ROWS
    vg_rows = GQA_KV_HEADS * VT_ROWS
    seq = lambda w, dt: jax.ShapeDtypeStruct((bsz, t_all, w), dt)
    vts = lambda rows: jax.ShapeDtypeStruct((bsz, t_all // tm, rows, tm), BF16)
    consts = (mod, w_inp, b_gate, g_qm, g_kvm, wuq_p, wukv_p)
    return pl.pallas_call(
        functools.partial(_inproj_kernel, tm=tm, n_lat=n_lat),
        grid=(bsz, t_all // tm),
        in_specs=_split_row_specs(tm, d, n_lat, ctx_row0) + [_const_spec(a.shape) for a in consts] + [tab, tab]
                 + [_const_spec(a.shape) for a in (g_qg, g_kg, bd)] + [tab, tab],
        out_specs=[row(SSD_INNER), row(SSD_CONV_CH), row(LANE), row(GATE_IN),
                   row(hw), row(hw), vt_spec(vm_rows), row(GQA_OUT), row(kw), vt_spec(vg_rows)],
        out_shape=[seq(SSD_INNER, F32), seq(SSD_CONV_CH, F32), seq(LANE, F32), seq(GATE_IN, BF16),
                   seq(hw, BF16), seq(hw, BF16), vts(vm_rows), seq(GQA_OUT, BF16), seq(kw, BF16), vts(vg_rows)],
        compiler_params=_cparams(("parallel", "parallel")),
        name="in_projection",
    )(x_lat, x_ctx, *consts, tab_mq, tab_mk, g_qg, g_kg, bd, tab_gq, tab_gk)


def _attn_kernel(q_ref, k_ref, vt_ref, o_ref, m_ref, acc_ref, a0_ref, a1_ref, c0_ref, c1_ref,
                 p0_ref, p1_ref, s0_ref, s1_ref, *, pairs, tq, tk, n_lat, ctx_len, ctx_queries):
    cb = ATTN_COL_BLOCK
    vr = VT_ROWS
    s_bufs, p_bufs, a_bufs, c_bufs = (s0_ref, s1_ref), (p0_ref, p1_ref), (a0_ref, a1_ref), (c0_ref, c1_ref)

    def scores(r0, size, par):
        for pi, (_, units) in enumerate(pairs):
            for ui, (qb, kb) in enumerate(units):
                s = lax.dot_general(
                    k_ref[0, pl.ds(r0, size), kb * LANE:(kb + 1) * LANE],
                    q_ref[0, :, qb * LANE:(qb + 1) * LANE],
                    (((1,), (1,)), ((), ())), preferred_element_type=F32)
                s_bufs[par][pi, :size, ui * tq:(ui + 1) * tq] = s
                c_bufs[par][pi, :, ui * tq:(ui + 1) * tq] = jnp.max(s, axis=0, keepdims=True)

    def softmax(size, par):
        for pi in range(len(pairs)):
            for c in range(0, 2 * tq, cb):
                m_prev = m_ref[pi, :, c:c + cb]
                m_next = jnp.maximum(m_prev, c_bufs[par][pi, :, c:c + cb])
                for r in range(0, size, ATTN_KEY_BLOCK):
                    blk = s_bufs[par][pi, r:r + ATTN_KEY_BLOCK, c:c + cb]
                    p_bufs[par][pi, r:r + ATTN_KEY_BLOCK, c:c + cb] = jnp.exp2(blk - m_next).astype(BF16)
                a_bufs[par][pi, :, c:c + cb] = jnp.exp2(m_prev - m_next)
                m_ref[pi, :, c:c + cb] = m_next

    def values(u0, size, par):
        for pi, (vb, _) in enumerate(pairs):
            vt = jnp.concatenate([vt_ref[0, u0 + i, vb * vr:(vb + 1) * vr, :]
                                  for i in range(size // VT_TILE)], axis=1)
            pv = jnp.dot(vt, p_bufs[par][pi, :size, :], preferred_element_type=F32)
            acc_ref[pi] = a_bufs[par][pi] * acc_ref[pi] + pv

    m_ref[...] = jnp.full(m_ref.shape, -jnp.inf, F32)
    acc_ref[...] = jnp.zeros(acc_ref.shape, F32)
    tku = tk // VT_TILE
    ctx_u = n_lat // VT_TILE
    scores(n_lat, ctx_len, 0)
    if ctx_queries:
        softmax(ctx_len, 0)
        values(ctx_u, ctx_len, 0)
    else:
        n = n_lat // tk
        scores(0, tk, 1)
        softmax(ctx_len, 0)
        scores(tk, tk, 0)
        softmax(tk, 1)
        values(ctx_u, ctx_len, 0)

        def body(jj, carry):
            r = pl.multiple_of(jj * (2 * tk), 2 * tk)
            u = jj * (2 * tku)
            scores(r + 2 * tk, tk, 1)
            softmax(tk, 0)
            values(u, tk, 1)
            scores(r + 3 * tk, tk, 0)
            softmax(tk, 1)
            values(u + tku, tk, 0)
            return carry

        lax.fori_loop(0, (n - 2) // 2, body, 0)
        softmax(tk, 0)
        values((n - 2) * tku, tk, 1)
        values((n - 1) * tku, tk, 0)
    row = lax.broadcasted_iota(jnp.int32, (LANE, tq), 0)
    for pi in range(len(pairs)):
        acc = acc_ref[pi]
        o_t = acc[:LANE] / acc[LANE:LANE + 1]
        o_t = jnp.where(row < LANE // 2, o_t[:, :tq], o_t[:, tq:])
        o_ref[0, :, pi * LANE:(pi + 1) * LANE] = o_t.T.astype(o_ref.dtype)


def _attention(q, k, vt, *, groups, q_w, k_w, vt_rows, pairs, n_lat, ctx_queries, tq, tk):
    bsz, t_all, _ = k.shape
    ctx_len = t_all - n_lat
    out_w = len(pairs) * LANE
    rows = ctx_len if ctx_queries else n_lat
    q_tile0 = n_lat // tq if ctx_queries else 0
    assert rows % tq == 0 and n_lat % tq == 0 and n_lat % (2 * tk) == 0 and ctx_len <= tk
    assert (2 * tq) % ATTN_COL_BLOCK == 0 and tk % VT_TILE == 0 and ctx_len % VT_TILE == 0
    state = pltpu.VMEM((len(pairs), 1, 2 * tq), F32)
    acc = pltpu.VMEM((len(pairs), VT_ROWS, 2 * tq), F32)
    pbuf = pltpu.VMEM((len(pairs), tk, 2 * tq), BF16)
    sbuf = pltpu.VMEM((len(pairs), tk, 2 * tq), F32)
    return pl.pallas_call(
        functools.partial(_attn_kernel, pairs=pairs, tq=tq, tk=tk, n_lat=n_lat, ctx_len=ctx_len,
                          ctx_queries=ctx_queries),
        grid=(bsz, groups, rows // tq),
        in_specs=[
            pl.BlockSpec((1, tq, q_w), lambda b, g, t: (b, t + q_tile0, g)),
            pl.BlockSpec((1, t_all, k_w), lambda b, g, t: (b, 0, g), pipeline_mode=pl.Buffered(1)),
            pl.BlockSpec((1, t_all // VT_TILE, vt_rows, VT_TILE), lambda b, g, t: (b, 0, g, 0),
                         pipeline_mode=pl.Buffered(1)),
        ],
        out_specs=pl.BlockSpec((1, tq, out_w), lambda b, g, t: (b, t, g)),
        out_shape=jax.ShapeDtypeStruct((bsz, rows, groups * out_w), BF16),
        scratch_shapes=[state, acc, state, state, state, state, pbuf, pbuf, sbuf, sbuf],
        compiler_params=_cparams(("parallel", "parallel", "arbitrary")),
        name="attention_ctx" if ctx_queries else "attention",
    )(q, k, vt)


def _ssd_chunk_index(i, n_chunks, ctx_chunks, reverse):
    if reverse:
        return n_chunks - 1 - i
    lat_chunks = n_chunks - ctx_chunks
    return jnp.where(i < ctx_chunks, lat_chunks + i, i - ctx_chunks)


def _ssd_kernel(*refs, reverse, n_chunks, ctx_chunks):
    if reverse:
        (u_ref, dt_ref, dtb_ref, a_ref, tri_ref, ehd_ref,
         yf_ref, z_ref, dsk_ref, g_ref, y_ref, h_ref) = refs
    else:
        (xbc_ref, prev_ref, next_ref, dt_ref, cw_ref, cb_ref, dtb_ref, a_ref, tri_ref, ehd_ref,
         y_ref, uo_ref, h_ref) = refs
    q = SSD_CHUNK
    rows = SSD_BLOCK
    i = pl.program_id(1)
    ci = _ssd_chunk_index(i, n_chunks, ctx_chunks, reverse)

    @pl.when(i == 0)
    def _():
        h_ref[...] = jnp.zeros_like(h_ref)

    if reverse:
        u_blk = u_ref[0]
    else:
        x = xbc_ref[0]
        lat_chunks = n_chunks - ctx_chunks
        at_start = jnp.logical_or(ci == 0, ci == lat_chunks)
        at_end = jnp.logical_or(ci == lat_chunks - 1, ci == n_chunks - 1)
        prev_row = jnp.where(at_start, 0.0, prev_ref[0, 7:8, :])
        next_row = jnp.where(at_end, 0.0, next_ref[0, 0:1, :])
        row = lax.broadcasted_iota(jnp.int32, (rows, 1), 0)
        x_prev = jnp.where(row == 0, prev_row, pltpu.roll(x, 1, axis=0))
        x_next = jnp.where(row == rows - 1, next_row, pltpu.roll(x, rows - 1, axis=0))
        u_blk = cw_ref[0:1, :] * x_prev + cw_ref[1:2, :] * x + cw_ref[2:3, :] * x_next + cb_ref[...]
        u_blk = u_blk * jax.nn.sigmoid(u_blk)
        uo_ref[0] = u_blk

    subs = range(rows // q)
    for sub in (reversed(subs) if reverse else subs):
        sl = slice(sub * q, (sub + 1) * q)
        y = _ssd_chunk(u_blk[sl], dt_ref[0, sl, :], dtb_ref, a_ref, tri_ref, ehd_ref, h_ref, reverse)
        if reverse:
            yt = yf_ref[0, sl, :] + y + dsk_ref[...] * u_blk[sl, :SSD_INNER]
            zz = z_ref[0, sl, :]
            yt = yt * (zz * jax.nn.sigmoid(zz))
            yt = yt * lax.rsqrt(jnp.mean(yt * yt, axis=-1, keepdims=True) + NORM_EPS) * g_ref[...]
            y_ref[0, sl, :] = yt.astype(y_ref.dtype)
        else:
            y_ref[0, sl, :] = y


def _ssd_chunk(u, dt_raw, dtb_ref, a_ref, tri_ref, ehd_ref, h_ref, reverse):
    q = SSD_CHUNK
    xs = u[:, :SSD_INNER]
    gs = SSD_STATE
    bm = [u[:, SSD_INNER + g * gs:SSD_INNER + (g + 1) * gs] for g in range(SSD_GROUPS)]
    cm = [u[:, SSD_INNER + (SSD_GROUPS + g) * gs:SSD_INNER + (SSD_GROUPS + g + 1) * gs]
          for g in range(SSD_GROUPS)]

    dt = jax.nn.softplus(dt_raw + dtb_ref[...])
    da = dt * a_ref[...]
    a_cum = _tri_cumsum(tri_ref, da)
    end = 0 if reverse else q - 1
    a_tot = a_cum[end:end + 1, :]
    e_cum = jnp.exp(a_cum)
    dte = jnp.exp(a_tot - a_cum)
    dt_x = _split_dot(dt, ehd_ref[...], 2)
    ec_x = _split_dot(e_cum, ehd_ref[...], 2)
    dte_x = _split_dot(dte, ehd_ref[...], 2)
    xdt = xs * dt_x
    xdt_b = xdt.astype(BF16)
    xst_b = (xdt * dte_x).astype(BF16)
    a_cum_t = a_cum.T

    kk = lax.broadcasted_iota(jnp.int32, (q, q), 1)
    qq = lax.broadcasted_iota(jnp.int32, (q, q), 0)
    in_order = (kk >= qq) if reverse else (kk <= qq)
    lane = lax.broadcasted_iota(jnp.int32, (q, LANE), 1)
    off = SSD_HEADS if reverse else 0
    hpg = SSD_HEADS // SSD_GROUPS
    gw = hpg * SSD_HEAD_DIM
    y_parts = []
    for g in range(SSD_GROUPS):
        bm_b = bm[g].astype(BF16)
        cm_b = cm[g].astype(BF16)
        scores = lax.dot_general(cm_b, bm_b, (((1,), (1,)), ((), ())), preferred_element_type=F32)
        h_prev = h_ref[g]
        y_off = jnp.dot(cm_b, h_prev.astype(BF16), preferred_element_type=F32) * ec_x[:, g * gw:(g + 1) * gw]
        pair_out = []
        for j in range(hpg // 2):
            xp = xdt_b[:, g * gw + j * LANE:g * gw + (j + 1) * LANE]
            res = []
            for e in range(2):
                hd = g * hpg + 2 * j + e
                a_q = a_cum[:, off + hd:off + hd + 1]
                a_k = a_cum_t[off + hd:off + hd + 1, :]
                decay = jnp.exp(jnp.where(in_order, a_q - a_k, -jnp.inf))
                pmat = (scores * decay).astype(BF16)
                res.append(jnp.dot(pmat, xp, preferred_element_type=F32))
            pair_out.append(jnp.where(lane < SSD_HEAD_DIM, res[0], res[1]))
        y_parts.append(jnp.concatenate(pair_out, axis=1) + y_off)
        st = jnp.dot(bm[g].T.astype(BF16), xst_b[:, g * gw:(g + 1) * gw], preferred_element_type=F32)
        h_ref[g] = h_prev * ec_x[end:end + 1, g * gw:(g + 1) * gw] + st
    return jnp.concatenate(y_parts, axis=1)


def _tri_cumsum(tri_ref, da):
    acc = None
    rem = da
    for _ in range(3):
        piece = rem.astype(BF16)
        term = jnp.dot(tri_ref[...], piece, preferred_element_type=F32)
        acc = term if acc is None else acc + term
        rem = rem - piece.astype(F32)
    return acc


def _ssd_direction(xbc, dt, conv, dt_bias, a_row, tri, ehd, extra, *, ctx_len, reverse):
    bsz, t_all, cch = xbc.shape
    q = SSD_BLOCK
    assert t_all % q == 0 and ctx_len % q == 0
    n_chunks = t_all // q
    ctx_chunks = ctx_len // q
    halo = 8
    hb = q // halo
    n_halo = t_all // halo
    cidx = lambda i: _ssd_chunk_index(i, n_chunks, ctx_chunks, reverse)
    chunk = lambda w: pl.BlockSpec((1, q, w), lambda b, i: (b, cidx(i), 0))
    tail = [_const_spec(dt_bias.shape), _const_spec(a_row.shape), _const_spec(tri.shape), _const_spec(ehd.shape)]
    if reverse:
        y_f, z, d_x, g_ssd = extra
        in_specs = [chunk(cch), chunk(LANE)] + tail + [
            chunk(SSD_INNER), chunk(SSD_INNER), _const_spec(d_x.shape), _const_spec(g_ssd.shape)]
        args = [xbc, dt, dt_bias, a_row, tri, ehd, y_f, z, d_x, g_ssd]
        out_specs = chunk(SSD_INNER)
        out_shape = jax.ShapeDtypeStruct((bsz, t_all, SSD_INNER), BF16)
    else:
        conv_w, conv_b = conv
        in_specs = [
            chunk(cch),
            pl.BlockSpec((1, halo, cch), lambda b, i: (b, jnp.maximum(cidx(i) * hb - 1, 0), 0)),
            pl.BlockSpec((1, halo, cch), lambda b, i: (b, jnp.minimum((cidx(i) + 1) * hb, n_halo - 1), 0)),
            chunk(LANE), _const_spec(conv_w.shape), _const_spec(conv_b.shape)] + tail
        args = [xbc, xbc, xbc, dt, conv_w, conv_b, dt_bias, a_row, tri, ehd]
        out_specs = [chunk(SSD_INNER), chunk(cch)]
        out_shape = [jax.ShapeDtypeStruct((bsz, t_all, SSD_INNER), F32),
                     jax.ShapeDtypeStruct((bsz, t_all, cch), F32)]
    hpg = SSD_HEADS // SSD_GROUPS
    return pl.pallas_call(
        functools.partial(_ssd_kernel, reverse=reverse, n_chunks=n_chunks, ctx_chunks=ctx_chunks),
        grid=(bsz, n_chunks),
        in_specs=in_specs,
        out_specs=out_specs,
        out_shape=out_shape,
        scratch_shapes=[pltpu.VMEM((SSD_GROUPS, SSD_STATE, hpg * SSD_HEAD_DIM), F32)],
        compiler_params=_cparams(("parallel", "arbitrary")),
        name="ssd_bwd" if reverse else "ssd_fwd",
    )(*args)


def _post_kernel(xl_ref, xc_ref, yml_ref, ymc_ref, ys_ref, ygl_ref, ygc_ref, gate_ref, mod_ref,
                 wo_ref, wi_ref, wf_ref, l1g_ref, l1b_ref, l2g_ref, l2b_ref, o_ref, act_ref, *, tm, n_lat, th):
    is_ctx = _is_ctx_rows(tm, n_lat)
    d = D_MODEL
    mix = None
    row0 = 0
    branches = (_pick_rows(yml_ref, ymc_ref, tm, n_lat), ys_ref[0], _pick_rows(ygl_ref, ygc_ref, tm, n_lat))
    for j, y in enumerate(branches):
        kw = y.shape[-1]
        br = jnp.dot(y, wo_ref[row0:row0 + kw, :], preferred_element_type=F32)
        br = gate_ref[0, :, j * d:(j + 1) * d].astype(F32) * br
        mix = br if mix is None else mix + br
        row0 += kw
    g1 = _mod_rows(mod_ref, 2 * d, d, is_ctx)
    x_in = _pick_rows(xl_ref, xc_ref, tm, n_lat)
    x1 = _layer_norm(DEEPNORM_ALPHA * x_in + g1 * mix) * l1g_ref[...] + l1b_ref[...]

    sh2 = _mod_rows(mod_ref, 3 * d, d, is_ctx)
    sc2 = _mod_rows(mod_ref, 4 * d, d, is_ctx)
    g2 = _mod_rows(mod_ref, 5 * d, d, is_ctx)
    h2 = (_layer_norm(x1) * (1.0 + sc2) + sh2).astype(BF16)
    hid = FFN_HIDDEN
    for c0 in range(0, hid, th):
        gt = jnp.dot(h2, wi_ref[:, c0:c0 + th], preferred_element_type=F32)
        up = jnp.dot(h2, wi_ref[:, hid + c0:hid + c0 + th], preferred_element_type=F32)
        act_ref[:, c0:c0 + th] = (gt * jax.nn.sigmoid(gt) * up).astype(BF16)
    fx = jnp.dot(act_ref[...], wf_ref[...], preferred_element_type=F32)
    o_ref[0] = _layer_norm(DEEPNORM_ALPHA * x1 + g2 * fx) * l2g_ref[...] + l2b_ref[...]


def _post_mixer(x_pair, y_mla_pair, y_ssd, y_gqa_pair, gate, mod, w_out, w_ffn_in, w_ffn_out,
                l1g, l1b, l2g, l2b, *, n_lat, tm, with_ctx):
    bsz, t_all, _ = y_ssd.shape
    d = D_MODEL
    row = lambda w: pl.BlockSpec((1, tm, w), lambda b, t: (b, t, 0))
    vec = _const_spec((1, d))
    t_out = t_all if with_ctx else n_lat
    split = lambda w, triple: _split_row_specs(tm, w, n_lat, triple[2] if with_ctx else 0)
    return pl.pallas_call(
        functools.partial(_post_kernel, tm=tm, n_lat=n_lat, th=256),
        grid=(bsz, t_out // tm),
        in_specs=split(d, x_pair) + split(MLA_OUT, y_mla_pair) + [row(SSD_INNER)] + split(GQA_OUT, y_gqa_pair)
                 + [row(GATE_IN), _const_spec(mod.shape),
                    _const_spec(w_out.shape), _const_spec(w_ffn_in.shape), _const_spec(w_ffn_out.shape),
                    vec, vec, vec, vec],
        out_specs=pl.BlockSpec((1, tm, d), lambda b, t: (b, t, 0)),
        out_shape=jax.ShapeDtypeStruct((bsz, t_out, d), F32),
        scratch_shapes=[pltpu.VMEM((tm, FFN_HIDDEN), BF16)],
        compiler_params=_cparams(("parallel", "parallel")),
        name="post_mixer",
    )(x_pair[0], x_pair[1], y_mla_pair[0], y_mla_pair[1], y_ssd, y_gqa_pair[0], y_gqa_pair[1], gate, mod,
      w_out, w_ffn_in, w_ffn_out, l1g, l1b, l2g, l2b)


def _rope_tables(n_lat, ctx_len, rot_dim, lane_dim, lane_off, scale):
    rows = n_lat // GRID_W
    row = np.repeat(np.arange(rows), GRID_W)
    col = np.tile(np.arange(GRID_W), rows)
    n_freq = rot_dim // 4
    inv_freq = ROPE_THETA ** (-np.arange(n_freq, dtype=np.float64) / n_freq)
    ang = np.concatenate([row[:, None] * inv_freq, col[:, None] * inv_freq], axis=-1)
    cos, sin = np.cos(ang), np.sin(ang)
    half = rot_dim // 2
    lane = np.arange(LANE)
    rel = (lane % lane_dim) - lane_off
    is_rot = (rel >= 0) & (rel < rot_dim)
    idx = np.where(is_rot, rel % half, 0)
    first = is_rot & (rel < half)
    second = is_rot & (rel >= half)
    c = np.where(is_rot[None, :], cos[:, idx], 1.0)
    s1 = np.where(first[None, :], -sin[:, idx], 0.0)
    s2 = np.where(second[None, :], sin[:, idx], 0.0)
    lat = np.stack([c, s1, s2])
    ctx = np.stack([np.ones((ctx_len, LANE)), np.zeros((ctx_len, LANE)), np.zeros((ctx_len, LANE))])
    return jnp.asarray((np.concatenate([lat, ctx], axis=1) * scale).astype(np.float32))


def _pad_cols(w, width, offset=0):
    return jnp.pad(w, ((0, 0), (offset, width - offset - w.shape[1])))


def _layout_w_in(w):
    c = 0
    q_lat = w[:, c:c + MLA_Q_RANK]; c += MLA_Q_RANK
    kv_lat = w[:, c:c + MLA_KV_RANK]; c += MLA_KV_RANK
    k_rope = _pad_cols(w[:, c:c + MLA_ROPE], LANE, MLA_NOPE); c += MLA_ROPE
    z = w[:, c:c + SSD_INNER]; c += SSD_INNER
    xbc = w[:, c:c + SSD_CONV_CH]; c += SSD_CONV_CH
    dt = _pad_cols(w[:, c:c + 2 * SSD_HEADS], LANE); c += 2 * SSD_HEADS
    rest = w[:, c:]
    return jnp.concatenate([q_lat, kv_lat, k_rope, z, xbc, dt, rest], axis=1).astype(BF16)


def _layout_w_uq(w):
    w = w.reshape(w.shape[0], MLA_HEADS, MLA_QK)
    w = jnp.pad(w, ((0, 0), (0, 0), (0, LANE - MLA_QK)))
    return w.reshape(w.shape[0], MLA_HEADS * LANE).astype(BF16)


def _layout_w_ukv(w):
    w = w.reshape(w.shape[0], MLA_HEADS, MLA_NOPE + MLA_V)
    k = jnp.pad(w[:, :, :MLA_NOPE], ((0, 0), (0, 0), (0, LANE - MLA_NOPE)))
    v = w[:, :, MLA_NOPE:]
    return jnp.concatenate([k.reshape(w.shape[0], -1), v.reshape(w.shape[0], -1)], axis=1).astype(BF16)


def _lane_row(vec, width=LANE):
    return _pad_cols(vec.reshape(1, -1).astype(F32), width)


def kernel(x, c, ctx, c_ctx, w_mod, b_mod, w_in, b_gate, w_uq, g_q_mla, w_ukv, g_kv_mla, conv_w, conv_b,
           a_log, dt_bias, d_skip, g_ssd, g_q_gqa, g_k_gqa, w_out, ln1_g, ln1_b, w_ffn_in, w_ffn_out,
           ln2_g, ln2_b):
    bsz, n_lat, d = x.shape
    ctx_len = ctx.shape[1]
    depth = w_mod.shape[0]
    assert d == D_MODEL and depth == DEPTH and bsz <= MOD_ROWS // 2
    tm = ROW_TILE
    assert ctx_len % tm == 0 and n_lat % tm == 0

    cvec = jnp.zeros((MOD_ROWS, d), F32).at[:bsz].set(c).at[MOD_ROWS // 2].set(c_ctx)
    mod_all = _modulation(cvec, w_mod, b_mod)

    mla_scale = MLA_QK ** -0.5 * LOG2E
    gqa_scale = GQA_HEAD_DIM ** -0.5 * LOG2E
    tab_mq = _rope_tables(n_lat, ctx_len, MLA_ROPE, LANE, MLA_NOPE, mla_scale)
    tab_mk = _rope_tables(n_lat, ctx_len, MLA_ROPE, LANE, MLA_NOPE, 1.0)
    tab_gq = _rope_tables(n_lat, ctx_len, GQA_HEAD_DIM, GQA_HEAD_DIM, 0, gqa_scale)
    tab_gk = _rope_tables(n_lat, ctx_len, GQA_HEAD_DIM, GQA_HEAD_DIM, 0, 1.0)

    qn = SSD_CHUNK
    r = np.arange(qn)
    tri_f = jnp.asarray((r[None, :] <= r[:, None]).astype(np.float32), BF16)
    tri_b = jnp.asarray((r[None, :] >= r[:, None]).astype(np.float32), BF16)
    lane1k = np.arange(SSD_INNER) // SSD_HEAD_DIM
    ehd_f = jnp.asarray((np.arange(LANE)[:, None] == lane1k[None, :]).astype(np.float32), BF16)
    ehd_b = jnp.asarray((np.arange(LANE)[:, None] == (lane1k[None, :] + SSD_HEADS)).astype(np.float32), BF16)
    blk = np.arange(GQA_OUT) // GQA_HEAD_DIM
    bd = jnp.asarray((blk[:, None] == blk[None, :]).astype(np.float32), BF16)

    mla_attn = functools.partial(_attention, groups=1, q_w=8 * LANE, k_w=8 * LANE, vt_rows=4 * VT_ROWS,
                                 pairs=tuple((j, ((2 * j, 2 * j), (2 * j + 1, 2 * j + 1))) for j in range(4)),
                                 n_lat=n_lat)
    gqa_attn = functools.partial(_attention, groups=1, q_w=4 * LANE, k_w=4 * LANE, vt_rows=2 * VT_ROWS,
                                 pairs=((0, ((0, 0), (0, 1))), (0, ((1, 0), (1, 1))),
                                        (1, ((2, 2), (2, 3))), (1, ((3, 2), (3, 3)))), n_lat=n_lat)

    def attend(fn, q, k, v, with_ctx):
        y = fn(q, k, v, ctx_queries=False, tq=ATTN_TQ, tk=ATTN_TK)
        y_c = fn(q, k, v, ctx_queries=True, tq=ctx_len, tk=ATTN_TK) if with_ctx else y
        return (y, y_c, 0)

    t_all = n_lat + ctx_len
    x_pair = (x, ctx, 0)
    for i in range(depth):
        with_ctx = i < depth - 1
        mod = mod_all[i]
        w_inp = _layout_w_in(w_in[i])
        mla_consts = (g_q_mla[i].reshape(1, -1), g_kv_mla[i].reshape(1, -1),
                      _layout_w_uq(w_uq[i]), _layout_w_ukv(w_ukv[i]), tab_mq, tab_mk)
        gqa_consts = (jnp.tile(g_q_gqa[i], GQA_HEADS).reshape(1, -1),
                      jnp.tile(g_k_gqa[i], GQA_KV_HEADS).reshape(1, -1), bd, tab_gq, tab_gk)
        z, xbc, dtr, gate, q_m, k_m, v_m, q_g, k_g, v_g = _in_projection(
            *x_pair, t_all, mod, w_inp, b_gate[i].reshape(1, -1), mla_consts, gqa_consts, n_lat=n_lat, tm=tm)
        y_mla = attend(mla_attn, q_m, k_m, v_m, with_ctx)
        y_gqa = attend(gqa_attn, q_g, k_g, v_g, with_ctx)
        a = -jnp.exp(a_log[i].astype(F32))
        a_row = _lane_row(a.reshape(-1))
        dtb_row = _lane_row(dt_bias[i].reshape(-1))
        y_f, u_act = _ssd_direction(xbc, dtr, (conv_w[i], conv_b[i].reshape(1, -1)), dtb_row, a_row,
                                    tri_f, ehd_f, None, ctx_len=ctx_len, reverse=False)
        d_x = jnp.repeat(d_skip[i], SSD_HEAD_DIM).reshape(1, -1)
        y_ssd = _ssd_direction(u_act, dtr, None, dtb_row, a_row, tri_b, ehd_b,
                               (y_f, z, d_x, g_ssd[i].reshape(1, -1)), ctx_len=ctx_len, reverse=True)
        tm_post = tm if with_ctx else LATENT_ROW_TILE
        xa = _post_mixer(x_pair, y_mla, y_ssd, y_gqa, gate, mod, w_out[i].astype(BF16),
                         w_ffn_in[i].astype(BF16), w_ffn_out[i].astype(BF16),
                         ln1_g[i].reshape(1, -1), ln1_b[i].reshape(1, -1),
                         ln2_g[i].reshape(1, -1), ln2_b[i].reshape(1, -1), n_lat=n_lat,
                         tm=tm_post, with_ctx=with_ctx)
        x_pair = (xa, xa, n_lat)
    return xa
```

```python
import functools

import jax
import jax.numpy as jnp
import numpy as np
from jax import lax
from jax.experimental import pallas as pl
from jax.experimental.pallas import tpu as pltpu

F32 = jnp.float32
BF16 = jnp.bfloat16

D_MODEL = 1024
DEPTH = 2
GRID_W = 64
ROPE_THETA = 10000.0
NORM_EPS = 1e-6

MLA_HEADS = 8
MLA_Q_RANK = 384
MLA_KV_RANK = 256
MLA_NOPE = 64
MLA_ROPE = 32
MLA_V = 64
MLA_QK = MLA_NOPE + MLA_ROPE

SSD_HEADS = 16
SSD_HEAD_DIM = 64
SSD_INNER = SSD_HEADS * SSD_HEAD_DIM
SSD_GROUPS = 2
SSD_STATE = 128
SSD_CHUNK = 128
SSD_CONV_CH = SSD_INNER + 2 * SSD_GROUPS * SSD_STATE

GQA_HEADS = 8
GQA_KV_HEADS = 2
GQA_HEAD_DIM = 64

GQA_IN = (GQA_HEADS + 2 * GQA_KV_HEADS) * GQA_HEAD_DIM
GATE_IN = 3 * D_MODEL
MLA_OUT = MLA_HEADS * MLA_V
GQA_OUT = GQA_HEADS * GQA_HEAD_DIM
FFN_HIDDEN = -(-8 * D_MODEL // (3 * 256)) * 256
DEEPNORM_ALPHA = (2 * DEPTH) ** 0.25

LANE = 128
MOD_ROWS = 8
LOG2E = 1.4426950408889634

C_PM = 0
W_PM = MLA_Q_RANK + MLA_KV_RANK + LANE
C_Z = C_PM + W_PM
C_XBC = C_Z + SSD_INNER
C_DT = C_XBC + SSD_CONV_CH
C_PG = C_DT + LANE
C_GATE = C_PG + GQA_IN
W_INP = C_GATE + GATE_IN

VMEM_LIMIT = 56 * 1024 * 1024
ROW_TILE = 256
LATENT_ROW_TILE = 512
ATTN_TQ = 512
ATTN_TK = 512
ATTN_COL_BLOCK = 256
ATTN_KEY_BLOCK = 64
SSD_BLOCK = 2 * SSD_CHUNK
SSD_BATCH = 2
VT_TILE = ROW_TILE
VT_ROWS = 144


def _cparams(sem):
    return pltpu.CompilerParams(dimension_semantics=sem, vmem_limit_bytes=VMEM_LIMIT)


def _const_spec(shape):
    nd = len(shape)
    return pl.BlockSpec(shape, lambda *_: (0,) * nd, pipeline_mode=pl.Buffered(1))


def _layer_norm(x):
    mu = jnp.mean(x, axis=-1, keepdims=True)
    xc = x - mu
    var = jnp.mean(xc * xc, axis=-1, keepdims=True)
    return xc * lax.rsqrt(var + NORM_EPS)


def _split_dot(x, m01, parts):
    pieces = []
    rem = x
    for _ in range(parts):
        piece = rem.astype(BF16)
        pieces.append(piece)
        rem = rem - piece.astype(F32)
    return jnp.dot(jnp.concatenate(pieces, axis=1), jnp.concatenate([m01] * parts, axis=0),
                   preferred_element_type=F32)


def _mod_rows(mod_ref, col, width, is_ctx):
    b = pl.program_id(0)
    lat = mod_ref[pl.ds(b, 1), col:col + width]
    ctx = mod_ref[MOD_ROWS // 2:MOD_ROWS // 2 + 1, col:col + width]
    return jnp.where(is_ctx, ctx, lat)


def _is_ctx_rows(tm, n_lat):
    t = pl.program_id(1)
    rows = t * tm + lax.broadcasted_iota(jnp.int32, (tm, 1), 0)
    return rows >= n_lat


def _mod_kernel(c_ref, w_ref, b_ref, o_ref):
    c = c_ref[...]
    s = (c * jax.nn.sigmoid(c)).astype(BF16)
    o_ref[0] = jnp.dot(s, w_ref[0].astype(BF16), preferred_element_type=F32) + b_ref[0]


def _modulation(cvec, w_mod, b_mod):
    depth, d, n = w_mod.shape
    tn = 1024
    return pl.pallas_call(
        _mod_kernel,
        grid=(depth, n // tn),
        in_specs=[
            pl.BlockSpec((MOD_ROWS, d), lambda l, j: (0, 0)),
            pl.BlockSpec((1, d, tn), lambda l, j: (l, 0, j)),
            pl.BlockSpec((1, 1, tn), lambda l, j: (l, 0, j)),
        ],
        out_specs=pl.BlockSpec((1, MOD_ROWS, tn), lambda l, j: (l, 0, j)),
        out_shape=jax.ShapeDtypeStruct((depth, MOD_ROWS, n), F32),
        compiler_params=_cparams(("arbitrary", "arbitrary")),
        name="modulation",
    )(cvec, w_mod, b_mod.reshape(depth, 1, n))


def _rope_lanes(x, c, s_fwd, s_bwd, half):
    n = x.shape[-1]
    return x * c + pltpu.roll(x, n - half, axis=1) * s_fwd + pltpu.roll(x, half, axis=1) * s_bwd


def _mla_prep(pm, gq_ref, gkv_ref, wuq_ref, wukv_ref, tq_ref, tk_ref, q_ref, k_ref, v_ref):
    pq = pm[:, :MLA_Q_RANK]
    ckv = pm[:, MLA_Q_RANK:MLA_Q_RANK + MLA_KV_RANK]
    kr = pm[:, MLA_Q_RANK + MLA_KV_RANK:]

    def rms(x, g):
        return x * lax.rsqrt(jnp.mean(x * x, axis=-1, keepdims=True) + NORM_EPS) * g

    q = jnp.dot(rms(pq, gq_ref[...]).astype(BF16), wuq_ref[...], preferred_element_type=F32)
    kv = jnp.dot(rms(ckv, gkv_ref[...]).astype(BF16), wukv_ref[...], preferred_element_type=F32)
    half = MLA_ROPE // 2
    kr = _rope_lanes(kr, tk_ref[0], tk_ref[1], tk_ref[2], half)
    cq, sq1, sq2 = tq_ref[0], tq_ref[1], tq_ref[2]
    for hd in range(MLA_HEADS):
        sl = slice(hd * LANE, (hd + 1) * LANE)
        q_ref[0, :, sl] = _rope_lanes(q[:, sl], cq, sq1, sq2, half).astype(q_ref.dtype)
        k_ref[0, :, sl] = (kv[:, sl] + kr).astype(k_ref.dtype)
    ones = jnp.ones((VT_ROWS - LANE, pm.shape[0]), v_ref.dtype)
    for j in range(MLA_HEADS // 2):
        c0 = MLA_HEADS * LANE + j * LANE
        v_ref[0, 0, j * VT_ROWS:j * VT_ROWS + LANE, :] = kv[:, c0:c0 + LANE].T.astype(v_ref.dtype)
        v_ref[0, 0, j * VT_ROWS + LANE:(j + 1) * VT_ROWS, :] = ones


def _gqa_prep(pg, gq_ref, gk_ref, bd_ref, tq_ref, tk_ref, q_ref, k_ref, v_ref):
    hd = GQA_HEAD_DIM
    q = pg[:, :GQA_OUT]
    k = pg[:, GQA_OUT:GQA_OUT + LANE]
    v = pg[:, GQA_OUT + LANE:]
    half = hd // 2

    def head_rms(x, g):
        w = x.shape[-1]
        ms = _split_dot(x * x, bd_ref[:w, :w], 2) * (1.0 / hd)
        return x * lax.rsqrt(ms + NORM_EPS) * g

    qn = head_rms(q, gq_ref[...])
    for j in range(GQA_OUT // LANE):
        sl = slice(j * LANE, (j + 1) * LANE)
        q_ref[0, :, sl] = _rope_lanes(qn[:, sl], tq_ref[0], tq_ref[1], tq_ref[2], half).astype(q_ref.dtype)
    kn = head_rms(k, gk_ref[...])
    kn = _rope_lanes(kn, tk_ref[0], tk_ref[1], tk_ref[2], half)
    lane = lax.broadcasted_iota(jnp.int32, kn.shape, 1)
    low = lane < hd
    kn_sw = pltpu.roll(kn, hd, axis=1)
    zero = jnp.zeros_like(kn)
    k_ref[0, :, 0 * LANE:1 * LANE] = jnp.where(low, kn, zero).astype(k_ref.dtype)
    k_ref[0, :, 1 * LANE:2 * LANE] = jnp.where(low, zero, kn_sw).astype(k_ref.dtype)
    k_ref[0, :, 2 * LANE:3 * LANE] = jnp.where(low, kn_sw, zero).astype(k_ref.dtype)
    k_ref[0, :, 3 * LANE:4 * LANE] = jnp.where(low, zero, kn).astype(k_ref.dtype)
    v_t = v.T.astype(v_ref.dtype)
    ones = jnp.ones((VT_ROWS - LANE, v.shape[0]), v_ref.dtype)
    for g in range(GQA_KV_HEADS):
        vg = v_t[g * hd:(g + 1) * hd]
        v_ref[0, 0, g * VT_ROWS:g * VT_ROWS + hd, :] = vg
        v_ref[0, 0, g * VT_ROWS + hd:g * VT_ROWS + LANE, :] = vg
        v_ref[0, 0, g * VT_ROWS + LANE:(g + 1) * VT_ROWS, :] = ones


def _split_row_specs(tm, width, n_lat, ctx_row0):
    lat_tiles = n_lat // tm
    assert ctx_row0 % tm == 0
    ctx_block0 = ctx_row0 // tm
    return [pl.BlockSpec((1, tm, width), lambda b, t: (b, jnp.minimum(t, lat_tiles - 1), 0)),
            pl.BlockSpec((1, tm, width), lambda b, t: (b, ctx_block0 + jnp.maximum(t - lat_tiles, 0), 0))]


def _pick_rows(lat_ref, ctx_ref, tm, n_lat):
    return jnp.where(pl.program_id(1) * tm >= n_lat, ctx_ref[0], lat_ref[0])


def _inproj_kernel(xl_ref, xc_ref, mod_ref, w_ref, bg_ref,
                   gqm_ref, gkvm_ref, wuq_ref, wukv_ref, tmq_ref, tmk_ref,
                   gqg_ref, gkg_ref, bd_ref, tgq_ref, tgk_ref,
                   z_ref, xbc_ref, dt_ref, gate_ref, qm_ref, km_ref, vm_ref, qg_ref, kg_ref, vg_ref,
                   *, tm, n_lat):
    is_ctx = _is_ctx_rows(tm, n_lat)
    d = D_MODEL
    sh = _mod_rows(mod_ref, 0, d, is_ctx)
    sc = _mod_rows(mod_ref, d, d, is_ctx)
    h = (_layer_norm(_pick_rows(xl_ref, xc_ref, tm, n_lat)) * (1.0 + sc) + sh).astype(BF16)

    def proj(c0, width):
        return jnp.dot(h, w_ref[:, c0:c0 + width], preferred_element_type=F32)

    _mla_prep(proj(C_PM, W_PM), gqm_ref, gkvm_ref, wuq_ref, wukv_ref, tmq_ref, tmk_ref, qm_ref, km_ref, vm_ref)
    _gqa_prep(proj(C_PG, GQA_IN), gqg_ref, gkg_ref, bd_ref, tgq_ref, tgk_ref, qg_ref, kg_ref, vg_ref)
    z_ref[0] = proj(C_Z, SSD_INNER)
    xbc_ref[0] = proj(C_XBC, SSD_CONV_CH)
    dt_ref[0] = proj(C_DT, LANE)
    for j in range(3):
        c0 = j * D_MODEL
        logits = proj(C_GATE + c0, D_MODEL) + bg_ref[:, c0:c0 + D_MODEL]
        gate_ref[0, :, c0:c0 + D_MODEL] = jax.nn.sigmoid(logits).astype(gate_ref.dtype)


def _in_projection(x_lat, x_ctx, ctx_row0, t_all, mod, w_inp, b_gate, mla_consts, gqa_consts, *, n_lat, tm):
    bsz, _, d = x_lat.shape
    assert tm == VT_TILE
    row = lambda w: pl.BlockSpec((1, tm, w), lambda b, t: (b, t, 0))
    tab = pl.BlockSpec((3, tm, LANE), lambda b, t: (0, t, 0))
    vt_spec = lambda rows: pl.BlockSpec((1, 1, rows, tm), lambda b, t: (b, t, 0, 0))
    g_qm, g_kvm, wuq_p, wukv_p, tab_mq, tab_mk = mla_consts
    g_qg, g_kg, bd, tab_gq, tab_gk = gqa_consts
    hw = MLA_HEADS * LANE
    kw = GQA_KV_HEADS * 2 * LANE
    vm_rows = (MLA_HEADS // 2) * VT_ROWS
    vg_rows = GQA_KV_HEADS * VT_ROWS
    seq = lambda w, dt: jax.ShapeDtypeStruct((bsz, t_all, w), dt)
    vts = lambda rows: jax.ShapeDtypeStruct((bsz, t_all // tm, rows, tm), BF16)
    consts = (mod, w_inp, b_gate, g_qm, g_kvm, wuq_p, wukv_p)
    return pl.pallas_call(
        functools.partial(_inproj_kernel, tm=tm, n_lat=n_lat),
        grid=(bsz, t_all // tm),
        in_specs=_split_row_specs(tm, d, n_lat, ctx_row0) + [_const_spec(a.shape) for a in consts] + [tab, tab]
                 + [_const_spec(a.shape) for a in (g_qg, g_kg, bd)] + [tab, tab],
        out_specs=[row(SSD_INNER), row(SSD_CONV_CH), row(LANE), row(GATE_IN),
                   row(hw), row(hw), vt_spec(vm_rows), row(GQA_OUT), row(kw), vt_spec(vg_rows)],
        out_shape=[seq(SSD_INNER, F32), seq(SSD_CONV_CH, F32), seq(LANE, F32), seq(GATE_IN, BF16),
                   seq(hw, BF16), seq(hw, BF16), vts(vm_rows), seq(GQA_OUT, BF16), seq(kw, BF16), vts(vg_rows)],
        compiler_params=_cparams(("parallel", "parallel")),
        name="in_projection",
    )(x_lat, x_ctx, *consts, tab_mq, tab_mk, g_qg, g_kg, bd, tab_gq, tab_gk)


def _attn_kernel(q_ref, k_ref, vt_ref, o_ref, m_ref, acc_ref, a0_ref, a1_ref, c0_ref, c1_ref,
                 p0_ref, p1_ref, s0_ref, s1_ref, *, pairs, tq, tk, n_lat, ctx_len, ctx_queries):
    cb = ATTN_COL_BLOCK
    vr = VT_ROWS
    s_bufs, p_bufs, a_bufs, c_bufs = (s0_ref, s1_ref), (p0_ref, p1_ref), (a0_ref, a1_ref), (c0_ref, c1_ref)

    def scores(r0, size, par):
        for pi, (_, units) in enumerate(pairs):
            for ui, (qb, kb) in enumerate(units):
                s = lax.dot_general(
                    k_ref[0, pl.ds(r0, size), kb * LANE:(kb + 1) * LANE],
                    q_ref[0, :, qb * LANE:(qb + 1) * LANE],
                    (((1,), (1,)), ((), ())), preferred_element_type=F32)
                s_bufs[par][pi, :size, ui * tq:(ui + 1) * tq] = s
                c_bufs[par][pi, :, ui * tq:(ui + 1) * tq] = jnp.max(s, axis=0, keepdims=True)

    def softmax(size, par):
        for pi in range(len(pairs)):
            for c in range(0, 2 * tq, cb):
                m_prev = m_ref[pi, :, c:c + cb]
                m_next = jnp.maximum(m_prev, c_bufs[par][pi, :, c:c + cb])
                for r in range(0, size, ATTN_KEY_BLOCK):
                    blk = s_bufs[par][pi, r:r + ATTN_KEY_BLOCK, c:c + cb]
                    p_bufs[par][pi, r:r + ATTN_KEY_BLOCK, c:c + cb] = jnp.exp2(blk - m_next).astype(BF16)
                a_bufs[par][pi, :, c:c + cb] = jnp.exp2(m_prev - m_next)
                m_ref[pi, :, c:c + cb] = m_next

    def values(u0, size, par):
        for pi, (vb, _) in enumerate(pairs):
            vt = jnp.concatenate([vt_ref[0, u0 + i, vb * vr:(vb + 1) * vr, :]
                                  for i in range(size // VT_TILE)], axis=1)
            pv = jnp.dot(vt, p_bufs[par][pi, :size, :], preferred_element_type=F32)
            acc_ref[pi] = a_bufs[par][pi] * acc_ref[pi] + pv

    m_ref[...] = jnp.full(m_ref.shape, -jnp.inf, F32)
    acc_ref[...] = jnp.zeros(acc_ref.shape, F32)
    tku = tk // VT_TILE
    ctx_u = n_lat // VT_TILE
    scores(n_lat, ctx_len, 0)
    if ctx_queries:
        softmax(ctx_len, 0)
        values(ctx_u, ctx_len, 0)
    else:
        n = n_lat // tk
        scores(0, tk, 1)
        softmax(ctx_len, 0)
        scores(tk, tk, 0)
        softmax(tk, 1)
        values(ctx_u, ctx_len, 0)

        def body(jj, carry):
            r = pl.multiple_of(jj * (2 * tk), 2 * tk)
            u = jj * (2 * tku)
            scores(r + 2 * tk, tk, 1)
            softmax(tk, 0)
            values(u, tk, 1)
            scores(r + 3 * tk, tk, 0)
            softmax(tk, 1)
            values(u + tku, tk, 0)
            return carry

        lax.fori_loop(0, (n - 2) // 2, body, 0)
        softmax(tk, 0)
        values((n - 2) * tku, tk, 1)
        values((n - 1) * tku, tk, 0)
    row = lax.broadcasted_iota(jnp.int32, (LANE, tq), 0)
    for pi in range(len(pairs)):
        acc = acc_ref[pi]
        o_t = acc[:LANE] / acc[LANE:LANE + 1]
        o_t = jnp.where(row < LANE // 2, o_t[:, :tq], o_t[:, tq:])
        o_ref[0, :, pi * LANE:(pi + 1) * LANE] = o_t.T.astype(o_ref.dtype)


def _attention(q, k, vt, *, groups, q_w, k_w, vt_rows, pairs, n_lat, ctx_queries, tq, tk):
    bsz, t_all, _ = k.shape
    ctx_len = t_all - n_lat
    out_w = len(pairs) * LANE
    rows = ctx_len if ctx_queries else n_lat
    q_tile0 = n_lat // tq if ctx_queries else 0
    assert rows % tq == 0 and n_lat % tq == 0 and n_lat % (2 * tk) == 0 and ctx_len <= tk
    assert (2 * tq) % ATTN_COL_BLOCK == 0 and tk % VT_TILE == 0 and ctx_len % VT_TILE == 0
    state = pltpu.VMEM((len(pairs), 1, 2 * tq), F32)
    acc = pltpu.VMEM((len(pairs), VT_ROWS, 2 * tq), F32)
    pbuf = pltpu.VMEM((len(pairs), tk, 2 * tq), BF16)
    sbuf = pltpu.VMEM((len(pairs), tk, 2 * tq), F32)
    return pl.pallas_call(
        functools.partial(_attn_kernel, pairs=pairs, tq=tq, tk=tk, n_lat=n_lat, ctx_len=ctx_len,
                          ctx_queries=ctx_queries),
        grid=(bsz, groups, rows // tq),
        in_specs=[
            pl.BlockSpec((1, tq, q_w), lambda b, g, t: (b, t + q_tile0, g)),
            pl.BlockSpec((1, t_all, k_w), lambda b, g, t: (b, 0, g), pipeline_mode=pl.Buffered(1)),
            pl.BlockSpec((1, t_all // VT_TILE, vt_rows, VT_TILE), lambda b, g, t: (b, 0, g, 0),
                         pipeline_mode=pl.Buffered(1)),
        ],
        out_specs=pl.BlockSpec((1, tq, out_w), lambda b, g, t: (b, t, g)),
        out_shape=jax.ShapeDtypeStruct((bsz, rows, groups * out_w), BF16),
        scratch_shapes=[state, acc, state, state, state, state, pbuf, pbuf, sbuf, sbuf],
        compiler_params=_cparams(("parallel", "parallel", "arbitrary")),
        name="attention_ctx" if ctx_queries else "attention",
    )(q, k, vt)


def _ssd_chunk_index(i, n_chunks, ctx_chunks, reverse):
    if reverse:
        return n_chunks - 1 - i
    lat_chunks = n_chunks - ctx_chunks
    return jnp.where(i < ctx_chunks, lat_chunks + i, i - ctx_chunks)


def _ssd_kernel(*refs, reverse, n_chunks, ctx_chunks):
    if reverse:
        (u_ref, dt_ref, dtb_ref, a_ref, tri_ref, ehd_ref,
         yf_ref, z_ref, dsk_ref, g_ref, y_ref, h_ref) = refs
    else:
        (xbc_ref, prev_ref, next_ref, dt_ref, cw_ref, cb_ref, dtb_ref, a_ref, tri_ref, ehd_ref,
         y_ref, uo_ref, h_ref) = refs
    q = SSD_CHUNK
    rows = SSD_BLOCK
    i = pl.program_id(1)
    ci = _ssd_chunk_index(i, n_chunks, ctx_chunks, reverse)

    @pl.when(i == 0)
    def _():
        h_ref[...] = jnp.zeros_like(h_ref)

    u_blks = []
    for bb in range(SSD_BATCH):
        if reverse:
            u_blks.append(u_ref[bb])
            continue
        x = xbc_ref[bb]
        lat_chunks = n_chunks - ctx_chunks
        at_start = jnp.logical_or(ci == 0, ci == lat_chunks)
        at_end = jnp.logical_or(ci == lat_chunks - 1, ci == n_chunks - 1)
        prev_row = jnp.where(at_start, 0.0, prev_ref[bb, 7:8, :])
        next_row = jnp.where(at_end, 0.0, next_ref[bb, 0:1, :])
        row = lax.broadcasted_iota(jnp.int32, (rows, 1), 0)
        x_prev = jnp.where(row == 0, prev_row, pltpu.roll(x, 1, axis=0))
        x_next = jnp.where(row == rows - 1, next_row, pltpu.roll(x, rows - 1, axis=0))
        u_blk = cw_ref[0:1, :] * x_prev + cw_ref[1:2, :] * x + cw_ref[2:3, :] * x_next + cb_ref[...]
        u_blk = u_blk * jax.nn.sigmoid(u_blk)
        uo_ref[bb] = u_blk
        u_blks.append(u_blk)

    subs = range(rows // q)
    for sub in (reversed(subs) if reverse else subs):
        sl = slice(sub * q, (sub + 1) * q)
        for bb in range(SSD_BATCH):
            u_blk = u_blks[bb]
            y = _ssd_chunk(u_blk[sl], dt_ref[bb, sl, :], dtb_ref, a_ref, tri_ref, ehd_ref, h_ref.at[bb], reverse)
            if reverse:
                yt = yf_ref[bb, sl, :] + y + dsk_ref[...] * u_blk[sl, :SSD_INNER]
                zz = z_ref[bb, sl, :]
                yt = yt * (zz * jax.nn.sigmoid(zz))
                yt = yt * lax.rsqrt(jnp.mean(yt * yt, axis=-1, keepdims=True) + NORM_EPS) * g_ref[...]
                y_ref[bb, sl, :] = yt.astype(y_ref.dtype)
            else:
                y_ref[bb, sl, :] = y


def _ssd_chunk(u, dt_raw, dtb_ref, a_ref, tri_ref, ehd_ref, h_ref, reverse):
    q = SSD_CHUNK
    xs = u[:, :SSD_INNER]
    gs = SSD_STATE
    bm = [u[:, SSD_INNER + g * gs:SSD_INNER + (g + 1) * gs] for g in range(SSD_GROUPS)]
    cm = [u[:, SSD_INNER + (SSD_GROUPS + g) * gs:SSD_INNER + (SSD_GROUPS + g + 1) * gs]
          for g in range(SSD_GROUPS)]

    dt = jax.nn.softplus(dt_raw + dtb_ref[...])
    da = dt * a_ref[...]
    a_cum = _tri_cumsum(tri_ref, da)
    end = 0 if reverse else q - 1
    a_tot = a_cum[end:end + 1, :]
    e_cum = jnp.exp(a_cum)
    dte = jnp.exp(a_tot - a_cum)
    dt_x = _split_dot(dt, ehd_ref[...], 2)
    ec_x = _split_dot(e_cum, ehd_ref[...], 2)
    dte_x = _split_dot(dte, ehd_ref[...], 2)
    xdt = xs * dt_x
    xdt_b = xdt.astype(BF16)
    xst_b = (xdt * dte_x).astype(BF16)
    a_cum_t = a_cum.T

    kk = lax.broadcasted_iota(jnp.int32, (q, q), 1)
    qq = lax.broadcasted_iota(jnp.int32, (q, q), 0)
    in_order = (kk >= qq) if reverse else (kk <= qq)
    lane = lax.broadcasted_iota(jnp.int32, (q, LANE), 1)
    off = SSD_HEADS if reverse else 0
    hpg = SSD_HEADS // SSD_GROUPS
    gw = hpg * SSD_HEAD_DIM
    y_parts = []
    for g in range(SSD_GROUPS):
        bm_b = bm[g].astype(BF16)
        cm_b = cm[g].astype(BF16)
        scores = lax.dot_general(cm_b, bm_b, (((1,), (1,)), ((), ())), preferred_element_type=F32)
        h_prev = h_ref[g]
        y_off = jnp.dot(cm_b, h_prev.astype(BF16), preferred_element_type=F32) * ec_x[:, g * gw:(g + 1) * gw]
        pair_out = []
        for j in range(hpg // 2):
            xp = xdt_b[:, g * gw + j * LANE:g * gw + (j + 1) * LANE]
            res = []
            for e in range(2):
                hd = g * hpg + 2 * j + e
                a_q = a_cum[:, off + hd:off + hd + 1]
                a_k = a_cum_t[off + hd:off + hd + 1, :]
                decay = jnp.exp(jnp.where(in_order, a_q - a_k, -jnp.inf))
                pmat = (scores * decay).astype(BF16)
                res.append(jnp.dot(pmat, xp, preferred_element_type=F32))
            pair_out.append(jnp.where(lane < SSD_HEAD_DIM, res[0], res[1]))
        y_parts.append(jnp.concatenate(pair_out, axis=1) + y_off)
        st = jnp.dot(bm[g].T.astype(BF16), xst_b[:, g * gw:(g + 1) * gw], preferred_element_type=F32)
        h_ref[g] = h_prev * ec_x[end:end + 1, g * gw:(g + 1) * gw] + st
    return jnp.concatenate(y_parts, axis=1)


def _tri_cumsum(tri_ref, da):
    acc = None
    rem = da
    for _ in range(3):
        piece = rem.astype(BF16)
        term = jnp.dot(tri_ref[...], piece, preferred_element_type=F32)
        acc = term if acc is None else acc + term
        rem = rem - piece.astype(F32)
    return acc


def _ssd_direction(xbc, dt, conv, dt_bias, a_row, tri, ehd, extra, *, ctx_len, reverse):
    bsz, t_all, cch = xbc.shape
    q = SSD_BLOCK
    assert t_all % q == 0 and ctx_len % q == 0
    n_chunks = t_all // q
    ctx_chunks = ctx_len // q
    halo = 8
    hb = q // halo
    n_halo = t_all // halo
    nb = SSD_BATCH
    assert bsz % nb == 0
    cidx = lambda i: _ssd_chunk_index(i, n_chunks, ctx_chunks, reverse)
    chunk = lambda w: pl.BlockSpec((nb, q, w), lambda b, i: (b, cidx(i), 0))
    tail = [_const_spec(dt_bias.shape), _const_spec(a_row.shape), _const_spec(tri.shape), _const_spec(ehd.shape)]
    if reverse:
        y_f, z, d_x, g_ssd = extra
        in_specs = [chunk(cch), chunk(LANE)] + tail + [
            chunk(SSD_INNER), chunk(SSD_INNER), _const_spec(d_x.shape), _const_spec(g_ssd.shape)]
        args = [xbc, dt, dt_bias, a_row, tri, ehd, y_f, z, d_x, g_ssd]
        out_specs = chunk(SSD_INNER)
        out_shape = jax.ShapeDtypeStruct((bsz, t_all, SSD_INNER), BF16)
    else:
        conv_w, conv_b = conv
        in_specs = [
            chunk(cch),
            pl.BlockSpec((nb, halo, cch), lambda b, i: (b, jnp.maximum(cidx(i) * hb - 1, 0), 0)),
            pl.BlockSpec((nb, halo, cch), lambda b, i: (b, jnp.minimum((cidx(i) + 1) * hb, n_halo - 1), 0)),
            chunk(LANE), _const_spec(conv_w.shape), _const_spec(conv_b.shape)] + tail
        args = [xbc, xbc, xbc, dt, conv_w, conv_b, dt_bias, a_row, tri, ehd]
        out_specs = [chunk(SSD_INNER), chunk(cch)]
        out_shape = [jax.ShapeDtypeStruct((bsz, t_all, SSD_INNER), F32),
                     jax.ShapeDtypeStruct((bsz, t_all, cch), F32)]
    hpg = SSD_HEADS // SSD_GROUPS
    return pl.pallas_call(
        functools.partial(_ssd_kernel, reverse=reverse, n_chunks=n_chunks, ctx_chunks=ctx_chunks),
        grid=(bsz // nb, n_chunks),
        in_specs=in_specs,
        out_specs=out_specs,
        out_shape=out_shape,
        scratch_shapes=[pltpu.VMEM((nb, SSD_GROUPS, SSD_STATE, hpg * SSD_HEAD_DIM), F32)],
        compiler_params=_cparams(("parallel", "arbitrary")),
        name="ssd_bwd" if reverse else "ssd_fwd",
    )(*args)


def _post_kernel(xl_ref, xc_ref, yml_ref, ymc_ref, ys_ref, ygl_ref, ygc_ref, gate_ref, mod_ref,
                 wo_ref, wi_ref, wf_ref, l1g_ref, l1b_ref, l2g_ref, l2b_ref, o_ref, act_ref, *, tm, n_lat, th):
    is_ctx = _is_ctx_rows(tm, n_lat)
    d = D_MODEL
    mix = None
    row0 = 0
    branches = (_pick_rows(yml_ref, ymc_ref, tm, n_lat), ys_ref[0], _pick_rows(ygl_ref, ygc_ref, tm, n_lat))
    for j, y in enumerate(branches):
        kw = y.shape[-1]
        br = jnp.dot(y, wo_ref[row0:row0 + kw, :], preferred_element_type=F32)
        br = gate_ref[0, :, j * d:(j + 1) * d].astype(F32) * br
        mix = br if mix is None else mix + br
        row0 += kw
    g1 = _mod_rows(mod_ref, 2 * d, d, is_ctx)
    x_in = _pick_rows(xl_ref, xc_ref, tm, n_lat)
    x1 = _layer_norm(DEEPNORM_ALPHA * x_in + g1 * mix) * l1g_ref[...] + l1b_ref[...]

    sh2 = _mod_rows(mod_ref, 3 * d, d, is_ctx)
    sc2 = _mod_rows(mod_ref, 4 * d, d, is_ctx)
    g2 = _mod_rows(mod_ref, 5 * d, d, is_ctx)
    h2 = (_layer_norm(x1) * (1.0 + sc2) + sh2).astype(BF16)
    hid = FFN_HIDDEN
    for c0 in range(0, hid, th):
        gt = jnp.dot(h2, wi_ref[:, c0:c0 + th], preferred_element_type=F32)
        up = jnp.dot(h2, wi_ref[:, hid + c0:hid + c0 + th], preferred_element_type=F32)
        act_ref[:, c0:c0 + th] = (gt * jax.nn.sigmoid(gt) * up).astype(BF16)
    fx = jnp.dot(act_ref[...], wf_ref[...], preferred_element_type=F32)
    o_ref[0] = _layer_norm(DEEPNORM_ALPHA * x1 + g2 * fx) * l2g_ref[...] + l2b_ref[...]


def _post_mixer(x_pair, y_mla_pair, y_ssd, y_gqa_pair, gate, mod, w_out, w_ffn_in, w_ffn_out,
                l1g, l1b, l2g, l2b, *, n_lat, tm, with_ctx):
    bsz, t_all, _ = y_ssd.shape
    d = D_MODEL
    row = lambda w: pl.BlockSpec((1, tm, w), lambda b, t: (b, t, 0))
    vec = _const_spec((1, d))
    t_out = t_all if with_ctx else n_lat
    split = lambda w, triple: _split_row_specs(tm, w, n_lat, triple[2] if with_ctx else 0)
    return pl.pallas_call(
        functools.partial(_post_kernel, tm=tm, n_lat=n_lat, th=256),
        grid=(bsz, t_out // tm),
        in_specs=split(d, x_pair) + split(MLA_OUT, y_mla_pair) + [row(SSD_INNER)] + split(GQA_OUT, y_gqa_pair)
                 + [row(GATE_IN), _const_spec(mod.shape),
                    _const_spec(w_out.shape), _const_spec(w_ffn_in.shape), _const_spec(w_ffn_out.shape),
                    vec, vec, vec, vec],
        out_specs=pl.BlockSpec((1, tm, d), lambda b, t: (b, t, 0)),
        out_shape=jax.ShapeDtypeStruct((bsz, t_out, d), F32),
        scratch_shapes=[pltpu.VMEM((tm, FFN_HIDDEN), BF16)],
        compiler_params=_cparams(("parallel", "parallel")),
        name="post_mixer",
    )(x_pair[0], x_pair[1], y_mla_pair[0], y_mla_pair[1], y_ssd, y_gqa_pair[0], y_gqa_pair[1], gate, mod,
      w_out, w_ffn_in, w_ffn_out, l1g, l1b, l2g, l2b)


def _rope_tables(n_lat, ctx_len, rot_dim, lane_dim, lane_off, scale):
    rows = n_lat // GRID_W
    row = np.repeat(np.arange(rows), GRID_W)
    col = np.tile(np.arange(GRID_W), rows)
    n_freq = rot_dim // 4
    inv_freq = ROPE_THETA ** (-np.arange(n_freq, dtype=np.float64) / n_freq)
    ang = np.concatenate([row[:, None] * inv_freq, col[:, None] * inv_freq], axis=-1)
    cos, sin = np.cos(ang), np.sin(ang)
    half = rot_dim // 2
    lane = np.arange(LANE)
    rel = (lane % lane_dim) - lane_off
    is_rot = (rel >= 0) & (rel < rot_dim)
    idx = np.where(is_rot, rel % half, 0)
    first = is_rot & (rel < half)
    second = is_rot & (rel >= half)
    c = np.where(is_rot[None, :], cos[:, idx], 1.0)
    s1 = np.where(first[None, :], -sin[:, idx], 0.0)
    s2 = np.where(second[None, :], sin[:, idx], 0.0)
    lat = np.stack([c, s1, s2])
    ctx = np.stack([np.ones((ctx_len, LANE)), np.zeros((ctx_len, LANE)), np.zeros((ctx_len, LANE))])
    return jnp.asarray((np.concatenate([lat, ctx], axis=1) * scale).astype(np.float32))


def _pad_cols(w, width, offset=0):
    return jnp.pad(w, ((0, 0), (offset, width - offset - w.shape[1])))


def _layout_w_in(w):
    c = 0
    q_lat = w[:, c:c + MLA_Q_RANK]; c += MLA_Q_RANK
    kv_lat = w[:, c:c + MLA_KV_RANK]; c += MLA_KV_RANK
    k_rope = _pad_cols(w[:, c:c + MLA_ROPE], LANE, MLA_NOPE); c += MLA_ROPE
    z = w[:, c:c + SSD_INNER]; c += SSD_INNER
    xbc = w[:, c:c + SSD_CONV_CH]; c += SSD_CONV_CH
    dt = _pad_cols(w[:, c:c + 2 * SSD_HEADS], LANE); c += 2 * SSD_HEADS
    rest = w[:, c:]
    return jnp.concatenate([q_lat, kv_lat, k_rope, z, xbc, dt, rest], axis=1).astype(BF16)


def _layout_w_uq(w):
    w = w.reshape(w.shape[0], MLA_HEADS, MLA_QK)
    w = jnp.pad(w, ((0, 0), (0, 0), (0, LANE - MLA_QK)))
    return w.reshape(w.shape[0], MLA_HEADS * LANE).astype(BF16)


def _layout_w_ukv(w):
    w = w.reshape(w.shape[0], MLA_HEADS, MLA_NOPE + MLA_V)
    k = jnp.pad(w[:, :, :MLA_NOPE], ((0, 0), (0, 0), (0, LANE - MLA_NOPE)))
    v = w[:, :, MLA_NOPE:]
    return jnp.concatenate([k.reshape(w.shape[0], -1), v.reshape(w.shape[0], -1)], axis=1).astype(BF16)


def _lane_row(vec, width=LANE):
    return _pad_cols(vec.reshape(1, -1).astype(F32), width)


def kernel(x, c, ctx, c_ctx, w_mod, b_mod, w_in, b_gate, w_uq, g_q_mla, w_ukv, g_kv_mla, conv_w, conv_b,
           a_log, dt_bias, d_skip, g_ssd, g_q_gqa, g_k_gqa, w_out, ln1_g, ln1_b, w_ffn_in, w_ffn_out,
           ln2_g, ln2_b):
    bsz, n_lat, d = x.shape
    ctx_len = ctx.shape[1]
    depth = w_mod.shape[0]
    assert d == D_MODEL and depth == DEPTH and bsz <= MOD_ROWS // 2
    tm = ROW_TILE
    assert ctx_len % tm == 0 and n_lat % tm == 0

    cvec = jnp.zeros((MOD_ROWS, d), F32).at[:bsz].set(c).at[MOD_ROWS // 2].set(c_ctx)
    mod_all = _modulation(cvec, w_mod, b_mod)

    mla_scale = MLA_QK ** -0.5 * LOG2E
    gqa_scale = GQA_HEAD_DIM ** -0.5 * LOG2E
    tab_mq = _rope_tables(n_lat, ctx_len, MLA_ROPE, LANE, MLA_NOPE, mla_scale)
    tab_mk = _rope_tables(n_lat, ctx_len, MLA_ROPE, LANE, MLA_NOPE, 1.0)
    tab_gq = _rope_tables(n_lat, ctx_len, GQA_HEAD_DIM, GQA_HEAD_DIM, 0, gqa_scale)
    tab_gk = _rope_tables(n_lat, ctx_len, GQA_HEAD_DIM, GQA_HEAD_DIM, 0, 1.0)

    qn = SSD_CHUNK
    r = np.arange(qn)
    tri_f = jnp.asarray((r[None, :] <= r[:, None]).astype(np.float32), BF16)
    tri_b = jnp.asarray((r[None, :] >= r[:, None]).astype(np.float32), BF16)
    lane1k = np.arange(SSD_INNER) // SSD_HEAD_DIM
    ehd_f = jnp.asarray((np.arange(LANE)[:, None] == lane1k[None, :]).astype(np.float32), BF16)
    ehd_b = jnp.asarray((np.arange(LANE)[:, None] == (lane1k[None, :] + SSD_HEADS)).astype(np.float32), BF16)
    blk = np.arange(GQA_OUT) // GQA_HEAD_DIM
    bd = jnp.asarray((blk[:, None] == blk[None, :]).astype(np.float32), BF16)

    mla_attn = functools.partial(_attention, groups=1, q_w=8 * LANE, k_w=8 * LANE, vt_rows=4 * VT_ROWS,
                                 pairs=tuple((j, ((2 * j, 2 * j), (2 * j + 1, 2 * j + 1))) for j in range(4)),
                                 n_lat=n_lat)
    gqa_attn = functools.partial(_attention, groups=1, q_w=4 * LANE, k_w=4 * LANE, vt_rows=2 * VT_ROWS,
                                 pairs=((0, ((0, 0), (0, 1))), (0, ((1, 0), (1, 1))),
                                        (1, ((2, 2), (2, 3))), (1, ((3, 2), (3, 3)))), n_lat=n_lat)

    def attend(fn, q, k, v, with_ctx):
        y = fn(q, k, v, ctx_queries=False, tq=ATTN_TQ, tk=ATTN_TK)
        y_c = fn(q, k, v, ctx_queries=True, tq=ctx_len, tk=ATTN_TK) if with_ctx else y
        return (y, y_c, 0)

    t_all = n_lat + ctx_len
    x_pair = (x, ctx, 0)
    for i in range(depth):
        with_ctx = i < depth - 1
        mod = mod_all[i]
        w_inp = _layout_w_in(w_in[i])
        mla_consts = (g_q_mla[i].reshape(1, -1), g_kv_mla[i].reshape(1, -1),
                      _layout_w_uq(w_uq[i]), _layout_w_ukv(w_ukv[i]), tab_mq, tab_mk)
        gqa_consts = (jnp.tile(g_q_gqa[i], GQA_HEADS).reshape(1, -1),
                      jnp.tile(g_k_gqa[i], GQA_KV_HEADS).reshape(1, -1), bd, tab_gq, tab_gk)
        z, xbc, dtr, gate, q_m, k_m, v_m, q_g, k_g, v_g = _in_projection(
            *x_pair, t_all, mod, w_inp, b_gate[i].reshape(1, -1), mla_consts, gqa_consts, n_lat=n_lat, tm=tm)
        y_mla = attend(mla_attn, q_m, k_m, v_m, with_ctx)
        y_gqa = attend(gqa_attn, q_g, k_g, v_g, with_ctx)
        a = -jnp.exp(a_log[i].astype(F32))
        a_row = _lane_row(a.reshape(-1))
        dtb_row = _lane_row(dt_bias[i].reshape(-1))
        y_f, u_act = _ssd_direction(xbc, dtr, (conv_w[i], conv_b[i].reshape(1, -1)), dtb_row, a_row,
                                    tri_f, ehd_f, None, ctx_len=ctx_len, reverse=False)
        d_x = jnp.repeat(d_skip[i], SSD_HEAD_DIM).reshape(1, -1)
        y_ssd = _ssd_direction(u_act, dtr, None, dtb_row, a_row, tri_b, ehd_b,
                               (y_f, z, d_x, g_ssd[i].reshape(1, -1)), ctx_len=ctx_len, reverse=True)
        tm_post = tm if with_ctx else LATENT_ROW_TILE
        xa = _post_mixer(x_pair, y_mla, y_ssd, y_gqa, gate, mod, w_out[i].astype(BF16),
                         w_ffn_in[i].astype(BF16), w_ffn_out[i].astype(BF16),
                         ln1_g[i].reshape(1, -1), ln1_b[i].reshape(1, -1),
                         ln2_g[i].reshape(1, -1), ln2_b[i].reshape(1, -1), n_lat=n_lat,
                         tm=tm_post, with_ctx=with_ctx)
        x_pair = (xa, xa, n_lat)
    return xa
```

```python
import functools

import jax
import jax.numpy as jnp
import numpy as np
from jax import lax
from jax.experimental import pallas as pl
from jax.experimental.pallas import tpu as pltpu

F32 = jnp.float32
BF16 = jnp.bfloat16

D_MODEL = 1024
DEPTH = 2
GRID_W = 64
ROPE_THETA = 10000.0
NORM_EPS = 1e-6

MLA_HEADS = 8
MLA_Q_RANK = 384
MLA_KV_RANK = 256
MLA_NOPE = 64
MLA_ROPE = 32
MLA_V = 64
MLA_QK = MLA_NOPE + MLA_ROPE

SSD_HEADS = 16
SSD_HEAD_DIM = 64
SSD_INNER = SSD_HEADS * SSD_HEAD_DIM
SSD_GROUPS = 2
SSD_STATE = 128
SSD_CHUNK = 128
SSD_CONV_CH = SSD_INNER + 2 * SSD_GROUPS * SSD_STATE

GQA_HEADS = 8
GQA_KV_HEADS = 2
GQA_HEAD_DIM = 64

GQA_IN = (GQA_HEADS + 2 * GQA_KV_HEADS) * GQA_HEAD_DIM
GATE_IN = 3 * D_MODEL
MLA_OUT = MLA_HEADS * MLA_V
GQA_OUT = GQA_HEADS * GQA_HEAD_DIM
FFN_HIDDEN = -(-8 * D_MODEL // (3 * 256)) * 256
DEEPNORM_ALPHA = (2 * DEPTH) ** 0.25

LANE = 128
MOD_ROWS = 8
LOG2E = 1.4426950408889634

C_PM = 0
W_PM = MLA_Q_RANK + MLA_KV_RANK + LANE
C_Z = C_PM + W_PM
C_XBC = C_Z + SSD_INNER
C_DT = C_XBC + SSD_CONV_CH
C_PG = C_DT + LANE
C_GATE = C_PG + GQA_IN
W_INP = C_GATE + GATE_IN

VMEM_LIMIT = 56 * 1024 * 1024
ROW_TILE = 256
LATENT_ROW_TILE = 512
ATTN_TQ = 512
ATTN_TK = 512
ATTN_COL_BLOCK = 512
ATTN_KEY_BLOCK = 32
SSD_BLOCK = 2 * SSD_CHUNK
SSD_BATCH = 2
VT_TILE = ROW_TILE
VT_ROWS = 144


def _cparams(sem):
    return pltpu.CompilerParams(dimension_semantics=sem, vmem_limit_bytes=VMEM_LIMIT)


def _const_spec(shape):
    nd = len(shape)
    return pl.BlockSpec(shape, lambda *_: (0,) * nd, pipeline_mode=pl.Buffered(1))


def _layer_norm(x):
    mu = jnp.mean(x, axis=-1, keepdims=True)
    xc = x - mu
    var = jnp.mean(xc * xc, axis=-1, keepdims=True)
    return xc * lax.rsqrt(var + NORM_EPS)


def _split_dot(x, m01, parts):
    pieces = []
    rem = x
    for _ in range(parts):
        piece = rem.astype(BF16)
        pieces.append(piece)
        rem = rem - piece.astype(F32)
    return jnp.dot(jnp.concatenate(pieces, axis=1), jnp.concatenate([m01] * parts, axis=0),
                   preferred_element_type=F32)


def _mod_rows(mod_ref, col, width, is_ctx):
    b = pl.program_id(0)
    lat = mod_ref[pl.ds(b, 1), col:col + width]
    ctx = mod_ref[MOD_ROWS // 2:MOD_ROWS // 2 + 1, col:col + width]
    return jnp.where(is_ctx, ctx, lat)


def _is_ctx_rows(tm, n_lat):
    t = pl.program_id(1)
    rows = t * tm + lax.broadcasted_iota(jnp.int32, (tm, 1), 0)
    return rows >= n_lat


def _mod_kernel(c_ref, w_ref, b_ref, o_ref):
    c = c_ref[...]
    s = (c * jax.nn.sigmoid(c)).astype(BF16)
    o_ref[0] = jnp.dot(s, w_ref[0].astype(BF16), preferred_element_type=F32) + b_ref[0]


def _modulation(cvec, w_mod, b_mod):
    depth, d, n = w_mod.shape
    tn = 1024
    return pl.pallas_call(
        _mod_kernel,
        grid=(depth, n // tn),
        in_specs=[
            pl.BlockSpec((MOD_ROWS, d), lambda l, j: (0, 0)),
            pl.BlockSpec((1, d, tn), lambda l, j: (l, 0, j)),
            pl.BlockSpec((1, 1, tn), lambda l, j: (l, 0, j)),
        ],
        out_specs=pl.BlockSpec((1, MOD_ROWS, tn), lambda l, j: (l, 0, j)),
        out_shape=jax.ShapeDtypeStruct((depth, MOD_ROWS, n), F32),
        compiler_params=_cparams(("arbitrary", "arbitrary")),
        name="modulation",
    )(cvec, w_mod, b_mod.reshape(depth, 1, n))


def _rope_lanes(x, c, s_fwd, s_bwd, half):
    n = x.shape[-1]
    return x * c + pltpu.roll(x, n - half, axis=1) * s_fwd + pltpu.roll(x, half, axis=1) * s_bwd


def _mla_prep(pm, gq_ref, gkv_ref, wuq_ref, wukv_ref, tq_ref, tk_ref, q_ref, k_ref, v_ref):
    pq = pm[:, :MLA_Q_RANK]
    ckv = pm[:, MLA_Q_RANK:MLA_Q_RANK + MLA_KV_RANK]
    kr = pm[:, MLA_Q_RANK + MLA_KV_RANK:]

    def rms(x, g):
        return x * lax.rsqrt(jnp.mean(x * x, axis=-1, keepdims=True) + NORM_EPS) * g

    q = jnp.dot(rms(pq, gq_ref[...]).astype(BF16), wuq_ref[...], preferred_element_type=F32)
    kv = jnp.dot(rms(ckv, gkv_ref[...]).astype(BF16), wukv_ref[...], preferred_element_type=F32)
    half = MLA_ROPE // 2
    kr = _rope_lanes(kr, tk_ref[0], tk_ref[1], tk_ref[2], half)
    cq, sq1, sq2 = tq_ref[0], tq_ref[1], tq_ref[2]
    for hd in range(MLA_HEADS):
        sl = slice(hd * LANE, (hd + 1) * LANE)
        q_ref[0, :, sl] = _rope_lanes(q[:, sl], cq, sq1, sq2, half).astype(q_ref.dtype)
        k_ref[0, :, sl] = (kv[:, sl] + kr).astype(k_ref.dtype)
    ones = jnp.ones((VT_ROWS - LANE, pm.shape[0]), v_ref.dtype)
    for j in range(MLA_HEADS // 2):
        c0 = MLA_HEADS * LANE + j * LANE
        v_ref[0, 0, j * VT_ROWS:j * VT_ROWS + LANE, :] = kv[:, c0:c0 + LANE].T.astype(v_ref.dtype)
        v_ref[0, 0, j * VT_ROWS + LANE:(j + 1) * VT_ROWS, :] = ones


def _gqa_prep(pg, gq_ref, gk_ref, bd_ref, tq_ref, tk_ref, q_ref, k_ref, v_ref):
    hd = GQA_HEAD_DIM
    q = pg[:, :GQA_OUT]
    k = pg[:, GQA_OUT:GQA_OUT + LANE]
    v = pg[:, GQA_OUT + LANE:]
    half = hd // 2

    def head_rms(x, g):
        w = x.shape[-1]
        ms = _split_dot(x * x, bd_ref[:w, :w], 2) * (1.0 / hd)
        return x * lax.rsqrt(ms + NORM_EPS) * g

    qn = head_rms(q, gq_ref[...])
    for j in range(GQA_OUT // LANE):
        sl = slice(j * LANE, (j + 1) * LANE)
        q_ref[0, :, sl] = _rope_lanes(qn[:, sl], tq_ref[0], tq_ref[1], tq_ref[2], half).astype(q_ref.dtype)
    kn = head_rms(k, gk_ref[...])
    kn = _rope_lanes(kn, tk_ref[0], tk_ref[1], tk_ref[2], half)
    lane = lax.broadcasted_iota(jnp.int32, kn.shape, 1)
    low = lane < hd
    kn_sw = pltpu.roll(kn, hd, axis=1)
    zero = jnp.zeros_like(kn)
    k_ref[0, :, 0 * LANE:1 * LANE] = jnp.where(low, kn, zero).astype(k_ref.dtype)
    k_ref[0, :, 1 * LANE:2 * LANE] = jnp.where(low, zero, kn_sw).astype(k_ref.dtype)
    k_ref[0, :, 2 * LANE:3 * LANE] = jnp.where(low, kn_sw, zero).astype(k_ref.dtype)
    k_ref[0, :, 3 * LANE:4 * LANE] = jnp.where(low, zero, kn).astype(k_ref.dtype)
    v_t = v.T.astype(v_ref.dtype)
    ones = jnp.ones((VT_ROWS - LANE, v.shape[0]), v_ref.dtype)
    for g in range(GQA_KV_HEADS):
        vg = v_t[g * hd:(g + 1) * hd]
        v_ref[0, 0, g * VT_ROWS:g * VT_ROWS + hd, :] = vg
        v_ref[0, 0, g * VT_ROWS + hd:g * VT_ROWS + LANE, :] = vg
        v_ref[0, 0, g * VT_ROWS + LANE:(g + 1) * VT_ROWS, :] = ones


def _split_row_specs(tm, width, n_lat, ctx_row0):
    lat_tiles = n_lat // tm
    assert ctx_row0 % tm == 0
    ctx_block0 = ctx_row0 // tm
    return [pl.BlockSpec((1, tm, width), lambda b, t: (b, jnp.minimum(t, lat_tiles - 1), 0)),
            pl.BlockSpec((1, tm, width), lambda b, t: (b, ctx_block0 + jnp.maximum(t - lat_tiles, 0), 0))]


def _pick_rows(lat_ref, ctx_ref, tm, n_lat):
    return jnp.where(pl.program_id(1) * tm >= n_lat, ctx_ref[0], lat_ref[0])


def _inproj_kernel(xl_ref, xc_ref, mod_ref, w_ref, bg_ref,
                   gqm_ref, gkvm_ref, wuq_ref, wukv_ref, tmq_ref, tmk_ref,
                   gqg_ref, gkg_ref, bd_ref, tgq_ref, tgk_ref,
                   z_ref, xbc_ref, dt_ref, gate_ref, qm_ref, km_ref, vm_ref, qg_ref, kg_ref, vg_ref,
                   *, tm, n_lat):
    is_ctx = _is_ctx_rows(tm, n_lat)
    d = D_MODEL
    sh = _mod_rows(mod_ref, 0, d, is_ctx)
    sc = _mod_rows(mod_ref, d, d, is_ctx)
    h = (_layer_norm(_pick_rows(xl_ref, xc_ref, tm, n_lat)) * (1.0 + sc) + sh).astype(BF16)

    def proj(c0, width):
        return jnp.dot(h, w_ref[:, c0:c0 + width], preferred_element_type=F32)

    _mla_prep(proj(C_PM, W_PM), gqm_ref, gkvm_ref, wuq_ref, wukv_ref, tmq_ref, tmk_ref, qm_ref, km_ref, vm_ref)
    _gqa_prep(proj(C_PG, GQA_IN), gqg_ref, gkg_ref, bd_ref, tgq_ref, tgk_ref, qg_ref, kg_ref, vg_ref)
    z_ref[0] = proj(C_Z, SSD_INNER)
    xbc_ref[0] = proj(C_XBC, SSD_CONV_CH)
    dt_ref[0] = proj(C_DT, LANE)
    for j in range(3):
        c0 = j * D_MODEL
        logits = proj(C_GATE + c0, D_MODEL) + bg_ref[:, c0:c0 + D_MODEL]
        gate_ref[0, :, c0:c0 + D_MODEL] = jax.nn.sigmoid(logits).astype(gate_ref.dtype)


def _in_projection(x_lat, x_ctx, ctx_row0, t_all, mod, w_inp, b_gate, mla_consts, gqa_consts, *, n_lat, tm):
    bsz, _, d = x_lat.shape
    assert tm == VT_TILE
    row = lambda w: pl.BlockSpec((1, tm, w), lambda b, t: (b, t, 0))
    tab = pl.BlockSpec((3, tm, LANE), lambda b, t: (0, t, 0))
    vt_spec = lambda rows: pl.BlockSpec((1, 1, rows, tm), lambda b, t: (b, t, 0, 0))
    g_qm, g_kvm, wuq_p, wukv_p, tab_mq, tab_mk = mla_consts
    g_qg, g_kg, bd, tab_gq, tab_gk = gqa_consts
    hw = MLA_HEADS * LANE
    kw = GQA_KV_HEADS * 2 * LANE
    vm_rows = (MLA_HEADS // 2) * VT_ROWS
    vg_rows = GQA_KV_HEADS * VT_ROWS
    seq = lambda w, dt: jax.ShapeDtypeStruct((bsz, t_all, w), dt)
    vts = lambda rows: jax.ShapeDtypeStruct((bsz, t_all // tm, rows, tm), BF16)
    consts = (mod, w_inp, b_gate, g_qm, g_kvm, wuq_p, wukv_p)
    return pl.pallas_call(
        functools.partial(_inproj_kernel, tm=tm, n_lat=n_lat),
        grid=(bsz, t_all // tm),
        in_specs=_split_row_specs(tm, d, n_lat, ctx_row0) + [_const_spec(a.shape) for a in consts] + [tab, tab]
                 + [_const_spec(a.shape) for a in (g_qg, g_kg, bd)] + [tab, tab],
        out_specs=[row(SSD_INNER), row(SSD_CONV_CH), row(LANE), row(GATE_IN),
                   row(hw), row(hw), vt_spec(vm_rows), row(GQA_OUT), row(kw), vt_spec(vg_rows)],
        out_shape=[seq(SSD_INNER, F32), seq(SSD_CONV_CH, F32), seq(LANE, F32), seq(GATE_IN, BF16),
                   seq(hw, BF16), seq(hw, BF16), vts(vm_rows), seq(GQA_OUT, BF16), seq(kw, BF16), vts(vg_rows)],
        compiler_params=_cparams(("parallel", "parallel")),
        name="in_projection",
    )(x_lat, x_ctx, *consts, tab_mq, tab_mk, g_qg, g_kg, bd, tab_gq, tab_gk)


def _attn_kernel(q_ref, k_ref, vt_ref, o_ref, m_ref, acc_ref, a0_ref, a1_ref, c0_ref, c1_ref,
                 p0_ref, p1_ref, s0_ref, s1_ref, *, pairs, tq, tk, n_lat, ctx_len, ctx_queries):
    cb = ATTN_COL_BLOCK
    vr = VT_ROWS
    s_bufs, p_bufs, a_bufs, c_bufs = (s0_ref, s1_ref), (p0_ref, p1_ref), (a0_ref, a1_ref), (c0_ref, c1_ref)

    def scores(r0, size, par):
        for pi, (_, units) in enumerate(pairs):
            for ui, (qb, kb) in enumerate(units):
                s = lax.dot_general(
                    k_ref[0, pl.ds(r0, size), kb * LANE:(kb + 1) * LANE],
                    q_ref[0, :, qb * LANE:(qb + 1) * LANE],
                    (((1,), (1,)), ((), ())), preferred_element_type=F32)
                s_bufs[par][pi, :size, ui * tq:(ui + 1) * tq] = s
                c_bufs[par][pi, :, ui * tq:(ui + 1) * tq] = jnp.max(s, axis=0, keepdims=True)

    def softmax(size, par):
        for pi in range(len(pairs)):
            for c in range(0, 2 * tq, cb):
                m_prev = m_ref[pi, :, c:c + cb]
                m_next = jnp.maximum(m_prev, c_bufs[par][pi, :, c:c + cb])
                for r in range(0, size, ATTN_KEY_BLOCK):
                    blk = s_bufs[par][pi, r:r + ATTN_KEY_BLOCK, c:c + cb]
                    p_bufs[par][pi, r:r + ATTN_KEY_BLOCK, c:c + cb] = jnp.exp2(blk - m_next).astype(BF16)
                a_bufs[par][pi, :, c:c + cb] = jnp.exp2(m_prev - m_next)
                m_ref[pi, :, c:c + cb] = m_next

    def values(u0, size, par):
        for pi, (vb, _) in enumerate(pairs):
            vt = jnp.concatenate([vt_ref[0, u0 + i, vb * vr:(vb + 1) * vr, :]
                                  for i in range(size // VT_TILE)], axis=1)
            pv = jnp.dot(vt, p_bufs[par][pi, :size, :], preferred_element_type=F32)
            acc_ref[pi] = a_bufs[par][pi] * acc_ref[pi] + pv

    m_ref[...] = jnp.full(m_ref.shape, -jnp.inf, F32)
    acc_ref[...] = jnp.zeros(acc_ref.shape, F32)
    tku = tk // VT_TILE
    ctx_u = n_lat // VT_TILE
    scores(n_lat, ctx_len, 0)
    if ctx_queries:
        softmax(ctx_len, 0)
        values(ctx_u, ctx_len, 0)
    else:
        n = n_lat // tk
        scores(0, tk, 1)
        softmax(ctx_len, 0)
        scores(tk, tk, 0)
        softmax(tk, 1)
        values(ctx_u, ctx_len, 0)

        def body(jj, carry):
            r = pl.multiple_of(jj * (2 * tk), 2 * tk)
            u = jj * (2 * tku)
            scores(r + 2 * tk, tk, 1)
            softmax(tk, 0)
            values(u, tk, 1)
            scores(r + 3 * tk, tk, 0)
            softmax(tk, 1)
            values(u + tku, tk, 0)
            return carry

        lax.fori_loop(0, (n - 2) // 2, body, 0)
        softmax(tk, 0)
        values((n - 2) * tku, tk, 1)
        values((n - 1) * tku, tk, 0)
    row = lax.broadcasted_iota(jnp.int32, (LANE, tq), 0)
    for pi in range(len(pairs)):
        acc = acc_ref[pi]
        o_t = acc[:LANE] / acc[LANE:LANE + 1]
        o_t = jnp.where(row < LANE // 2, o_t[:, :tq], o_t[:, tq:])
        o_ref[0, :, pi * LANE:(pi + 1) * LANE] = o_t.T.astype(o_ref.dtype)


def _attention(q, k, vt, *, groups, q_w, k_w, vt_rows, pairs, n_lat, ctx_queries, tq, tk):
    bsz, t_all, _ = k.shape
    ctx_len = t_all - n_lat
    out_w = len(pairs) * LANE
    rows = ctx_len if ctx_queries else n_lat
    q_tile0 = n_lat // tq if ctx_queries else 0
    assert rows % tq == 0 and n_lat % tq == 0 and n_lat % (2 * tk) == 0 and ctx_len <= tk
    assert (2 * tq) % ATTN_COL_BLOCK == 0 and tk % VT_TILE == 0 and ctx_len % VT_TILE == 0
    state = pltpu.VMEM((len(pairs), 1, 2 * tq), F32)
    acc = pltpu.VMEM((len(pairs), VT_ROWS, 2 * tq), F32)
    pbuf = pltpu.VMEM((len(pairs), tk, 2 * tq), BF16)
    sbuf = pltpu.VMEM((len(pairs), tk, 2 * tq), F32)
    return pl.pallas_call(
        functools.partial(_attn_kernel, pairs=pairs, tq=tq, tk=tk, n_lat=n_lat, ctx_len=ctx_len,
                          ctx_queries=ctx_queries),
        grid=(bsz, groups, rows // tq),
        in_specs=[
            pl.BlockSpec((1, tq, q_w), lambda b, g, t: (b, t + q_tile0, g)),
            pl.BlockSpec((1, t_all, k_w), lambda b, g, t: (b, 0, g), pipeline_mode=pl.Buffered(1)),
            pl.BlockSpec((1, t_all // VT_TILE, vt_rows, VT_TILE), lambda b, g, t: (b, 0, g, 0),
                         pipeline_mode=pl.Buffered(1)),
        ],
        out_specs=pl.BlockSpec((1, tq, out_w), lambda b, g, t: (b, t, g)),
        out_shape=jax.ShapeDtypeStruct((bsz, rows, groups * out_w), BF16),
        scratch_shapes=[state, acc, state, state, state, state, pbuf, pbuf, sbuf, sbuf],
        compiler_params=_cparams(("parallel", "parallel", "arbitrary")),
        name="attention_ctx" if ctx_queries else "attention",
    )(q, k, vt)


def _ssd_chunk_index(i, n_chunks, ctx_chunks, reverse):
    if reverse:
        return n_chunks - 1 - i
    lat_chunks = n_chunks - ctx_chunks
    return jnp.where(i < ctx_chunks, lat_chunks + i, i - ctx_chunks)


def _ssd_kernel(*refs, reverse, n_chunks, ctx_chunks):
    if reverse:
        (u_ref, dt_ref, dtb_ref, a_ref, tri_ref, ehd_ref,
         yf_ref, z_ref, dsk_ref, g_ref, y_ref, h_ref) = refs
    else:
        (xbc_ref, prev_ref, next_ref, dt_ref, cw_ref, cb_ref, dtb_ref, a_ref, tri_ref, ehd_ref,
         y_ref, uo_ref, h_ref) = refs
    q = SSD_CHUNK
    rows = SSD_BLOCK
    i = pl.program_id(1)
    ci = _ssd_chunk_index(i, n_chunks, ctx_chunks, reverse)

    @pl.when(i == 0)
    def _():
        h_ref[...] = jnp.zeros_like(h_ref)

    u_blks = []
    for bb in range(SSD_BATCH):
        if reverse:
            u_blks.append(u_ref[bb])
            continue
        x = xbc_ref[bb]
        lat_chunks = n_chunks - ctx_chunks
        at_start = jnp.logical_or(ci == 0, ci == lat_chunks)
        at_end = jnp.logical_or(ci == lat_chunks - 1, ci == n_chunks - 1)
        prev_row = jnp.where(at_start, 0.0, prev_ref[bb, 7:8, :])
        next_row = jnp.where(at_end, 0.0, next_ref[bb, 0:1, :])
        row = lax.broadcasted_iota(jnp.int32, (rows, 1), 0)
        x_prev = jnp.where(row == 0, prev_row, pltpu.roll(x, 1, axis=0))
        x_next = jnp.where(row == rows - 1, next_row, pltpu.roll(x, rows - 1, axis=0))
        u_blk = cw_ref[0:1, :] * x_prev + cw_ref[1:2, :] * x + cw_ref[2:3, :] * x_next + cb_ref[...]
        u_blk = u_blk * jax.nn.sigmoid(u_blk)
        uo_ref[bb] = u_blk
        u_blks.append(u_blk)

    subs = range(rows // q)
    for sub in (reversed(subs) if reverse else subs):
        sl = slice(sub * q, (sub + 1) * q)
        for bb in range(SSD_BATCH):
            u_blk = u_blks[bb]
            y = _ssd_chunk(u_blk[sl], dt_ref[bb, sl, :], dtb_ref, a_ref, tri_ref, ehd_ref, h_ref.at[bb], reverse)
            if reverse:
                yt = yf_ref[bb, sl, :] + y + dsk_ref[...] * u_blk[sl, :SSD_INNER]
                zz = z_ref[bb, sl, :]
                yt = yt * (zz * jax.nn.sigmoid(zz))
                yt = yt * lax.rsqrt(jnp.mean(yt * yt, axis=-1, keepdims=True) + NORM_EPS) * g_ref[...]
                y_ref[bb, sl, :] = yt.astype(y_ref.dtype)
            else:
                y_ref[bb, sl, :] = y


def _ssd_chunk(u, dt_raw, dtb_ref, a_ref, tri_ref, ehd_ref, h_ref, reverse):
    q = SSD_CHUNK
    xs = u[:, :SSD_INNER]
    gs = SSD_STATE
    bm = [u[:, SSD_INNER + g * gs:SSD_INNER + (g + 1) * gs] for g in range(SSD_GROUPS)]
    cm = [u[:, SSD_INNER + (SSD_GROUPS + g) * gs:SSD_INNER + (SSD_GROUPS + g + 1) * gs]
          for g in range(SSD_GROUPS)]

    dt = jax.nn.softplus(dt_raw + dtb_ref[...])
    da = dt * a_ref[...]
    a_cum = _tri_cumsum(tri_ref, da)
    end = 0 if reverse else q - 1
    a_tot = a_cum[end:end + 1, :]
    e_cum = jnp.exp(a_cum)
    dte = jnp.exp(a_tot - a_cum)
    dt_x = _split_dot(dt, ehd_ref[...], 2)
    ec_x = _split_dot(e_cum, ehd_ref[...], 2)
    dte_x = _split_dot(dte, ehd_ref[...], 2)
    xdt = xs * dt_x
    xdt_b = xdt.astype(BF16)
    xst_b = (xdt * dte_x).astype(BF16)
    a_cum_t = a_cum.T

    kk = lax.broadcasted_iota(jnp.int32, (q, q), 1)
    qq = lax.broadcasted_iota(jnp.int32, (q, q), 0)
    in_order = (kk >= qq) if reverse else (kk <= qq)
    lane = lax.broadcasted_iota(jnp.int32, (q, LANE), 1)
    off = SSD_HEADS if reverse else 0
    hpg = SSD_HEADS // SSD_GROUPS
    gw = hpg * SSD_HEAD_DIM
    y_parts = []
    for g in range(SSD_GROUPS):
        bm_b = bm[g].astype(BF16)
        cm_b = cm[g].astype(BF16)
        scores = lax.dot_general(cm_b, bm_b, (((1,), (1,)), ((), ())), preferred_element_type=F32)
        h_prev = h_ref[g]
        y_off = jnp.dot(cm_b, h_prev.astype(BF16), preferred_element_type=F32) * ec_x[:, g * gw:(g + 1) * gw]
        pair_out = []
        for j in range(hpg // 2):
            xp = xdt_b[:, g * gw + j * LANE:g * gw + (j + 1) * LANE]
            res = []
            for e in range(2):
                hd = g * hpg + 2 * j + e
                a_q = a_cum[:, off + hd:off + hd + 1]
                a_k = a_cum_t[off + hd:off + hd + 1, :]
                decay = jnp.exp(jnp.where(in_order, a_q - a_k, -jnp.inf))
                pmat = (scores * decay).astype(BF16)
                res.append(jnp.dot(pmat, xp, preferred_element_type=F32))
            pair_out.append(jnp.where(lane < SSD_HEAD_DIM, res[0], res[1]))
        y_parts.append(jnp.concatenate(pair_out, axis=1) + y_off)
        st = jnp.dot(bm[g].T.astype(BF16), xst_b[:, g * gw:(g + 1) * gw], preferred_element_type=F32)
        h_ref[g] = h_prev * ec_x[end:end + 1, g * gw:(g + 1) * gw] + st
    return jnp.concatenate(y_parts, axis=1)


def _tri_cumsum(tri_ref, da):
    acc = None
    rem = da
    for _ in range(3):
        piece = rem.astype(BF16)
        term = jnp.dot(tri_ref[...], piece, preferred_element_type=F32)
        acc = term if acc is None else acc + term
        rem = rem - piece.astype(F32)
    return acc


def _ssd_direction(xbc, dt, conv, dt_bias, a_row, tri, ehd, extra, *, ctx_len, reverse):
    bsz, t_all, cch = xbc.shape
    q = SSD_BLOCK
    assert t_all % q == 0 and ctx_len % q == 0
    n_chunks = t_all // q
    ctx_chunks = ctx_len // q
    halo = 8
    hb = q // halo
    n_halo = t_all // halo
    nb = SSD_BATCH
    assert bsz % nb == 0
    cidx = lambda i: _ssd_chunk_index(i, n_chunks, ctx_chunks, reverse)
    chunk = lambda w: pl.BlockSpec((nb, q, w), lambda b, i: (b, cidx(i), 0))
    tail = [_const_spec(dt_bias.shape), _const_spec(a_row.shape), _const_spec(tri.shape), _const_spec(ehd.shape)]
    if reverse:
        y_f, z, d_x, g_ssd = extra
        in_specs = [chunk(cch), chunk(LANE)] + tail + [
            chunk(SSD_INNER), chunk(SSD_INNER), _const_spec(d_x.shape), _const_spec(g_ssd.shape)]
        args = [xbc, dt, dt_bias, a_row, tri, ehd, y_f, z, d_x, g_ssd]
        out_specs = chunk(SSD_INNER)
        out_shape = jax.ShapeDtypeStruct((bsz, t_all, SSD_INNER), BF16)
    else:
        conv_w, conv_b = conv
        in_specs = [
            chunk(cch),
            pl.BlockSpec((nb, halo, cch), lambda b, i: (b, jnp.maximum(cidx(i) * hb - 1, 0), 0)),
            pl.BlockSpec((nb, halo, cch), lambda b, i: (b, jnp.minimum((cidx(i) + 1) * hb, n_halo - 1), 0)),
            chunk(LANE), _const_spec(conv_w.shape), _const_spec(conv_b.shape)] + tail
        args = [xbc, xbc, xbc, dt, conv_w, conv_b, dt_bias, a_row, tri, ehd]
        out_specs = [chunk(SSD_INNER), chunk(cch)]
        out_shape = [jax.ShapeDtypeStruct((bsz, t_all, SSD_INNER), F32),
                     jax.ShapeDtypeStruct((bsz, t_all, cch), F32)]
    hpg = SSD_HEADS // SSD_GROUPS
    return pl.pallas_call(
        functools.partial(_ssd_kernel, reverse=reverse, n_chunks=n_chunks, ctx_chunks=ctx_chunks),
        grid=(bsz // nb, n_chunks),
        in_specs=in_specs,
        out_specs=out_specs,
        out_shape=out_shape,
        scratch_shapes=[pltpu.VMEM((nb, SSD_GROUPS, SSD_STATE, hpg * SSD_HEAD_DIM), F32)],
        compiler_params=_cparams(("parallel", "arbitrary")),
        name="ssd_bwd" if reverse else "ssd_fwd",
    )(*args)


def _post_kernel(xl_ref, xc_ref, yml_ref, ymc_ref, ys_ref, ygl_ref, ygc_ref, gate_ref, mod_ref,
                 wo_ref, wi_ref, wf_ref, l1g_ref, l1b_ref, l2g_ref, l2b_ref, o_ref, act_ref, *, tm, n_lat, th):
    is_ctx = _is_ctx_rows(tm, n_lat)
    d = D_MODEL
    mix = None
    row0 = 0
    branches = (_pick_rows(yml_ref, ymc_ref, tm, n_lat), ys_ref[0], _pick_rows(ygl_ref, ygc_ref, tm, n_lat))
    for j, y in enumerate(branches):
        kw = y.shape[-1]
        br = jnp.dot(y, wo_ref[row0:row0 + kw, :], preferred_element_type=F32)
        br = gate_ref[0, :, j * d:(j + 1) * d].astype(F32) * br
        mix = br if mix is None else mix + br
        row0 += kw
    g1 = _mod_rows(mod_ref, 2 * d, d, is_ctx)
    x_in = _pick_rows(xl_ref, xc_ref, tm, n_lat)
    x1 = _layer_norm(DEEPNORM_ALPHA * x_in + g1 * mix) * l1g_ref[...] + l1b_ref[...]

    sh2 = _mod_rows(mod_ref, 3 * d, d, is_ctx)
    sc2 = _mod_rows(mod_ref, 4 * d, d, is_ctx)
    g2 = _mod_rows(mod_ref, 5 * d, d, is_ctx)
    h2 = (_layer_norm(x1) * (1.0 + sc2) + sh2).astype(BF16)
    hid = FFN_HIDDEN
    for c0 in range(0, hid, th):
        gt = jnp.dot(h2, wi_ref[:, c0:c0 + th], preferred_element_type=F32)
        up = jnp.dot(h2, wi_ref[:, hid + c0:hid + c0 + th], preferred_element_type=F32)
        act_ref[:, c0:c0 + th] = (gt * jax.nn.sigmoid(gt) * up).astype(BF16)
    fx = jnp.dot(act_ref[...], wf_ref[...], preferred_element_type=F32)
    o_ref[0] = _layer_norm(DEEPNORM_ALPHA * x1 + g2 * fx) * l2g_ref[...] + l2b_ref[...]


def _post_mixer(x_pair, y_mla_pair, y_ssd, y_gqa_pair, gate, mod, w_out, w_ffn_in, w_ffn_out,
                l1g, l1b, l2g, l2b, *, n_lat, tm, with_ctx):
    bsz, t_all, _ = y_ssd.shape
    d = D_MODEL
    row = lambda w: pl.BlockSpec((1, tm, w), lambda b, t: (b, t, 0))
    vec = _const_spec((1, d))
    t_out = t_all if with_ctx else n_lat
    split = lambda w, triple: _split_row_specs(tm, w, n_lat, triple[2] if with_ctx else 0)
    return pl.pallas_call(
        functools.partial(_post_kernel, tm=tm, n_lat=n_lat, th=256),
        grid=(bsz, t_out // tm),
        in_specs=split(d, x_pair) + split(MLA_OUT, y_mla_pair) + [row(SSD_INNER)] + split(GQA_OUT, y_gqa_pair)
                 + [row(GATE_IN), _const_spec(mod.shape),
                    _const_spec(w_out.shape), _const_spec(w_ffn_in.shape), _const_spec(w_ffn_out.shape),
                    vec, vec, vec, vec],
        out_specs=pl.BlockSpec((1, tm, d), lambda b, t: (b, t, 0)),
        out_shape=jax.ShapeDtypeStruct((bsz, t_out, d), F32),
        scratch_shapes=[pltpu.VMEM((tm, FFN_HIDDEN), BF16)],
        compiler_params=_cparams(("parallel", "parallel")),
        name="post_mixer",
    )(x_pair[0], x_pair[1], y_mla_pair[0], y_mla_pair[1], y_ssd, y_gqa_pair[0], y_gqa_pair[1], gate, mod,
      w_out, w_ffn_in, w_ffn_out, l1g, l1b, l2g, l2b)


def _rope_tables(n_lat, ctx_len, rot_dim, lane_dim, lane_off, scale):
    rows = n_lat // GRID_W
    row = np.repeat(np.arange(rows), GRID_W)
    col = np.tile(np.arange(GRID_W), rows)
    n_freq = rot_dim // 4
    inv_freq = ROPE_THETA ** (-np.arange(n_freq, dtype=np.float64) / n_freq)
    ang = np.concatenate([row[:, None] * inv_freq, col[:, None] * inv_freq], axis=-1)
    cos, sin = np.cos(ang), np.sin(ang)
    half = rot_dim // 2
    lane = np.arange(LANE)
    rel = (lane % lane_dim) - lane_off
    is_rot = (rel >= 0) & (rel < rot_dim)
    idx = np.where(is_rot, rel % half, 0)
    first = is_rot & (rel < half)
    second = is_rot & (rel >= half)
    c = np.where(is_rot[None, :], cos[:, idx], 1.0)
    s1 = np.where(first[None, :], -sin[:, idx], 0.0)
    s2 = np.where(second[None, :], sin[:, idx], 0.0)
    lat = np.stack([c, s1, s2])
    ctx = np.stack([np.ones((ctx_len, LANE)), np.zeros((ctx_len, LANE)), np.zeros((ctx_len, LANE))])
    return jnp.asarray((np.concatenate([lat, ctx], axis=1) * scale).astype(np.float32))


def _pad_cols(w, width, offset=0):
    return jnp.pad(w, ((0, 0), (offset, width - offset - w.shape[1])))


def _layout_w_in(w):
    c = 0
    q_lat = w[:, c:c + MLA_Q_RANK]; c += MLA_Q_RANK
    kv_lat = w[:, c:c + MLA_KV_RANK]; c += MLA_KV_RANK
    k_rope = _pad_cols(w[:, c:c + MLA_ROPE], LANE, MLA_NOPE); c += MLA_ROPE
    z = w[:, c:c + SSD_INNER]; c += SSD_INNER
    xbc = w[:, c:c + SSD_CONV_CH]; c += SSD_CONV_CH
    dt = _pad_cols(w[:, c:c + 2 * SSD_HEADS], LANE); c += 2 * SSD_HEADS
    rest = w[:, c:]
    return jnp.concatenate([q_lat, kv_lat, k_rope, z, xbc, dt, rest], axis=1).astype(BF16)


def _layout_w_uq(w):
    w = w.reshape(w.shape[0], MLA_HEADS, MLA_QK)
    w = jnp.pad(w, ((0, 0), (0, 0), (0, LANE - MLA_QK)))
    return w.reshape(w.shape[0], MLA_HEADS * LANE).astype(BF16)


def _layout_w_ukv(w):
    w = w.reshape(w.shape[0], MLA_HEADS, MLA_NOPE + MLA_V)
    k = jnp.pad(w[:, :, :MLA_NOPE], ((0, 0), (0, 0), (0, LANE - MLA_NOPE)))
    v = w[:, :, MLA_NOPE:]
    return jnp.concatenate([k.reshape(w.shape[0], -1), v.reshape(w.shape[0], -1)], axis=1).astype(BF16)


def _lane_row(vec, width=LANE):
    return _pad_cols(vec.reshape(1, -1).astype(F32), width)


def kernel(x, c, ctx, c_ctx, w_mod, b_mod, w_in, b_gate, w_uq, g_q_mla, w_ukv, g_kv_mla, conv_w, conv_b,
           a_log, dt_bias, d_skip, g_ssd, g_q_gqa, g_k_gqa, w_out, ln1_g, ln1_b, w_ffn_in, w_ffn_out,
           ln2_g, ln2_b):
    bsz, n_lat, d = x.shape
    ctx_len = ctx.shape[1]
    depth = w_mod.shape[0]
    assert d == D_MODEL and depth == DEPTH and bsz <= MOD_ROWS // 2
    tm = ROW_TILE
    assert ctx_len % tm == 0 and n_lat % tm == 0

    cvec = jnp.zeros((MOD_ROWS, d), F32).at[:bsz].set(c).at[MOD_ROWS // 2].set(c_ctx)
    mod_all = _modulation(cvec, w_mod, b_mod)

    mla_scale = MLA_QK ** -0.5 * LOG2E
    gqa_scale = GQA_HEAD_DIM ** -0.5 * LOG2E
    tab_mq = _rope_tables(n_lat, ctx_len, MLA_ROPE, LANE, MLA_NOPE, mla_scale)
    tab_mk = _rope_tables(n_lat, ctx_len, MLA_ROPE, LANE, MLA_NOPE, 1.0)
    tab_gq = _rope_tables(n_lat, ctx_len, GQA_HEAD_DIM, GQA_HEAD_DIM, 0, gqa_scale)
    tab_gk = _rope_tables(n_lat, ctx_len, GQA_HEAD_DIM, GQA_HEAD_DIM, 0, 1.0)

    qn = SSD_CHUNK
    r = np.arange(qn)
    tri_f = jnp.asarray((r[None, :] <= r[:, None]).astype(np.float32), BF16)
    tri_b = jnp.asarray((r[None, :] >= r[:, None]).astype(np.float32), BF16)
    lane1k = np.arange(SSD_INNER) // SSD_HEAD_DIM
    ehd_f = jnp.asarray((np.arange(LANE)[:, None] == lane1k[None, :]).astype(np.float32), BF16)
    ehd_b = jnp.asarray((np.arange(LANE)[:, None] == (lane1k[None, :] + SSD_HEADS)).astype(np.float32), BF16)
    blk = np.arange(GQA_OUT) // GQA_HEAD_DIM
    bd = jnp.asarray((blk[:, None] == blk[None, :]).astype(np.float32), BF16)

    mla_attn = functools.partial(_attention, groups=1, q_w=8 * LANE, k_w=8 * LANE, vt_rows=4 * VT_ROWS,
                                 pairs=tuple((j, ((2 * j, 2 * j), (2 * j + 1, 2 * j + 1))) for j in range(4)),
                                 n_lat=n_lat)
    gqa_attn = functools.partial(_attention, groups=1, q_w=4 * LANE, k_w=4 * LANE, vt_rows=2 * VT_ROWS,
                                 pairs=((0, ((0, 0), (0, 1))), (0, ((1, 0), (1, 1))),
                                        (1, ((2, 2), (2, 3))), (1, ((3, 2), (3, 3)))), n_lat=n_lat)

    def attend(fn, q, k, v, with_ctx):
        y = fn(q, k, v, ctx_queries=False, tq=ATTN_TQ, tk=ATTN_TK)
        y_c = fn(q, k, v, ctx_queries=True, tq=ctx_len, tk=ATTN_TK) if with_ctx else y
        return (y, y_c, 0)

    t_all = n_lat + ctx_len
    x_pair = (x, ctx, 0)
    for i in range(depth):
        with_ctx = i < depth - 1
        mod = mod_all[i]
        w_inp = _layout_w_in(w_in[i])
        mla_consts = (g_q_mla[i].reshape(1, -1), g_kv_mla[i].reshape(1, -1),
                      _layout_w_uq(w_uq[i]), _layout_w_ukv(w_ukv[i]), tab_mq, tab_mk)
        gqa_consts = (jnp.tile(g_q_gqa[i], GQA_HEADS).reshape(1, -1),
                      jnp.tile(g_k_gqa[i], GQA_KV_HEADS).reshape(1, -1), bd, tab_gq, tab_gk)
        z, xbc, dtr, gate, q_m, k_m, v_m, q_g, k_g, v_g = _in_projection(
            *x_pair, t_all, mod, w_inp, b_gate[i].reshape(1, -1), mla_consts, gqa_consts, n_lat=n_lat, tm=tm)
        y_mla = attend(mla_attn, q_m, k_m, v_m, with_ctx)
        y_gqa = attend(gqa_attn, q_g, k_g, v_g, with_ctx)
        a = -jnp.exp(a_log[i].astype(F32))
        a_row = _lane_row(a.reshape(-1))
        dtb_row = _lane_row(dt_bias[i].reshape(-1))
        y_f, u_act = _ssd_direction(xbc, dtr, (conv_w[i], conv_b[i].reshape(1, -1)), dtb_row, a_row,
                                    tri_f, ehd_f, None, ctx_len=ctx_len, reverse=False)
        d_x = jnp.repeat(d_skip[i], SSD_HEAD_DIM).reshape(1, -1)
        y_ssd = _ssd_direction(u_act, dtr, None, dtb_row, a_row, tri_b, ehd_b,
                               (y_f, z, d_x, g_ssd[i].reshape(1, -1)), ctx_len=ctx_len, reverse=True)
        tm_post = tm if with_ctx else LATENT_ROW_TILE
        xa = _post_mixer(x_pair, y_mla, y_ssd, y_gqa, gate, mod, w_out[i].astype(BF16),
                         w_ffn_in[i].astype(BF16), w_ffn_out[i].astype(BF16),
                         ln1_g[i].reshape(1, -1), ln1_b[i].reshape(1, -1),
                         ln2_g[i].reshape(1, -1), ln2_b[i].reshape(1, -1), n_lat=n_lat,
                         tm=tm_post, with_ctx=with_ctx)
        x_pair = (xa, xa, n_lat)
    return xa
```

```python
import functools

import jax
import jax.numpy as jnp
import numpy as np
from jax import lax
from jax.experimental import pallas as pl
from jax.experimental.pallas import tpu as pltpu

F32 = jnp.float32
BF16 = jnp.bfloat16

D_MODEL = 1024
DEPTH = 2
GRID_W = 64
ROPE_THETA = 10000.0
NORM_EPS = 1e-6

MLA_HEADS = 8
MLA_Q_RANK = 384
MLA_KV_RANK = 256
MLA_NOPE = 64
MLA_ROPE = 32
MLA_V = 64
MLA_QK = MLA_NOPE + MLA_ROPE

SSD_HEADS = 16
SSD_HEAD_DIM = 64
SSD_INNER = SSD_HEADS * SSD_HEAD_DIM
SSD_GROUPS = 2
SSD_STATE = 128
SSD_CHUNK = 128
SSD_CONV_CH = SSD_INNER + 2 * SSD_GROUPS * SSD_STATE

GQA_HEADS = 8
GQA_KV_HEADS = 2
GQA_HEAD_DIM = 64

GQA_IN = (GQA_HEADS + 2 * GQA_KV_HEADS) * GQA_HEAD_DIM
GATE_IN = 3 * D_MODEL
MLA_OUT = MLA_HEADS * MLA_V
GQA_OUT = GQA_HEADS * GQA_HEAD_DIM
FFN_HIDDEN = -(-8 * D_MODEL // (3 * 256)) * 256
DEEPNORM_ALPHA = (2 * DEPTH) ** 0.25

LANE = 128
MOD_ROWS = 8
LOG2E = 1.4426950408889634

C_PM = 0
W_PM = MLA_Q_RANK + MLA_KV_RANK + LANE
C_Z = C_PM + W_PM
C_XBC = C_Z + SSD_INNER
C_DT = C_XBC + SSD_CONV_CH
C_PG = C_DT + LANE
C_GATE = C_PG + GQA_IN
W_INP = C_GATE + GATE_IN

VMEM_LIMIT = 56 * 1024 * 1024
ROW_TILE = 256
LATENT_ROW_TILE = 512
ATTN_TQ = 512
ATTN_TK = 512
ATTN_COL_BLOCK = 512
ATTN_KEY_BLOCK = 32
SSD_BLOCK = 2 * SSD_CHUNK
SSD_BATCH = 2
VT_TILE = ROW_TILE
VT_ROWS = 144


def _cparams(sem):
    return pltpu.CompilerParams(dimension_semantics=sem, vmem_limit_bytes=VMEM_LIMIT)


def _const_spec(shape):
    nd = len(shape)
    return pl.BlockSpec(shape, lambda *_: (0,) * nd, pipeline_mode=pl.Buffered(1))


def _layer_norm(x):
    mu = jnp.mean(x, axis=-1, keepdims=True)
    xc = x - mu
    var = jnp.mean(xc * xc, axis=-1, keepdims=True)
    return xc * lax.rsqrt(var + NORM_EPS)


def _split_dot(x, m01, parts):
    pieces = []
    rem = x
    for _ in range(parts):
        piece = rem.astype(BF16)
        pieces.append(piece)
        rem = rem - piece.astype(F32)
    return jnp.dot(jnp.concatenate(pieces, axis=1), jnp.concatenate([m01] * parts, axis=0),
                   preferred_element_type=F32)


def _mod_rows(mod_ref, col, width, is_ctx):
    b = pl.program_id(0)
    lat = mod_ref[pl.ds(b, 1), col:col + width]
    ctx = mod_ref[MOD_ROWS // 2:MOD_ROWS // 2 + 1, col:col + width]
    return jnp.where(is_ctx, ctx, lat)


def _is_ctx_rows(tm, n_lat):
    t = pl.program_id(1)
    rows = t * tm + lax.broadcasted_iota(jnp.int32, (tm, 1), 0)
    return rows >= n_lat


def _mod_kernel(c_ref, w_ref, b_ref, o_ref):
    c = c_ref[...]
    s = (c * jax.nn.sigmoid(c)).astype(BF16)
    o_ref[0] = jnp.dot(s, w_ref[0].astype(BF16), preferred_element_type=F32) + b_ref[0]


def _modulation(cvec, w_mod, b_mod):
    depth, d, n = w_mod.shape
    tn = 1024
    return pl.pallas_call(
        _mod_kernel,
        grid=(depth, n // tn),
        in_specs=[
            pl.BlockSpec((MOD_ROWS, d), lambda l, j: (0, 0)),
            pl.BlockSpec((1, d, tn), lambda l, j: (l, 0, j)),
            pl.BlockSpec((1, 1, tn), lambda l, j: (l, 0, j)),
        ],
        out_specs=pl.BlockSpec((1, MOD_ROWS, tn), lambda l, j: (l, 0, j)),
        out_shape=jax.ShapeDtypeStruct((depth, MOD_ROWS, n), F32),
        compiler_params=_cparams(("arbitrary", "arbitrary")),
        name="modulation",
    )(cvec, w_mod, b_mod.reshape(depth, 1, n))


def _rope_lanes(x, c, s_fwd, s_bwd, half):
    n = x.shape[-1]
    return x * c + pltpu.roll(x, n - half, axis=1) * s_fwd + pltpu.roll(x, half, axis=1) * s_bwd


def _mla_prep(pm, gq_ref, gkv_ref, wuq_ref, wukv_ref, tq_ref, tk_ref, q_ref, k_ref, v_ref):
    pq = pm[:, :MLA_Q_RANK]
    ckv = pm[:, MLA_Q_RANK:MLA_Q_RANK + MLA_KV_RANK]
    kr = pm[:, MLA_Q_RANK + MLA_KV_RANK:]

    def rms(x, g):
        return x * lax.rsqrt(jnp.mean(x * x, axis=-1, keepdims=True) + NORM_EPS) * g

    q = jnp.dot(rms(pq, gq_ref[...]).astype(BF16), wuq_ref[...], preferred_element_type=F32)
    kv = jnp.dot(rms(ckv, gkv_ref[...]).astype(BF16), wukv_ref[...], preferred_element_type=F32)
    half = MLA_ROPE // 2
    kr = _rope_lanes(kr, tk_ref[0], tk_ref[1], tk_ref[2], half)
    cq, sq1, sq2 = tq_ref[0], tq_ref[1], tq_ref[2]
    for hd in range(MLA_HEADS):
        sl = slice(hd * LANE, (hd + 1) * LANE)
        q_ref[0, :, sl] = _rope_lanes(q[:, sl], cq, sq1, sq2, half).astype(q_ref.dtype)
        k_ref[0, :, sl] = (kv[:, sl] + kr).astype(k_ref.dtype)
    ones = jnp.ones((VT_ROWS - LANE, pm.shape[0]), v_ref.dtype)
    for j in range(MLA_HEADS // 2):
        c0 = MLA_HEADS * LANE + j * LANE
        v_ref[0, 0, j * VT_ROWS:j * VT_ROWS + LANE, :] = kv[:, c0:c0 + LANE].T.astype(v_ref.dtype)
        v_ref[0, 0, j * VT_ROWS + LANE:(j + 1) * VT_ROWS, :] = ones


def _gqa_prep(pg, gq_ref, gk_ref, bd_ref, tq_ref, tk_ref, q_ref, k_ref, v_ref):
    hd = GQA_HEAD_DIM
    q = pg[:, :GQA_OUT]
    k = pg[:, GQA_OUT:GQA_OUT + LANE]
    v = pg[:, GQA_OUT + LANE:]
    half = hd // 2

    def head_rms(x, g):
        w = x.shape[-1]
        ms = _split_dot(x * x, bd_ref[:w, :w], 2) * (1.0 / hd)
        return x * lax.rsqrt(ms + NORM_EPS) * g

    qn = head_rms(q, gq_ref[...])
    for j in range(GQA_OUT // LANE):
        sl = slice(j * LANE, (j + 1) * LANE)
        q_ref[0, :, sl] = _rope_lanes(qn[:, sl], tq_ref[0], tq_ref[1], tq_ref[2], half).astype(q_ref.dtype)
    kn = head_rms(k, gk_ref[...])
    kn = _rope_lanes(kn, tk_ref[0], tk_ref[1], tk_ref[2], half)
    lane = lax.broadcasted_iota(jnp.int32, kn.shape, 1)
    low = lane < hd
    kn_sw = pltpu.roll(kn, hd, axis=1)
    zero = jnp.zeros_like(kn)
    k_ref[0, :, 0 * LANE:1 * LANE] = jnp.where(low, kn, zero).astype(k_ref.dtype)
    k_ref[0, :, 1 * LANE:2 * LANE] = jnp.where(low, zero, kn_sw).astype(k_ref.dtype)
    k_ref[0, :, 2 * LANE:3 * LANE] = jnp.where(low, kn_sw, zero).astype(k_ref.dtype)
    k_ref[0, :, 3 * LANE:4 * LANE] = jnp.where(low, zero, kn).astype(k_ref.dtype)
    v_t = v.T.astype(v_ref.dtype)
    ones = jnp.ones((VT_ROWS - LANE, v.shape[0]), v_ref.dtype)
    for g in range(GQA_KV_HEADS):
        vg = v_t[g * hd:(g + 1) * hd]
        v_ref[0, 0, g * VT_ROWS:g * VT_ROWS + hd, :] = vg
        v_ref[0, 0, g * VT_ROWS + hd:g * VT_ROWS + LANE, :] = vg
        v_ref[0, 0, g * VT_ROWS + LANE:(g + 1) * VT_ROWS, :] = ones


def _split_row_specs(tm, width, n_lat, ctx_row0):
    lat_tiles = n_lat // tm
    assert ctx_row0 % tm == 0
    ctx_block0 = ctx_row0 // tm
    return [pl.BlockSpec((1, tm, width), lambda b, t: (b, jnp.minimum(t, lat_tiles - 1), 0)),
            pl.BlockSpec((1, tm, width), lambda b, t: (b, ctx_block0 + jnp.maximum(t - lat_tiles, 0), 0))]


def _pick_rows(lat_ref, ctx_ref, tm, n_lat):
    return jnp.where(pl.program_id(1) * tm >= n_lat, ctx_ref[0], lat_ref[0])


def _inproj_kernel(xl_ref, xc_ref, mod_ref, w_ref, bg_ref,
                   gqm_ref, gkvm_ref, wuq_ref, wukv_ref, tmq_ref, tmk_ref,
                   gqg_ref, gkg_ref, bd_ref, tgq_ref, tgk_ref,
                   z_ref, xbc_ref, dt_ref, gate_ref, qm_ref, km_ref, vm_ref, qg_ref, kg_ref, vg_ref,
                   *, tm, n_lat):
    is_ctx = _is_ctx_rows(tm, n_lat)
    d = D_MODEL
    sh = _mod_rows(mod_ref, 0, d, is_ctx)
    sc = _mod_rows(mod_ref, d, d, is_ctx)
    h = (_layer_norm(_pick_rows(xl_ref, xc_ref, tm, n_lat)) * (1.0 + sc) + sh).astype(BF16)

    def proj(c0, width):
        return jnp.dot(h, w_ref[:, c0:c0 + width], preferred_element_type=F32)

    _mla_prep(proj(C_PM, W_PM), gqm_ref, gkvm_ref, wuq_ref, wukv_ref, tmq_ref, tmk_ref, qm_ref, km_ref, vm_ref)
    _gqa_prep(proj(C_PG, GQA_IN), gqg_ref, gkg_ref, bd_ref, tgq_ref, tgk_ref, qg_ref, kg_ref, vg_ref)
    z_ref[0] = proj(C_Z, SSD_INNER)
    xbc_ref[0] = proj(C_XBC, SSD_CONV_CH)
    dt_ref[0] = proj(C_DT, LANE)
    for j in range(3):
        c0 = j * D_MODEL
        logits = proj(C_GATE + c0, D_MODEL) + bg_ref[:, c0:c0 + D_MODEL]
        gate_ref[0, :, c0:c0 + D_MODEL] = jax.nn.sigmoid(logits).astype(gate_ref.dtype)


def _in_projection(x_lat, x_ctx, ctx_row0, t_all, mod, w_inp, b_gate, mla_consts, gqa_consts, *, n_lat, tm):
    bsz, _, d = x_lat.shape
    assert tm == VT_TILE
    row = lambda w: pl.BlockSpec((1, tm, w), lambda b, t: (b, t, 0))
    tab = pl.BlockSpec((3, tm, LANE), lambda b, t: (0, t, 0))
    vt_spec = lambda rows: pl.BlockSpec((1, 1, rows, tm), lambda b, t: (b, t, 0, 0))
    g_qm, g_kvm, wuq_p, wukv_p, tab_mq, tab_mk = mla_consts
    g_qg, g_kg, bd, tab_gq, tab_gk = gqa_consts
    hw = MLA_HEADS * LANE
    kw = GQA_KV_HEADS * 2 * LANE
    vm_rows = (MLA_HEADS // 2) * VT_ROWS
    vg_rows = GQA_KV_HEADS * VT_ROWS
    seq = lambda w, dt: jax.ShapeDtypeStruct((bsz, t_all, w), dt)
    vts = lambda rows: jax.ShapeDtypeStruct((bsz, t_all // tm, rows, tm), BF16)
    consts = (mod, w_inp, b_gate, g_qm, g_kvm, wuq_p, wukv_p)
    return pl.pallas_call(
        functools.partial(_inproj_kernel, tm=tm, n_lat=n_lat),
        grid=(bsz, t_all // tm),
        in_specs=_split_row_specs(tm, d, n_lat, ctx_row0) + [_const_spec(a.shape) for a in consts] + [tab, tab]
                 + [_const_spec(a.shape) for a in (g_qg, g_kg, bd)] + [tab, tab],
        out_specs=[row(SSD_INNER), row(SSD_CONV_CH), row(LANE), row(GATE_IN),
                   row(hw), row(hw), vt_spec(vm_rows), row(GQA_OUT), row(kw), vt_spec(vg_rows)],
        out_shape=[seq(SSD_INNER, F32), seq(SSD_CONV_CH, F32), seq(LANE, F32), seq(GATE_IN, BF16),
                   seq(hw, BF16), seq(hw, BF16), vts(vm_rows), seq(GQA_OUT, BF16), seq(kw, BF16), vts(vg_rows)],
        compiler_params=_cparams(("parallel", "parallel")),
        name="in_projection",
    )(x_lat, x_ctx, *consts, tab_mq, tab_mk, g_qg, g_kg, bd, tab_gq, tab_gk)


def _attn_kernel(q_ref, k_ref, vt_ref, o_ref, m_ref, acc_ref, a0_ref, a1_ref, c0_ref, c1_ref,
                 p0_ref, p1_ref, s0_ref, s1_ref, *, pairs, tq, tk, n_lat, ctx_len, ctx_queries):
    cb = ATTN_COL_BLOCK
    vr = VT_ROWS
    s_bufs, p_bufs, a_bufs, c_bufs = (s0_ref, s1_ref), (p0_ref, p1_ref), (a0_ref, a1_ref), (c0_ref, c1_ref)

    def scores(r0, size, par):
        for pi, (_, units) in enumerate(pairs):
            for ui, (qb, kb) in enumerate(units):
                s = lax.dot_general(
                    k_ref[0, pl.ds(r0, size), kb * LANE:(kb + 1) * LANE],
                    q_ref[0, :, qb * LANE:(qb + 1) * LANE],
                    (((1,), (1,)), ((), ())), preferred_element_type=F32)
                s_bufs[par][pi, :size, ui * tq:(ui + 1) * tq] = s
                c_bufs[par][pi, :, ui * tq:(ui + 1) * tq] = jnp.max(s, axis=0, keepdims=True)

    def softmax(size, par):
        for pi in range(len(pairs)):
            for c in range(0, 2 * tq, cb):
                m_prev = m_ref[pi, :, c:c + cb]
                m_next = jnp.maximum(m_prev, c_bufs[par][pi, :, c:c + cb])
                for r in range(0, size, ATTN_KEY_BLOCK):
                    blk = s_bufs[par][pi, r:r + ATTN_KEY_BLOCK, c:c + cb]
                    p_bufs[par][pi, r:r + ATTN_KEY_BLOCK, c:c + cb] = jnp.exp2(blk - m_next).astype(BF16)
                a_bufs[par][pi, :, c:c + cb] = jnp.exp2(m_prev - m_next)
                m_ref[pi, :, c:c + cb] = m_next

    def values(u0, size, par):
        for pi, (vb, _) in enumerate(pairs):
            vt = jnp.concatenate([vt_ref[0, u0 + i, vb * vr:(vb + 1) * vr, :]
                                  for i in range(size // VT_TILE)], axis=1)
            pv = jnp.dot(vt, p_bufs[par][pi, :size, :], preferred_element_type=F32)
            acc_ref[pi] = a_bufs[par][pi] * acc_ref[pi] + pv

    m_ref[...] = jnp.full(m_ref.shape, -jnp.inf, F32)
    acc_ref[...] = jnp.zeros(acc_ref.shape, F32)
    tku = tk // VT_TILE
    ctx_row0 = 0 if ctx_queries else n_lat
    ctx_u = ctx_row0 // VT_TILE
    scores(ctx_row0, ctx_len, 0)
    if ctx_queries:
        softmax(ctx_len, 0)
        values(ctx_u, ctx_len, 0)
    else:
        n = n_lat // tk
        scores(0, tk, 1)
        softmax(ctx_len, 0)
        scores(tk, tk, 0)
        softmax(tk, 1)
        values(ctx_u, ctx_len, 0)

        def body(jj, carry):
            r = pl.multiple_of(jj * (2 * tk), 2 * tk)
            u = jj * (2 * tku)
            scores(r + 2 * tk, tk, 1)
            softmax(tk, 0)
            values(u, tk, 1)
            scores(r + 3 * tk, tk, 0)
            softmax(tk, 1)
            values(u + tku, tk, 0)
            return carry

        lax.fori_loop(0, (n - 2) // 2, body, 0)
        softmax(tk, 0)
        values((n - 2) * tku, tk, 1)
        values((n - 1) * tku, tk, 0)
    row = lax.broadcasted_iota(jnp.int32, (LANE, tq), 0)
    for pi in range(len(pairs)):
        acc = acc_ref[pi]
        o_t = acc[:LANE] / acc[LANE:LANE + 1]
        o_t = jnp.where(row < LANE // 2, o_t[:, :tq], o_t[:, tq:])
        o_ref[0, :, pi * LANE:(pi + 1) * LANE] = o_t.T.astype(o_ref.dtype)


def _attention(q, k, vt, *, groups, q_w, k_w, vt_rows, pairs, n_lat, ctx_queries, tq, tk):
    bsz, t_all, _ = k.shape
    ctx_len = t_all - n_lat
    out_w = len(pairs) * LANE
    rows = ctx_len if ctx_queries else n_lat
    q_tile0 = n_lat // tq if ctx_queries else 0
    key_rows = ctx_len if ctx_queries else t_all
    key_blk = n_lat // ctx_len if ctx_queries else 0
    assert n_lat % ctx_len == 0
    assert rows % tq == 0 and n_lat % tq == 0 and n_lat % (2 * tk) == 0 and ctx_len <= tk
    assert (2 * tq) % ATTN_COL_BLOCK == 0 and tk % VT_TILE == 0 and ctx_len % VT_TILE == 0
    state = pltpu.VMEM((len(pairs), 1, 2 * tq), F32)
    acc = pltpu.VMEM((len(pairs), VT_ROWS, 2 * tq), F32)
    pbuf = pltpu.VMEM((len(pairs), tk, 2 * tq), BF16)
    sbuf = pltpu.VMEM((len(pairs), tk, 2 * tq), F32)
    return pl.pallas_call(
        functools.partial(_attn_kernel, pairs=pairs, tq=tq, tk=tk, n_lat=n_lat, ctx_len=ctx_len,
                          ctx_queries=ctx_queries),
        grid=(bsz, groups, rows // tq),
        in_specs=[
            pl.BlockSpec((1, tq, q_w), lambda b, g, t: (b, t + q_tile0, g)),
            pl.BlockSpec((1, key_rows, k_w), lambda b, g, t: (b, key_blk, g), pipeline_mode=pl.Buffered(1)),
            pl.BlockSpec((1, key_rows // VT_TILE, vt_rows, VT_TILE), lambda b, g, t: (b, key_blk, g, 0),
                         pipeline_mode=pl.Buffered(1)),
        ],
        out_specs=pl.BlockSpec((1, tq, out_w), lambda b, g, t: (b, t, g)),
        out_shape=jax.ShapeDtypeStruct((bsz, rows, groups * out_w), BF16),
        scratch_shapes=[state, acc, state, state, state, state, pbuf, pbuf, sbuf, sbuf],
        compiler_params=_cparams(("parallel", "parallel", "arbitrary")),
        name="attention_ctx" if ctx_queries else "attention",
    )(q, k, vt)


def _ssd_chunk_index(i, n_chunks, ctx_chunks, reverse):
    if reverse:
        return n_chunks - 1 - i
    lat_chunks = n_chunks - ctx_chunks
    return jnp.where(i < ctx_chunks, lat_chunks + i, i - ctx_chunks)


def _ssd_kernel(*refs, reverse, n_chunks, ctx_chunks):
    if reverse:
        (u_ref, dt_ref, dtb_ref, a_ref, tri_ref, ehd_ref,
         yf_ref, z_ref, dsk_ref, g_ref, y_ref, h_ref) = refs
    else:
        (xbc_ref, prev_ref, next_ref, dt_ref, cw_ref, cb_ref, dtb_ref, a_ref, tri_ref, ehd_ref,
         y_ref, uo_ref, h_ref) = refs
    q = SSD_CHUNK
    rows = SSD_BLOCK
    i = pl.program_id(1)
    ci = _ssd_chunk_index(i, n_chunks, ctx_chunks, reverse)

    @pl.when(i == 0)
    def _():
        h_ref[...] = jnp.zeros_like(h_ref)

    u_blks = []
    for bb in range(SSD_BATCH):
        if reverse:
            u_blks.append(u_ref[bb])
            continue
        x = xbc_ref[bb]
        lat_chunks = n_chunks - ctx_chunks
        at_start = jnp.logical_or(ci == 0, ci == lat_chunks)
        at_end = jnp.logical_or(ci == lat_chunks - 1, ci == n_chunks - 1)
        prev_row = jnp.where(at_start, 0.0, prev_ref[bb, 7:8, :])
        next_row = jnp.where(at_end, 0.0, next_ref[bb, 0:1, :])
        row = lax.broadcasted_iota(jnp.int32, (rows, 1), 0)
        x_prev = jnp.where(row == 0, prev_row, pltpu.roll(x, 1, axis=0))
        x_next = jnp.where(row == rows - 1, next_row, pltpu.roll(x, rows - 1, axis=0))
        u_blk = cw_ref[0:1, :] * x_prev + cw_ref[1:2, :] * x + cw_ref[2:3, :] * x_next + cb_ref[...]
        u_blk = u_blk * jax.nn.sigmoid(u_blk)
        uo_ref[bb] = u_blk
        u_blks.append(u_blk)

    subs = range(rows // q)
    for sub in (reversed(subs) if reverse else subs):
        sl = slice(sub * q, (sub + 1) * q)
        for bb in range(SSD_BATCH):
            u_blk = u_blks[bb]
            y = _ssd_chunk(u_blk[sl], dt_ref[bb, sl, :], dtb_ref, a_ref, tri_ref, ehd_ref, h_ref.at[bb], reverse)
            if reverse:
                yt = yf_ref[bb, sl, :] + y + dsk_ref[...] * u_blk[sl, :SSD_INNER]
                zz = z_ref[bb, sl, :]
                yt = yt * (zz * jax.nn.sigmoid(zz))
                yt = yt * lax.rsqrt(jnp.mean(yt * yt, axis=-1, keepdims=True) + NORM_EPS) * g_ref[...]
                y_ref[bb, sl, :] = yt.astype(y_ref.dtype)
            else:
                y_ref[bb, sl, :] = y


def _ssd_chunk(u, dt_raw, dtb_ref, a_ref, tri_ref, ehd_ref, h_ref, reverse):
    q = SSD_CHUNK
    xs = u[:, :SSD_INNER]
    gs = SSD_STATE
    bm = [u[:, SSD_INNER + g * gs:SSD_INNER + (g + 1) * gs] for g in range(SSD_GROUPS)]
    cm = [u[:, SSD_INNER + (SSD_GROUPS + g) * gs:SSD_INNER + (SSD_GROUPS + g + 1) * gs]
          for g in range(SSD_GROUPS)]

    dt = jax.nn.softplus(dt_raw + dtb_ref[...])
    da = dt * a_ref[...]
    a_cum = _tri_cumsum(tri_ref, da)
    end = 0 if reverse else q - 1
    a_tot = a_cum[end:end + 1, :]
    e_cum = jnp.exp(a_cum)
    dte = jnp.exp(a_tot - a_cum)
    dt_x = _split_dot(dt, ehd_ref[...], 2)
    ec_x = _split_dot(e_cum, ehd_ref[...], 2)
    dte_x = _split_dot(dte, ehd_ref[...], 2)
    xdt = xs * dt_x
    xdt_b = xdt.astype(BF16)
    xst_b = (xdt * dte_x).astype(BF16)
    a_cum_t = a_cum.T

    kk = lax.broadcasted_iota(jnp.int32, (q, q), 1)
    qq = lax.broadcasted_iota(jnp.int32, (q, q), 0)
    in_order = (kk >= qq) if reverse else (kk <= qq)
    lane = lax.broadcasted_iota(jnp.int32, (q, LANE), 1)
    off = SSD_HEADS if reverse else 0
    hpg = SSD_HEADS // SSD_GROUPS
    gw = hpg * SSD_HEAD_DIM
    y_parts = []
    for g in range(SSD_GROUPS):
        bm_b = bm[g].astype(BF16)
        cm_b = cm[g].astype(BF16)
        scores = lax.dot_general(cm_b, bm_b, (((1,), (1,)), ((), ())), preferred_element_type=F32)
        h_prev = h_ref[g]
        y_off = jnp.dot(cm_b, h_prev.astype(BF16), preferred_element_type=F32) * ec_x[:, g * gw:(g + 1) * gw]
        pair_out = []
        for j in range(hpg // 2):
            xp = xdt_b[:, g * gw + j * LANE:g * gw + (j + 1) * LANE]
            res = []
            for e in range(2):
                hd = g * hpg + 2 * j + e
                a_q = a_cum[:, off + hd:off + hd + 1]
                a_k = a_cum_t[off + hd:off + hd + 1, :]
                decay = jnp.exp(jnp.where(in_order, a_q - a_k, -jnp.inf))
                pmat = (scores * decay).astype(BF16)
                res.append(jnp.dot(pmat, xp, preferred_element_type=F32))
            pair_out.append(jnp.where(lane < SSD_HEAD_DIM, res[0], res[1]))
        y_parts.append(jnp.concatenate(pair_out, axis=1) + y_off)
        st = jnp.dot(bm[g].T.astype(BF16), xst_b[:, g * gw:(g + 1) * gw], preferred_element_type=F32)
        h_ref[g] = h_prev * ec_x[end:end + 1, g * gw:(g + 1) * gw] + st
    return jnp.concatenate(y_parts, axis=1)


def _tri_cumsum(tri_ref, da):
    acc = None
    rem = da
    for _ in range(3):
        piece = rem.astype(BF16)
        term = jnp.dot(tri_ref[...], piece, preferred_element_type=F32)
        acc = term if acc is None else acc + term
        rem = rem - piece.astype(F32)
    return acc


def _ssd_direction(xbc, dt, conv, dt_bias, a_row, tri, ehd, extra, *, ctx_len, reverse):
    bsz, t_all, cch = xbc.shape
    q = SSD_BLOCK
    assert t_all % q == 0 and ctx_len % q == 0
    n_chunks = t_all // q
    ctx_chunks = ctx_len // q
    halo = 8
    hb = q // halo
    n_halo = t_all // halo
    nb = SSD_BATCH
    assert bsz % nb == 0
    cidx = lambda i: _ssd_chunk_index(i, n_chunks, ctx_chunks, reverse)
    chunk = lambda w: pl.BlockSpec((nb, q, w), lambda b, i: (b, cidx(i), 0))
    tail = [_const_spec(dt_bias.shape), _const_spec(a_row.shape), _const_spec(tri.shape), _const_spec(ehd.shape)]
    if reverse:
        y_f, z, d_x, g_ssd = extra
        in_specs = [chunk(cch), chunk(LANE)] + tail + [
            chunk(SSD_INNER), chunk(SSD_INNER), _const_spec(d_x.shape), _const_spec(g_ssd.shape)]
        args = [xbc, dt, dt_bias, a_row, tri, ehd, y_f, z, d_x, g_ssd]
        out_specs = chunk(SSD_INNER)
        out_shape = jax.ShapeDtypeStruct((bsz, t_all, SSD_INNER), BF16)
    else:
        conv_w, conv_b = conv
        in_specs = [
            chunk(cch),
            pl.BlockSpec((nb, halo, cch), lambda b, i: (b, jnp.maximum(cidx(i) * hb - 1, 0), 0)),
            pl.BlockSpec((nb, halo, cch), lambda b, i: (b, jnp.minimum((cidx(i) + 1) * hb, n_halo - 1), 0)),
            chunk(LANE), _const_spec(conv_w.shape), _const_spec(conv_b.shape)] + tail
        args = [xbc, xbc, xbc, dt, conv_w, conv_b, dt_bias, a_row, tri, ehd]
        out_specs = [chunk(SSD_INNER), chunk(cch)]
        out_shape = [jax.ShapeDtypeStruct((bsz, t_all, SSD_INNER), F32),
                     jax.ShapeDtypeStruct((bsz, t_all, cch), F32)]
    hpg = SSD_HEADS // SSD_GROUPS
    return pl.pallas_call(
        functools.partial(_ssd_kernel, reverse=reverse, n_chunks=n_chunks, ctx_chunks=ctx_chunks),
        grid=(bsz // nb, n_chunks),
        in_specs=in_specs,
        out_specs=out_specs,
        out_shape=out_shape,
        scratch_shapes=[pltpu.VMEM((nb, SSD_GROUPS, SSD_STATE, hpg * SSD_HEAD_DIM), F32)],
        compiler_params=_cparams(("parallel", "arbitrary")),
        name="ssd_bwd" if reverse else "ssd_fwd",
    )(*args)


def _post_kernel(xl_ref, xc_ref, yml_ref, ymc_ref, ys_ref, ygl_ref, ygc_ref, gate_ref, mod_ref,
                 wo_ref, wi_ref, wf_ref, l1g_ref, l1b_ref, l2g_ref, l2b_ref, o_ref, act_ref, *, tm, n_lat, th):
    is_ctx = _is_ctx_rows(tm, n_lat)
    d = D_MODEL
    mix = None
    row0 = 0
    branches = (_pick_rows(yml_ref, ymc_ref, tm, n_lat), ys_ref[0], _pick_rows(ygl_ref, ygc_ref, tm, n_lat))
    for j, y in enumerate(branches):
        kw = y.shape[-1]
        br = jnp.dot(y, wo_ref[row0:row0 + kw, :], preferred_element_type=F32)
        br = gate_ref[0, :, j * d:(j + 1) * d].astype(F32) * br
        mix = br if mix is None else mix + br
        row0 += kw
    g1 = _mod_rows(mod_ref, 2 * d, d, is_ctx)
    x_in = _pick_rows(xl_ref, xc_ref, tm, n_lat)
    x1 = _layer_norm(DEEPNORM_ALPHA * x_in + g1 * mix) * l1g_ref[...] + l1b_ref[...]

    sh2 = _mod_rows(mod_ref, 3 * d, d, is_ctx)
    sc2 = _mod_rows(mod_ref, 4 * d, d, is_ctx)
    g2 = _mod_rows(mod_ref, 5 * d, d, is_ctx)
    h2 = (_layer_norm(x1) * (1.0 + sc2) + sh2).astype(BF16)
    hid = FFN_HIDDEN
    for c0 in range(0, hid, th):
        gt = jnp.dot(h2, wi_ref[:, c0:c0 + th], preferred_element_type=F32)
        up = jnp.dot(h2, wi_ref[:, hid + c0:hid + c0 + th], preferred_element_type=F32)
        act_ref[:, c0:c0 + th] = (gt * jax.nn.sigmoid(gt) * up).astype(BF16)
    fx = jnp.dot(act_ref[...], wf_ref[...], preferred_element_type=F32)
    o_ref[0] = _layer_norm(DEEPNORM_ALPHA * x1 + g2 * fx) * l2g_ref[...] + l2b_ref[...]


def _post_mixer(x_pair, y_mla_pair, y_ssd, y_gqa_pair, gate, mod, w_out, w_ffn_in, w_ffn_out,
                l1g, l1b, l2g, l2b, *, n_lat, tm, with_ctx):
    bsz, t_all, _ = y_ssd.shape
    d = D_MODEL
    row = lambda w: pl.BlockSpec((1, tm, w), lambda b, t: (b, t, 0))
    vec = _const_spec((1, d))
    t_out = t_all if with_ctx else n_lat
    split = lambda w, triple: _split_row_specs(tm, w, n_lat, triple[2] if with_ctx else 0)
    return pl.pallas_call(
        functools.partial(_post_kernel, tm=tm, n_lat=n_lat, th=256),
        grid=(bsz, t_out // tm),
        in_specs=split(d, x_pair) + split(MLA_OUT, y_mla_pair) + [row(SSD_INNER)] + split(GQA_OUT, y_gqa_pair)
                 + [row(GATE_IN), _const_spec(mod.shape),
                    _const_spec(w_out.shape), _const_spec(w_ffn_in.shape), _const_spec(w_ffn_out.shape),
                    vec, vec, vec, vec],
        out_specs=pl.BlockSpec((1, tm, d), lambda b, t: (b, t, 0)),
        out_shape=jax.ShapeDtypeStruct((bsz, t_out, d), F32),
        scratch_shapes=[pltpu.VMEM((tm, FFN_HIDDEN), BF16)],
        compiler_params=_cparams(("parallel", "parallel")),
        name="post_mixer",
    )(x_pair[0], x_pair[1], y_mla_pair[0], y_mla_pair[1], y_ssd, y_gqa_pair[0], y_gqa_pair[1], gate, mod,
      w_out, w_ffn_in, w_ffn_out, l1g, l1b, l2g, l2b)


def _rope_tables(n_lat, ctx_len, rot_dim, lane_dim, lane_off, scale):
    rows = n_lat // GRID_W
    row = np.repeat(np.arange(rows), GRID_W)
    col = np.tile(np.arange(GRID_W), rows)
    n_freq = rot_dim // 4
    inv_freq = ROPE_THETA ** (-np.arange(n_freq, dtype=np.float64) / n_freq)
    ang = np.concatenate([row[:, None] * inv_freq, col[:, None] * inv_freq], axis=-1)
    cos, sin = np.cos(ang), np.sin(ang)
    half = rot_dim // 2
    lane = np.arange(LANE)
    rel = (lane % lane_dim) - lane_off
    is_rot = (rel >= 0) & (rel < rot_dim)
    idx = np.where(is_rot, rel % half, 0)
    first = is_rot & (rel < half)
    second = is_rot & (rel >= half)
    c = np.where(is_rot[None, :], cos[:, idx], 1.0)
    s1 = np.where(first[None, :], -sin[:, idx], 0.0)
    s2 = np.where(second[None, :], sin[:, idx], 0.0)
    lat = np.stack([c, s1, s2])
    ctx = np.stack([np.ones((ctx_len, LANE)), np.zeros((ctx_len, LANE)), np.zeros((ctx_len, LANE))])
    return jnp.asarray((np.concatenate([lat, ctx], axis=1) * scale).astype(np.float32))


def _pad_cols(w, width, offset=0):
    return jnp.pad(w, ((0, 0), (offset, width - offset - w.shape[1])))


def _layout_w_in(w):
    c = 0
    q_lat = w[:, c:c + MLA_Q_RANK]; c += MLA_Q_RANK
    kv_lat = w[:, c:c + MLA_KV_RANK]; c += MLA_KV_RANK
    k_rope = _pad_cols(w[:, c:c + MLA_ROPE], LANE, MLA_NOPE); c += MLA_ROPE
    z = w[:, c:c + SSD_INNER]; c += SSD_INNER
    xbc = w[:, c:c + SSD_CONV_CH]; c += SSD_CONV_CH
    dt = _pad_cols(w[:, c:c + 2 * SSD_HEADS], LANE); c += 2 * SSD_HEADS
    rest = w[:, c:]
    return jnp.concatenate([q_lat, kv_lat, k_rope, z, xbc, dt, rest], axis=1).astype(BF16)


def _layout_w_uq(w):
    w = w.reshape(w.shape[0], MLA_HEADS, MLA_QK)
    w = jnp.pad(w, ((0, 0), (0, 0), (0, LANE - MLA_QK)))
    return w.reshape(w.shape[0], MLA_HEADS * LANE).astype(BF16)


def _layout_w_ukv(w):
    w = w.reshape(w.shape[0], MLA_HEADS, MLA_NOPE + MLA_V)
    k = jnp.pad(w[:, :, :MLA_NOPE], ((0, 0), (0, 0), (0, LANE - MLA_NOPE)))
    v = w[:, :, MLA_NOPE:]
    return jnp.concatenate([k.reshape(w.shape[0], -1), v.reshape(w.shape[0], -1)], axis=1).astype(BF16)


def _lane_row(vec, width=LANE):
    return _pad_cols(vec.reshape(1, -1).astype(F32), width)


def kernel(x, c, ctx, c_ctx, w_mod, b_mod, w_in, b_gate, w_uq, g_q_mla, w_ukv, g_kv_mla, conv_w, conv_b,
           a_log, dt_bias, d_skip, g_ssd, g_q_gqa, g_k_gqa, w_out, ln1_g, ln1_b, w_ffn_in, w_ffn_out,
           ln2_g, ln2_b):
    bsz, n_lat, d = x.shape
    ctx_len = ctx.shape[1]
    depth = w_mod.shape[0]
    assert d == D_MODEL and depth == DEPTH and bsz <= MOD_ROWS // 2
    tm = ROW_TILE
    assert ctx_len % tm == 0 and n_lat % tm == 0

    cvec = jnp.zeros((MOD_ROWS, d), F32).at[:bsz].set(c).at[MOD_ROWS // 2].set(c_ctx)
    mod_all = _modulation(cvec, w_mod, b_mod)

    mla_scale = MLA_QK ** -0.5 * LOG2E
    gqa_scale = GQA_HEAD_DIM ** -0.5 * LOG2E
    tab_mq = _rope_tables(n_lat, ctx_len, MLA_ROPE, LANE, MLA_NOPE, mla_scale)
    tab_mk = _rope_tables(n_lat, ctx_len, MLA_ROPE, LANE, MLA_NOPE, 1.0)
    tab_gq = _rope_tables(n_lat, ctx_len, GQA_HEAD_DIM, GQA_HEAD_DIM, 0, gqa_scale)
    tab_gk = _rope_tables(n_lat, ctx_len, GQA_HEAD_DIM, GQA_HEAD_DIM, 0, 1.0)

    qn = SSD_CHUNK
    r = np.arange(qn)
    tri_f = jnp.asarray((r[None, :] <= r[:, None]).astype(np.float32), BF16)
    tri_b = jnp.asarray((r[None, :] >= r[:, None]).astype(np.float32), BF16)
    lane1k = np.arange(SSD_INNER) // SSD_HEAD_DIM
    ehd_f = jnp.asarray((np.arange(LANE)[:, None] == lane1k[None, :]).astype(np.float32), BF16)
    ehd_b = jnp.asarray((np.arange(LANE)[:, None] == (lane1k[None, :] + SSD_HEADS)).astype(np.float32), BF16)
    blk = np.arange(GQA_OUT) // GQA_HEAD_DIM
    bd = jnp.asarray((blk[:, None] == blk[None, :]).astype(np.float32), BF16)

    mla_attn = functools.partial(_attention, groups=1, q_w=8 * LANE, k_w=8 * LANE, vt_rows=4 * VT_ROWS,
                                 pairs=tuple((j, ((2 * j, 2 * j), (2 * j + 1, 2 * j + 1))) for j in range(4)),
                                 n_lat=n_lat)
    gqa_attn = functools.partial(_attention, groups=1, q_w=4 * LANE, k_w=4 * LANE, vt_rows=2 * VT_ROWS,
                                 pairs=((0, ((0, 0), (0, 1))), (0, ((1, 0), (1, 1))),
                                        (1, ((2, 2), (2, 3))), (1, ((3, 2), (3, 3)))), n_lat=n_lat)

    def attend(fn, q, k, v, with_ctx):
        y = fn(q, k, v, ctx_queries=False, tq=ATTN_TQ, tk=ATTN_TK)
        y_c = fn(q, k, v, ctx_queries=True, tq=ctx_len, tk=ATTN_TK) if with_ctx else y
        return (y, y_c, 0)

    t_all = n_lat + ctx_len
    x_pair = (x, ctx, 0)
    for i in range(depth):
        with_ctx = i < depth - 1
        mod = mod_all[i]
        w_inp = _layout_w_in(w_in[i])
        mla_consts = (g_q_mla[i].reshape(1, -1), g_kv_mla[i].reshape(1, -1),
                      _layout_w_uq(w_uq[i]), _layout_w_ukv(w_ukv[i]), tab_mq, tab_mk)
        gqa_consts = (jnp.tile(g_q_gqa[i], GQA_HEADS).reshape(1, -1),
                      jnp.tile(g_k_gqa[i], GQA_KV_HEADS).reshape(1, -1), bd, tab_gq, tab_gk)
        z, xbc, dtr, gate, q_m, k_m, v_m, q_g, k_g, v_g = _in_projection(
            *x_pair, t_all, mod, w_inp, b_gate[i].reshape(1, -1), mla_consts, gqa_consts, n_lat=n_lat, tm=tm)
        y_mla = attend(mla_attn, q_m, k_m, v_m, with_ctx)
        y_gqa = attend(gqa_attn, q_g, k_g, v_g, with_ctx)
        a = -jnp.exp(a_log[i].astype(F32))
        a_row = _lane_row(a.reshape(-1))
        dtb_row = _lane_row(dt_bias[i].reshape(-1))
        y_f, u_act = _ssd_direction(xbc, dtr, (conv_w[i], conv_b[i].reshape(1, -1)), dtb_row, a_row,
                                    tri_f, ehd_f, None, ctx_len=ctx_len, reverse=False)
        d_x = jnp.repeat(d_skip[i], SSD_HEAD_DIM).reshape(1, -1)
        y_ssd = _ssd_direction(u_act, dtr, None, dtb_row, a_row, tri_b, ehd_b,
                               (y_f, z, d_x, g_ssd[i].reshape(1, -1)), ctx_len=ctx_len, reverse=True)
        tm_post = tm if with_ctx else LATENT_ROW_TILE
        xa = _post_mixer(x_pair, y_mla, y_ssd, y_gqa, gate, mod, w_out[i].astype(BF16),
                         w_ffn_in[i].astype(BF16), w_ffn_out[i].astype(BF16),
                         ln1_g[i].reshape(1, -1), ln1_b[i].reshape(1, -1),
                         ln2_g[i].reshape(1, -1), ln2_b[i].reshape(1, -1), n_lat=n_lat,
                         tm=tm_post, with_ctx=with_ctx)
        x_pair = (xa, xa, n_lat)
    return xa
```
